```python
import jax
import jax.numpy as jnp
from jax import lax
import numpy as np

D_MODEL = 1024
BATCH = 4
SEQ = 4096
DEPTH = 1
DEC_BATCH = 32
DEC_SEQ = 1
PAST_LEN = 8192
PAGE_SIZE = 128

MIX_WIDTH = D_MODEL
H_M = 4
MIX_M = MIX_WIDTH // 2
DH_M = MIX_M // H_M
H_A = 4
MIX_A = MIX_WIDTH - MIX_M
DH_A = MIX_A // H_A
H_IDX = 4
D_IDX = 64
TOPK_MAX = 256
N_MEM = 256
H_C = 4
DH_C = D_MODEL // H_C
D_FF = ((8 * D_MODEL // 3 + 127) // 128) * 128
CONV_W = 3
CHUNK = 128
Q_BLOCK = 128
ROPE_THETA = 10000.0
EPS = 1e-6
N_IN = 4 * MIX_M + 2 * H_M + 3 * MIX_A + H_IDX * D_IDX + D_IDX + H_IDX

kernel_name = 'hymba_mlstm_dsa_memxattn_convffn_step'


def rms_norm(x, g):
    xf = x.astype(jnp.float32)
    y = xf * lax.rsqrt(jnp.mean(xf * xf, axis=-1, keepdims=True) + EPS)
    return (y * g.astype(jnp.float32)).astype(x.dtype)


def rope(x, pos):
    half = x.shape[-1] // 2
    inv = ROPE_THETA ** (-jnp.arange(half, dtype=jnp.float32) / half)
    ang = pos.astype(jnp.float32)[:, None] * inv[None, :]
    cos = jnp.cos(ang)[:, None, :]
    sin = jnp.sin(ang)[:, None, :]
    xf = x.astype(jnp.float32)
    x1, x2 = xf[..., :half], xf[..., half:]
    return jnp.concatenate([x1 * cos - x2 * sin, x2 * cos + x1 * sin], axis=-1).astype(x.dtype)


def split_in(z):
    sizes = [MIX_M, MIX_M, MIX_M, MIX_M, H_M, H_M, MIX_A, MIX_A, MIX_A, H_IDX * D_IDX, D_IDX, H_IDX]
    offs = [int(o) for o in np.cumsum(sizes)[:-1]]
    return jnp.split(z, offs, axis=-1)


def mixer_inputs(h, pos, lp):
    B, T, _ = h.shape
    f32 = jnp.float32
    z = h @ lp['w_in']
    mq, mk, mv, mo, mi, mf, aq, ak, av, iq, ik, iw = split_in(z)
    gates = jnp.concatenate([mi, mf], axis=-1).astype(f32) + lp['b_if'].astype(f32)
    return dict(
        m_q=mq.reshape(B, T, H_M, DH_M).astype(f32),
        m_k=mk.reshape(B, T, H_M, DH_M).astype(f32) * (DH_M ** -0.5),
        m_v=mv.reshape(B, T, H_M, DH_M).astype(f32),
        m_o=mo,
        log_i=gates[..., :H_M],
        log_f=jax.nn.log_sigmoid(gates[..., H_M:]),
        a_q=rope(rms_norm(aq.reshape(B, T, H_A, DH_A), lp['q_norm']), pos),
        a_k=rope(rms_norm(ak.reshape(B, T, H_A, DH_A), lp['k_norm']), pos),
        a_v=av.reshape(B, T, H_A, DH_A),
        i_q=rope(iq.reshape(B, T, H_IDX, D_IDX), pos),
        i_k=rope(ik[:, :, None, :], pos)[:, :, 0, :],
        i_w=iw,
    )


def mlstm_chunk(carry, inp):
    C, n, m = carry
    q, k, v, li, lf = inp
    L = q.shape[2]
    b = jnp.cumsum(lf, axis=-1)
    causal = jnp.tril(jnp.ones((L, L), dtype=bool))
    logd = jnp.where(causal, b[..., :, None] - b[..., None, :] + li[..., None, :], -jnp.inf)
    inter = b + m[..., None]
    m_t = jnp.maximum(inter, jnp.max(logd, axis=-1))
    s = jnp.einsum('bhtd,bhsd->bhts', q, k) * jnp.exp(logd - m_t[..., None])
    g_inter = jnp.exp(inter - m_t)
    num = g_inter[..., None] * jnp.einsum('bhed,bhtd->bhte', C, q) + jnp.einsum('bhts,bhse->bhte', s, v)
    den = g_inter * jnp.einsum('bhd,bhtd->bht', n, q) + jnp.sum(s, axis=-1)
    h = num / jnp.maximum(jnp.abs(den), jnp.exp(-m_t))[..., None]
    m_new = m_t[..., -1]
    g_prev = jnp.exp(b[..., -1] + m - m_new)
    w_s = jnp.exp(b[..., -1:] - b + li - m_new[..., None])
    C_new = g_prev[..., None, None] * C + jnp.einsum('bhs,bhse,bhsd->bhed', w_s, v, k)
    n_new = g_prev[..., None] * n + jnp.einsum('bhs,bhsd->bhd', w_s, k)
    return (C_new, n_new, m_new), h


def mlstm_prompt(q, k, v, li, lf):
    B, S, H, D = q.shape
    L = min(CHUNK, S)
    nc = S // L
    to_c = lambda t: t.reshape(B, nc, L, H, D).transpose(1, 0, 3, 2, 4)
    to_g = lambda g: g.reshape(B, nc, L, H).transpose(1, 0, 3, 2)
    f32 = jnp.float32
    carry0 = (jnp.zeros((B, H, D, D), f32), jnp.zeros((B, H, D), f32), jnp.zeros((B, H), f32))
    carry, hs = lax.scan(mlstm_chunk, carry0, (to_c(q), to_c(k), to_c(v), to_g(li), to_g(lf)))
    return hs.transpose(1, 0, 3, 2, 4).reshape(B, S, H, D), carry


def mlstm_sample(q, k, v, li, lf, C, n, m):
    f32 = jnp.float32
    carry = (C.astype(f32), n.astype(f32), m.astype(f32))
    tr = lambda t: t.transpose(0, 2, 1, 3)
    carry, h = mlstm_chunk(carry, (tr(q), tr(k), tr(v), li.transpose(0, 2, 1), lf.transpose(0, 2, 1)))
    return tr(h), carry


def mlstm_out(h, o, lp):
    B, T = h.shape[:2]
    hn = rms_norm(h, lp['mlstm_norm'].reshape(H_M, DH_M)).reshape(B, T, MIX_M)
    return (jax.nn.sigmoid(o.astype(jnp.float32)) * hn).astype(o.dtype)


def indexer_topk(iq, iw, ik, q_pos, topk):
    s = jnp.einsum('bqhd,bld->bqhl', iq.astype(jnp.float32), ik.astype(jnp.float32))
    score = jnp.einsum('bqh,bqhl->bql', iw.astype(jnp.float32), jax.nn.relu(s))
    key_pos = jnp.arange(ik.shape[1])
    score = jnp.where(key_pos[None, None, :] <= q_pos[None, :, None], score, -jnp.inf)
    _, sel = lax.top_k(score, topk)
    return sel


def sparse_attend(q, k_sel, v_sel, sel, q_pos):
    s = jnp.einsum('bqhd,bqkhd->bqhk', q, k_sel).astype(jnp.float32) * (DH_A ** -0.5)
    valid = sel <= q_pos[None, :, None]
    s = jnp.where(valid[:, :, None, :], s, -jnp.inf)
    p = jax.nn.softmax(s, axis=-1).astype(v_sel.dtype)
    return jnp.einsum('bqhk,bqkhd->bqhd', p, v_sel)


def gather_rows(t, idx):
    return jax.vmap(lambda tb, ib: tb[ib])(t, idx)


def dsa_prompt(q, k, v, iq, iw, ik):
    B, S, H, D = q.shape
    topk = min(TOPK_MAX, S // 4)
    nb = S // Q_BLOCK

    def block(i):
        start = i * Q_BLOCK
        qb = lax.dynamic_slice_in_dim(q, start, Q_BLOCK, axis=1)
        iqb = lax.dynamic_slice_in_dim(iq, start, Q_BLOCK, axis=1)
        iwb = lax.dynamic_slice_in_dim(iw, start, Q_BLOCK, axis=1)
        q_pos = start + jnp.arange(Q_BLOCK)
        sel = indexer_topk(iqb, iwb, ik, q_pos, topk)
        return sparse_attend(qb, gather_rows(k, sel), gather_rows(v, sel), sel, q_pos)

    out = lax.map(block, jnp.arange(nb))
    return out.transpose(1, 0, 2, 3, 4).reshape(B, S, H, D)


def dsa_sample(q, k_new, v_new, iq, iw, ik_new, pool_k, pool_v, pool_ik, page_table):
    Bd, T = q.shape[:2]
    past = page_table.shape[1] * PAGE_SIZE
    topk = min(TOPK_MAX, (past + T) // 4)
    ik_past = pool_ik[page_table].reshape(Bd, past, D_IDX)
    ik_all = jnp.concatenate([ik_past, ik_new.astype(ik_past.dtype)], axis=1)
    q_pos = past + jnp.arange(T)
    sel = indexer_topk(iq, iw, ik_all, q_pos, topk)
    in_past = sel < past
    sp = jnp.minimum(sel, past - 1)
    phys_page = jax.vmap(lambda pt, s: pt[s])(page_table, sp // PAGE_SIZE)
    row = phys_page * PAGE_SIZE + sp % PAGE_SIZE
    k_flat = pool_k.reshape(-1, H_A, DH_A)
    v_flat = pool_v.reshape(-1, H_A, DH_A)
    jn = jnp.clip(sel - past, 0, T - 1)
    mask = in_past[..., None, None]
    k_sel = jnp.where(mask, k_flat[row], gather_rows(k_new, jn).astype(k_flat.dtype))
    v_sel = jnp.where(mask, v_flat[row], gather_rows(v_new, jn).astype(v_flat.dtype))
    return sparse_attend(q.astype(k_sel.dtype), k_sel, v_sel, sel, q_pos)


def memory_kv(mem, lp):
    B, M, _ = mem.shape
    hm = rms_norm(mem, lp['norm_mem'])
    k = rms_norm((hm @ lp['w_ck']).reshape(B, M, H_C, DH_C), lp['ck_norm'])
    v = (hm @ lp['w_cv']).reshape(B, M, H_C, DH_C)
    return k, v


def finish_layer(x, y_m, y_a, mem_k, mem_v, conv_hist, lp):
    B, T, _ = x.shape
    x = x + jnp.concatenate([y_m, y_a.astype(y_m.dtype)], axis=-1) @ lp['w_out']
    h = rms_norm(x, lp['norm_cross'])
    q = rms_norm((h @ lp['w_cq']).reshape(B, T, H_C, DH_C), lp['cq_norm'])
    s = jnp.einsum('bthd,bmhd->bhtm', q, mem_k.astype(q.dtype)).astype(jnp.float32) * (DH_C ** -0.5)
    p = jax.nn.softmax(s, axis=-1).astype(x.dtype)
    o = jnp.einsum('bhtm,bmhd->bthd', p, mem_v.astype(x.dtype)).reshape(B, T, D_MODEL)
    x = x + o @ lp['w_co']
    h = rms_norm(x, lp['norm_ffn'])
    u = h @ lp['w_up']
    u_ext = jnp.concatenate([conv_hist.astype(u.dtype), u], axis=1)
    c = lp['conv_b'] + u_ext[:, 0:T] * lp['conv_w'][0]
    for j in range(1, CONV_W):
        c = c + u_ext[:, j:j + T] * lp['conv_w'][j]
    a, g = jnp.split(c, 2, axis=-1)
    x = x + (jax.nn.gelu(g) * a) @ lp['w_down']
    return x, u_ext[:, -(CONV_W - 1):]


def setup_inputs(seed: int = 0) -> dict:
    key = jax.random.key(seed)
    keys = iter(jax.random.split(key, 48))
    f32 = jnp.float32
    n_pages = PAST_LEN // PAGE_SIZE
    n_pool = (DEC_BATCH * n_pages * 5) // 4

    def normal(shape, scale=1.0):
        return jax.random.normal(next(keys), shape, f32) * scale

    def gain(shape):
        return 1.0 + normal(shape, 0.02)

    x_prompt = normal((BATCH, SEQ, D_MODEL))
    x_sample = normal((DEC_BATCH, DEC_SEQ, D_MODEL))
    mem_prompt = normal((BATCH, N_MEM, D_MODEL))
    cache_k = normal((DEPTH, n_pool, PAGE_SIZE, H_A, DH_A))
    cache_v = normal((DEPTH, n_pool, PAGE_SIZE, H_A, DH_A))
    cache_idx_k = normal((DEPTH, n_pool, PAGE_SIZE, D_IDX))
    cache_mem_k = normal((DEPTH, DEC_BATCH, N_MEM, H_C, DH_C))
    cache_mem_v = normal((DEPTH, DEC_BATCH, N_MEM, H_C, DH_C))
    state_mlstm_c = normal((DEPTH, DEC_BATCH, H_M, DH_M, DH_M), DH_M ** -0.5)
    state_mlstm_n = normal((DEPTH, DEC_BATCH, H_M, DH_M), DH_M ** -0.5)
    state_mlstm_m = normal((DEPTH, DEC_BATCH, H_M))
    state_conv = normal((DEPTH, DEC_BATCH, CONV_W - 1, 2 * D_FF))
    perm = jax.random.permutation(next(keys), n_pool)[: DEC_BATCH * n_pages]
    page_table = perm.reshape(DEC_BATCH, n_pages).astype(jnp.int32)
    b_if = jnp.concatenate([normal((DEPTH, H_M), 0.1),
                            jnp.linspace(3.0, 6.0, H_M, dtype=f32)[None, :] + normal((DEPTH, H_M), 0.1)], axis=-1)
    return {
        'x_prompt': x_prompt, 'x_sample': x_sample, 'mem_prompt': mem_prompt,
        'cache_k': cache_k, 'cache_v': cache_v, 'cache_idx_k': cache_idx_k,
        'cache_mem_k': cache_mem_k, 'cache_mem_v': cache_mem_v,
        'state_mlstm_c': state_mlstm_c, 'state_mlstm_n': state_mlstm_n, 'state_mlstm_m': state_mlstm_m,
        'state_conv': state_conv, 'page_table': page_table,
        'norm_mix': gain((DEPTH, D_MODEL)),
        'w_in': normal((DEPTH, D_MODEL, N_IN), D_MODEL ** -0.5),
        'b_if': b_if,
        'mlstm_norm': gain((DEPTH, MIX_M)),
        'q_norm': gain((DEPTH, DH_A)),
        'k_norm': gain((DEPTH, DH_A)),
        'w_out': normal((DEPTH, MIX_WIDTH, D_MODEL), MIX_WIDTH ** -0.5),
        'norm_cross': gain((DEPTH, D_MODEL)),
        'norm_mem': gain((DEPTH, D_MODEL)),
        'w_cq': normal((DEPTH, D_MODEL, D_MODEL), D_MODEL ** -0.5),
        'w_ck': normal((DEPTH, D_MODEL, D_MODEL), D_MODEL ** -0.5),
        'w_cv': normal((DEPTH, D_MODEL, D_MODEL), D_MODEL ** -0.5),
        'w_co': normal((DEPTH, D_MODEL, D_MODEL), D_MODEL ** -0.5),
        'cq_norm': gain((DEPTH, DH_C)),
        'ck_norm': gain((DEPTH, DH_C)),
        'norm_ffn': gain((DEPTH, D_MODEL)),
        'w_up': normal((DEPTH, D_MODEL, 2 * D_FF), D_MODEL ** -0.5),
        'conv_w': normal((DEPTH, CONV_W, 2 * D_FF), 0.5),
        'conv_b': normal((DEPTH, 2 * D_FF), 0.02),
        'w_down': normal((DEPTH, D_FF, D_MODEL), D_FF ** -0.5),
    }


def reference(x_prompt, x_sample, mem_prompt, cache_k, cache_v, cache_idx_k, cache_mem_k, cache_mem_v,
              state_mlstm_c, state_mlstm_n, state_mlstm_m, state_conv, page_table,
              norm_mix, w_in, b_if, mlstm_norm, q_norm, k_norm, w_out, norm_cross, norm_mem,
              w_cq, w_ck, w_cv, w_co, cq_norm, ck_norm, norm_ffn, w_up, conv_w, conv_b, w_down):
    B, S, _ = x_prompt.shape
    Bd, T, _ = x_sample.shape
    past = page_table.shape[1] * PAGE_SIZE
    pos_p = jnp.arange(S)
    pos_s = past + jnp.arange(T)
    names = ('p_k', 'p_v', 'p_ik', 'p_c', 'p_n', 'p_m', 'p_mk', 'p_mv', 'p_conv',
             's_k', 's_v', 's_ik', 's_c', 's_n', 's_m', 's_conv')
    new = {nm: [] for nm in names}
    xp, xs = x_prompt, x_sample
    for l in range(DEPTH):
        lp = {'norm_mix': norm_mix[l], 'w_in': w_in[l], 'b_if': b_if[l], 'mlstm_norm': mlstm_norm[l],
              'q_norm': q_norm[l], 'k_norm': k_norm[l], 'w_out': w_out[l], 'norm_cross': norm_cross[l],
              'norm_mem': norm_mem[l], 'w_cq': w_cq[l], 'w_ck': w_ck[l], 'w_cv': w_cv[l], 'w_co': w_co[l],
              'cq_norm': cq_norm[l], 'ck_norm': ck_norm[l], 'norm_ffn': norm_ffn[l], 'w_up': w_up[l],
              'conv_w': conv_w[l], 'conv_b': conv_b[l], 'w_down': w_down[l]}
        mi = mixer_inputs(rms_norm(xp, lp['norm_mix']), pos_p, lp)
        h_m, (c_p, n_p, m_p) = mlstm_prompt(mi['m_q'], mi['m_k'], mi['m_v'], mi['log_i'], mi['log_f'])
        y_m = mlstm_out(h_m, mi['m_o'], lp)
        y_a = dsa_prompt(mi['a_q'], mi['a_k'], mi['a_v'], mi['i_q'], mi['i_w'], mi['i_k']).reshape(B, S, MIX_A)
        mk_p, mv_p = memory_kv(mem_prompt, lp)
        xp, conv_p = finish_layer(xp, y_m, y_a, mk_p, mv_p,
                                  jnp.zeros((B, CONV_W - 1, 2 * D_FF), xp.dtype), lp)
        new['p_k'].append(mi['a_k'])
        new['p_v'].append(mi['a_v'])
        new['p_ik'].append(mi['i_k'])
        new['p_c'].append(c_p)
        new['p_n'].append(n_p)
        new['p_m'].append(m_p)
        new['p_mk'].append(mk_p)
        new['p_mv'].append(mv_p)
        new['p_conv'].append(conv_p)
        ms = mixer_inputs(rms_norm(xs, lp['norm_mix']), pos_s, lp)
        h_s, (c_s, n_s, m_s) = mlstm_sample(ms['m_q'], ms['m_k'], ms['m_v'], ms['log_i'], ms['log_f'],
                                            state_mlstm_c[l], state_mlstm_n[l], state_mlstm_m[l])
        y_ms = mlstm_out(h_s, ms['m_o'], lp)
        y_as = dsa_sample(ms['a_q'], ms['a_k'], ms['a_v'], ms['i_q'], ms['i_w'], ms['i_k'],
                          cache_k[l], cache_v[l], cache_idx_k[l], page_table).reshape(Bd, T, MIX_A)
        xs, conv_s = finish_layer(xs, y_ms, y_as, cache_mem_k[l], cache_mem_v[l], state_conv[l], lp)
        new['s_k'].append(ms['a_k'])
        new['s_v'].append(ms['a_v'])
        new['s_ik'].append(ms['i_k'])
        new['s_c'].append(c_s)
        new['s_n'].append(n_s)
        new['s_m'].append(m_s)
        new['s_conv'].append(conv_s)
    st = {nm: jnp.stack(v) for nm, v in new.items()}
    return (xp, xs, st['p_k'], st['p_v'], st['p_ik'], st['p_c'], st['p_n'], st['p_m'], st['p_mk'], st['p_mv'],
            st['p_conv'], st['s_k'], st['s_v'], st['s_ik'], st['s_c'], st['s_n'], st['s_m'], st['s_conv'])
```

```python
import functools

import jax
import jax.numpy as jnp
import numpy as np
from jax import lax
from jax.experimental import pallas as pl
from jax.experimental.pallas import tpu as pltpu

F32 = jnp.float32
BF16 = jnp.bfloat16

H_M = 4
H_A = 4
H_IDX = 4
D_IDX = 64
H_C = 4
TOPK_MAX = 256
CONV_W = 3
ROPE_THETA = 10000.0
EPS = 1e-6
NEG_INF = float("-inf")
POS_INF = float("inf")

LANES = 128
VMEM_LIMIT = 56 * 1024 * 1024
N_BISECT = 24


def _cparams(sem):
    return pltpu.CompilerParams(dimension_semantics=sem, vmem_limit_bytes=VMEM_LIMIT)


def _nt(a, b):
    return lax.dot_general(a, b, (((1,), (1,)), ((), ())), preferred_element_type=F32)


def _tn(a, b):
    return lax.dot_general(a, b, (((0,), (0,)), ((), ())), preferred_element_type=F32)


def _mm(a, b):
    return jnp.dot(a, b, preferred_element_type=F32)


def _rms(x, g):
    ms = jnp.mean(x * x, axis=-1, keepdims=True)
    return x * lax.rsqrt(ms + EPS) * g


def _sigmoid(x):
    return 1.0 / (1.0 + jnp.exp(-x))


def _in_proj_kernel(x_ref, nm_ref, w_ref, bias_ref, qn_ref, kn_ref, tab_ref,
                    mq_ref, mk_ref, mv_ref, mo_ref, aqb_ref, ak_ref, av_ref, akb_ref, avb_ref,
                    iqb_ref, misc_ref, *, mix_m, mix_a, dh_m, dh_a):
    h = _rms(x_ref[...], nm_ref[...]).astype(BF16)

    def proj(lo, width):
        return _mm(h, w_ref[:, lo:lo + width])

    o_mq, o_mk, o_mv, o_mo = 0, mix_m, 2 * mix_m, 3 * mix_m
    o_aq = 4 * mix_m
    o_ak = o_aq + mix_a
    o_av = o_ak + mix_a
    o_iq = o_av + mix_a
    o_tail = o_iq + H_IDX * D_IDX

    mq_ref[...] = proj(o_mq, mix_m)
    mk_ref[...] = proj(o_mk, mix_m) * (dh_m ** -0.5)
    mv_ref[...] = proj(o_mv, mix_m)
    mo_ref[...] = proj(o_mo, mix_m)

    cos_a = tab_ref[:, 0:LANES]
    sin_a = tab_ref[:, LANES:2 * LANES]

    def norm_rope(z, g_ref):
        outs = []
        for hh in range(mix_a // dh_a):
            zh = _rms(z[:, hh * dh_a:(hh + 1) * dh_a], g_ref[...])
            outs.append(zh * cos_a + pltpu.roll(zh, dh_a // 2, 1) * sin_a)
        return jnp.concatenate(outs, axis=1)

    aq = norm_rope(proj(o_aq, mix_a), qn_ref)
    aqb_ref[...] = aq.astype(BF16)
    ak = norm_rope(proj(o_ak, mix_a), kn_ref)
    ak_ref[...] = ak
    akb_ref[...] = ak.astype(BF16)
    av = proj(o_av, mix_a)
    av_ref[...] = av
    avb_ref[...] = av.astype(BF16)

    c_i = tab_ref[:, 2 * LANES:3 * LANES]
    s1_i = tab_ref[:, 3 * LANES:4 * LANES]
    s2_i = tab_ref[:, 4 * LANES:5 * LANES]
    ziq = proj(o_iq, H_IDX * D_IDX)
    cols = []
    for c in range(H_IDX * D_IDX // LANES):
        zc = ziq[:, c * LANES:(c + 1) * LANES]
        cols.append(zc * c_i + pltpu.roll(zc, D_IDX // 2, 1) * s1_i
                    + pltpu.roll(zc, LANES - D_IDX // 2, 1) * s2_i)
    iqb_ref[...] = jnp.concatenate(cols, axis=1).astype(BF16)

    c_t = tab_ref[:, 5 * LANES:6 * LANES]
    s1_t = tab_ref[:, 6 * LANES:7 * LANES]
    s2_t = tab_ref[:, 7 * LANES:8 * LANES]
    zt = proj(o_tail, LANES) + bias_ref[...]
    zt = zt * c_t + pltpu.roll(zt, D_IDX // 2, 1) * s1_t + pltpu.roll(zt, LANES - D_IDX // 2, 1) * s2_t
    lane = lax.broadcasted_iota(jnp.int32, zt.shape, 1)
    f_lo = D_IDX + H_IDX + H_M
    log_sig = jnp.minimum(zt, 0.0) - jnp.log(1.0 + jnp.exp(-jnp.abs(zt)))
    misc_ref[...] = jnp.where((lane >= f_lo) & (lane < f_lo + H_M), log_sig, zt)


def _in_proj(x2d, norm_mix, w_r, bias_tail, q_norm, k_norm, tab, *, tm, tab_tiles, mix_m, mix_a):
    rows, d = x2d.shape
    dh_m = mix_m // H_M
    dh_a = mix_a // H_A
    nw = w_r.shape[1]
    grid = (rows // tm,)
    row_spec = lambda wdt: pl.BlockSpec((tm, wdt), lambda i: (i, 0))
    const = lambda shp: pl.BlockSpec(shp, lambda i: (0, 0))
    out_shapes = [
        jax.ShapeDtypeStruct((rows, mix_m), F32),
        jax.ShapeDtypeStruct((rows, mix_m), F32),
        jax.ShapeDtypeStruct((rows, mix_m), F32),
        jax.ShapeDtypeStruct((rows, mix_m), F32),
        jax.ShapeDtypeStruct((rows, mix_a), BF16),
        jax.ShapeDtypeStruct((rows, mix_a), F32),
        jax.ShapeDtypeStruct((rows, mix_a), F32),
        jax.ShapeDtypeStruct((rows, mix_a), BF16),
        jax.ShapeDtypeStruct((rows, mix_a), BF16),
        jax.ShapeDtypeStruct((rows, H_IDX * D_IDX), BF16),
        jax.ShapeDtypeStruct((rows, LANES), F32),
    ]
    out_specs = [row_spec(mix_m)] * 4 + [row_spec(mix_a)] * 5 + [row_spec(H_IDX * D_IDX), row_spec(LANES)]
    return pl.pallas_call(
        functools.partial(_in_proj_kernel, mix_m=mix_m, mix_a=mix_a, dh_m=dh_m, dh_a=dh_a),
        grid=grid,
        in_specs=[row_spec(d), const((1, d)), const((d, nw)), const((1, LANES)),
                  const((1, dh_a)), const((1, dh_a)),
                  pl.BlockSpec((tm, 8 * LANES), lambda i: (i % tab_tiles, 0))],
        out_specs=out_specs,
        out_shape=out_shapes,
        compiler_params=_cparams(("parallel",)),
        name="in_proj",
    )(x2d, norm_mix, w_r, bias_tail, q_norm, k_norm, tab)


def _mlstm_prompt_kernel(q_ref, k_ref, v_ref, o_ref, grow_ref, gcol_ref, gain_ref,
                         y_ref, c_ref, n_ref, m_ref, *, chunk, n_chunks):
    L = chunk
    row_i = lax.broadcasted_iota(jnp.int32, (L, L), 0)
    col_i = lax.broadcasted_iota(jnp.int32, (L, L), 1)
    tril = col_i <= row_i
    triu = row_i <= col_i
    gain = gain_ref[...]

    def body(c, carry):
        C, n, m = carry
        t0 = pl.multiple_of(c * L, L)
        q = q_ref[0, pl.ds(t0, L), :]
        k = k_ref[0, pl.ds(t0, L), :]
        v = v_ref[0, pl.ds(t0, L), :]
        o = o_ref[0, pl.ds(t0, L), :]
        li_r = grow_ref[0, 0, pl.ds(c, 1), :]
        lf_r = grow_ref[0, 1, pl.ds(c, 1), :]
        gc = gcol_ref[0, pl.ds(t0, L), :]
        li_c = gc[:, 0:1]
        lf_c = gc[:, 1:2]
        b_c = jnp.sum(jnp.where(tril, lf_r, 0.0), axis=1, keepdims=True)
        b_r = jnp.sum(jnp.where(triu, lf_c, 0.0), axis=0, keepdims=True)
        logd = jnp.where(tril, b_c - b_r + li_r, NEG_INF)
        inter = b_c + m
        m_t = jnp.maximum(inter, jnp.max(logd, axis=1, keepdims=True))
        qb = q.astype(BF16)
        kb = k.astype(BF16)
        s = _nt(qb, kb) * jnp.exp(logd - m_t)
        g_inter = jnp.exp(inter - m_t)
        num = g_inter * _nt(qb, C.astype(BF16)) + _mm(s.astype(BF16), v.astype(BF16))
        den = g_inter * jnp.sum(q * n, axis=1, keepdims=True) + jnp.sum(s, axis=1, keepdims=True)
        h = num / jnp.maximum(jnp.abs(den), jnp.exp(-m_t))
        m_new = m_t[L - 1:L, :]
        b_last = b_c[L - 1:L, :]
        g_prev = jnp.exp(b_last + m - m_new)
        w_c = jnp.exp(b_last - b_c + li_c - m_new)
        C_new = g_prev * C + _tn((v * w_c).astype(BF16), kb)
        n_new = g_prev * n + jnp.sum(k * w_c, axis=0, keepdims=True)
        y_ref[0, pl.ds(t0, L), :] = _sigmoid(o) * _rms(h, gain)
        return C_new, n_new, m_new

    d = q_ref.shape[2]
    carry0 = (jnp.zeros((d, d), F32), jnp.zeros((1, d), F32), jnp.zeros((1, 1), F32))
    C, n, m = lax.fori_loop(0, n_chunks, body, carry0)
    c_ref[0, 0] = C
    n_ref[0] = n
    m_ref[0] = jnp.broadcast_to(m, (1, LANES))


def _mlstm_prompt(mq, mk, mv, mo, grow, gcol, gain, *, chunk):
    B, S, mix_m = mq.shape
    d = mix_m // H_M
    n_chunks = S // chunk
    seq = pl.BlockSpec((1, S, d), lambda b, h: (b, 0, h))
    return pl.pallas_call(
        functools.partial(_mlstm_prompt_kernel, chunk=chunk, n_chunks=n_chunks),
        grid=(B, H_M),
        in_specs=[seq, seq, seq, seq,
                  pl.BlockSpec((1, 2, n_chunks, chunk), lambda b, h: (b * H_M + h, 0, 0, 0)),
                  pl.BlockSpec((1, S, 2), lambda b, h: (b * H_M + h, 0, 0)),
                  pl.BlockSpec((1, d), lambda b, h: (0, h))],
        out_specs=[seq,
                   pl.BlockSpec((1, 1, d, d), lambda b, h: (b, h, 0, 0)),
                   pl.BlockSpec((1, 1, d), lambda b, h: (b * H_M + h, 0, 0)),
                   pl.BlockSpec((1, 1, LANES), lambda b, h: (b * H_M + h, 0, 0))],
        out_shape=[jax.ShapeDtypeStruct((B, S, mix_m), F32),
                   jax.ShapeDtypeStruct((B, H_M, d, d), F32),
                   jax.ShapeDtypeStruct((B * H_M, 1, d), F32),
                   jax.ShapeDtypeStruct((B * H_M, 1, LANES), F32)],
        compiler_params=_cparams(("parallel", "parallel")),
        name="mlstm_prompt",
    )(mq, mk, mv, mo, grow, gcol, gain)


def _mlstm_sample_kernel(q_ref, k_ref, v_ref, o_ref, gs_ref, c_ref, n_ref, gain_ref,
                         y_ref, co_ref, no_ref, mo_ref, *, d):
    gs = gs_ref[0]
    eye = (lax.broadcasted_iota(jnp.int32, (d, d), 0) == lax.broadcasted_iota(jnp.int32, (d, d), 1))
    lane = lax.broadcasted_iota(jnp.int32, (1, LANES), 1)
    m_out = jnp.zeros((1, LANES), F32)
    for h in range(H_M):
        sl = slice(h * d, (h + 1) * d)
        q = q_ref[0, :, sl]
        k = k_ref[0, :, sl]
        v = v_ref[0, :, sl]
        o = o_ref[0, :, sl]
        li = gs[:, h:h + 1]
        lf = gs[:, H_M + h:H_M + h + 1]
        m = gs[:, 2 * H_M + h:2 * H_M + h + 1]
        C = c_ref[0, h]
        n = n_ref[0, :, sl]
        inter = lf + m
        m_t = jnp.maximum(inter, li)
        s = jnp.sum(q * k, axis=1, keepdims=True) * jnp.exp(li - m_t)
        g = jnp.exp(inter - m_t)
        q8 = jnp.broadcast_to(q, (8, d)).astype(BF16)
        cq = _nt(q8, C.astype(BF16))[0:1, :]
        num = g * cq + s * v
        den = g * jnp.sum(n * q, axis=1, keepdims=True) + s
        hh = num / jnp.maximum(jnp.abs(den), jnp.exp(-m_t))
        w = jnp.exp(li - m_t)
        v_col = jnp.sum(jnp.where(eye, v, 0.0), axis=1, keepdims=True)
        co_ref[0, h] = g * C + (w * v_col) * k
        no_ref[0, :, sl] = g * n + w * k
        m_out = jnp.where(lane == h, m_t, m_out)
        y_ref[0, :, sl] = _sigmoid(o) * _rms(hh, gain_ref[:, sl])
    mo_ref[0] = m_out


def _mlstm_sample(mq, mk, mv, mo, gs, c_state, n_state, gain):
    Bd, _, mix_m = mq.shape
    d = mix_m // H_M
    row = pl.BlockSpec((1, 1, mix_m), lambda b: (b, 0, 0))
    return pl.pallas_call(
        functools.partial(_mlstm_sample_kernel, d=d),
        grid=(Bd,),
        in_specs=[row, row, row, row,
                  pl.BlockSpec((1, 1, 3 * H_M), lambda b: (b, 0, 0)),
                  pl.BlockSpec((1, H_M, d, d), lambda b: (b, 0, 0, 0)),
                  row,
                  pl.BlockSpec((1, mix_m), lambda b: (0, 0))],
        out_specs=[row,
                   pl.BlockSpec((1, H_M, d, d), lambda b: (b, 0, 0, 0)),
                   row,
                   pl.BlockSpec((1, 1, LANES), lambda b: (b, 0, 0))],
        out_shape=[jax.ShapeDtypeStruct((Bd, 1, mix_m), F32),
                   jax.ShapeDtypeStruct((Bd, H_M, d, d), F32),
                   jax.ShapeDtypeStruct((Bd, 1, mix_m), F32),
                   jax.ShapeDtypeStruct((Bd, 1, LANES), F32)],
        compiler_params=_cparams(("parallel",)),
        name="mlstm_sample",
    )(mq, mk, mv, mo, gs, c_state, n_state, gain)


def _dsa_prompt_kernel(iq_ref, misc_ref, ikt_ref, aq_ref, ak_ref, av_ref, ya_ref, sc_ref,
                       *, tq, w, topk, dh, scale):
    i = pl.program_id(1)
    nk = ((i + 1) * tq + w - 1) // w
    kf = float(topk)
    nsub = w // LANES

    q_pos = i * tq + lax.broadcasted_iota(jnp.int32, (tq, 1), 0)
    lane_w = lax.broadcasted_iota(jnp.int32, (1, w), 1)
    iq = iq_ref[0]
    iq_h = [iq[:, h * D_IDX:(h + 1) * D_IDX] for h in range(H_IDX)]
    misc = misc_ref[0]
    w_h = [misc[:, D_IDX + h:D_IDX + h + 1] for h in range(H_IDX)]

    def score_body(c, carry):
        rmax, rmin = carry
        ikc = ikt_ref[0, c]
        score = jnp.zeros((tq, w), F32)
        for h in range(H_IDX):
            score = score + w_h[h] * jnp.maximum(_mm(iq_h[h], ikc), 0.0)
        valid = (c * w + lane_w) <= q_pos
        sc_ref[c] = jnp.where(valid, score, NEG_INF)
        rmax = jnp.maximum(rmax, jnp.max(jnp.where(valid, score, NEG_INF), axis=1, keepdims=True))
        rmin = jnp.minimum(rmin, jnp.min(jnp.where(valid, score, POS_INF), axis=1, keepdims=True))
        return rmax, rmin

    rmax, rmin = lax.fori_loop(0, nk, score_body,
                               (jnp.full((tq, 1), NEG_INF, F32), jnp.full((tq, 1), POS_INF, F32)))

    def row_pass(fn, init, combine, finish):
        def body(c, acc):
            x = sc_ref[c]
            for j in range(nsub):
                acc = combine(acc, fn(x[:, j * LANES:(j + 1) * LANES]))
            return acc
        return finish(lax.fori_loop(0, nk, body, jnp.full((tq, LANES), init, F32)))

    def count(pred, thr):
        thr_b = jnp.broadcast_to(thr, (tq, LANES))
        return row_pass(lambda x: jnp.where(pred(x, thr_b), 1.0, 0.0), 0.0, jnp.add,
                        lambda a: jnp.sum(a, axis=1, keepdims=True))

    def min_where(pred, thr):
        thr_b = jnp.broadcast_to(thr, (tq, LANES))
        return row_pass(lambda x: jnp.where(pred(x, thr_b), x, POS_INF), POS_INF, jnp.minimum,
                        lambda a: jnp.min(a, axis=1, keepdims=True))

    ge = lambda x, t: x >= t
    gt = lambda x, t: x > t
    eq = lambda x, t: x == t

    active = (q_pos + 1) > topk
    hi0 = rmax + jnp.abs(rmax) + 1.0

    def bis_body(_, carry):
        lo, hi = carry
        mid = 0.5 * (lo + hi)
        ok = count(ge, mid) >= kf
        return jnp.where(ok, mid, lo), jnp.where(ok, hi, mid)

    lo, _ = lax.fori_loop(0, N_BISECT, bis_body, (rmin, hi0))
    tau = min_where(ge, lo)
    g = count(gt, tau)

    def undone(tau, g):
        return active & (g >= kf)

    def fix_cond(st):
        tau, g = st
        return jnp.max(jnp.where(undone(tau, g), 1.0, 0.0)) > 0.5

    def fix_body(st):
        tau, g = st
        nd = undone(tau, g)
        tau2 = jnp.where(nd, min_where(gt, tau), tau)
        return tau2, jnp.where(nd, count(gt, tau2), g)

    tau, g = lax.while_loop(fix_cond, fix_body, (tau, g))
    tau = jnp.where(active, tau, rmin)
    n_eq = count(eq, tau)
    need = jnp.where(active, kf - g, 1e9)
    any_excess = jnp.max(jnp.where(n_eq > need, 1.0, 0.0)) > 0.5

    @pl.when(jnp.logical_not(any_excess))
    def _():
        def body(c, _):
            x = sc_ref[c]
            sc_ref[c] = jnp.where(x >= tau, 0.0, NEG_INF)
            return 0
        lax.fori_loop(0, nk, body, 0)

    @pl.when(any_excess)
    def _():
        tri = (lax.broadcasted_iota(jnp.int32, (w, w), 0)
               <= lax.broadcasted_iota(jnp.int32, (w, w), 1)).astype(BF16)

        def body(c, run):
            x = sc_ref[c]
            is_eq = x == tau
            e = jnp.where(is_eq, 1.0, 0.0)
            pref = _mm(e.astype(BF16), tri) + run
            sel = (x > tau) | (is_eq & (pref <= need))
            sc_ref[c] = jnp.where(sel, 0.0, NEG_INF)
            return run + jnp.sum(e, axis=1, keepdims=True)
        lax.fori_loop(0, nk, body, jnp.zeros((tq, 1), F32))

    aq = aq_ref[0]
    for h in range(H_A):
        hs = slice(h * dh, (h + 1) * dh)
        qh = aq[:, hs]

        def att_body(c, carry, hs=hs, qh=qh):
            m, l, acc = carry
            k0 = pl.multiple_of(c * w, w)
            kc = ak_ref[0, pl.ds(k0, w), hs]
            vc = av_ref[0, pl.ds(k0, w), hs]
            s = _nt(qh, kc) * scale + sc_ref[c]
            m_new = jnp.maximum(m, jnp.max(s, axis=1, keepdims=True))
            m_safe = jnp.where(m_new == NEG_INF, 0.0, m_new)
            alpha = jnp.exp(m - m_safe)
            p = jnp.exp(s - m_safe)
            l = alpha * l + jnp.sum(p, axis=1, keepdims=True)
            acc = alpha * acc + _mm(p.astype(BF16), vc)
            return m_new, l, acc

        m, l, acc = lax.fori_loop(0, nk, att_body,
                                  (jnp.full((tq, 1), NEG_INF, F32), jnp.zeros((tq, 1), F32),
                                   jnp.zeros((tq, dh), F32)))
        ya_ref[0, :, hs] = acc / l


def _dsa_prompt(iqb, misc, ikt, aqb, akb, avb, *, tq, w, topk):
    B, S, mix_a = aqb.shape
    dh = mix_a // H_A
    nq = S // tq
    nw = S // w
    return pl.pallas_call(
        functools.partial(_dsa_prompt_kernel, tq=tq, w=w, topk=topk, dh=dh, scale=dh ** -0.5),
        grid=(B, nq),
        in_specs=[pl.BlockSpec((1, tq, H_IDX * D_IDX), lambda b, i: (b, i, 0)),
                  pl.BlockSpec((1, tq, LANES), lambda b, i: (b, i, 0)),
                  pl.BlockSpec((1, nw, D_IDX, w), lambda b, i: (b, 0, 0, 0)),
                  pl.BlockSpec((1, tq, mix_a), lambda b, i: (b, i, 0)),
                  pl.BlockSpec((1, S, mix_a), lambda b, i: (b, 0, 0)),
                  pl.BlockSpec((1, S, mix_a), lambda b, i: (b, 0, 0))],
        out_specs=pl.BlockSpec((1, tq, mix_a), lambda b, i: (b, i, 0)),
        out_shape=jax.ShapeDtypeStruct((B, S, mix_a), F32),
        scratch_shapes=[pltpu.VMEM((nw, tq, w), F32)],
        compiler_params=_cparams(("parallel", "arbitrary")),
        name="dsa_prompt",
    )(iqb, misc, ikt, aqb, akb, avb)


def _dsa_sample_select_kernel(pt_ref, iq_ref, w_ref, ikn_ref, pool_ref, idx_ref,
                              ikbuf, sem, tri_ref, *, n_pages, page, topk, cw):
    b = pl.program_id(0)
    nb = pl.num_programs(0)
    past = n_pages * page
    kf = float(topk)
    n_cw = past // cw

    def page_copy(bb, p, slot):
        return pltpu.make_async_copy(pool_ref.at[pt_ref[bb, p]],
                                     ikbuf.at[slot, pl.ds(p * page, page), :],
                                     sem.at[slot])

    def start_all(bb, slot):
        def body(p, _):
            page_copy(bb, p, slot).start()
            return 0
        lax.fori_loop(0, n_pages, body, 0)

    slot = b % 2

    @pl.when(b == 0)
    def _():
        start_all(0, 0)
        tri_ref[...] = (lax.broadcasted_iota(jnp.int32, (cw, cw), 0)
                        < lax.broadcasted_iota(jnp.int32, (cw, cw), 1)).astype(BF16)

    @pl.when(b + 1 < nb)
    def _():
        start_all(b + 1, 1 - slot)

    def wait_body(p, _):
        page_copy(b, p, slot).wait()
        return 0
    lax.fori_loop(0, n_pages, wait_body, 0)

    iq8 = iq_ref[0]
    w8 = w_ref[0]
    ikp = ikbuf[slot].astype(BF16)
    s8 = _nt(iq8, ikp)
    x = jnp.sum(w8 * jnp.maximum(s8, 0.0), axis=0, keepdims=True)
    ikn = ikn_ref[0].astype(BF16).astype(F32)
    sn8 = jnp.sum(iq8.astype(F32) * ikn, axis=1, keepdims=True)
    xn = jnp.sum(w8 * jnp.maximum(sn8, 0.0), axis=0, keepdims=True)

    def cnt(mask_row, mask_new):
        return (jnp.sum(jnp.where(mask_row, 1.0, 0.0), axis=1, keepdims=True)
                + jnp.where(mask_new, 1.0, 0.0))

    rmax = jnp.maximum(jnp.max(x, axis=1, keepdims=True), xn)
    rmin = jnp.minimum(jnp.min(x, axis=1, keepdims=True), xn)
    hi0 = rmax + jnp.abs(rmax) + 1.0

    def bis_body(_, carry):
        lo, hi = carry
        mid = 0.5 * (lo + hi)
        ok = cnt(x >= mid, xn >= mid) >= kf
        return jnp.where(ok, mid, lo), jnp.where(ok, hi, mid)

    lo, _ = lax.fori_loop(0, N_BISECT, bis_body, (rmin, hi0))

    def min_where(mask_row, mask_new):
        return jnp.minimum(jnp.min(jnp.where(mask_row, x, POS_INF), axis=1, keepdims=True),
                           jnp.where(mask_new, xn, POS_INF))

    tau = min_where(x >= lo, xn >= lo)
    g = cnt(x > tau, xn > tau)

    def fix_cond(st):
        tau, g = st
        return jnp.max(jnp.where(g >= kf, 1.0, 0.0)) > 0.5

    def fix_body(st):
        tau, g = st
        tau2 = min_where(x > tau, xn > tau)
        return tau2, cnt(x > tau2, xn > tau2)

    tau, g = lax.while_loop(fix_cond, fix_body, (tau, g))
    need = kf - g

    def excl_prefix(flag_row):
        outs = []
        run = jnp.zeros((1, 1), F32)
        for c in range(n_cw):
            f = flag_row[:, c * cw:(c + 1) * cw]
            f8 = jnp.broadcast_to(f, (8, cw)).astype(BF16)
            outs.append(_mm(f8, tri_ref[...])[0:1, :] + run)
            run = run + jnp.sum(f, axis=1, keepdims=True)
        return jnp.concatenate(outs, axis=1), run

    is_eq = x == tau
    e = jnp.where(is_eq, 1.0, 0.0)
    pre_eq, n_eq_past = excl_prefix(e)
    sel = (x > tau) | (is_eq & (pre_eq < need))
    new_sel = (xn > tau) | ((xn == tau) & (n_eq_past < need))
    sel_f = jnp.where(sel, 1.0, 0.0)
    slot_row, n_sel_past = excl_prefix(sel_f)
    slot_row = jnp.where(sel, slot_row, -1.0)

    slot_col = lax.broadcasted_iota(jnp.int32, (topk, 1), 0).astype(F32)
    acc = jnp.zeros((topk, LANES), F32)
    for c in range(past // LANES):
        sr = slot_row[:, c * LANES:(c + 1) * LANES]
        jr = (lax.broadcasted_iota(jnp.int32, (1, LANES), 1) + c * LANES).astype(F32)
        acc = acc + jnp.where(sr == slot_col, jr, 0.0)
    idx = jnp.sum(acc, axis=1, keepdims=True)
    idx = idx + jnp.where(new_sel & (slot_col == n_sel_past), float(past), 0.0)
    idx_ref[0] = idx.astype(jnp.int32)


def _dsa_sample_select(page_table, iq8, w8, ik_new, pool_ik, *, topk, cw):
    Bd, n_pages = page_table.shape
    n_pool, page, d_idx = pool_ik.shape
    past = n_pages * page
    grid_spec = pltpu.PrefetchScalarGridSpec(
        num_scalar_prefetch=1,
        grid=(Bd,),
        in_specs=[pl.BlockSpec((1, 8, d_idx), lambda b, pt: (b, 0, 0)),
                  pl.BlockSpec((1, 8, 1), lambda b, pt: (b, 0, 0)),
                  pl.BlockSpec((1, 1, d_idx), lambda b, pt: (b, 0, 0)),
                  pl.BlockSpec(memory_space=pl.ANY)],
        out_specs=pl.BlockSpec((1, topk, 1), lambda b, pt: (b, 0, 0)),
        scratch_shapes=[pltpu.VMEM((2, past, d_idx), F32),
                        pltpu.SemaphoreType.DMA((2,)),
                        pltpu.VMEM((cw, cw), BF16)],
    )
    return pl.pallas_call(
        functools.partial(_dsa_sample_select_kernel, n_pages=n_pages, page=page, topk=topk, cw=cw),
        grid_spec=grid_spec,
        out_shape=jax.ShapeDtypeStruct((Bd, topk, 1), jnp.int32),
        compiler_params=_cparams(("arbitrary",)),
        name="dsa_sample_select",
    )(page_table, iq8, w8, ik_new, pool_ik)


def _dsa_sample_attend_kernel(idx_ref, pt_ref, aq_ref, knew_ref, vnew_ref, kpool_ref, vpool_ref,
                              ya_ref, kbuf, vbuf, sem, *, page, n_pages, topk, dh, scale):
    b = pl.program_id(0)
    nb = pl.num_programs(0)
    past = n_pages * page

    def copies(bb, t, slot):
        j = idx_ref[bb, t]
        is_new = j >= past
        jj = jnp.minimum(j, past - 1)
        row = pt_ref[bb, jj // page] * page + jj % page
        dst_k = kbuf.at[slot, pl.ds(t, 1), :]
        dst_v = vbuf.at[slot, pl.ds(t, 1), :]
        old = (pltpu.make_async_copy(kpool_ref.at[pl.ds(row, 1), :], dst_k, sem.at[0, slot]),
               pltpu.make_async_copy(vpool_ref.at[pl.ds(row, 1), :], dst_v, sem.at[1, slot]))
        new = (pltpu.make_async_copy(knew_ref.at[pl.ds(bb, 1), :], dst_k, sem.at[0, slot]),
               pltpu.make_async_copy(vnew_ref.at[pl.ds(bb, 1), :], dst_v, sem.at[1, slot]))
        return is_new, old, new

    def start_all(bb, slot):
        def body(t, _):
            is_new, old, new = copies(bb, t, slot)

            @pl.when(is_new)
            def _():
                new[0].start()
                new[1].start()

            @pl.when(jnp.logical_not(is_new))
            def _():
                old[0].start()
                old[1].start()
            return 0
        lax.fori_loop(0, topk, body, 0)

    slot = b % 2

    @pl.when(b == 0)
    def _():
        start_all(0, 0)

    @pl.when(b + 1 < nb)
    def _():
        start_all(b + 1, 1 - slot)

    def wait_body(t, _):
        pltpu.make_async_copy(knew_ref.at[pl.ds(0, 1), :], kbuf.at[slot, pl.ds(t, 1), :], sem.at[0, slot]).wait()
        pltpu.make_async_copy(vnew_ref.at[pl.ds(0, 1), :], vbuf.at[slot, pl.ds(t, 1), :], sem.at[1, slot]).wait()
        return 0
    lax.fori_loop(0, topk, wait_body, 0)

    kb = kbuf[slot].astype(BF16)
    vb = vbuf[slot].astype(BF16)
    aq = aq_ref[0]
    for h in range(H_A):
        hs = slice(h * dh, (h + 1) * dh)
        q8 = jnp.broadcast_to(aq[:, hs], (8, dh))
        s = _nt(q8, kb[:, hs]) * scale
        m = jnp.max(s, axis=1, keepdims=True)
        p = jnp.exp(s - m)
        p = p / jnp.sum(p, axis=1, keepdims=True)
        ya_ref[0, :, hs] = _mm(p.astype(BF16), vb[:, hs])[0:1, :]


def _dsa_sample_attend(idx, page_table, aqb, k_new, v_new, pool_k, pool_v, *, page, topk):
    Bd, n_pages = page_table.shape
    mix_a = aqb.shape[2]
    dh = mix_a // H_A
    grid_spec = pltpu.PrefetchScalarGridSpec(
        num_scalar_prefetch=2,
        grid=(Bd,),
        in_specs=[pl.BlockSpec((1, 1, mix_a), lambda b, ix, pt: (b, 0, 0)),
                  pl.BlockSpec(memory_space=pl.ANY),
                  pl.BlockSpec(memory_space=pl.ANY),
                  pl.BlockSpec(memory_space=pl.ANY),
                  pl.BlockSpec(memory_space=pl.ANY)],
        out_specs=pl.BlockSpec((1, 1, mix_a), lambda b, ix, pt: (b, 0, 0)),
        scratch_shapes=[pltpu.VMEM((2, topk, mix_a), F32),
                        pltpu.VMEM((2, topk, mix_a), F32),
                        pltpu.SemaphoreType.DMA((2, 2))],
    )
    return pl.pallas_call(
        functools.partial(_dsa_sample_attend_kernel, page=page, n_pages=n_pages, topk=topk,
                          dh=dh, scale=dh ** -0.5),
        grid_spec=grid_spec,
        out_shape=jax.ShapeDtypeStruct((Bd, 1, mix_a), F32),
        compiler_params=_cparams(("arbitrary",)),
        name="dsa_sample_attend",
    )(idx, page_table, aqb, k_new, v_new, pool_k, pool_v)


def _mem_kv_kernel(mem_ref, nm_ref, wk_ref, wv_ref, kn_ref, k_ref, v_ref, *, dh):
    hm = _rms(mem_ref[...], nm_ref[...]).astype(BF16)
    kk = _mm(hm, wk_ref[...])
    for h in range(H_C):
        hs = slice(h * dh, (h + 1) * dh)
        k_ref[:, hs] = _rms(kk[:, hs], kn_ref[...])
    v_ref[...] = _mm(hm, wv_ref[...])


def _mem_kv(mem2d, norm_mem, w_ck, w_cv, ck_norm, *, tm):
    rows, d = mem2d.shape
    dh = d // H_C
    row = pl.BlockSpec((tm, d), lambda i: (i, 0))
    const = lambda shp: pl.BlockSpec(shp, lambda i: (0, 0))
    return pl.pallas_call(
        functools.partial(_mem_kv_kernel, dh=dh),
        grid=(rows // tm,),
        in_specs=[row, const((1, d)), const((d, d)), const((d, d)), const((1, dh))],
        out_specs=[row, row],
        out_shape=[jax.ShapeDtypeStruct((rows, d), F32)] * 2,
        compiler_params=_cparams(("parallel",)),
        name="mem_kv",
    )(mem2d, norm_mem, w_ck, w_cv, ck_norm)


def _out_cq_kernel(x_ref, ym_ref, ya_ref, wo_ref, nc_ref, wq_ref, qn_ref, x1_ref, qc_ref, *, mix_m, dh):
    upd = (_mm(ym_ref[...].astype(BF16), wo_ref[0:mix_m, :])
           + _mm(ya_ref[...].astype(BF16), wo_ref[mix_m:, :]))
    x1 = x_ref[...] + upd
    x1_ref[...] = x1
    hq = _mm(_rms(x1, nc_ref[...]).astype(BF16), wq_ref[...])
    for h in range(H_C):
        hs = slice(h * dh, (h + 1) * dh)
        qc_ref[:, hs] = _rms(hq[:, hs], qn_ref[...]).astype(BF16)


def _out_cq(x2d, ym, ya, w_out, norm_cross, w_cq, cq_norm, *, tm):
    rows, d = x2d.shape
    mix_m = ym.shape[1]
    mix_a = ya.shape[1]
    dh = d // H_C
    row = lambda wdt: pl.BlockSpec((tm, wdt), lambda i: (i, 0))
    const = lambda shp: pl.BlockSpec(shp, lambda i: (0, 0))
    return pl.pallas_call(
        functools.partial(_out_cq_kernel, mix_m=mix_m, dh=dh),
        grid=(rows // tm,),
        in_specs=[row(d), row(mix_m), row(mix_a), const((mix_m + mix_a, d)), const((1, d)),
                  const((d, d)), const((1, dh))],
        out_specs=[row(d), row(d)],
        out_shape=[jax.ShapeDtypeStruct((rows, d), F32), jax.ShapeDtypeStruct((rows, d), BF16)],
        compiler_params=_cparams(("parallel",)),
        name="out_cq",
    )(x2d, ym, ya, w_out, norm_cross, w_cq, cq_norm)


def _cross_kernel(q_ref, k_ref, v_ref, o_ref, *, dh, scale):
    q = q_ref[0]
    rows = q.shape[0]
    if rows < 8:
        q = jnp.broadcast_to(q, (8, q.shape[1]))
    kb = k_ref[0].astype(BF16)
    vb = v_ref[0].astype(BF16)
    for h in range(H_C):
        hs = slice(h * dh, (h + 1) * dh)
        s = _nt(q[:, hs], kb[:, hs]) * scale
        m = jnp.max(s, axis=1, keepdims=True)
        p = jnp.exp(s - m)
        p = p / jnp.sum(p, axis=1, keepdims=True)
        o = _mm(p.astype(BF16), vb[:, hs])
        o_ref[0, :, hs] = o[0:rows].astype(BF16)


def _cross(qc, mem_k, mem_v, *, tq):
    B, T, d = qc.shape
    M = mem_k.shape[1]
    dh = d // H_C
    return pl.pallas_call(
        functools.partial(_cross_kernel, dh=dh, scale=dh ** -0.5),
        grid=(B, T // tq),
        in_specs=[pl.BlockSpec((1, tq, d), lambda b, t: (b, t, 0)),
                  pl.BlockSpec((1, M, d), lambda b, t: (b, 0, 0)),
                  pl.BlockSpec((1, M, d), lambda b, t: (b, 0, 0))],
        out_specs=pl.BlockSpec((1, tq, d), lambda b, t: (b, t, 0)),
        out_shape=jax.ShapeDtypeStruct((B, T, d), BF16),
        compiler_params=_cparams(("parallel", "parallel")),
        name="cross_attn",
    )(qc, mem_k, mem_v)


def _gelu_tanh(x):
    return 0.5 * x * (1.0 + jnp.tanh(np.sqrt(2.0 / np.pi) * (x + 0.044715 * (x * x * x))))


def _ffn_front(x1_ref, o_ref, wco_ref, nf_ref, x2_ref, hb_ref, acc_ref):
    x2 = x1_ref[0] + _mm(o_ref[0], wco_ref[...])
    x2_ref[...] = x2
    hb_ref[...] = _rms(x2, nf_ref[...]).astype(BF16)
    acc_ref[...] = jnp.zeros_like(acc_ref)


def _ffn_prompt_kernel(x1_ref, o_ref, wco_ref, nf_ref, wua_ref, wug_ref, cwa_ref, cwg_ref,
                       cba_ref, cbg_ref, wd_ref, ha_ref, hg_ref,
                       y_ref, ca_ref, cg_ref, x2_ref, hb_ref, acc_ref, carry_ref, *, tm, rs):
    t = pl.program_id(1)
    j = pl.program_id(2)
    nj = pl.num_programs(2)

    @pl.when(j == 0)
    def _():
        _ffn_front(x1_ref, o_ref, wco_ref, nf_ref, x2_ref, hb_ref, acc_ref)

    @pl.when(t == 0)
    def _():
        carry_ref[j, 0, 6:8, :] = ha_ref[0]
        carry_ref[j, 1, 6:8, :] = hg_ref[0]

    rid = lax.broadcasted_iota(jnp.int32, (rs, 1), 0)

    def conv_part(hb, part, wu_ref, cw_ref, cb_ref):
        u = _mm(hb, wu_ref[...])
        p2 = carry_ref[j, part, 6:7, :]
        p1 = carry_ref[j, part, 7:8, :]
        um1 = jnp.where(rid == 0, p1, pltpu.roll(u, 1, 0))
        um2 = jnp.where(rid == 0, p2, jnp.where(rid == 1, p1, pltpu.roll(u, 2, 0)))
        carry_ref[j, part] = u[rs - 8:rs, :]
        return cb_ref[...] + um2 * cw_ref[0:1, :] + um1 * cw_ref[1:2, :] + u * cw_ref[2:3, :]

    def sub_body(r, _):
        r0 = pl.multiple_of(r * rs, rs)
        hb = hb_ref[pl.ds(r0, rs), :]
        a = conv_part(hb, 0, wua_ref, cwa_ref, cba_ref)
        g = conv_part(hb, 1, wug_ref, cwg_ref, cbg_ref)
        acc_ref[pl.ds(r0, rs), :] += _mm((_gelu_tanh(g) * a).astype(BF16), wd_ref[...])
        return 0

    lax.fori_loop(0, tm // rs, sub_body, 0)
    ca_ref[0, 0] = carry_ref[j, 0, 6:8, :]
    cg_ref[0, 0] = carry_ref[j, 1, 6:8, :]

    @pl.when(j == nj - 1)
    def _():
        y_ref[0] = x2_ref[...] + acc_ref[...]


def _ffn_prompt(x1, o, w_co, norm_ffn, w_up, conv_w, conv_b, w_down, hist, *, tm, tf):
    B, T, d = x1.shape
    d_ff = w_down.shape[0]
    nj = d_ff // tf
    nt = T // tm
    idx3 = lambda b, t, j: (b, t, 0)
    c2 = lambda shp: pl.BlockSpec(shp, lambda b, t, j: (0, 0))
    return pl.pallas_call(
        functools.partial(_ffn_prompt_kernel, tm=tm, rs=min(128, tm)),
        grid=(B, nt, nj),
        in_specs=[pl.BlockSpec((1, tm, d), idx3), pl.BlockSpec((1, tm, d), idx3),
                  c2((d, d)), c2((1, d)),
                  pl.BlockSpec((d, tf), lambda b, t, j: (0, j)),
                  pl.BlockSpec((d, tf), lambda b, t, j: (0, nj + j)),
                  pl.BlockSpec((CONV_W, tf), lambda b, t, j: (0, j)),
                  pl.BlockSpec((CONV_W, tf), lambda b, t, j: (0, nj + j)),
                  pl.BlockSpec((1, tf), lambda b, t, j: (0, j)),
                  pl.BlockSpec((1, tf), lambda b, t, j: (0, nj + j)),
                  pl.BlockSpec((tf, d), lambda b, t, j: (j, 0)),
                  pl.BlockSpec((1, CONV_W - 1, tf), lambda b, t, j: (b, 0, j)),
                  pl.BlockSpec((1, CONV_W - 1, tf), lambda b, t, j: (b, 0, nj + j))],
        out_specs=[pl.BlockSpec((1, tm, d), idx3),
                   pl.BlockSpec((1, 1, CONV_W - 1, tf), lambda b, t, j: (b, t, 0, j)),
                   pl.BlockSpec((1, 1, CONV_W - 1, tf), lambda b, t, j: (b, t, 0, j))],
        out_shape=[jax.ShapeDtypeStruct((B, T, d), F32),
                   jax.ShapeDtypeStruct((B, nt, CONV_W - 1, d_ff), F32),
                   jax.ShapeDtypeStruct((B, nt, CONV_W - 1, d_ff), F32)],
        scratch_shapes=[pltpu.VMEM((tm, d), F32), pltpu.VMEM((tm, d), BF16), pltpu.VMEM((tm, d), F32),
                        pltpu.VMEM((nj, 2, 8, tf), F32)],
        compiler_params=_cparams(("arbitrary", "arbitrary", "arbitrary")),
        name="ffn_prompt",
    )(x1, o, w_co, norm_ffn, w_up, w_up, conv_w, conv_w, conv_b, conv_b, w_down, hist, hist)


def _ffn_sample_kernel(x1_ref, o_ref, wco_ref, nf_ref, wua_ref, wug_ref, cwa_ref, cwg_ref,
                       cba_ref, cbg_ref, wd_ref, h0a_ref, h0g_ref, h1a_ref, h1g_ref,
                       y_ref, ua_ref, ug_ref, x2_ref, hb_ref, acc_ref):
    j = pl.program_id(0)
    nj = pl.num_programs(0)

    @pl.when(j == 0)
    def _():
        _ffn_front(x1_ref, o_ref, wco_ref, nf_ref, x2_ref, hb_ref, acc_ref)

    hb = hb_ref[...]

    def conv_part(wu_ref, cw_ref, cb_ref, h0_ref, h1_ref, u_out_ref):
        u = _mm(hb, wu_ref[...])
        u_out_ref[...] = u
        return cb_ref[...] + h0_ref[...] * cw_ref[0:1, :] + h1_ref[...] * cw_ref[1:2, :] + u * cw_ref[2:3, :]

    a = conv_part(wua_ref, cwa_ref, cba_ref, h0a_ref, h1a_ref, ua_ref)
    g = conv_part(wug_ref, cwg_ref, cbg_ref, h0g_ref, h1g_ref, ug_ref)
    acc_ref[...] += _mm((_gelu_tanh(g) * a).astype(BF16), wd_ref[...])

    @pl.when(j == nj - 1)
    def _():
        y_ref[0] = x2_ref[...] + acc_ref[...]


def _ffn_sample(x1, o, w_co, norm_ffn, w_up, conv_w, conv_b, w_down, h0, h1, *, tf):
    _, rows, d = x1.shape
    d_ff = w_down.shape[0]
    nj = d_ff // tf
    c2 = lambda shp: pl.BlockSpec(shp, lambda j: (0, 0))
    c3 = lambda shp: pl.BlockSpec(shp, lambda j: (0, 0, 0))
    col_a = lambda r: pl.BlockSpec((r, tf), lambda j: (0, j))
    col_g = lambda r: pl.BlockSpec((r, tf), lambda j: (0, nj + j))
    return pl.pallas_call(
        _ffn_sample_kernel,
        grid=(nj,),
        in_specs=[c3((1, rows, d)), c3((1, rows, d)), c2((d, d)), c2((1, d)),
                  col_a(d), col_g(d), col_a(CONV_W), col_g(CONV_W), col_a(1), col_g(1),
                  pl.BlockSpec((tf, d), lambda j: (j, 0)),
                  col_a(rows), col_g(rows), col_a(rows), col_g(rows)],
        out_specs=[c3((1, rows, d)), col_a(rows), col_a(rows)],
        out_shape=[jax.ShapeDtypeStruct((1, rows, d), F32),
                   jax.ShapeDtypeStruct((rows, d_ff), F32),
                   jax.ShapeDtypeStruct((rows, d_ff), F32)],
        scratch_shapes=[pltpu.VMEM((rows, d), F32), pltpu.VMEM((rows, d), BF16), pltpu.VMEM((rows, d), F32)],
        compiler_params=_cparams(("arbitrary",)),
        name="ffn_sample",
    )(x1, o, w_co, norm_ffn, w_up, w_up, conv_w, conv_w, conv_b, conv_b, w_down, h0, h0, h1, h1)


def _rope_tables(pos, dh_a):
    posf = pos.astype(F32)[:, None]
    half_a = dh_a // 2
    inv_a = ROPE_THETA ** (-jnp.arange(half_a, dtype=F32) / half_a)
    ang_a = posf * inv_a[None, :]
    cos_a, sin_a = jnp.cos(ang_a), jnp.sin(ang_a)
    half_i = D_IDX // 2
    inv_i = ROPE_THETA ** (-jnp.arange(half_i, dtype=F32) / half_i)
    ang_i = posf * inv_i[None, :]
    cos_i, sin_i = jnp.cos(ang_i), jnp.sin(ang_i)
    z = jnp.zeros_like(sin_i)
    one = jnp.ones_like(sin_i)
    return jnp.concatenate([
        cos_a, cos_a, -sin_a, sin_a,
        cos_i, cos_i, cos_i, cos_i,
        z, sin_i, z, sin_i,
        -sin_i, z, -sin_i, z,
        cos_i, cos_i, one, one,
        z, sin_i, z, z,
        -sin_i, z, z, z,
    ], axis=1)


def kernel(x_prompt, x_sample, mem_prompt, cache_k, cache_v, cache_idx_k, cache_mem_k, cache_mem_v,
           state_mlstm_c, state_mlstm_n, state_mlstm_m, state_conv, page_table,
           norm_mix, w_in, b_if, mlstm_norm, q_norm, k_norm, w_out, norm_cross, norm_mem,
           w_cq, w_ck, w_cv, w_co, cq_norm, ck_norm, norm_ffn, w_up, conv_w, conv_b, w_down):
    B, S, D = x_prompt.shape
    Bd, T, _ = x_sample.shape
    assert T == 1 and w_in.shape[0] == 1
    n_pool, page = cache_k.shape[1], cache_k.shape[2]
    n_pages = page_table.shape[1]
    past = n_pages * page
    mix_m = mlstm_norm.shape[1]
    dh_m = mix_m // H_M
    dh_a = q_norm.shape[1]
    mix_a = H_A * dh_a
    d_ff = w_down.shape[1]
    M = mem_prompt.shape[1]
    chunk = min(128, S)
    topk_p = min(TOPK_MAX, S // 4)
    topk_s = min(TOPK_MAX, (past + T) // 4)

    w = w_in[0]
    o_gate = 4 * mix_m
    o_aq = o_gate + 2 * H_M
    o_iq = o_aq + 3 * mix_a
    o_ik = o_iq + H_IDX * D_IDX
    o_iw = o_ik + D_IDX
    tail_pad = LANES - (D_IDX + H_IDX + 2 * H_M)
    w_r = jnp.concatenate([w[:, :o_gate], w[:, o_aq:o_iq], w[:, o_iq:o_ik], w[:, o_ik:o_iw],
                           w[:, o_iw:o_iw + H_IDX], w[:, o_gate:o_aq],
                           jnp.zeros((D, tail_pad), w.dtype)], axis=1).astype(BF16)
    bias_tail = jnp.concatenate([jnp.zeros((D_IDX + H_IDX,), F32), b_if[0].astype(F32),
                                 jnp.zeros((tail_pad,), F32)])[None, :]
    w_out_b = w_out[0].astype(BF16)
    w_cq_b, w_ck_b, w_cv_b, w_co_b = (a[0].astype(BF16) for a in (w_cq, w_ck, w_cv, w_co))
    w_up_b = w_up[0].astype(BF16)
    w_down_b = w_down[0].astype(BF16)
    row = lambda a: a[0][None, :]

    def split_misc(misc):
        ik = misc[:, :D_IDX]
        li = misc[:, D_IDX + H_IDX:D_IDX + H_IDX + H_M]
        lf = misc[:, D_IDX + H_IDX + H_M:D_IDX + H_IDX + 2 * H_M]
        return ik, li, lf

    tm_in = min(256, S)
    tab_p = _rope_tables(jnp.arange(S), dh_a)
    (mq, mk, mv, mo, aqb, ak, av, akb, avb, iqb, misc) = _in_proj(
        x_prompt.reshape(B * S, D), row(norm_mix), w_r, bias_tail, row(q_norm), row(k_norm), tab_p,
        tm=tm_in, tab_tiles=S // tm_in, mix_m=mix_m, mix_a=mix_a)
    ik_p, li_p, lf_p = split_misc(misc)
    r3 = lambda a: a.reshape(B, S, a.shape[-1])
    gates = jnp.stack([li_p.reshape(B, S, H_M), lf_p.reshape(B, S, H_M)], axis=-1)
    gcol = gates.transpose(0, 2, 1, 3).reshape(B * H_M, S, 2)
    grow = gates.transpose(0, 2, 3, 1).reshape(B * H_M, 2, S // chunk, chunk)
    y_m, c_p, n_p, m_p = _mlstm_prompt(r3(mq), r3(mk), r3(mv), r3(mo), grow, gcol, row(mlstm_norm), chunk=chunk)

    tq = min(256, S)
    wk = min(512, S)
    ikt = ik_p.astype(BF16).reshape(B, S // wk, wk, D_IDX).transpose(0, 1, 3, 2)
    y_a = _dsa_prompt(r3(iqb), r3(misc), ikt, r3(aqb), r3(akb), r3(avb), tq=tq, w=wk, topk=topk_p)

    mk_p, mv_p = _mem_kv(mem_prompt.reshape(B * M, D), row(norm_mem), w_ck_b, w_cv_b, row(ck_norm),
                         tm=min(256, B * M))
    x1, qc = _out_cq(x_prompt.reshape(B * S, D), y_m.reshape(B * S, mix_m), y_a.reshape(B * S, mix_a),
                     w_out_b, row(norm_cross), w_cq_b, row(cq_norm), tm=min(512, S))
    o_c = _cross(qc.reshape(B, S, D), mk_p.reshape(B, M, D), mv_p.reshape(B, M, D), tq=min(512, S))
    tf = d_ff // 2 if (d_ff // 2) % LANES == 0 else d_ff
    xp, conv_a, conv_g = _ffn_prompt(x1.reshape(B, S, D), o_c, w_co_b, row(norm_ffn), w_up_b, conv_w[0],
                                     conv_b[0][None, :], w_down_b,
                                     jnp.zeros((B, CONV_W - 1, 2 * d_ff), F32), tm=min(512, S), tf=tf)
    conv_p = jnp.concatenate([conv_a[:, -1], conv_g[:, -1]], axis=-1)

    tab_s = jnp.broadcast_to(_rope_tables(jnp.full((1,), past, jnp.int32), dh_a), (Bd, 8 * LANES))
    (mq_s, mk_s, mv_s, mo_s, aqb_s, ak_s, av_s, _, _, iqb_s, misc_s) = _in_proj(
        x_sample.reshape(Bd, D), row(norm_mix), w_r, bias_tail, row(q_norm), row(k_norm), tab_s,
        tm=Bd, tab_tiles=1, mix_m=mix_m, mix_a=mix_a)
    ik_s, li_s, lf_s = split_misc(misc_s)
    gs = jnp.concatenate([li_s, lf_s, state_mlstm_m[0].astype(F32)], axis=-1)[:, None, :]
    e1 = lambda a: a[:, None, :]
    y_ms, c_s, n_s, m_s = _mlstm_sample(e1(mq_s), e1(mk_s), e1(mv_s), e1(mo_s), gs,
                                        state_mlstm_c[0], state_mlstm_n[0].reshape(Bd, 1, mix_m),
                                        row(mlstm_norm))

    iq8 = jnp.pad(iqb_s.reshape(Bd, H_IDX, D_IDX), ((0, 0), (0, 8 - H_IDX), (0, 0)))
    w8 = jnp.pad(misc_s[:, D_IDX:D_IDX + H_IDX], ((0, 0), (0, 8 - H_IDX)))[:, :, None]
    sel_idx = _dsa_sample_select(page_table, iq8, w8, e1(ik_s), cache_idx_k[0], topk=topk_s,
                                 cw=min(512, past))
    y_as = _dsa_sample_attend(sel_idx.reshape(Bd, topk_s), page_table, e1(aqb_s), ak_s, av_s,
                              cache_k[0].reshape(n_pool * page, mix_a),
                              cache_v[0].reshape(n_pool * page, mix_a), page=page, topk=topk_s)

    x1_s, qc_s = _out_cq(x_sample.reshape(Bd, D), y_ms.reshape(Bd, mix_m), y_as.reshape(Bd, mix_a),
                         w_out_b, row(norm_cross), w_cq_b, row(cq_norm), tm=Bd)
    o_s = _cross(qc_s.reshape(Bd, 1, D), cache_mem_k[0].reshape(Bd, M, D), cache_mem_v[0].reshape(Bd, M, D), tq=1)
    xs, u_a, u_g = _ffn_sample(x1_s.reshape(1, Bd, D), o_s.reshape(1, Bd, D), w_co_b, row(norm_ffn), w_up_b,
                               conv_w[0], conv_b[0][None, :], w_down_b,
                               state_conv[0, :, 0, :], state_conv[0, :, 1, :], tf=tf)
    conv_s = jnp.stack([state_conv[0, :, 1, :], jnp.concatenate([u_a, u_g], axis=-1)], axis=1)

    lead = lambda a: a[None]
    return (xp, xs.reshape(Bd, 1, D),
            lead(ak.reshape(B, S, H_A, dh_a)), lead(av.reshape(B, S, H_A, dh_a)), lead(ik_p.reshape(B, S, D_IDX)),
            lead(c_p), lead(n_p.reshape(B, H_M, dh_m)), lead(m_p[:, 0, 0].reshape(B, H_M)),
            lead(mk_p.reshape(B, M, H_C, D // H_C)), lead(mv_p.reshape(B, M, H_C, D // H_C)), lead(conv_p),
            lead(ak_s.reshape(Bd, 1, H_A, dh_a)), lead(av_s.reshape(Bd, 1, H_A, dh_a)),
            lead(ik_s.reshape(Bd, 1, D_IDX)),
            lead(c_s), lead(n_s.reshape(Bd, H_M, dh_m)), lead(m_s[:, 0, :H_M]), lead(conv_s))
```

```python
import functools

import jax
import jax.numpy as jnp
import numpy as np
from jax import lax
from jax.experimental import pallas as pl
from jax.experimental.pallas import tpu as pltpu

F32 = jnp.float32
BF16 = jnp.bfloat16

H_M = 4
H_A = 4
H_IDX = 4
D_IDX = 64
H_C = 4
TOPK_MAX = 256
CONV_W = 3
ROPE_THETA = 10000.0
EPS = 1e-6
NEG_INF = float("-inf")
POS_INF = float("inf")

LANES = 128
VMEM_LIMIT = 56 * 1024 * 1024
N_BISECT = 24


def _cparams(sem):
    return pltpu.CompilerParams(dimension_semantics=sem, vmem_limit_bytes=VMEM_LIMIT)


def _nt(a, b):
    return lax.dot_general(a, b, (((1,), (1,)), ((), ())), preferred_element_type=F32)


def _tn(a, b):
    return lax.dot_general(a, b, (((0,), (0,)), ((), ())), preferred_element_type=F32)


def _mm(a, b):
    return jnp.dot(a, b, preferred_element_type=F32)


def _rms(x, g):
    ms = jnp.mean(x * x, axis=-1, keepdims=True)
    return x * lax.rsqrt(ms + EPS) * g


def _sigmoid(x):
    return 1.0 / (1.0 + jnp.exp(-x))


def _in_proj_kernel(x_ref, nm_ref, w_ref, bias_ref, qn_ref, kn_ref, tab_ref,
                    mq_ref, mk_ref, mv_ref, mo_ref, aqb_ref, ak_ref, av_ref, akb_ref, avb_ref,
                    iqb_ref, misc_ref, *, mix_m, mix_a, dh_m, dh_a):
    h = _rms(x_ref[...], nm_ref[...]).astype(BF16)

    def proj(lo, width):
        return _mm(h, w_ref[:, lo:lo + width])

    o_mq, o_mk, o_mv, o_mo = 0, mix_m, 2 * mix_m, 3 * mix_m
    o_aq = 4 * mix_m
    o_ak = o_aq + mix_a
    o_av = o_ak + mix_a
    o_iq = o_av + mix_a
    o_tail = o_iq + H_IDX * D_IDX

    mq_ref[...] = proj(o_mq, mix_m)
    mk_ref[...] = proj(o_mk, mix_m) * (dh_m ** -0.5)
    mv_ref[...] = proj(o_mv, mix_m)
    mo_ref[...] = proj(o_mo, mix_m)

    cos_a = tab_ref[:, 0:LANES]
    sin_a = tab_ref[:, LANES:2 * LANES]

    def norm_rope(z, g_ref):
        outs = []
        for hh in range(mix_a // dh_a):
            zh = _rms(z[:, hh * dh_a:(hh + 1) * dh_a], g_ref[...])
            outs.append(zh * cos_a + pltpu.roll(zh, dh_a // 2, 1) * sin_a)
        return outs

    aq = norm_rope(proj(o_aq, mix_a), qn_ref)
    aqb_ref[...] = jnp.concatenate(aq, axis=1).astype(BF16)
    ak = norm_rope(proj(o_ak, mix_a), kn_ref)
    av = proj(o_av, mix_a)
    for hh in range(mix_a // dh_a):
        ak_ref[:, hh, :] = ak[hh]
        av_ref[:, hh, :] = av[:, hh * dh_a:(hh + 1) * dh_a]
    akb_ref[...] = jnp.concatenate(ak, axis=1).astype(BF16)
    avb_ref[...] = av.astype(BF16)

    c_i = tab_ref[:, 2 * LANES:3 * LANES]
    s1_i = tab_ref[:, 3 * LANES:4 * LANES]
    s2_i = tab_ref[:, 4 * LANES:5 * LANES]
    ziq = proj(o_iq, H_IDX * D_IDX)
    cols = []
    for c in range(H_IDX * D_IDX // LANES):
        zc = ziq[:, c * LANES:(c + 1) * LANES]
        cols.append(zc * c_i + pltpu.roll(zc, D_IDX // 2, 1) * s1_i
                    + pltpu.roll(zc, LANES - D_IDX // 2, 1) * s2_i)
    iqb_ref[...] = jnp.concatenate(cols, axis=1).astype(BF16)

    c_t = tab_ref[:, 5 * LANES:6 * LANES]
    s1_t = tab_ref[:, 6 * LANES:7 * LANES]
    s2_t = tab_ref[:, 7 * LANES:8 * LANES]
    zt = proj(o_tail, LANES) + bias_ref[...]
    zt = zt * c_t + pltpu.roll(zt, D_IDX // 2, 1) * s1_t + pltpu.roll(zt, LANES - D_IDX // 2, 1) * s2_t
    lane = lax.broadcasted_iota(jnp.int32, zt.shape, 1)
    f_lo = D_IDX + H_IDX + H_M
    log_sig = jnp.minimum(zt, 0.0) - jnp.log(1.0 + jnp.exp(-jnp.abs(zt)))
    misc_ref[...] = jnp.where((lane >= f_lo) & (lane < f_lo + H_M), log_sig, zt)


def _in_proj(x2d, norm_mix, w_r, bias_tail, q_norm, k_norm, tab, *, tm, tab_tiles, mix_m, mix_a):
    rows, d = x2d.shape
    dh_m = mix_m // H_M
    dh_a = mix_a // H_A
    nw = w_r.shape[1]
    grid = (rows // tm,)
    row_spec = lambda wdt: pl.BlockSpec((tm, wdt), lambda i: (i, 0))
    const = lambda shp: pl.BlockSpec(shp, lambda i: (0, 0))
    out_shapes = [
        jax.ShapeDtypeStruct((rows, mix_m), F32),
        jax.ShapeDtypeStruct((rows, mix_m), F32),
        jax.ShapeDtypeStruct((rows, mix_m), F32),
        jax.ShapeDtypeStruct((rows, mix_m), F32),
        jax.ShapeDtypeStruct((rows, mix_a), BF16),
        jax.ShapeDtypeStruct((rows, H_A, dh_a), F32),
        jax.ShapeDtypeStruct((rows, H_A, dh_a), F32),
        jax.ShapeDtypeStruct((rows, mix_a), BF16),
        jax.ShapeDtypeStruct((rows, mix_a), BF16),
        jax.ShapeDtypeStruct((rows, H_IDX * D_IDX), BF16),
        jax.ShapeDtypeStruct((rows, LANES), F32),
    ]
    head_spec = pl.BlockSpec((tm, H_A, dh_a), lambda i: (i, 0, 0))
    out_specs = ([row_spec(mix_m)] * 4 + [row_spec(mix_a), head_spec, head_spec, row_spec(mix_a), row_spec(mix_a)]
                 + [row_spec(H_IDX * D_IDX), row_spec(LANES)])
    return pl.pallas_call(
        functools.partial(_in_proj_kernel, mix_m=mix_m, mix_a=mix_a, dh_m=dh_m, dh_a=dh_a),
        grid=grid,
        in_specs=[row_spec(d), const((1, d)), const((d, nw)), const((1, LANES)),
                  const((1, dh_a)), const((1, dh_a)),
                  pl.BlockSpec((tm, 8 * LANES), lambda i: (i % tab_tiles, 0))],
        out_specs=out_specs,
        out_shape=out_shapes,
        compiler_params=_cparams(("parallel",)),
        name="in_proj",
    )(x2d, norm_mix, w_r, bias_tail, q_norm, k_norm, tab)


def _mlstm_prompt_kernel(q_ref, k_ref, v_ref, o_ref, grow_ref, gcol_ref, gain_ref,
                         y_ref, c_ref, n_ref, m_ref, cs_ref, ns_ref, ms_ref, *, chunk, d):
    c_idx = pl.program_id(1)
    L = chunk
    row_i = lax.broadcasted_iota(jnp.int32, (L, L), 0)
    col_i = lax.broadcasted_iota(jnp.int32, (L, L), 1)
    tril = col_i <= row_i
    triu = row_i <= col_i

    @pl.when(c_idx == 0)
    def _():
        cs_ref[...] = jnp.zeros_like(cs_ref)
        ns_ref[...] = jnp.zeros_like(ns_ref)
        ms_ref[...] = jnp.zeros_like(ms_ref)

    gr = grow_ref[0, 0]
    gc = gcol_ref[0]
    for hd in range(H_M):
        hs = slice(hd * d, (hd + 1) * d)
        C = cs_ref[hd]
        n = ns_ref[hd, 0:1, :]
        m = ms_ref[hd, 0:1, 0:1]
        q = q_ref[0, :, hs]
        k = k_ref[0, :, hs]
        v = v_ref[0, :, hs]
        o = o_ref[0, :, hs]
        li_r = gr[hd:hd + 1, :]
        lf_r = gr[H_M + hd:H_M + hd + 1, :]
        li_c = gc[:, hd:hd + 1]
        lf_c = gc[:, H_M + hd:H_M + hd + 1]
        b_c = jnp.sum(jnp.where(tril, lf_r, 0.0), axis=1, keepdims=True)
        b_r = jnp.sum(jnp.where(triu, lf_c, 0.0), axis=0, keepdims=True)
        logd = jnp.where(tril, b_c - b_r + li_r, NEG_INF)
        inter = b_c + m
        m_t = jnp.maximum(inter, jnp.max(logd, axis=1, keepdims=True))
        qb = q.astype(BF16)
        kb = k.astype(BF16)
        s = _nt(qb, kb) * jnp.exp(logd - m_t)
        g_inter = jnp.exp(inter - m_t)
        num = g_inter * _nt(qb, C.astype(BF16)) + _mm(s.astype(BF16), v.astype(BF16))
        den = g_inter * jnp.sum(q * n, axis=1, keepdims=True) + jnp.sum(s, axis=1, keepdims=True)
        h = num / jnp.maximum(jnp.abs(den), jnp.exp(-m_t))
        m_new = m_t[L - 1:L, :]
        b_last = b_c[L - 1:L, :]
        g_prev = jnp.exp(b_last + m - m_new)
        w_c = jnp.exp(b_last - b_c + li_c - m_new)
        cs_ref[hd] = g_prev * C + _tn((v * w_c).astype(BF16), kb)
        ns_ref[hd, 0:1, :] = g_prev * n + jnp.sum(k * w_c, axis=0, keepdims=True)
        ms_ref[hd, 0:1, :] = jnp.broadcast_to(m_new, (1, LANES))
        y_ref[0, :, hs] = _sigmoid(o) * _rms(h, gain_ref[:, hs])

    @pl.when(c_idx == pl.num_programs(1) - 1)
    def _():
        c_ref[0] = cs_ref[...]
        for hd in range(H_M):
            n_ref[0, hd:hd + 1, :] = ns_ref[hd, 0:1, :]
            m_ref[0, hd:hd + 1, :] = ms_ref[hd, 0:1, :]


def _mlstm_prompt(mq, mk, mv, mo, grow, gcol, gain, *, chunk):
    B, S, mix_m = mq.shape
    d = mix_m // H_M
    n_chunks = S // chunk
    seq = pl.BlockSpec((1, chunk, mix_m), lambda b, c: (b, c, 0))
    return pl.pallas_call(
        functools.partial(_mlstm_prompt_kernel, chunk=chunk, d=d),
        grid=(B, n_chunks),
        in_specs=[seq, seq, seq, seq,
                  pl.BlockSpec((1, 1, 2 * H_M, chunk), lambda b, c: (b, c, 0, 0)),
                  pl.BlockSpec((1, chunk, 2 * H_M), lambda b, c: (b, c, 0)),
                  pl.BlockSpec((1, mix_m), lambda b, c: (0, 0))],
        out_specs=[seq,
                   pl.BlockSpec((1, H_M, d, d), lambda b, c: (b, 0, 0, 0)),
                   pl.BlockSpec((1, H_M, d), lambda b, c: (b, 0, 0)),
                   pl.BlockSpec((1, H_M, LANES), lambda b, c: (b, 0, 0))],
        out_shape=[jax.ShapeDtypeStruct((B, S, mix_m), F32),
                   jax.ShapeDtypeStruct((B, H_M, d, d), F32),
                   jax.ShapeDtypeStruct((B, H_M, d), F32),
                   jax.ShapeDtypeStruct((B, H_M, LANES), F32)],
        scratch_shapes=[pltpu.VMEM((H_M, d, d), F32), pltpu.VMEM((H_M, 8, d), F32),
                        pltpu.VMEM((H_M, 8, LANES), F32)],
        compiler_params=_cparams(("parallel", "arbitrary")),
        name="mlstm_prompt",
    )(mq, mk, mv, mo, grow, gcol, gain)


def _mlstm_sample_kernel(q_ref, k_ref, v_ref, o_ref, gs_ref, c_ref, n_ref, gain_ref,
                         y_ref, co_ref, no_ref, mo_ref, *, d):
    gs = gs_ref[0]
    eye = (lax.broadcasted_iota(jnp.int32, (d, d), 0) == lax.broadcasted_iota(jnp.int32, (d, d), 1))
    lane = lax.broadcasted_iota(jnp.int32, (1, LANES), 1)
    m_out = jnp.zeros((1, LANES), F32)
    for h in range(H_M):
        sl = slice(h * d, (h + 1) * d)
        q = q_ref[0, :, sl]
        k = k_ref[0, :, sl]
        v = v_ref[0, :, sl]
        o = o_ref[0, :, sl]
        li = gs[:, h:h + 1]
        lf = gs[:, H_M + h:H_M + h + 1]
        m = gs[:, 2 * H_M + h:2 * H_M + h + 1]
        C = c_ref[0, h]
        n = n_ref[0, :, sl]
        inter = lf + m
        m_t = jnp.maximum(inter, li)
        s = jnp.sum(q * k, axis=1, keepdims=True) * jnp.exp(li - m_t)
        g = jnp.exp(inter - m_t)
        q8 = jnp.broadcast_to(q, (8, d)).astype(BF16)
        cq = _nt(q8, C.astype(BF16))[0:1, :]
        num = g * cq + s * v
        den = g * jnp.sum(n * q, axis=1, keepdims=True) + s
        hh = num / jnp.maximum(jnp.abs(den), jnp.exp(-m_t))
        w = jnp.exp(li - m_t)
        v_col = jnp.sum(jnp.where(eye, v, 0.0), axis=1, keepdims=True)
        co_ref[0, h] = g * C + (w * v_col) * k
        no_ref[0, :, sl] = g * n + w * k
        m_out = jnp.where(lane == h, m_t, m_out)
        y_ref[0, :, sl] = _sigmoid(o) * _rms(hh, gain_ref[:, sl])
    mo_ref[0] = m_out


def _mlstm_sample(mq, mk, mv, mo, gs, c_state, n_state, gain):
    Bd, _, mix_m = mq.shape
    d = mix_m // H_M
    row = pl.BlockSpec((1, 1, mix_m), lambda b: (b, 0, 0))
    return pl.pallas_call(
        functools.partial(_mlstm_sample_kernel, d=d),
        grid=(Bd,),
        in_specs=[row, row, row, row,
                  pl.BlockSpec((1, 1, 3 * H_M), lambda b: (b, 0, 0)),
                  pl.BlockSpec((1, H_M, d, d), lambda b: (b, 0, 0, 0)),
                  row,
                  pl.BlockSpec((1, mix_m), lambda b: (0, 0))],
        out_specs=[row,
                   pl.BlockSpec((1, H_M, d, d), lambda b: (b, 0, 0, 0)),
                   row,
                   pl.BlockSpec((1, 1, LANES), lambda b: (b, 0, 0))],
        out_shape=[jax.ShapeDtypeStruct((Bd, 1, mix_m), F32),
                   jax.ShapeDtypeStruct((Bd, H_M, d, d), F32),
                   jax.ShapeDtypeStruct((Bd, 1, mix_m), F32),
                   jax.ShapeDtypeStruct((Bd, 1, LANES), F32)],
        compiler_params=_cparams(("parallel",)),
        name="mlstm_sample",
    )(mq, mk, mv, mo, gs, c_state, n_state, gain)


def _dsa_prompt_kernel(iq_ref, misc_ref, ikt_ref, aq_ref, ak_ref, av_ref, ya_ref, sc_ref,
                       *, tq, w, topk, dh, scale):
    i = pl.program_id(1)
    nk = ((i + 1) * tq + w - 1) // w
    kf = float(topk)
    nsub = w // LANES

    q_pos = i * tq + lax.broadcasted_iota(jnp.int32, (tq, 1), 0)
    lane_w = lax.broadcasted_iota(jnp.int32, (1, w), 1)
    iq = iq_ref[0]
    iq_h = [iq[:, h * D_IDX:(h + 1) * D_IDX] for h in range(H_IDX)]
    misc = misc_ref[0]
    w_h = [misc[:, D_IDX + h:D_IDX + h + 1] for h in range(H_IDX)]

    def score_body(c, carry):
        rmax, rmin = carry
        ikc = ikt_ref[0, c]
        score = jnp.zeros((tq, w), F32)
        for h in range(H_IDX):
            score = score + w_h[h] * jnp.maximum(_mm(iq_h[h], ikc), 0.0)
        valid = (c * w + lane_w) <= q_pos
        sc_ref[c] = jnp.where(valid, score, NEG_INF)
        rmax = jnp.maximum(rmax, jnp.max(jnp.where(valid, score, NEG_INF), axis=1, keepdims=True))
        rmin = jnp.minimum(rmin, jnp.min(jnp.where(valid, score, POS_INF), axis=1, keepdims=True))
        return rmax, rmin

    rmax, rmin = lax.fori_loop(0, nk, score_body,
                               (jnp.full((tq, 1), NEG_INF, F32), jnp.full((tq, 1), POS_INF, F32)))

    ge = lambda x, t: x >= t
    gt = lambda x, t: x > t
    eq = lambda x, t: x == t

    rh = min(tq, LANES)

    def select_rows(r0):
        rows = pl.ds(r0, rh)
        qp = q_pos[r0:r0 + rh]
        rmin_h = rmin[r0:r0 + rh]
        rmax_h = rmax[r0:r0 + rh]
        active = (qp + 1) > topk

        def row_pass(fn, init, combine, finish):
            def body(c, acc):
                x = sc_ref[c, rows, :]
                for j in range(nsub):
                    acc = combine(acc, fn(x[:, j * LANES:(j + 1) * LANES]))
                return acc
            return finish(lax.fori_loop(0, nk, body, jnp.full((rh, LANES), init, F32)))

        def count(pred, thr):
            thr_b = jnp.broadcast_to(thr, (rh, LANES))
            return row_pass(lambda x: jnp.where(pred(x, thr_b), 1.0, 0.0), 0.0, jnp.add,
                            lambda a: jnp.sum(a, axis=1, keepdims=True))

        def min_where(pred, thr):
            thr_b = jnp.broadcast_to(thr, (rh, LANES))
            return row_pass(lambda x: jnp.where(pred(x, thr_b), x, POS_INF), POS_INF, jnp.minimum,
                            lambda a: jnp.min(a, axis=1, keepdims=True))

        def bis_body(_, carry):
            lo, hi, c_lo = carry
            mid = 0.5 * (lo + hi)
            c_mid = count(ge, mid)
            ok = c_mid >= kf
            return jnp.where(ok, mid, lo), jnp.where(ok, hi, mid), jnp.where(ok, c_mid, c_lo)

        hi0 = rmax_h + jnp.abs(rmax_h) + 1.0
        lo, _, c_lo = lax.fori_loop(0, N_BISECT, bis_body, (rmin_h, hi0, (qp + 1).astype(F32)))
        unresolved = jnp.max(jnp.where(active & (c_lo != kf), 1.0, 0.0)) > 0.5

        @pl.when(jnp.logical_not(unresolved))
        def _():
            thr = jnp.where(active, lo, rmin_h)

            def body(c, _):
                sc_ref[c, rows, :] = jnp.where(sc_ref[c, rows, :] >= thr, 0.0, NEG_INF)
                return 0
            lax.fori_loop(0, nk, body, 0)

        @pl.when(unresolved)
        def _():
            tau = min_where(ge, lo)
            g = count(gt, tau)

            def undone(tau, g):
                return active & (g >= kf)

            def fix_cond(st):
                return jnp.max(jnp.where(undone(*st), 1.0, 0.0)) > 0.5

            def fix_body(st):
                tau, g = st
                nd = undone(tau, g)
                tau2 = jnp.where(nd, min_where(gt, tau), tau)
                return tau2, jnp.where(nd, count(gt, tau2), g)

            tau, g = lax.while_loop(fix_cond, fix_body, (tau, g))
            tau = jnp.where(active, tau, rmin_h)
            need = jnp.where(active, kf - g, 1e9)
            tri = (lax.broadcasted_iota(jnp.int32, (w, w), 0)
                   <= lax.broadcasted_iota(jnp.int32, (w, w), 1)).astype(BF16)

            def body(c, run):
                x = sc_ref[c, rows, :]
                is_eq = x == tau
                e = jnp.where(is_eq, 1.0, 0.0)
                pref = _mm(e.astype(BF16), tri) + run
                sel = (x > tau) | (is_eq & (pref <= need))
                sc_ref[c, rows, :] = jnp.where(sel, 0.0, NEG_INF)
                return run + jnp.sum(e, axis=1, keepdims=True)
            lax.fori_loop(0, nk, body, jnp.zeros((rh, 1), F32))

    for r0 in range(0, tq, rh):
        select_rows(r0)


    aq = aq_ref[0]
    for h in range(H_A):
        hs = slice(h * dh, (h + 1) * dh)
        qh = aq[:, hs]

        def att_body(c, carry, hs=hs, qh=qh):
            m, l, acc = carry
            k0 = pl.multiple_of(c * w, w)
            kc = ak_ref[0, pl.ds(k0, w), hs]
            vc = av_ref[0, pl.ds(k0, w), hs]
            s = _nt(qh, kc) * scale + sc_ref[c]
            m_new = jnp.maximum(m, jnp.max(s, axis=1, keepdims=True))
            m_safe = jnp.where(m_new == NEG_INF, 0.0, m_new)
            alpha = jnp.exp(m - m_safe)
            p = jnp.exp(s - m_safe)
            l = alpha * l + jnp.sum(p, axis=1, keepdims=True)
            acc = alpha * acc + _mm(p.astype(BF16), vc)
            return m_new, l, acc

        m, l, acc = lax.fori_loop(0, nk, att_body,
                                  (jnp.full((tq, 1), NEG_INF, F32), jnp.zeros((tq, 1), F32),
                                   jnp.zeros((tq, dh), F32)))
        ya_ref[0, :, hs] = acc / l


def _dsa_prompt(iqb, misc, ikt, aqb, akb, avb, *, tq, w, topk):
    B, S, mix_a = aqb.shape
    dh = mix_a // H_A
    nq = S // tq
    nw = S // w
    return pl.pallas_call(
        functools.partial(_dsa_prompt_kernel, tq=tq, w=w, topk=topk, dh=dh, scale=dh ** -0.5),
        grid=(B, nq),
        in_specs=[pl.BlockSpec((1, tq, H_IDX * D_IDX), lambda b, i: (b, i, 0)),
                  pl.BlockSpec((1, tq, LANES), lambda b, i: (b, i, 0)),
                  pl.BlockSpec((1, nw, D_IDX, w), lambda b, i: (b, 0, 0, 0)),
                  pl.BlockSpec((1, tq, mix_a), lambda b, i: (b, i, 0)),
                  pl.BlockSpec((1, S, mix_a), lambda b, i: (b, 0, 0)),
                  pl.BlockSpec((1, S, mix_a), lambda b, i: (b, 0, 0))],
        out_specs=pl.BlockSpec((1, tq, mix_a), lambda b, i: (b, i, 0)),
        out_shape=jax.ShapeDtypeStruct((B, S, mix_a), F32),
        scratch_shapes=[pltpu.VMEM((nw, tq, w), F32)],
        compiler_params=_cparams(("parallel", "arbitrary")),
        name="dsa_prompt",
    )(iqb, misc, ikt, aqb, akb, avb)


def _dsa_sample_select_kernel(pt_ref, iq_ref, w_ref, ikn_ref, ptv_ref, pool_ref, rows_ref, flag_ref,
                              ikbuf, sem, sc_ref, xn_ref, slot_ref, phys_ref,
                              *, n_pages, page, topk, cw):
    nb = iq_ref.shape[0]
    past = n_pages * page
    kf = float(topk)
    n_cw = past // cw

    def page_copy(bb, p, slot):
        return pltpu.make_async_copy(pool_ref.at[pt_ref[bb, p]],
                                     ikbuf.at[slot, :, pl.ds(p * page, page)],
                                     sem.at[slot])

    def start_all(bb, slot):
        def body(p, _):
            page_copy(bb, p, slot).start()
            return 0
        lax.fori_loop(0, n_pages, body, 0)

    start_all(0, 0)

    def score_body(b, _):
        slot = b % 2

        @pl.when(b + 1 < nb)
        def _():
            start_all(b + 1, 1 - slot)

        def wait_body(p, _):
            page_copy(b, p, slot).wait()
            return 0
        lax.fori_loop(0, n_pages, wait_body, 0)

        iq8 = iq_ref[b]
        w8 = w_ref[b]
        s8 = _mm(iq8, ikbuf[slot].astype(BF16))
        sc_ref[pl.ds(b, 1), :] = jnp.sum(w8 * jnp.maximum(s8, 0.0), axis=0, keepdims=True)
        ikn = ikn_ref[b].astype(BF16).astype(F32)
        sn8 = jnp.sum(iq8.astype(F32) * ikn, axis=1, keepdims=True)
        xn_b = jnp.sum(w8 * jnp.maximum(sn8, 0.0), axis=0, keepdims=True)
        xn_ref[pl.ds(b, 1), :] = jnp.broadcast_to(xn_b, (1, LANES))
        return 0

    lax.fori_loop(0, nb, score_body, 0)

    x = sc_ref[...]
    xn = xn_ref[:, 0:1]

    def cnt(mask_row, mask_new):
        return (jnp.sum(jnp.where(mask_row, 1.0, 0.0), axis=1, keepdims=True)
                + jnp.where(mask_new, 1.0, 0.0))

    rmax = jnp.maximum(jnp.max(x, axis=1, keepdims=True), xn)
    rmin = jnp.minimum(jnp.min(x, axis=1, keepdims=True), xn)
    hi0 = rmax + jnp.abs(rmax) + 1.0

    def bis_body(_, carry):
        lo, hi = carry
        mid = 0.5 * (lo + hi)
        ok = cnt(x >= mid, xn >= mid) >= kf
        return jnp.where(ok, mid, lo), jnp.where(ok, hi, mid)

    lo, _ = lax.fori_loop(0, N_BISECT, bis_body, (rmin, hi0))

    def min_where(mask_row, mask_new):
        return jnp.minimum(jnp.min(jnp.where(mask_row, x, POS_INF), axis=1, keepdims=True),
                           jnp.where(mask_new, xn, POS_INF))

    tau = min_where(x >= lo, xn >= lo)
    g = cnt(x > tau, xn > tau)

    def fix_cond(st):
        tau, g = st
        return jnp.max(jnp.where(g >= kf, 1.0, 0.0)) > 0.5

    def fix_body(st):
        tau, g = st
        tau2 = jnp.where(g >= kf, min_where(x > tau, xn > tau), tau)
        return tau2, cnt(x > tau2, xn > tau2)

    tau, g = lax.while_loop(fix_cond, fix_body, (tau, g))
    need = kf - g

    tri = (lax.broadcasted_iota(jnp.int32, (cw, cw), 0)
           < lax.broadcasted_iota(jnp.int32, (cw, cw), 1)).astype(BF16)

    def excl_prefix(flag):
        outs = []
        run = jnp.zeros((nb, 1), F32)
        for c in range(n_cw):
            f = flag[:, c * cw:(c + 1) * cw]
            outs.append(_mm(f.astype(BF16), tri) + run)
            run = run + jnp.sum(f, axis=1, keepdims=True)
        return jnp.concatenate(outs, axis=1), run

    is_eq = x == tau
    pre_eq, n_eq_past = excl_prefix(jnp.where(is_eq, 1.0, 0.0))
    sel = (x > tau) | (is_eq & (pre_eq < need))
    new_sel = (xn > tau) | ((xn == tau) & (n_eq_past < need))
    slot, _ = excl_prefix(jnp.where(sel, 1.0, 0.0))
    slot_ref[...] = jnp.where(sel, slot, -1.0)

    ptv = ptv_ref[...]
    jrow = lax.broadcasted_iota(jnp.int32, (1, past), 1)
    prow = lax.broadcasted_iota(jnp.int32, (n_pages, 1), 0)
    expand = ((jrow >= prow * page) & (jrow < (prow + 1) * page)).astype(BF16)
    digit_bits = 6
    pt_hi = _mm((ptv >> digit_bits).astype(F32).astype(BF16), expand)
    pt_lo = _mm((ptv & ((1 << digit_bits) - 1)).astype(F32).astype(BF16), expand)
    pidx = lax.broadcasted_iota(jnp.int32, (8, n_pages), 1).astype(F32).astype(BF16)
    pg = _mm(pidx, expand)[0:1, :]
    phys_ref[...] = (pt_hi * (1 << digit_bits) + pt_lo) * page + (jrow.astype(F32) - pg * page)

    slot_col = lax.broadcasted_iota(jnp.int32, (topk, 1), 0).astype(F32)
    lane_b = lax.broadcasted_iota(jnp.int32, (1, LANES), 1)

    def extract_body(b, out):
        srow = slot_ref[pl.ds(b, 1), :]
        frow = phys_ref[pl.ds(b, 1), :]
        acc = jnp.zeros((topk, LANES), F32)
        for c in range(past // LANES):
            cs = slice(c * LANES, (c + 1) * LANES)
            acc = acc + jnp.where(srow[:, cs] == slot_col, frow[:, cs], 0.0)
        return jnp.where(lane_b == b, jnp.sum(acc, axis=1, keepdims=True), out)

    out = lax.fori_loop(0, nb, extract_body, jnp.zeros((topk, LANES), F32))
    rows_ref[...] = out.astype(jnp.int32)
    flag_ref[...] = jnp.broadcast_to(jnp.where(new_sel, 1, 0), (nb, LANES)).astype(jnp.int32)


def _dsa_sample_select(page_table, iq8, w8, ik_new, pool_ik_t, *, topk, cw):
    Bd, n_pages = page_table.shape
    n_pool, d_idx, page = pool_ik_t.shape
    past = n_pages * page
    assert Bd <= LANES and n_pool <= 64 * 256
    full = lambda shp: pl.BlockSpec(shp, lambda i, pt: (0,) * len(shp))
    grid_spec = pltpu.PrefetchScalarGridSpec(
        num_scalar_prefetch=1,
        grid=(1,),
        in_specs=[full((Bd, 8, d_idx)), full((Bd, 8, 1)), full((Bd, 1, d_idx)), full((Bd, n_pages)),
                  pl.BlockSpec(memory_space=pl.ANY)],
        out_specs=[full((topk, LANES)), full((Bd, LANES))],
        scratch_shapes=[pltpu.VMEM((2, d_idx, past), F32),
                        pltpu.SemaphoreType.DMA((2,)),
                        pltpu.VMEM((Bd, past), F32),
                        pltpu.VMEM((Bd, LANES), F32),
                        pltpu.VMEM((Bd, past), F32),
                        pltpu.VMEM((Bd, past), F32)],
    )
    return pl.pallas_call(
        functools.partial(_dsa_sample_select_kernel, n_pages=n_pages, page=page, topk=topk, cw=cw),
        grid_spec=grid_spec,
        out_shape=[jax.ShapeDtypeStruct((topk, LANES), jnp.int32),
                   jax.ShapeDtypeStruct((Bd, LANES), jnp.int32)],
        compiler_params=_cparams(("arbitrary",)),
        name="dsa_sample_select",
    )(page_table, iq8, w8, ik_new, page_table, pool_ik_t)


def _dsa_sample_attend_kernel(rows_ref, flag_ref, aq_ref, knew_ref, vnew_ref, kpool_ref, vpool_ref,
                              ya_ref, kbuf, vbuf, sem, *, topk, dh, scale):
    b = pl.program_id(0)
    nb = pl.num_programs(0)

    def row_copies(bb, t, slot):
        r = rows_ref[bb, t]
        dst = pl.ds(t * H_A, H_A)
        return (pltpu.make_async_copy(kpool_ref.at[r], kbuf.at[slot, dst, :], sem.at[0, slot]),
                pltpu.make_async_copy(vpool_ref.at[r], vbuf.at[slot, dst, :], sem.at[1, slot]))

    def start_all(bb, slot):
        def body(t, _):
            ck, cv = row_copies(bb, t, slot)
            ck.start()
            cv.start()
            return 0
        lax.fori_loop(0, topk, body, 0)

    slot = b % 2

    @pl.when(b == 0)
    def _():
        start_all(0, 0)

    @pl.when(b + 1 < nb)
    def _():
        start_all(b + 1, 1 - slot)

    def wait_body(t, _):
        ck, cv = row_copies(b, t, slot)
        ck.wait()
        cv.wait()
        return 0
    lax.fori_loop(0, topk, wait_body, 0)

    take_new = (lax.broadcasted_iota(jnp.int32, (topk, 1), 0) == topk - 1) & (flag_ref[b] > 0)
    aq = aq_ref[0]
    for h in range(H_A):
        hs = slice(h * dh, (h + 1) * dh)
        kh = kbuf[slot, pl.ds(h, topk, stride=H_A), :]
        vh = vbuf[slot, pl.ds(h, topk, stride=H_A), :]
        kh = jnp.where(take_new, knew_ref[0, h:h + 1, :], kh).astype(BF16)
        vh = jnp.where(take_new, vnew_ref[0, h:h + 1, :], vh).astype(BF16)
        q8 = jnp.broadcast_to(aq[:, hs], (8, dh))
        s = _nt(q8, kh) * scale
        m = jnp.max(s, axis=1, keepdims=True)
        p = jnp.exp(s - m)
        p = p / jnp.sum(p, axis=1, keepdims=True)
        ya_ref[0, :, hs] = _mm(p.astype(BF16), vh)[0:1, :]


def _dsa_sample_attend(rows, flags, aqb, k_new, v_new, pool_k, pool_v, *, topk):
    Bd, _, mix_a = aqb.shape
    dh = pool_k.shape[2]
    new_spec = pl.BlockSpec((1, H_A, dh), lambda b, r, f: (b, 0, 0))
    grid_spec = pltpu.PrefetchScalarGridSpec(
        num_scalar_prefetch=2,
        grid=(Bd,),
        in_specs=[pl.BlockSpec((1, 1, mix_a), lambda b, r, f: (b, 0, 0)),
                  new_spec, new_spec,
                  pl.BlockSpec(memory_space=pl.ANY),
                  pl.BlockSpec(memory_space=pl.ANY)],
        out_specs=pl.BlockSpec((1, 1, mix_a), lambda b, r, f: (b, 0, 0)),
        scratch_shapes=[pltpu.VMEM((2, topk * H_A, dh), F32),
                        pltpu.VMEM((2, topk * H_A, dh), F32),
                        pltpu.SemaphoreType.DMA((2, 2))],
    )
    return pl.pallas_call(
        functools.partial(_dsa_sample_attend_kernel, topk=topk, dh=dh, scale=dh ** -0.5),
        grid_spec=grid_spec,
        out_shape=jax.ShapeDtypeStruct((Bd, 1, mix_a), F32),
        compiler_params=_cparams(("arbitrary",)),
        name="dsa_sample_attend",
    )(rows, flags, aqb, k_new, v_new, pool_k, pool_v)


def _mem_kv_kernel(mem_ref, nm_ref, wk_ref, wv_ref, kn_ref, k_ref, v_ref, *, dh):
    hm = _rms(mem_ref[...], nm_ref[...]).astype(BF16)
    kk = _mm(hm, wk_ref[...])
    vv = _mm(hm, wv_ref[...])
    for h in range(H_C):
        hs = slice(h * dh, (h + 1) * dh)
        k_ref[:, h, :] = _rms(kk[:, hs], kn_ref[...])
        v_ref[:, h, :] = vv[:, hs]


def _mem_kv(mem2d, norm_mem, w_ck, w_cv, ck_norm, *, tm):
    rows, d = mem2d.shape
    dh = d // H_C
    row = pl.BlockSpec((tm, d), lambda i: (i, 0))
    heads = pl.BlockSpec((tm, H_C, dh), lambda i: (i, 0, 0))
    const = lambda shp: pl.BlockSpec(shp, lambda i: (0, 0))
    return pl.pallas_call(
        functools.partial(_mem_kv_kernel, dh=dh),
        grid=(rows // tm,),
        in_specs=[row, const((1, d)), const((d, d)), const((d, d)), const((1, dh))],
        out_specs=[heads, heads],
        out_shape=[jax.ShapeDtypeStruct((rows, H_C, dh), F32)] * 2,
        compiler_params=_cparams(("parallel",)),
        name="mem_kv",
    )(mem2d, norm_mem, w_ck, w_cv, ck_norm)


def _out_cq_kernel(x_ref, ym_ref, ya_ref, wo_ref, nc_ref, wq_ref, qn_ref, x1_ref, qc_ref, *, mix_m, dh):
    upd = (_mm(ym_ref[...].astype(BF16), wo_ref[0:mix_m, :])
           + _mm(ya_ref[...].astype(BF16), wo_ref[mix_m:, :]))
    x1 = x_ref[...] + upd
    x1_ref[...] = x1
    hq = _mm(_rms(x1, nc_ref[...]).astype(BF16), wq_ref[...])
    for h in range(H_C):
        hs = slice(h * dh, (h + 1) * dh)
        qc_ref[:, hs] = _rms(hq[:, hs], qn_ref[...]).astype(BF16)


def _out_cq(x2d, ym, ya, w_out, norm_cross, w_cq, cq_norm, *, tm):
    rows, d = x2d.shape
    mix_m = ym.shape[1]
    mix_a = ya.shape[1]
    dh = d // H_C
    row = lambda wdt: pl.BlockSpec((tm, wdt), lambda i: (i, 0))
    const = lambda shp: pl.BlockSpec(shp, lambda i: (0, 0))
    return pl.pallas_call(
        functools.partial(_out_cq_kernel, mix_m=mix_m, dh=dh),
        grid=(rows // tm,),
        in_specs=[row(d), row(mix_m), row(mix_a), const((mix_m + mix_a, d)), const((1, d)),
                  const((d, d)), const((1, dh))],
        out_specs=[row(d), row(d)],
        out_shape=[jax.ShapeDtypeStruct((rows, d), F32), jax.ShapeDtypeStruct((rows, d), BF16)],
        compiler_params=_cparams(("parallel",)),
        name="out_cq",
    )(x2d, ym, ya, w_out, norm_cross, w_cq, cq_norm)


def _cross_kernel(q_ref, k_ref, v_ref, o_ref, *, dh, scale):
    q = q_ref[0]
    rows = q.shape[0]
    if rows < 8:
        q = jnp.broadcast_to(q, (8, q.shape[1]))
    for h in range(H_C):
        hs = slice(h * dh, (h + 1) * dh)
        kb = k_ref[0, :, h, :].astype(BF16)
        vb = v_ref[0, :, h, :].astype(BF16)
        s = _nt(q[:, hs], kb) * scale
        m = jnp.max(s, axis=1, keepdims=True)
        p = jnp.exp(s - m)
        p = p / jnp.sum(p, axis=1, keepdims=True)
        o = _mm(p.astype(BF16), vb)
        o_ref[0, :, hs] = o[0:rows].astype(BF16)


def _cross(qc, mem_k, mem_v, *, tq):
    B, T, d = qc.shape
    M = mem_k.shape[1]
    dh = d // H_C
    return pl.pallas_call(
        functools.partial(_cross_kernel, dh=dh, scale=dh ** -0.5),
        grid=(B, T // tq),
        in_specs=[pl.BlockSpec((1, tq, d), lambda b, t: (b, t, 0)),
                  pl.BlockSpec((1, M, H_C, dh), lambda b, t: (b, 0, 0, 0)),
                  pl.BlockSpec((1, M, H_C, dh), lambda b, t: (b, 0, 0, 0))],
        out_specs=pl.BlockSpec((1, tq, d), lambda b, t: (b, t, 0)),
        out_shape=jax.ShapeDtypeStruct((B, T, d), BF16),
        compiler_params=_cparams(("parallel", "parallel")),
        name="cross_attn",
    )(qc, mem_k, mem_v)


def _gelu_tanh(x):
    return 0.5 * x * (1.0 + jnp.tanh(np.sqrt(2.0 / np.pi) * (x + 0.044715 * (x * x * x))))


def _ffn_front(x1_ref, o_ref, wco_ref, nf_ref, x2_ref, hb_ref, acc_ref):
    x2 = x1_ref[0] + _mm(o_ref[0], wco_ref[...])
    x2_ref[...] = x2
    hb_ref[...] = _rms(x2, nf_ref[...]).astype(BF16)
    acc_ref[...] = jnp.zeros_like(acc_ref)


def _ffn_prompt_kernel(x1_ref, o_ref, wco_ref, nf_ref, wua_ref, wug_ref, cwa_ref, cwg_ref,
                       cba_ref, cbg_ref, wd_ref, ha_ref, hg_ref,
                       y_ref, ca_ref, cg_ref, x2_ref, hb_ref, acc_ref, carry_ref, *, tm, rs):
    t = pl.program_id(1)
    j = pl.program_id(2)
    nj = pl.num_programs(2)

    @pl.when(j == 0)
    def _():
        _ffn_front(x1_ref, o_ref, wco_ref, nf_ref, x2_ref, hb_ref, acc_ref)

    @pl.when(t == 0)
    def _():
        carry_ref[j, 0, 6:8, :] = ha_ref[0]
        carry_ref[j, 1, 6:8, :] = hg_ref[0]

    rid = lax.broadcasted_iota(jnp.int32, (rs, 1), 0)

    def conv_part(hb, part, wu_ref, cw_ref, cb_ref):
        u = _mm(hb, wu_ref[...])
        p2 = carry_ref[j, part, 6:7, :]
        p1 = carry_ref[j, part, 7:8, :]
        um1 = jnp.where(rid == 0, p1, pltpu.roll(u, 1, 0))
        um2 = jnp.where(rid == 0, p2, jnp.where(rid == 1, p1, pltpu.roll(u, 2, 0)))
        carry_ref[j, part] = u[rs - 8:rs, :]
        return cb_ref[...] + um2 * cw_ref[0:1, :] + um1 * cw_ref[1:2, :] + u * cw_ref[2:3, :]

    def sub_body(r, _):
        r0 = pl.multiple_of(r * rs, rs)
        hb = hb_ref[pl.ds(r0, rs), :]
        a = conv_part(hb, 0, wua_ref, cwa_ref, cba_ref)
        g = conv_part(hb, 1, wug_ref, cwg_ref, cbg_ref)
        acc_ref[pl.ds(r0, rs), :] += _mm((_gelu_tanh(g) * a).astype(BF16), wd_ref[...])
        return 0

    lax.fori_loop(0, tm // rs, sub_body, 0)
    ca_ref[0, 0] = carry_ref[j, 0, 6:8, :]
    cg_ref[0, 0] = carry_ref[j, 1, 6:8, :]

    @pl.when(j == nj - 1)
    def _():
        y_ref[0] = x2_ref[...] + acc_ref[...]


def _ffn_prompt(x1, o, w_co, norm_ffn, w_up, conv_w, conv_b, w_down, hist, *, tm, tf):
    B, T, d = x1.shape
    d_ff = w_down.shape[0]
    nj = d_ff // tf
    nt = T // tm
    idx3 = lambda b, t, j: (b, t, 0)
    c2 = lambda shp: pl.BlockSpec(shp, lambda b, t, j: (0, 0))
    return pl.pallas_call(
        functools.partial(_ffn_prompt_kernel, tm=tm, rs=min(128, tm)),
        grid=(B, nt, nj),
        in_specs=[pl.BlockSpec((1, tm, d), idx3), pl.BlockSpec((1, tm, d), idx3),
                  c2((d, d)), c2((1, d)),
                  pl.BlockSpec((d, tf), lambda b, t, j: (0, j)),
                  pl.BlockSpec((d, tf), lambda b, t, j: (0, nj + j)),
                  pl.BlockSpec((CONV_W, tf), lambda b, t, j: (0, j)),
                  pl.BlockSpec((CONV_W, tf), lambda b, t, j: (0, nj + j)),
                  pl.BlockSpec((1, tf), lambda b, t, j: (0, j)),
                  pl.BlockSpec((1, tf), lambda b, t, j: (0, nj + j)),
                  pl.BlockSpec((tf, d), lambda b, t, j: (j, 0)),
                  pl.BlockSpec((1, CONV_W - 1, tf), lambda b, t, j: (b, 0, j)),
                  pl.BlockSpec((1, CONV_W - 1, tf), lambda b, t, j: (b, 0, nj + j))],
        out_specs=[pl.BlockSpec((1, tm, d), idx3),
                   pl.BlockSpec((1, 1, CONV_W - 1, tf), lambda b, t, j: (b, t, 0, j)),
                   pl.BlockSpec((1, 1, CONV_W - 1, tf), lambda b, t, j: (b, t, 0, j))],
        out_shape=[jax.ShapeDtypeStruct((B, T, d), F32),
                   jax.ShapeDtypeStruct((B, nt, CONV_W - 1, d_ff), F32),
                   jax.ShapeDtypeStruct((B, nt, CONV_W - 1, d_ff), F32)],
        scratch_shapes=[pltpu.VMEM((tm, d), F32), pltpu.VMEM((tm, d), BF16), pltpu.VMEM((tm, d), F32),
                        pltpu.VMEM((nj, 2, 8, tf), F32)],
        compiler_params=_cparams(("arbitrary", "arbitrary", "arbitrary")),
        name="ffn_prompt",
    )(x1, o, w_co, norm_ffn, w_up, w_up, conv_w, conv_w, conv_b, conv_b, w_down, hist, hist)


def _ffn_sample_kernel(x1_ref, o_ref, wco_ref, nf_ref, wua_ref, wug_ref, cwa_ref, cwg_ref,
                       cba_ref, cbg_ref, wd_ref, h0a_ref, h0g_ref, h1a_ref, h1g_ref,
                       y_ref, ua_ref, ug_ref, x2_ref, hb_ref, acc_ref):
    j = pl.program_id(0)
    nj = pl.num_programs(0)

    @pl.when(j == 0)
    def _():
        _ffn_front(x1_ref, o_ref, wco_ref, nf_ref, x2_ref, hb_ref, acc_ref)

    hb = hb_ref[...]

    def conv_part(wu_ref, cw_ref, cb_ref, h0_ref, h1_ref, u_out_ref):
        u = _mm(hb, wu_ref[...])
        u_out_ref[...] = u
        return cb_ref[...] + h0_ref[...] * cw_ref[0:1, :] + h1_ref[...] * cw_ref[1:2, :] + u * cw_ref[2:3, :]

    a = conv_part(wua_ref, cwa_ref, cba_ref, h0a_ref, h1a_ref, ua_ref)
    g = conv_part(wug_ref, cwg_ref, cbg_ref, h0g_ref, h1g_ref, ug_ref)
    acc_ref[...] += _mm((_gelu_tanh(g) * a).astype(BF16), wd_ref[...])

    @pl.when(j == nj - 1)
    def _():
        y_ref[0] = x2_ref[...] + acc_ref[...]


def _ffn_sample(x1, o, w_co, norm_ffn, w_up, conv_w, conv_b, w_down, h0, h1, *, tf):
    _, rows, d = x1.shape
    d_ff = w_down.shape[0]
    nj = d_ff // tf
    c2 = lambda shp: pl.BlockSpec(shp, lambda j: (0, 0))
    c3 = lambda shp: pl.BlockSpec(shp, lambda j: (0, 0, 0))
    col_a = lambda r: pl.BlockSpec((r, tf), lambda j: (0, j))
    col_g = lambda r: pl.BlockSpec((r, tf), lambda j: (0, nj + j))
    return pl.pallas_call(
        _ffn_sample_kernel,
        grid=(nj,),
        in_specs=[c3((1, rows, d)), c3((1, rows, d)), c2((d, d)), c2((1, d)),
                  col_a(d), col_g(d), col_a(CONV_W), col_g(CONV_W), col_a(1), col_g(1),
                  pl.BlockSpec((tf, d), lambda j: (j, 0)),
                  col_a(rows), col_g(rows), col_a(rows), col_g(rows)],
        out_specs=[c3((1, rows, d)), col_a(rows), col_a(rows)],
        out_shape=[jax.ShapeDtypeStruct((1, rows, d), F32),
                   jax.ShapeDtypeStruct((rows, d_ff), F32),
                   jax.ShapeDtypeStruct((rows, d_ff), F32)],
        scratch_shapes=[pltpu.VMEM((rows, d), F32), pltpu.VMEM((rows, d), BF16), pltpu.VMEM((rows, d), F32)],
        compiler_params=_cparams(("arbitrary",)),
        name="ffn_sample",
    )(x1, o, w_co, norm_ffn, w_up, w_up, conv_w, conv_w, conv_b, conv_b, w_down, h0, h0, h1, h1)


def _rope_tables(pos, dh_a):
    posf = pos.astype(F32)[:, None]
    half_a = dh_a // 2
    inv_a = ROPE_THETA ** (-jnp.arange(half_a, dtype=F32) / half_a)
    ang_a = posf * inv_a[None, :]
    cos_a, sin_a = jnp.cos(ang_a), jnp.sin(ang_a)
    half_i = D_IDX // 2
    inv_i = ROPE_THETA ** (-jnp.arange(half_i, dtype=F32) / half_i)
    ang_i = posf * inv_i[None, :]
    cos_i, sin_i = jnp.cos(ang_i), jnp.sin(ang_i)
    z = jnp.zeros_like(sin_i)
    one = jnp.ones_like(sin_i)
    return jnp.concatenate([
        cos_a, cos_a, -sin_a, sin_a,
        cos_i, cos_i, cos_i, cos_i,
        z, sin_i, z, sin_i,
        -sin_i, z, -sin_i, z,
        cos_i, cos_i, one, one,
        z, sin_i, z, z,
        -sin_i, z, z, z,
    ], axis=1)


def kernel(x_prompt, x_sample, mem_prompt, cache_k, cache_v, cache_idx_k, cache_mem_k, cache_mem_v,
           state_mlstm_c, state_mlstm_n, state_mlstm_m, state_conv, page_table,
           norm_mix, w_in, b_if, mlstm_norm, q_norm, k_norm, w_out, norm_cross, norm_mem,
           w_cq, w_ck, w_cv, w_co, cq_norm, ck_norm, norm_ffn, w_up, conv_w, conv_b, w_down):
    B, S, D = x_prompt.shape
    Bd, T, _ = x_sample.shape
    assert T == 1 and w_in.shape[0] == 1
    n_pool, page = cache_k.shape[1], cache_k.shape[2]
    n_pages = page_table.shape[1]
    past = n_pages * page
    mix_m = mlstm_norm.shape[1]
    dh_m = mix_m // H_M
    dh_a = q_norm.shape[1]
    mix_a = H_A * dh_a
    d_ff = w_down.shape[1]
    M = mem_prompt.shape[1]
    chunk = min(128, S)
    topk_p = min(TOPK_MAX, S // 4)
    topk_s = min(TOPK_MAX, (past + T) // 4)

    w = w_in[0]
    o_gate = 4 * mix_m
    o_aq = o_gate + 2 * H_M
    o_iq = o_aq + 3 * mix_a
    o_ik = o_iq + H_IDX * D_IDX
    o_iw = o_ik + D_IDX
    tail_pad = LANES - (D_IDX + H_IDX + 2 * H_M)
    w_r = jnp.concatenate([w[:, :o_gate], w[:, o_aq:o_iq], w[:, o_iq:o_ik], w[:, o_ik:o_iw],
                           w[:, o_iw:o_iw + H_IDX], w[:, o_gate:o_aq],
                           jnp.zeros((D, tail_pad), w.dtype)], axis=1).astype(BF16)
    bias_tail = jnp.concatenate([jnp.zeros((D_IDX + H_IDX,), F32), b_if[0].astype(F32),
                                 jnp.zeros((tail_pad,), F32)])[None, :]
    w_out_b = w_out[0].astype(BF16)
    w_cq_b, w_ck_b, w_cv_b, w_co_b = (a[0].astype(BF16) for a in (w_cq, w_ck, w_cv, w_co))
    w_up_b = w_up[0].astype(BF16)
    w_down_b = w_down[0].astype(BF16)
    row = lambda a: a[0][None, :]

    def split_misc(misc):
        ik = misc[:, :D_IDX]
        li = misc[:, D_IDX + H_IDX:D_IDX + H_IDX + H_M]
        lf = misc[:, D_IDX + H_IDX + H_M:D_IDX + H_IDX + 2 * H_M]
        return ik, li, lf

    tm_in = min(256, S)
    tab_p = _rope_tables(jnp.arange(S), dh_a)
    (mq, mk, mv, mo, aqb, ak, av, akb, avb, iqb, misc) = _in_proj(
        x_prompt.reshape(B * S, D), row(norm_mix), w_r, bias_tail, row(q_norm), row(k_norm), tab_p,
        tm=tm_in, tab_tiles=S // tm_in, mix_m=mix_m, mix_a=mix_a)
    ik_p, li_p, lf_p = split_misc(misc)
    r3 = lambda a: a.reshape(B, S, a.shape[-1])
    gcol = jnp.concatenate([li_p, lf_p], axis=-1).reshape(B, S, 2 * H_M)
    grow = gcol.reshape(B, S // chunk, chunk, 2 * H_M).transpose(0, 1, 3, 2)
    y_m, c_p, n_p, m_p = _mlstm_prompt(r3(mq), r3(mk), r3(mv), r3(mo), grow, gcol, row(mlstm_norm), chunk=chunk)

    tq = min(256, S)
    wk = min(512, S)
    ikt = ik_p.astype(BF16).reshape(B, S // wk, wk, D_IDX).transpose(0, 1, 3, 2)
    y_a = _dsa_prompt(r3(iqb), r3(misc), ikt, r3(aqb), r3(akb), r3(avb), tq=tq, w=wk, topk=topk_p)

    mk_p, mv_p = _mem_kv(mem_prompt.reshape(B * M, D), row(norm_mem), w_ck_b, w_cv_b, row(ck_norm),
                         tm=min(256, B * M))
    x1, qc = _out_cq(x_prompt.reshape(B * S, D), y_m.reshape(B * S, mix_m), y_a.reshape(B * S, mix_a),
                     w_out_b, row(norm_cross), w_cq_b, row(cq_norm), tm=min(512, S))
    dh_c = D // H_C
    o_c = _cross(qc.reshape(B, S, D), mk_p.reshape(B, M, H_C, dh_c), mv_p.reshape(B, M, H_C, dh_c), tq=min(512, S))
    tf = d_ff // 2 if (d_ff // 2) % LANES == 0 else d_ff
    xp, conv_a, conv_g = _ffn_prompt(x1.reshape(B, S, D), o_c, w_co_b, row(norm_ffn), w_up_b, conv_w[0],
                                     conv_b[0][None, :], w_down_b,
                                     jnp.zeros((B, CONV_W - 1, 2 * d_ff), F32), tm=min(512, S), tf=tf)
    conv_p = jnp.concatenate([conv_a[:, -1], conv_g[:, -1]], axis=-1)

    tab_s = jnp.broadcast_to(_rope_tables(jnp.full((1,), past, jnp.int32), dh_a), (Bd, 8 * LANES))
    (mq_s, mk_s, mv_s, mo_s, aqb_s, ak_s, av_s, _, _, iqb_s, misc_s) = _in_proj(
        x_sample.reshape(Bd, D), row(norm_mix), w_r, bias_tail, row(q_norm), row(k_norm), tab_s,
        tm=Bd, tab_tiles=1, mix_m=mix_m, mix_a=mix_a)
    ik_s, li_s, lf_s = split_misc(misc_s)
    gs = jnp.concatenate([li_s, lf_s, state_mlstm_m[0].astype(F32)], axis=-1)[:, None, :]
    e1 = lambda a: a[:, None, :]
    y_ms, c_s, n_s, m_s = _mlstm_sample(e1(mq_s), e1(mk_s), e1(mv_s), e1(mo_s), gs,
                                        state_mlstm_c[0], state_mlstm_n[0].reshape(Bd, 1, mix_m),
                                        row(mlstm_norm))

    iq8 = jnp.pad(iqb_s.reshape(Bd, H_IDX, D_IDX), ((0, 0), (0, 8 - H_IDX), (0, 0)))
    w8 = jnp.pad(misc_s[:, D_IDX:D_IDX + H_IDX], ((0, 0), (0, 8 - H_IDX)))[:, :, None]
    assert n_pages <= 256
    rows_t, flags = _dsa_sample_select(page_table, iq8, w8, e1(ik_s), jnp.swapaxes(cache_idx_k[0], 1, 2),
                                       topk=topk_s, cw=min(512, past))
    y_as = _dsa_sample_attend(rows_t[:, :Bd].T, flags[:, 0], e1(aqb_s), ak_s, av_s,
                              cache_k[0].reshape(n_pool * page, H_A, dh_a),
                              cache_v[0].reshape(n_pool * page, H_A, dh_a), topk=topk_s)

    x1_s, qc_s = _out_cq(x_sample.reshape(Bd, D), y_ms.reshape(Bd, mix_m), y_as.reshape(Bd, mix_a),
                         w_out_b, row(norm_cross), w_cq_b, row(cq_norm), tm=Bd)
    o_s = _cross(qc_s.reshape(Bd, 1, D), cache_mem_k[0], cache_mem_v[0], tq=1)
    xs, u_a, u_g = _ffn_sample(x1_s.reshape(1, Bd, D), o_s.reshape(1, Bd, D), w_co_b, row(norm_ffn), w_up_b,
                               conv_w[0], conv_b[0][None, :], w_down_b,
                               state_conv[0, :, 0, :], state_conv[0, :, 1, :], tf=tf)
    conv_s = jnp.stack([state_conv[0, :, 1, :], jnp.concatenate([u_a, u_g], axis=-1)], axis=1)

    lead = lambda a: a[None]
    return (xp, xs.reshape(Bd, 1, D),
            lead(ak.reshape(B, S, H_A, dh_a)), lead(av.reshape(B, S, H_A, dh_a)), lead(ik_p.reshape(B, S, D_IDX)),
            lead(c_p), lead(n_p), lead(m_p[:, :, 0]),
            lead(mk_p.reshape(B, M, H_C, D // H_C)), lead(mv_p.reshape(B, M, H_C, D // H_C)), lead(conv_p),
            lead(ak_s.reshape(Bd, 1, H_A, dh_a)), lead(av_s.reshape(Bd, 1, H_A, dh_a)),
            lead(ik_s.reshape(Bd, 1, D_IDX)),
            lead(c_s), lead(n_s.reshape(Bd, H_M, dh_m)), lead(m_s[:, 0, :H_M]), lead(conv_s))
```

```python
import functools

import jax
import jax.numpy as jnp
import numpy as np
from jax import lax
from jax.experimental import pallas as pl
from jax.experimental.pallas import tpu as pltpu

F32 = jnp.float32
BF16 = jnp.bfloat16

H_M = 4
H_A = 4
H_IDX = 4
D_IDX = 64
H_C = 4
TOPK_MAX = 256
CONV_W = 3
ROPE_THETA = 10000.0
EPS = 1e-6
NEG_INF = float("-inf")
POS_INF = float("inf")

LANES = 128
VMEM_LIMIT = 56 * 1024 * 1024
N_BISECT = 20


def _cparams(sem):
    return pltpu.CompilerParams(dimension_semantics=sem, vmem_limit_bytes=VMEM_LIMIT)


def _nt(a, b):
    return lax.dot_general(a, b, (((1,), (1,)), ((), ())), preferred_element_type=F32)


def _tn(a, b):
    return lax.dot_general(a, b, (((0,), (0,)), ((), ())), preferred_element_type=F32)


def _mm(a, b):
    return jnp.dot(a, b, preferred_element_type=F32)


def _rms(x, g):
    ms = jnp.mean(x * x, axis=-1, keepdims=True)
    return x * lax.rsqrt(ms + EPS) * g


def _sigmoid(x):
    return 1.0 / (1.0 + jnp.exp(-x))


def _in_proj_kernel(x_ref, nm_ref, w_ref, bias_ref, qn_ref, kn_ref, tab_ref,
                    mq_ref, mk_ref, mv_ref, mo_ref, aqb_ref, ak_ref, av_ref, akb_ref, avb_ref,
                    iqb_ref, misc_ref, *, mix_m, mix_a, dh_m, dh_a):
    h = _rms(x_ref[...], nm_ref[...]).astype(BF16)

    def proj(lo, width):
        return _mm(h, w_ref[:, lo:lo + width])

    o_mq, o_mk, o_mv, o_mo = 0, mix_m, 2 * mix_m, 3 * mix_m
    o_aq = 4 * mix_m
    o_ak = o_aq + mix_a
    o_av = o_ak + mix_a
    o_iq = o_av + mix_a
    o_tail = o_iq + H_IDX * D_IDX

    mq_ref[...] = proj(o_mq, mix_m)
    mk_ref[...] = proj(o_mk, mix_m) * (dh_m ** -0.5)
    mv_ref[...] = proj(o_mv, mix_m)
    mo_ref[...] = proj(o_mo, mix_m)

    cos_a = tab_ref[:, 0:LANES]
    sin_a = tab_ref[:, LANES:2 * LANES]

    def norm_rope(z, g_ref):
        outs = []
        for hh in range(mix_a // dh_a):
            zh = _rms(z[:, hh * dh_a:(hh + 1) * dh_a], g_ref[...])
            outs.append(zh * cos_a + pltpu.roll(zh, dh_a // 2, 1) * sin_a)
        return outs

    aq = norm_rope(proj(o_aq, mix_a), qn_ref)
    aqb_ref[...] = jnp.concatenate(aq, axis=1).astype(BF16)
    ak = norm_rope(proj(o_ak, mix_a), kn_ref)
    av = proj(o_av, mix_a)
    for hh in range(mix_a // dh_a):
        ak_ref[:, hh, :] = ak[hh]
        av_ref[:, hh, :] = av[:, hh * dh_a:(hh + 1) * dh_a]
    akb_ref[...] = jnp.concatenate(ak, axis=1).astype(BF16)
    avb_ref[...] = av.astype(BF16)

    c_i = tab_ref[:, 2 * LANES:3 * LANES]
    s1_i = tab_ref[:, 3 * LANES:4 * LANES]
    s2_i = tab_ref[:, 4 * LANES:5 * LANES]
    ziq = proj(o_iq, H_IDX * D_IDX)
    cols = []
    for c in range(H_IDX * D_IDX // LANES):
        zc = ziq[:, c * LANES:(c + 1) * LANES]
        cols.append(zc * c_i + pltpu.roll(zc, D_IDX // 2, 1) * s1_i
                    + pltpu.roll(zc, LANES - D_IDX // 2, 1) * s2_i)
    iqb_ref[...] = jnp.concatenate(cols, axis=1).astype(BF16)

    c_t = tab_ref[:, 5 * LANES:6 * LANES]
    s1_t = tab_ref[:, 6 * LANES:7 * LANES]
    s2_t = tab_ref[:, 7 * LANES:8 * LANES]
    zt = proj(o_tail, LANES) + bias_ref[...]
    zt = zt * c_t + pltpu.roll(zt, D_IDX // 2, 1) * s1_t + pltpu.roll(zt, LANES - D_IDX // 2, 1) * s2_t
    lane = lax.broadcasted_iota(jnp.int32, zt.shape, 1)
    f_lo = D_IDX + H_IDX + H_M
    log_sig = jnp.minimum(zt, 0.0) - jnp.log(1.0 + jnp.exp(-jnp.abs(zt)))
    misc_ref[...] = jnp.where((lane >= f_lo) & (lane < f_lo + H_M), log_sig, zt)


def _in_proj(x2d, norm_mix, w_r, bias_tail, q_norm, k_norm, tab, *, tm, tab_tiles, mix_m, mix_a):
    rows, d = x2d.shape
    dh_m = mix_m // H_M
    dh_a = mix_a // H_A
    nw = w_r.shape[1]
    grid = (rows // tm,)
    row_spec = lambda wdt: pl.BlockSpec((tm, wdt), lambda i: (i, 0))
    const = lambda shp: pl.BlockSpec(shp, lambda i: (0, 0))
    out_shapes = [
        jax.ShapeDtypeStruct((rows, mix_m), F32),
        jax.ShapeDtypeStruct((rows, mix_m), F32),
        jax.ShapeDtypeStruct((rows, mix_m), F32),
        jax.ShapeDtypeStruct((rows, mix_m), F32),
        jax.ShapeDtypeStruct((rows, mix_a), BF16),
        jax.ShapeDtypeStruct((rows, H_A, dh_a), F32),
        jax.ShapeDtypeStruct((rows, H_A, dh_a), F32),
        jax.ShapeDtypeStruct((rows, mix_a), BF16),
        jax.ShapeDtypeStruct((rows, mix_a), BF16),
        jax.ShapeDtypeStruct((rows, H_IDX * D_IDX), BF16),
        jax.ShapeDtypeStruct((rows, LANES), F32),
    ]
    head_spec = pl.BlockSpec((tm, H_A, dh_a), lambda i: (i, 0, 0))
    out_specs = ([row_spec(mix_m)] * 4 + [row_spec(mix_a), head_spec, head_spec, row_spec(mix_a), row_spec(mix_a)]
                 + [row_spec(H_IDX * D_IDX), row_spec(LANES)])
    return pl.pallas_call(
        functools.partial(_in_proj_kernel, mix_m=mix_m, mix_a=mix_a, dh_m=dh_m, dh_a=dh_a),
        grid=grid,
        in_specs=[row_spec(d), const((1, d)), const((d, nw)), const((1, LANES)),
                  const((1, dh_a)), const((1, dh_a)),
                  pl.BlockSpec((tm, 8 * LANES), lambda i: (i % tab_tiles, 0))],
        out_specs=out_specs,
        out_shape=out_shapes,
        compiler_params=_cparams(("parallel",)),
        name="in_proj",
    )(x2d, norm_mix, w_r, bias_tail, q_norm, k_norm, tab)


def _mlstm_prompt_kernel(q_ref, k_ref, v_ref, o_ref, grow_ref, gcol_ref, gain_ref,
                         y_ref, c_ref, n_ref, m_ref, cs_ref, ns_ref, ms_ref, *, chunk, d):
    c_idx = pl.program_id(1)
    L = chunk
    row_i = lax.broadcasted_iota(jnp.int32, (L, L), 0)
    col_i = lax.broadcasted_iota(jnp.int32, (L, L), 1)
    tril = col_i <= row_i
    triu = row_i <= col_i

    @pl.when(c_idx == 0)
    def _():
        cs_ref[...] = jnp.zeros_like(cs_ref)
        ns_ref[...] = jnp.zeros_like(ns_ref)
        ms_ref[...] = jnp.zeros_like(ms_ref)

    gr = grow_ref[0, 0]
    gc = gcol_ref[0]
    for hd in range(H_M):
        hs = slice(hd * d, (hd + 1) * d)
        C = cs_ref[hd]
        n = ns_ref[hd, 0:1, :]
        m = ms_ref[hd, 0:1, 0:1]
        q = q_ref[0, :, hs]
        k = k_ref[0, :, hs]
        v = v_ref[0, :, hs]
        o = o_ref[0, :, hs]
        li_r = gr[hd:hd + 1, :]
        lf_r = gr[H_M + hd:H_M + hd + 1, :]
        li_c = gc[:, hd:hd + 1]
        lf_c = gc[:, H_M + hd:H_M + hd + 1]
        b_c = jnp.sum(jnp.where(tril, lf_r, 0.0), axis=1, keepdims=True)
        b_r = jnp.sum(jnp.where(triu, lf_c, 0.0), axis=0, keepdims=True)
        logd = jnp.where(tril, b_c - b_r + li_r, NEG_INF)
        inter = b_c + m
        m_t = jnp.maximum(inter, jnp.max(logd, axis=1, keepdims=True))
        qb = q.astype(BF16)
        kb = k.astype(BF16)
        s = _nt(qb, kb) * jnp.exp(logd - m_t)
        g_inter = jnp.exp(inter - m_t)
        num = g_inter * _nt(qb, C.astype(BF16)) + _mm(s.astype(BF16), v.astype(BF16))
        den = g_inter * jnp.sum(q * n, axis=1, keepdims=True) + jnp.sum(s, axis=1, keepdims=True)
        h = num / jnp.maximum(jnp.abs(den), jnp.exp(-m_t))
        m_new = m_t[L - 1:L, :]
        b_last = b_c[L - 1:L, :]
        g_prev = jnp.exp(b_last + m - m_new)
        w_c = jnp.exp(b_last - b_c + li_c - m_new)
        cs_ref[hd] = g_prev * C + _tn((v * w_c).astype(BF16), kb)
        ns_ref[hd, 0:1, :] = g_prev * n + jnp.sum(k * w_c, axis=0, keepdims=True)
        ms_ref[hd, 0:1, :] = jnp.broadcast_to(m_new, (1, LANES))
        y_ref[0, :, hs] = _sigmoid(o) * _rms(h, gain_ref[:, hs])

    @pl.when(c_idx == pl.num_programs(1) - 1)
    def _():
        c_ref[0] = cs_ref[...]
        for hd in range(H_M):
            n_ref[0, hd:hd + 1, :] = ns_ref[hd, 0:1, :]
            m_ref[0, hd:hd + 1, :] = ms_ref[hd, 0:1, :]


def _mlstm_prompt(mq, mk, mv, mo, grow, gcol, gain, *, chunk):
    B, S, mix_m = mq.shape
    d = mix_m // H_M
    n_chunks = S // chunk
    seq = pl.BlockSpec((1, chunk, mix_m), lambda b, c: (b, c, 0))
    return pl.pallas_call(
        functools.partial(_mlstm_prompt_kernel, chunk=chunk, d=d),
        grid=(B, n_chunks),
        in_specs=[seq, seq, seq, seq,
                  pl.BlockSpec((1, 1, 2 * H_M, chunk), lambda b, c: (b, c, 0, 0)),
                  pl.BlockSpec((1, chunk, 2 * H_M), lambda b, c: (b, c, 0)),
                  pl.BlockSpec((1, mix_m), lambda b, c: (0, 0))],
        out_specs=[seq,
                   pl.BlockSpec((1, H_M, d, d), lambda b, c: (b, 0, 0, 0)),
                   pl.BlockSpec((1, H_M, d), lambda b, c: (b, 0, 0)),
                   pl.BlockSpec((1, H_M, LANES), lambda b, c: (b, 0, 0))],
        out_shape=[jax.ShapeDtypeStruct((B, S, mix_m), F32),
                   jax.ShapeDtypeStruct((B, H_M, d, d), F32),
                   jax.ShapeDtypeStruct((B, H_M, d), F32),
                   jax.ShapeDtypeStruct((B, H_M, LANES), F32)],
        scratch_shapes=[pltpu.VMEM((H_M, d, d), F32), pltpu.VMEM((H_M, 8, d), F32),
                        pltpu.VMEM((H_M, 8, LANES), F32)],
        compiler_params=_cparams(("parallel", "arbitrary")),
        name="mlstm_prompt",
    )(mq, mk, mv, mo, grow, gcol, gain)


def _mlstm_sample_kernel(q_ref, k_ref, v_ref, o_ref, gs_ref, c_ref, n_ref, gain_ref,
                         y_ref, co_ref, no_ref, mo_ref, *, d):
    gs = gs_ref[0]
    eye = (lax.broadcasted_iota(jnp.int32, (d, d), 0) == lax.broadcasted_iota(jnp.int32, (d, d), 1))
    lane = lax.broadcasted_iota(jnp.int32, (1, LANES), 1)
    m_out = jnp.zeros((1, LANES), F32)
    for h in range(H_M):
        sl = slice(h * d, (h + 1) * d)
        q = q_ref[0, :, sl]
        k = k_ref[0, :, sl]
        v = v_ref[0, :, sl]
        o = o_ref[0, :, sl]
        li = gs[:, h:h + 1]
        lf = gs[:, H_M + h:H_M + h + 1]
        m = gs[:, 2 * H_M + h:2 * H_M + h + 1]
        C = c_ref[0, h]
        n = n_ref[0, :, sl]
        inter = lf + m
        m_t = jnp.maximum(inter, li)
        s = jnp.sum(q * k, axis=1, keepdims=True) * jnp.exp(li - m_t)
        g = jnp.exp(inter - m_t)
        q8 = jnp.broadcast_to(q, (8, d)).astype(BF16)
        cq = _nt(q8, C.astype(BF16))[0:1, :]
        num = g * cq + s * v
        den = g * jnp.sum(n * q, axis=1, keepdims=True) + s
        hh = num / jnp.maximum(jnp.abs(den), jnp.exp(-m_t))
        w = jnp.exp(li - m_t)
        v_col = jnp.sum(jnp.where(eye, v, 0.0), axis=1, keepdims=True)
        co_ref[0, h] = g * C + (w * v_col) * k
        no_ref[0, :, sl] = g * n + w * k
        m_out = jnp.where(lane == h, m_t, m_out)
        y_ref[0, :, sl] = _sigmoid(o) * _rms(hh, gain_ref[:, sl])
    mo_ref[0] = m_out


def _mlstm_sample(mq, mk, mv, mo, gs, c_state, n_state, gain):
    Bd, _, mix_m = mq.shape
    d = mix_m // H_M
    row = pl.BlockSpec((1, 1, mix_m), lambda b: (b, 0, 0))
    return pl.pallas_call(
        functools.partial(_mlstm_sample_kernel, d=d),
        grid=(Bd,),
        in_specs=[row, row, row, row,
                  pl.BlockSpec((1, 1, 3 * H_M), lambda b: (b, 0, 0)),
                  pl.BlockSpec((1, H_M, d, d), lambda b: (b, 0, 0, 0)),
                  row,
                  pl.BlockSpec((1, mix_m), lambda b: (0, 0))],
        out_specs=[row,
                   pl.BlockSpec((1, H_M, d, d), lambda b: (b, 0, 0, 0)),
                   row,
                   pl.BlockSpec((1, 1, LANES), lambda b: (b, 0, 0))],
        out_shape=[jax.ShapeDtypeStruct((Bd, 1, mix_m), F32),
                   jax.ShapeDtypeStruct((Bd, H_M, d, d), F32),
                   jax.ShapeDtypeStruct((Bd, 1, mix_m), F32),
                   jax.ShapeDtypeStruct((Bd, 1, LANES), F32)],
        compiler_params=_cparams(("parallel",)),
        name="mlstm_sample",
    )(mq, mk, mv, mo, gs, c_state, n_state, gain)


def _dsa_prompt_kernel(iq_ref, misc_ref, ikt_ref, aq_ref, ak_ref, av_ref, ya_ref, sc_ref, acc_ref,
                       *, tq, w, topk, dh, scale):
    i = pl.program_id(1)
    nk = ((i + 1) * tq + w - 1) // w
    kf = float(topk)
    nsub = w // LANES

    q_pos = i * tq + lax.broadcasted_iota(jnp.int32, (tq, 1), 0)
    lane_w = lax.broadcasted_iota(jnp.int32, (1, w), 1)
    iq = iq_ref[0]
    iq_h = [iq[:, h * D_IDX:(h + 1) * D_IDX] for h in range(H_IDX)]
    misc = misc_ref[0]
    w_h = [misc[:, D_IDX + h:D_IDX + h + 1] for h in range(H_IDX)]

    def score_body(c, carry):
        rmax, rmin = carry
        ikc = ikt_ref[0, c]
        score = jnp.zeros((tq, w), F32)
        for h in range(H_IDX):
            score = score + w_h[h] * jnp.maximum(_mm(iq_h[h], ikc), 0.0)
        valid = (c * w + lane_w) <= q_pos
        sc_ref[c] = jnp.where(valid, score, NEG_INF)
        rmax = jnp.maximum(rmax, jnp.max(jnp.where(valid, score, NEG_INF), axis=1, keepdims=True))
        rmin = jnp.minimum(rmin, jnp.min(jnp.where(valid, score, POS_INF), axis=1, keepdims=True))
        return rmax, rmin

    rmax, rmin = lax.fori_loop(0, nk, score_body,
                               (jnp.full((tq, 1), NEG_INF, F32), jnp.full((tq, 1), POS_INF, F32)))

    ge = lambda x, t: x >= t
    gt = lambda x, t: x > t

    rh = min(tq, LANES)
    groups = [pl.ds(r0, rh) for r0 in range(0, tq, rh)]
    part = lambda a: [a[r0:r0 + rh] for r0 in range(0, tq, rh)]

    def pass_acc(rows, fn, init, combine):
        def body(c, acc):
            x = sc_ref[c, rows, :]
            for j in range(nsub):
                acc = combine(acc, fn(x[:, j * LANES:(j + 1) * LANES]))
            return acc
        return lax.fori_loop(0, nk, body, jnp.full((rh, LANES), init, F32))

    def count_acc(rows, pred, thr):
        thr_b = jnp.broadcast_to(thr, (rh, LANES))
        return pass_acc(rows, lambda x: jnp.where(pred(x, thr_b), 1.0, 0.0), 0.0, jnp.add)

    def count(rows, pred, thr):
        return jnp.sum(count_acc(rows, pred, thr), axis=1, keepdims=True)

    def min_where(rows, pred, thr):
        thr_b = jnp.broadcast_to(thr, (rh, LANES))
        acc = pass_acc(rows, lambda x: jnp.where(pred(x, thr_b), x, POS_INF), POS_INF, jnp.minimum)
        return jnp.min(acc, axis=1, keepdims=True)

    def bis_body(_, carry):
        los, his, clos = carry
        mids = [0.5 * (lo + hi) for lo, hi in zip(los, his)]
        accs = [count_acc(rows, ge, mid) for rows, mid in zip(groups, mids)]
        cms = [jnp.sum(a, axis=1, keepdims=True) for a in accs]
        oks = [cm >= kf for cm in cms]
        return (tuple(jnp.where(ok, mid, lo) for ok, mid, lo in zip(oks, mids, los)),
                tuple(jnp.where(ok, hi, mid) for ok, mid, hi in zip(oks, mids, his)),
                tuple(jnp.where(ok, cm, cl) for ok, cm, cl in zip(oks, cms, clos)))

    los, _, clos = lax.fori_loop(
        0, N_BISECT, bis_body,
        (tuple(part(rmin)), tuple(part(rmax + jnp.abs(rmax) + 1.0)), tuple(part((q_pos + 1).astype(F32)))))

    def finish_rows(rows, qp, rmin_h, lo, c_lo):
        active = (qp + 1) > topk
        unresolved = jnp.max(jnp.where(active & (c_lo != kf), 1.0, 0.0)) > 0.5

        @pl.when(jnp.logical_not(unresolved))
        def _():
            thr = jnp.where(active, lo, rmin_h)

            def body(c, _):
                sc_ref[c, rows, :] = jnp.where(sc_ref[c, rows, :] >= thr, 0.0, NEG_INF)
                return 0
            lax.fori_loop(0, nk, body, 0)

        @pl.when(unresolved)
        def _():
            tau = min_where(rows, ge, lo)
            g = count(rows, gt, tau)

            def undone(tau, g):
                return active & (g >= kf)

            def fix_cond(st):
                return jnp.max(jnp.where(undone(*st), 1.0, 0.0)) > 0.5

            def fix_body(st):
                tau, g = st
                nd = undone(tau, g)
                tau2 = jnp.where(nd, min_where(rows, gt, tau), tau)
                return tau2, jnp.where(nd, count(rows, gt, tau2), g)

            tau, g = lax.while_loop(fix_cond, fix_body, (tau, g))
            tau_b = jnp.broadcast_to(jnp.where(active, tau, rmin_h), (rh, LANES))
            need_b = jnp.broadcast_to(jnp.where(active, kf - g, 1e9), (rh, LANES))
            r_i = lax.broadcasted_iota(jnp.int32, (LANES, 2 * LANES), 0)
            c_i = lax.broadcasted_iota(jnp.int32, (LANES, 2 * LANES), 1)
            tri_ones = ((r_i <= c_i) | (c_i >= LANES)).astype(BF16)

            def body(c, run):
                x = sc_ref[c, rows, :]
                outs = []
                for j in range(nsub):
                    xj = x[:, j * LANES:(j + 1) * LANES]
                    is_eq = xj == tau_b
                    cnt2 = _mm(jnp.where(is_eq, 1.0, 0.0).astype(BF16), tri_ones)
                    sel = (xj > tau_b) | (is_eq & (cnt2[:, :LANES] + run <= need_b))
                    outs.append(jnp.where(sel, 0.0, NEG_INF))
                    run = run + cnt2[:, LANES:]
                sc_ref[c, rows, :] = jnp.concatenate(outs, axis=1)
                return run
            lax.fori_loop(0, nk, body, jnp.zeros((rh, LANES), F32))

    for rows, qp, rmin_h, lo, c_lo in zip(groups, part(q_pos), part(rmin), los, clos):
        finish_rows(rows, qp, rmin_h, lo, c_lo)

    aq = aq_ref[0]
    q_heads = [aq[:, h * dh:(h + 1) * dh] for h in range(H_A)]
    acc_ref[...] = jnp.zeros_like(acc_ref)

    def att_body(c, carry):
        ms, ls = carry
        k0 = pl.multiple_of(c * w, w)
        bias = sc_ref[c]
        ms_new, ls_new = [], []
        for h in range(H_A):
            hs = slice(h * dh, (h + 1) * dh)
            s = _nt(q_heads[h], ak_ref[0, pl.ds(k0, w), hs]) * scale + bias
            m_new = jnp.maximum(ms[h], jnp.max(s, axis=1, keepdims=True))
            m_safe = jnp.where(m_new == NEG_INF, 0.0, m_new)
            alpha = jnp.exp(ms[h] - m_safe)
            p = jnp.exp(s - m_safe)
            ls_new.append(alpha * ls[h] + jnp.sum(p, axis=1, keepdims=True))
            acc_ref[h] = alpha * acc_ref[h] + _mm(p.astype(BF16), av_ref[0, pl.ds(k0, w), hs])
            ms_new.append(m_new)
        return tuple(ms_new), tuple(ls_new)

    _, ls = lax.fori_loop(0, nk, att_body,
                          (tuple(jnp.full((tq, 1), NEG_INF, F32) for _ in range(H_A)),
                           tuple(jnp.zeros((tq, 1), F32) for _ in range(H_A))))
    for h in range(H_A):
        ya_ref[0, :, h * dh:(h + 1) * dh] = acc_ref[h] / ls[h]


def _dsa_prompt(iqb, misc, ikt, aqb, akb, avb, *, tq, w, topk):
    B, S, mix_a = aqb.shape
    dh = mix_a // H_A
    nq = S // tq
    nw = S // w
    return pl.pallas_call(
        functools.partial(_dsa_prompt_kernel, tq=tq, w=w, topk=topk, dh=dh, scale=dh ** -0.5),
        grid=(B, nq),
        in_specs=[pl.BlockSpec((1, tq, H_IDX * D_IDX), lambda b, i: (b, i, 0)),
                  pl.BlockSpec((1, tq, LANES), lambda b, i: (b, i, 0)),
                  pl.BlockSpec((1, nw, D_IDX, w), lambda b, i: (b, 0, 0, 0)),
                  pl.BlockSpec((1, tq, mix_a), lambda b, i: (b, i, 0)),
                  pl.BlockSpec((1, S, mix_a), lambda b, i: (b, 0, 0)),
                  pl.BlockSpec((1, S, mix_a), lambda b, i: (b, 0, 0))],
        out_specs=pl.BlockSpec((1, tq, mix_a), lambda b, i: (b, i, 0)),
        out_shape=jax.ShapeDtypeStruct((B, S, mix_a), F32),
        scratch_shapes=[pltpu.VMEM((nw, tq, w), F32), pltpu.VMEM((H_A, tq, dh), F32)],
        compiler_params=_cparams(("parallel", "arbitrary")),
        name="dsa_prompt",
    )(iqb, misc, ikt, aqb, akb, avb)


def _dsa_sample_select_kernel(pt_ref, iq_ref, w_ref, ikn_ref, ptv_ref, pool_ref, rows_ref, flag_ref,
                              ikbuf, sem, sc_ref, xn_ref, slot_ref, phys_ref,
                              *, n_pages, page, topk, cw):
    nb = iq_ref.shape[0]
    past = n_pages * page
    kf = float(topk)
    n_cw = past // cw

    def page_copy(bb, p, slot):
        return pltpu.make_async_copy(pool_ref.at[pt_ref[bb, p]],
                                     ikbuf.at[slot, :, pl.ds(p * page, page)],
                                     sem.at[slot])

    def start_all(bb, slot):
        def body(p, _):
            page_copy(bb, p, slot).start()
            return 0
        lax.fori_loop(0, n_pages, body, 0)

    start_all(0, 0)

    def score_body(b, _):
        slot = b % 2

        @pl.when(b + 1 < nb)
        def _():
            start_all(b + 1, 1 - slot)

        def wait_body(p, _):
            page_copy(b, p, slot).wait()
            return 0
        lax.fori_loop(0, n_pages, wait_body, 0)

        iq8 = iq_ref[b]
        w8 = w_ref[b]
        s8 = _mm(iq8, ikbuf[slot].astype(BF16))
        sc_ref[pl.ds(b, 1), :] = jnp.sum(w8 * jnp.maximum(s8, 0.0), axis=0, keepdims=True)
        ikn = ikn_ref[b].astype(BF16).astype(F32)
        sn8 = jnp.sum(iq8.astype(F32) * ikn, axis=1, keepdims=True)
        xn_b = jnp.sum(w8 * jnp.maximum(sn8, 0.0), axis=0, keepdims=True)
        xn_ref[pl.ds(b, 1), :] = jnp.broadcast_to(xn_b, (1, LANES))
        return 0

    lax.fori_loop(0, nb, score_body, 0)

    x = sc_ref[...]
    xn = xn_ref[:, 0:1]

    def cnt(mask_row, mask_new):
        return (jnp.sum(jnp.where(mask_row, 1.0, 0.0), axis=1, keepdims=True)
                + jnp.where(mask_new, 1.0, 0.0))

    rmax = jnp.maximum(jnp.max(x, axis=1, keepdims=True), xn)
    rmin = jnp.minimum(jnp.min(x, axis=1, keepdims=True), xn)
    hi0 = rmax + jnp.abs(rmax) + 1.0

    def bis_body(_, carry):
        lo, hi = carry
        mid = 0.5 * (lo + hi)
        ok = cnt(x >= mid, xn >= mid) >= kf
        return jnp.where(ok, mid, lo), jnp.where(ok, hi, mid)

    lo, _ = lax.fori_loop(0, N_BISECT, bis_body, (rmin, hi0))

    def min_where(mask_row, mask_new):
        return jnp.minimum(jnp.min(jnp.where(mask_row, x, POS_INF), axis=1, keepdims=True),
                           jnp.where(mask_new, xn, POS_INF))

    tau = min_where(x >= lo, xn >= lo)
    g = cnt(x > tau, xn > tau)

    def fix_cond(st):
        tau, g = st
        return jnp.max(jnp.where(g >= kf, 1.0, 0.0)) > 0.5

    def fix_body(st):
        tau, g = st
        tau2 = jnp.where(g >= kf, min_where(x > tau, xn > tau), tau)
        return tau2, cnt(x > tau2, xn > tau2)

    tau, g = lax.while_loop(fix_cond, fix_body, (tau, g))
    need = kf - g

    tri = (lax.broadcasted_iota(jnp.int32, (cw, cw), 0)
           < lax.broadcasted_iota(jnp.int32, (cw, cw), 1)).astype(BF16)

    def excl_prefix(flag):
        outs = []
        run = jnp.zeros((nb, 1), F32)
        for c in range(n_cw):
            f = flag[:, c * cw:(c + 1) * cw]
            outs.append(_mm(f.astype(BF16), tri) + run)
            run = run + jnp.sum(f, axis=1, keepdims=True)
        return jnp.concatenate(outs, axis=1), run

    is_eq = x == tau
    pre_eq, n_eq_past = excl_prefix(jnp.where(is_eq, 1.0, 0.0))
    sel = (x > tau) | (is_eq & (pre_eq < need))
    new_sel = (xn > tau) | ((xn == tau) & (n_eq_past < need))
    slot, _ = excl_prefix(jnp.where(sel, 1.0, 0.0))
    slot_ref[...] = jnp.where(sel, slot, -1.0)

    ptv = ptv_ref[...]
    jrow = lax.broadcasted_iota(jnp.int32, (1, past), 1)
    prow = lax.broadcasted_iota(jnp.int32, (n_pages, 1), 0)
    expand = ((jrow >= prow * page) & (jrow < (prow + 1) * page)).astype(BF16)
    digit_bits = 6
    pt_hi = _mm((ptv >> digit_bits).astype(F32).astype(BF16), expand)
    pt_lo = _mm((ptv & ((1 << digit_bits) - 1)).astype(F32).astype(BF16), expand)
    pidx = lax.broadcasted_iota(jnp.int32, (8, n_pages), 1).astype(F32).astype(BF16)
    pg = _mm(pidx, expand)[0:1, :]
    phys_ref[...] = (pt_hi * (1 << digit_bits) + pt_lo) * page + (jrow.astype(F32) - pg * page)

    slot_col = lax.broadcasted_iota(jnp.int32, (topk, 1), 0).astype(F32)
    lane_b = lax.broadcasted_iota(jnp.int32, (1, LANES), 1)

    def extract_body(b, out):
        srow = slot_ref[pl.ds(b, 1), :]
        frow = phys_ref[pl.ds(b, 1), :]
        acc = jnp.zeros((topk, LANES), F32)
        for c in range(past // LANES):
            cs = slice(c * LANES, (c + 1) * LANES)
            acc = acc + jnp.where(srow[:, cs] == slot_col, frow[:, cs], 0.0)
        return jnp.where(lane_b == b, jnp.sum(acc, axis=1, keepdims=True), out)

    out = lax.fori_loop(0, nb, extract_body, jnp.zeros((topk, LANES), F32))
    rows_ref[...] = out.astype(jnp.int32)
    flag_ref[...] = jnp.broadcast_to(jnp.where(new_sel, 1, 0), (nb, LANES)).astype(jnp.int32)


def _dsa_sample_select(page_table, iq8, w8, ik_new, pool_ik_t, *, topk, cw):
    Bd, n_pages = page_table.shape
    n_pool, d_idx, page = pool_ik_t.shape
    past = n_pages * page
    assert Bd <= LANES and n_pool <= 64 * 256
    full = lambda shp: pl.BlockSpec(shp, lambda i, pt: (0,) * len(shp))
    grid_spec = pltpu.PrefetchScalarGridSpec(
        num_scalar_prefetch=1,
        grid=(1,),
        in_specs=[full((Bd, 8, d_idx)), full((Bd, 8, 1)), full((Bd, 1, d_idx)), full((Bd, n_pages)),
                  pl.BlockSpec(memory_space=pl.ANY)],
        out_specs=[full((topk, LANES)), full((Bd, LANES))],
        scratch_shapes=[pltpu.VMEM((2, d_idx, past), F32),
                        pltpu.SemaphoreType.DMA((2,)),
                        pltpu.VMEM((Bd, past), F32),
                        pltpu.VMEM((Bd, LANES), F32),
                        pltpu.VMEM((Bd, past), F32),
                        pltpu.VMEM((Bd, past), F32)],
    )
    return pl.pallas_call(
        functools.partial(_dsa_sample_select_kernel, n_pages=n_pages, page=page, topk=topk, cw=cw),
        grid_spec=grid_spec,
        out_shape=[jax.ShapeDtypeStruct((topk, LANES), jnp.int32),
                   jax.ShapeDtypeStruct((Bd, LANES), jnp.int32)],
        compiler_params=_cparams(("arbitrary",)),
        name="dsa_sample_select",
    )(page_table, iq8, w8, ik_new, page_table, pool_ik_t)


def _dsa_sample_attend_kernel(rows_ref, flag_ref, aq_ref, knew_ref, vnew_ref, kpool_ref, vpool_ref,
                              ya_ref, kbuf, vbuf, sem, *, topk, dh, scale):
    b = pl.program_id(0)
    nb = pl.num_programs(0)

    def row_copies(bb, t, slot):
        r = rows_ref[bb, t]
        dst = pl.ds(t * H_A, H_A)
        return (pltpu.make_async_copy(kpool_ref.at[r], kbuf.at[slot, dst, :], sem.at[0, slot]),
                pltpu.make_async_copy(vpool_ref.at[r], vbuf.at[slot, dst, :], sem.at[1, slot]))

    def start_all(bb, slot):
        def body(t, _):
            ck, cv = row_copies(bb, t, slot)
            ck.start()
            cv.start()
            return 0
        lax.fori_loop(0, topk, body, 0)

    slot = b % 2

    @pl.when(b == 0)
    def _():
        start_all(0, 0)

    @pl.when(b + 1 < nb)
    def _():
        start_all(b + 1, 1 - slot)

    def wait_body(t, _):
        ck, cv = row_copies(b, t, slot)
        ck.wait()
        cv.wait()
        return 0
    lax.fori_loop(0, topk, wait_body, 0)

    take_new = (lax.broadcasted_iota(jnp.int32, (topk, 1), 0) == topk - 1) & (flag_ref[b] > 0)
    aq = aq_ref[0]
    for h in range(H_A):
        hs = slice(h * dh, (h + 1) * dh)
        kh = kbuf[slot, pl.ds(h, topk, stride=H_A), :]
        vh = vbuf[slot, pl.ds(h, topk, stride=H_A), :]
        kh = jnp.where(take_new, knew_ref[0, h:h + 1, :], kh).astype(BF16)
        vh = jnp.where(take_new, vnew_ref[0, h:h + 1, :], vh).astype(BF16)
        q8 = jnp.broadcast_to(aq[:, hs], (8, dh))
        s = _nt(q8, kh) * scale
        m = jnp.max(s, axis=1, keepdims=True)
        p = jnp.exp(s - m)
        p = p / jnp.sum(p, axis=1, keepdims=True)
        ya_ref[0, :, hs] = _mm(p.astype(BF16), vh)[0:1, :]


def _dsa_sample_attend(rows, flags, aqb, k_new, v_new, pool_k, pool_v, *, topk):
    Bd, _, mix_a = aqb.shape
    dh = pool_k.shape[2]
    new_spec = pl.BlockSpec((1, H_A, dh), lambda b, r, f: (b, 0, 0))
    grid_spec = pltpu.PrefetchScalarGridSpec(
        num_scalar_prefetch=2,
        grid=(Bd,),
        in_specs=[pl.BlockSpec((1, 1, mix_a), lambda b, r, f: (b, 0, 0)),
                  new_spec, new_spec,
                  pl.BlockSpec(memory_space=pl.ANY),
                  pl.BlockSpec(memory_space=pl.ANY)],
        out_specs=pl.BlockSpec((1, 1, mix_a), lambda b, r, f: (b, 0, 0)),
        scratch_shapes=[pltpu.VMEM((2, topk * H_A, dh), F32),
                        pltpu.VMEM((2, topk * H_A, dh), F32),
                        pltpu.SemaphoreType.DMA((2, 2))],
    )
    return pl.pallas_call(
        functools.partial(_dsa_sample_attend_kernel, topk=topk, dh=dh, scale=dh ** -0.5),
        grid_spec=grid_spec,
        out_shape=jax.ShapeDtypeStruct((Bd, 1, mix_a), F32),
        compiler_params=_cparams(("arbitrary",)),
        name="dsa_sample_attend",
    )(rows, flags, aqb, k_new, v_new, pool_k, pool_v)


def _mem_kv_kernel(mem_ref, nm_ref, wk_ref, wv_ref, kn_ref, k_ref, v_ref, *, dh):
    hm = _rms(mem_ref[...], nm_ref[...]).astype(BF16)
    kk = _mm(hm, wk_ref[...])
    vv = _mm(hm, wv_ref[...])
    for h in range(H_C):
        hs = slice(h * dh, (h + 1) * dh)
        k_ref[:, h, :] = _rms(kk[:, hs], kn_ref[...])
        v_ref[:, h, :] = vv[:, hs]


def _mem_kv(mem2d, norm_mem, w_ck, w_cv, ck_norm, *, tm):
    rows, d = mem2d.shape
    dh = d // H_C
    row = pl.BlockSpec((tm, d), lambda i: (i, 0))
    heads = pl.BlockSpec((tm, H_C, dh), lambda i: (i, 0, 0))
    const = lambda shp: pl.BlockSpec(shp, lambda i: (0, 0))
    return pl.pallas_call(
        functools.partial(_mem_kv_kernel, dh=dh),
        grid=(rows // tm,),
        in_specs=[row, const((1, d)), const((d, d)), const((d, d)), const((1, dh))],
        out_specs=[heads, heads],
        out_shape=[jax.ShapeDtypeStruct((rows, H_C, dh), F32)] * 2,
        compiler_params=_cparams(("parallel",)),
        name="mem_kv",
    )(mem2d, norm_mem, w_ck, w_cv, ck_norm)


def _out_cq_kernel(x_ref, ym_ref, ya_ref, wo_ref, nc_ref, wq_ref, qn_ref, x1_ref, qc_ref, *, mix_m, dh):
    upd = (_mm(ym_ref[...].astype(BF16), wo_ref[0:mix_m, :])
           + _mm(ya_ref[...].astype(BF16), wo_ref[mix_m:, :]))
    x1 = x_ref[...] + upd
    x1_ref[...] = x1
    hq = _mm(_rms(x1, nc_ref[...]).astype(BF16), wq_ref[...])
    for h in range(H_C):
        hs = slice(h * dh, (h + 1) * dh)
        qc_ref[:, hs] = _rms(hq[:, hs], qn_ref[...]).astype(BF16)


def _out_cq(x2d, ym, ya, w_out, norm_cross, w_cq, cq_norm, *, tm):
    rows, d = x2d.shape
    mix_m = ym.shape[1]
    mix_a = ya.shape[1]
    dh = d // H_C
    row = lambda wdt: pl.BlockSpec((tm, wdt), lambda i: (i, 0))
    const = lambda shp: pl.BlockSpec(shp, lambda i: (0, 0))
    return pl.pallas_call(
        functools.partial(_out_cq_kernel, mix_m=mix_m, dh=dh),
        grid=(rows // tm,),
        in_specs=[row(d), row(mix_m), row(mix_a), const((mix_m + mix_a, d)), const((1, d)),
                  const((d, d)), const((1, dh))],
        out_specs=[row(d), row(d)],
        out_shape=[jax.ShapeDtypeStruct((rows, d), F32), jax.ShapeDtypeStruct((rows, d), BF16)],
        compiler_params=_cparams(("parallel",)),
        name="out_cq",
    )(x2d, ym, ya, w_out, norm_cross, w_cq, cq_norm)


def _cross_kernel(q_ref, k_hbm, v_hbm, o_ref, kv_buf, sem, *, dh, scale):
    b = pl.program_id(0)
    t = pl.program_id(1)
    nb = pl.num_programs(0)
    slot = b % 2

    def head_copies(bb, sl):
        cps = []
        for h in range(H_C):
            cps.append(pltpu.make_async_copy(k_hbm.at[bb, :, h, :], kv_buf.at[sl, 0, h], sem.at[sl]))
            cps.append(pltpu.make_async_copy(v_hbm.at[bb, :, h, :], kv_buf.at[sl, 1, h], sem.at[sl]))
        return cps

    @pl.when(t == 0)
    def _():
        @pl.when(b == 0)
        def _():
            for cp in head_copies(0, 0):
                cp.start()

        @pl.when(b + 1 < nb)
        def _():
            for cp in head_copies(b + 1, 1 - slot):
                cp.start()

        for cp in head_copies(b, slot):
            cp.wait()

    q = q_ref[0]
    rows = q.shape[0]
    if rows < 8:
        q = jnp.broadcast_to(q, (8, q.shape[1]))
    for h in range(H_C):
        hs = slice(h * dh, (h + 1) * dh)
        kb = kv_buf[slot, 0, h].astype(BF16)
        vb = kv_buf[slot, 1, h].astype(BF16)
        s = _nt(q[:, hs], kb) * scale
        m = jnp.max(s, axis=1, keepdims=True)
        p = jnp.exp(s - m)
        p = p / jnp.sum(p, axis=1, keepdims=True)
        o = _mm(p.astype(BF16), vb)
        o_ref[0, :, hs] = o[0:rows].astype(BF16)


def _cross(qc, mem_k, mem_v, *, tq):
    B, T, d = qc.shape
    M = mem_k.shape[1]
    dh = d // H_C
    return pl.pallas_call(
        functools.partial(_cross_kernel, dh=dh, scale=dh ** -0.5),
        grid=(B, T // tq),
        in_specs=[pl.BlockSpec((1, tq, d), lambda b, t: (b, t, 0)),
                  pl.BlockSpec(memory_space=pl.ANY),
                  pl.BlockSpec(memory_space=pl.ANY)],
        out_specs=pl.BlockSpec((1, tq, d), lambda b, t: (b, t, 0)),
        out_shape=jax.ShapeDtypeStruct((B, T, d), BF16),
        scratch_shapes=[pltpu.VMEM((2, 2, H_C, M, dh), F32), pltpu.SemaphoreType.DMA((2,))],
        compiler_params=_cparams(("arbitrary", "arbitrary")),
        name="cross_attn",
    )(qc, mem_k, mem_v)


def _gelu_tanh(x):
    return 0.5 * x * (1.0 + jnp.tanh(np.sqrt(2.0 / np.pi) * (x + 0.044715 * (x * x * x))))


def _ffn_front(x1_ref, o_ref, wco_ref, nf_ref, x2_ref, hb_ref, acc_ref):
    x2 = x1_ref[0] + _mm(o_ref[0], wco_ref[...])
    x2_ref[...] = x2
    hb_ref[...] = _rms(x2, nf_ref[...]).astype(BF16)
    acc_ref[...] = jnp.zeros_like(acc_ref)


def _ffn_prompt_kernel(x1_ref, o_ref, wco_ref, nf_ref, wua_ref, wug_ref, cwa_ref, cwg_ref,
                       cba_ref, cbg_ref, wd_ref, ha_ref, hg_ref,
                       y_ref, ca_ref, cg_ref, x2_ref, hb_ref, acc_ref, carry_ref, *, tm, rs):
    t = pl.program_id(1)
    j = pl.program_id(2)
    nj = pl.num_programs(2)

    @pl.when(j == 0)
    def _():
        _ffn_front(x1_ref, o_ref, wco_ref, nf_ref, x2_ref, hb_ref, acc_ref)

    @pl.when(t == 0)
    def _():
        carry_ref[j, 0, 6:8, :] = ha_ref[0]
        carry_ref[j, 1, 6:8, :] = hg_ref[0]

    rid = lax.broadcasted_iota(jnp.int32, (rs, 1), 0)

    def conv_part(hb, part, wu_ref, cw_ref, cb_ref):
        u = _mm(hb, wu_ref[...])
        p2 = carry_ref[j, part, 6:7, :]
        p1 = carry_ref[j, part, 7:8, :]
        um1 = jnp.where(rid == 0, p1, pltpu.roll(u, 1, 0))
        um2 = jnp.where(rid == 0, p2, jnp.where(rid == 1, p1, pltpu.roll(u, 2, 0)))
        carry_ref[j, part] = u[rs - 8:rs, :]
        return cb_ref[...] + um2 * cw_ref[0:1, :] + um1 * cw_ref[1:2, :] + u * cw_ref[2:3, :]

    def sub_body(r, _):
        r0 = pl.multiple_of(r * rs, rs)
        hb = hb_ref[pl.ds(r0, rs), :]
        a = conv_part(hb, 0, wua_ref, cwa_ref, cba_ref)
        g = conv_part(hb, 1, wug_ref, cwg_ref, cbg_ref)
        acc_ref[pl.ds(r0, rs), :] += _mm((_gelu_tanh(g) * a).astype(BF16), wd_ref[...])
        return 0

    lax.fori_loop(0, tm // rs, sub_body, 0)
    ca_ref[0, 0] = carry_ref[j, 0, 6:8, :]
    cg_ref[0, 0] = carry_ref[j, 1, 6:8, :]

    @pl.when(j == nj - 1)
    def _():
        y_ref[0] = x2_ref[...] + acc_ref[...]


def _ffn_prompt(x1, o, w_co, norm_ffn, w_up, conv_w, conv_b, w_down, hist, *, tm, tf):
    B, T, d = x1.shape
    d_ff = w_down.shape[0]
    nj = d_ff // tf
    nt = T // tm
    idx3 = lambda b, t, j: (b, t, 0)
    c2 = lambda shp: pl.BlockSpec(shp, lambda b, t, j: (0, 0))
    return pl.pallas_call(
        functools.partial(_ffn_prompt_kernel, tm=tm, rs=min(256, tm)),
        grid=(B, nt, nj),
        in_specs=[pl.BlockSpec((1, tm, d), idx3), pl.BlockSpec((1, tm, d), idx3),
                  c2((d, d)), c2((1, d)),
                  pl.BlockSpec((d, tf), lambda b, t, j: (0, j)),
                  pl.BlockSpec((d, tf), lambda b, t, j: (0, nj + j)),
                  pl.BlockSpec((CONV_W, tf), lambda b, t, j: (0, j)),
                  pl.BlockSpec((CONV_W, tf), lambda b, t, j: (0, nj + j)),
                  pl.BlockSpec((1, tf), lambda b, t, j: (0, j)),
                  pl.BlockSpec((1, tf), lambda b, t, j: (0, nj + j)),
                  pl.BlockSpec((tf, d), lambda b, t, j: (j, 0)),
                  pl.BlockSpec((1, CONV_W - 1, tf), lambda b, t, j: (b, 0, j)),
                  pl.BlockSpec((1, CONV_W - 1, tf), lambda b, t, j: (b, 0, nj + j))],
        out_specs=[pl.BlockSpec((1, tm, d), idx3),
                   pl.BlockSpec((1, 1, CONV_W - 1, tf), lambda b, t, j: (b, t, 0, j)),
                   pl.BlockSpec((1, 1, CONV_W - 1, tf), lambda b, t, j: (b, t, 0, j))],
        out_shape=[jax.ShapeDtypeStruct((B, T, d), F32),
                   jax.ShapeDtypeStruct((B, nt, CONV_W - 1, d_ff), F32),
                   jax.ShapeDtypeStruct((B, nt, CONV_W - 1, d_ff), F32)],
        scratch_shapes=[pltpu.VMEM((tm, d), F32), pltpu.VMEM((tm, d), BF16), pltpu.VMEM((tm, d), F32),
                        pltpu.VMEM((nj, 2, 8, tf), F32)],
        compiler_params=_cparams(("arbitrary", "arbitrary", "arbitrary")),
        name="ffn_prompt",
    )(x1, o, w_co, norm_ffn, w_up, w_up, conv_w, conv_w, conv_b, conv_b, w_down, hist, hist)


def _ffn_sample_kernel(x1_ref, o_ref, wco_ref, nf_ref, wua_ref, wug_ref, cwa_ref, cwg_ref,
                       cba_ref, cbg_ref, wd_ref, h0a_ref, h0g_ref, h1a_ref, h1g_ref,
                       y_ref, ua_ref, ug_ref, x2_ref, hb_ref, acc_ref):
    j = pl.program_id(0)
    nj = pl.num_programs(0)

    @pl.when(j == 0)
    def _():
        _ffn_front(x1_ref, o_ref, wco_ref, nf_ref, x2_ref, hb_ref, acc_ref)

    hb = hb_ref[...]

    def conv_part(wu_ref, cw_ref, cb_ref, h0_ref, h1_ref, u_out_ref):
        u = _mm(hb, wu_ref[...])
        u_out_ref[...] = u
        return cb_ref[...] + h0_ref[...] * cw_ref[0:1, :] + h1_ref[...] * cw_ref[1:2, :] + u * cw_ref[2:3, :]

    a = conv_part(wua_ref, cwa_ref, cba_ref, h0a_ref, h1a_ref, ua_ref)
    g = conv_part(wug_ref, cwg_ref, cbg_ref, h0g_ref, h1g_ref, ug_ref)
    acc_ref[...] += _mm((_gelu_tanh(g) * a).astype(BF16), wd_ref[...])

    @pl.when(j == nj - 1)
    def _():
        y_ref[0] = x2_ref[...] + acc_ref[...]


def _ffn_sample(x1, o, w_co, norm_ffn, w_up, conv_w, conv_b, w_down, h0, h1, *, tf):
    _, rows, d = x1.shape
    d_ff = w_down.shape[0]
    nj = d_ff // tf
    c2 = lambda shp: pl.BlockSpec(shp, lambda j: (0, 0))
    c3 = lambda shp: pl.BlockSpec(shp, lambda j: (0, 0, 0))
    col_a = lambda r: pl.BlockSpec((r, tf), lambda j: (0, j))
    col_g = lambda r: pl.BlockSpec((r, tf), lambda j: (0, nj + j))
    return pl.pallas_call(
        _ffn_sample_kernel,
        grid=(nj,),
        in_specs=[c3((1, rows, d)), c3((1, rows, d)), c2((d, d)), c2((1, d)),
                  col_a(d), col_g(d), col_a(CONV_W), col_g(CONV_W), col_a(1), col_g(1),
                  pl.BlockSpec((tf, d), lambda j: (j, 0)),
                  col_a(rows), col_g(rows), col_a(rows), col_g(rows)],
        out_specs=[c3((1, rows, d)), col_a(rows), col_a(rows)],
        out_shape=[jax.ShapeDtypeStruct((1, rows, d), F32),
                   jax.ShapeDtypeStruct((rows, d_ff), F32),
                   jax.ShapeDtypeStruct((rows, d_ff), F32)],
        scratch_shapes=[pltpu.VMEM((rows, d), F32), pltpu.VMEM((rows, d), BF16), pltpu.VMEM((rows, d), F32)],
        compiler_params=_cparams(("arbitrary",)),
        name="ffn_sample",
    )(x1, o, w_co, norm_ffn, w_up, w_up, conv_w, conv_w, conv_b, conv_b, w_down, h0, h0, h1, h1)


def _rope_tables(pos, dh_a):
    posf = pos.astype(F32)[:, None]
    half_a = dh_a // 2
    inv_a = ROPE_THETA ** (-jnp.arange(half_a, dtype=F32) / half_a)
    ang_a = posf * inv_a[None, :]
    cos_a, sin_a = jnp.cos(ang_a), jnp.sin(ang_a)
    half_i = D_IDX // 2
    inv_i = ROPE_THETA ** (-jnp.arange(half_i, dtype=F32) / half_i)
    ang_i = posf * inv_i[None, :]
    cos_i, sin_i = jnp.cos(ang_i), jnp.sin(ang_i)
    z = jnp.zeros_like(sin_i)
    one = jnp.ones_like(sin_i)
    return jnp.concatenate([
        cos_a, cos_a, -sin_a, sin_a,
        cos_i, cos_i, cos_i, cos_i,
        z, sin_i, z, sin_i,
        -sin_i, z, -sin_i, z,
        cos_i, cos_i, one, one,
        z, sin_i, z, z,
        -sin_i, z, z, z,
    ], axis=1)


def kernel(x_prompt, x_sample, mem_prompt, cache_k, cache_v, cache_idx_k, cache_mem_k, cache_mem_v,
           state_mlstm_c, state_mlstm_n, state_mlstm_m, state_conv, page_table,
           norm_mix, w_in, b_if, mlstm_norm, q_norm, k_norm, w_out, norm_cross, norm_mem,
           w_cq, w_ck, w_cv, w_co, cq_norm, ck_norm, norm_ffn, w_up, conv_w, conv_b, w_down):
    B, S, D = x_prompt.shape
    Bd, T, _ = x_sample.shape
    assert T == 1 and w_in.shape[0] == 1
    n_pool, page = cache_k.shape[1], cache_k.shape[2]
    n_pages = page_table.shape[1]
    past = n_pages * page
    mix_m = mlstm_norm.shape[1]
    dh_m = mix_m // H_M
    dh_a = q_norm.shape[1]
    mix_a = H_A * dh_a
    d_ff = w_down.shape[1]
    M = mem_prompt.shape[1]
    chunk = min(128, S)
    topk_p = min(TOPK_MAX, S // 4)
    topk_s = min(TOPK_MAX, (past + T) // 4)

    w = w_in[0]
    o_gate = 4 * mix_m
    o_aq = o_gate + 2 * H_M
    o_iq = o_aq + 3 * mix_a
    o_ik = o_iq + H_IDX * D_IDX
    o_iw = o_ik + D_IDX
    tail_pad = LANES - (D_IDX + H_IDX + 2 * H_M)
    w_r = jnp.concatenate([w[:, :o_gate], w[:, o_aq:o_iq], w[:, o_iq:o_ik], w[:, o_ik:o_iw],
                           w[:, o_iw:o_iw + H_IDX], w[:, o_gate:o_aq],
                           jnp.zeros((D, tail_pad), w.dtype)], axis=1).astype(BF16)
    bias_tail = jnp.concatenate([jnp.zeros((D_IDX + H_IDX,), F32), b_if[0].astype(F32),
                                 jnp.zeros((tail_pad,), F32)])[None, :]
    w_out_b = w_out[0].astype(BF16)
    w_cq_b, w_ck_b, w_cv_b, w_co_b = (a[0].astype(BF16) for a in (w_cq, w_ck, w_cv, w_co))
    w_up_b = w_up[0].astype(BF16)
    w_down_b = w_down[0].astype(BF16)
    row = lambda a: a[0][None, :]

    def split_misc(misc):
        ik = misc[:, :D_IDX]
        li = misc[:, D_IDX + H_IDX:D_IDX + H_IDX + H_M]
        lf = misc[:, D_IDX + H_IDX + H_M:D_IDX + H_IDX + 2 * H_M]
        return ik, li, lf

    tm_in = min(256, S)
    tab_p = _rope_tables(jnp.arange(S), dh_a)
    (mq, mk, mv, mo, aqb, ak, av, akb, avb, iqb, misc) = _in_proj(
        x_prompt.reshape(B * S, D), row(norm_mix), w_r, bias_tail, row(q_norm), row(k_norm), tab_p,
        tm=tm_in, tab_tiles=S // tm_in, mix_m=mix_m, mix_a=mix_a)
    ik_p, li_p, lf_p = split_misc(misc)
    r3 = lambda a: a.reshape(B, S, a.shape[-1])
    gcol = jnp.concatenate([li_p, lf_p], axis=-1).reshape(B, S, 2 * H_M)
    grow = gcol.reshape(B, S // chunk, chunk, 2 * H_M).transpose(0, 1, 3, 2)
    y_m, c_p, n_p, m_p = _mlstm_prompt(r3(mq), r3(mk), r3(mv), r3(mo), grow, gcol, row(mlstm_norm), chunk=chunk)

    tq = min(256, S)
    wk = min(512, S)
    ikt = ik_p.astype(BF16).reshape(B, S // wk, wk, D_IDX).transpose(0, 1, 3, 2)
    y_a = _dsa_prompt(r3(iqb), r3(misc), ikt, r3(aqb), r3(akb), r3(avb), tq=tq, w=wk, topk=topk_p)

    mk_p, mv_p = _mem_kv(mem_prompt.reshape(B * M, D), row(norm_mem), w_ck_b, w_cv_b, row(ck_norm),
                         tm=min(256, B * M))
    x1, qc = _out_cq(x_prompt.reshape(B * S, D), y_m.reshape(B * S, mix_m), y_a.reshape(B * S, mix_a),
                     w_out_b, row(norm_cross), w_cq_b, row(cq_norm), tm=min(512, S))
    dh_c = D // H_C
    o_c = _cross(qc.reshape(B, S, D), mk_p.reshape(B, M, H_C, dh_c), mv_p.reshape(B, M, H_C, dh_c), tq=min(512, S))
    tf = d_ff // 2 if (d_ff // 2) % LANES == 0 else d_ff
    xp, conv_a, conv_g = _ffn_prompt(x1.reshape(B, S, D), o_c, w_co_b, row(norm_ffn), w_up_b, conv_w[0],
                                     conv_b[0][None, :], w_down_b,
                                     jnp.zeros((B, CONV_W - 1, 2 * d_ff), F32), tm=min(512, S), tf=tf)
    conv_p = jnp.concatenate([conv_a[:, -1], conv_g[:, -1]], axis=-1)

    tab_s = jnp.broadcast_to(_rope_tables(jnp.full((1,), past, jnp.int32), dh_a), (Bd, 8 * LANES))
    (mq_s, mk_s, mv_s, mo_s, aqb_s, ak_s, av_s, _, _, iqb_s, misc_s) = _in_proj(
        x_sample.reshape(Bd, D), row(norm_mix), w_r, bias_tail, row(q_norm), row(k_norm), tab_s,
        tm=Bd, tab_tiles=1, mix_m=mix_m, mix_a=mix_a)
    ik_s, li_s, lf_s = split_misc(misc_s)
    gs = jnp.concatenate([li_s, lf_s, state_mlstm_m[0].astype(F32)], axis=-1)[:, None, :]
    e1 = lambda a: a[:, None, :]
    y_ms, c_s, n_s, m_s = _mlstm_sample(e1(mq_s), e1(mk_s), e1(mv_s), e1(mo_s), gs,
                                        state_mlstm_c[0], state_mlstm_n[0].reshape(Bd, 1, mix_m),
                                        row(mlstm_norm))

    iq8 = jnp.pad(iqb_s.reshape(Bd, H_IDX, D_IDX), ((0, 0), (0, 8 - H_IDX), (0, 0)))
    w8 = jnp.pad(misc_s[:, D_IDX:D_IDX + H_IDX], ((0, 0), (0, 8 - H_IDX)))[:, :, None]
    assert n_pages <= 256
    rows_t, flags = _dsa_sample_select(page_table, iq8, w8, e1(ik_s), jnp.swapaxes(cache_idx_k[0], 1, 2),
                                       topk=topk_s, cw=min(512, past))
    y_as = _dsa_sample_attend(rows_t[:, :Bd].T, flags[:, 0], e1(aqb_s), ak_s, av_s,
                              cache_k[0].reshape(n_pool * page, H_A, dh_a),
                              cache_v[0].reshape(n_pool * page, H_A, dh_a), topk=topk_s)

    x1_s, qc_s = _out_cq(x_sample.reshape(Bd, D), y_ms.reshape(Bd, mix_m), y_as.reshape(Bd, mix_a),
                         w_out_b, row(norm_cross), w_cq_b, row(cq_norm), tm=Bd)
    o_s = _cross(qc_s.reshape(Bd, 1, D), cache_mem_k[0], cache_mem_v[0], tq=1)
    xs, u_a, u_g = _ffn_sample(x1_s.reshape(1, Bd, D), o_s.reshape(1, Bd, D), w_co_b, row(norm_ffn), w_up_b,
                               conv_w[0], conv_b[0][None, :], w_down_b,
                               state_conv[0, :, 0, :], state_conv[0, :, 1, :], tf=tf)
    conv_s = jnp.stack([state_conv[0, :, 1, :], jnp.concatenate([u_a, u_g], axis=-1)], axis=1)

    lead = lambda a: a[None]
    return (xp, xs.reshape(Bd, 1, D),
            lead(ak.reshape(B, S, H_A, dh_a)), lead(av.reshape(B, S, H_A, dh_a)), lead(ik_p.reshape(B, S, D_IDX)),
            lead(c_p), lead(n_p), lead(m_p[:, :, 0]),
            lead(mk_p.reshape(B, M, H_C, D // H_C)), lead(mv_p.reshape(B, M, H_C, D // H_C)), lead(conv_p),
            lead(ak_s.reshape(Bd, 1, H_A, dh_a)), lead(av_s.reshape(Bd, 1, H_A, dh_a)),
            lead(ik_s.reshape(Bd, 1, D_IDX)),
            lead(c_s), lead(n_s.reshape(Bd, H_M, dh_m)), lead(m_s[:, 0, :H_M]), lead(conv_s))
```

```python
import functools

import jax
import jax.numpy as jnp
import numpy as np
from jax import lax
from jax.experimental import pallas as pl
from jax.experimental.pallas import tpu as pltpu

F32 = jnp.float32
BF16 = jnp.bfloat16

H_M = 4
H_A = 4
H_IDX = 4
D_IDX = 64
H_C = 4
TOPK_MAX = 256
CONV_W = 3
ROPE_THETA = 10000.0
EPS = 1e-6
LOG2E = 1.4426950408889634
NEG_INF = float("-inf")
POS_INF = float("inf")

LANES = 128
VMEM_LIMIT = 56 * 1024 * 1024
N_BISECT = 20


def _cparams(sem):
    return pltpu.CompilerParams(dimension_semantics=sem, vmem_limit_bytes=VMEM_LIMIT)


def _nt(a, b):
    return lax.dot_general(a, b, (((1,), (1,)), ((), ())), preferred_element_type=F32)


def _tn(a, b):
    return lax.dot_general(a, b, (((0,), (0,)), ((), ())), preferred_element_type=F32)


def _mm(a, b):
    return jnp.dot(a, b, preferred_element_type=F32)


def _rms(x, g):
    ms = jnp.mean(x * x, axis=-1, keepdims=True)
    return x * lax.rsqrt(ms + EPS) * g


def _sigmoid(x):
    return 1.0 / (1.0 + jnp.exp(-x))


def _in_proj_kernel(x_ref, nm_ref, w_ref, bias_ref, qn_ref, kn_ref, tab_ref,
                    mq_ref, mk_ref, mv_ref, mo_ref, aqb_ref, ak_ref, av_ref, akb_ref, avb_ref,
                    iqb_ref, misc_ref, *, mix_m, mix_a, dh_m, dh_a):
    h = _rms(x_ref[...], nm_ref[...]).astype(BF16)

    def proj(lo, width):
        return _mm(h, w_ref[:, lo:lo + width])

    o_mq, o_mk, o_mv, o_mo = 0, mix_m, 2 * mix_m, 3 * mix_m
    o_aq = 4 * mix_m
    o_ak = o_aq + mix_a
    o_av = o_ak + mix_a
    o_iq = o_av + mix_a
    o_tail = o_iq + H_IDX * D_IDX

    mq_ref[...] = proj(o_mq, mix_m)
    mk_ref[...] = proj(o_mk, mix_m) * (dh_m ** -0.5)
    mv_ref[...] = proj(o_mv, mix_m)
    mo_ref[...] = proj(o_mo, mix_m)

    cos_a = tab_ref[:, 0:LANES]
    sin_a = tab_ref[:, LANES:2 * LANES]

    def norm_rope(z, g_ref):
        outs = []
        for hh in range(mix_a // dh_a):
            zh = _rms(z[:, hh * dh_a:(hh + 1) * dh_a], g_ref[...])
            outs.append(zh * cos_a + pltpu.roll(zh, dh_a // 2, 1) * sin_a)
        return outs

    aq = norm_rope(proj(o_aq, mix_a), qn_ref)
    aqb_ref[...] = jnp.concatenate(aq, axis=1).astype(BF16)
    ak = norm_rope(proj(o_ak, mix_a), kn_ref)
    av = proj(o_av, mix_a)
    for hh in range(mix_a // dh_a):
        ak_ref[:, hh, :] = ak[hh]
        av_ref[:, hh, :] = av[:, hh * dh_a:(hh + 1) * dh_a]
    akb_ref[...] = jnp.concatenate(ak, axis=1).astype(BF16)
    avb_ref[...] = av.astype(BF16)

    c_i = tab_ref[:, 2 * LANES:3 * LANES]
    s1_i = tab_ref[:, 3 * LANES:4 * LANES]
    s2_i = tab_ref[:, 4 * LANES:5 * LANES]
    ziq = proj(o_iq, H_IDX * D_IDX)
    cols = []
    for c in range(H_IDX * D_IDX // LANES):
        zc = ziq[:, c * LANES:(c + 1) * LANES]
        cols.append(zc * c_i + pltpu.roll(zc, D_IDX // 2, 1) * s1_i
                    + pltpu.roll(zc, LANES - D_IDX // 2, 1) * s2_i)
    iqb_ref[...] = jnp.concatenate(cols, axis=1).astype(BF16)

    c_t = tab_ref[:, 5 * LANES:6 * LANES]
    s1_t = tab_ref[:, 6 * LANES:7 * LANES]
    s2_t = tab_ref[:, 7 * LANES:8 * LANES]
    zt = proj(o_tail, LANES) + bias_ref[...]
    zt = zt * c_t + pltpu.roll(zt, D_IDX // 2, 1) * s1_t + pltpu.roll(zt, LANES - D_IDX // 2, 1) * s2_t
    lane = lax.broadcasted_iota(jnp.int32, zt.shape, 1)
    f_lo = D_IDX + H_IDX + H_M
    log_sig = jnp.minimum(zt, 0.0) - jnp.log(1.0 + jnp.exp(-jnp.abs(zt)))
    misc_ref[...] = jnp.where((lane >= f_lo) & (lane < f_lo + H_M), log_sig, zt)


def _in_proj(x2d, norm_mix, w_r, bias_tail, q_norm, k_norm, tab, *, tm, tab_tiles, mix_m, mix_a):
    rows, d = x2d.shape
    dh_m = mix_m // H_M
    dh_a = mix_a // H_A
    nw = w_r.shape[1]
    grid = (rows // tm,)
    row_spec = lambda wdt: pl.BlockSpec((tm, wdt), lambda i: (i, 0))
    const = lambda shp: pl.BlockSpec(shp, lambda i: (0, 0))
    out_shapes = [
        jax.ShapeDtypeStruct((rows, mix_m), F32),
        jax.ShapeDtypeStruct((rows, mix_m), F32),
        jax.ShapeDtypeStruct((rows, mix_m), F32),
        jax.ShapeDtypeStruct((rows, mix_m), F32),
        jax.ShapeDtypeStruct((rows, mix_a), BF16),
        jax.ShapeDtypeStruct((rows, H_A, dh_a), F32),
        jax.ShapeDtypeStruct((rows, H_A, dh_a), F32),
        jax.ShapeDtypeStruct((rows, mix_a), BF16),
        jax.ShapeDtypeStruct((rows, mix_a), BF16),
        jax.ShapeDtypeStruct((rows, H_IDX * D_IDX), BF16),
        jax.ShapeDtypeStruct((rows, LANES), F32),
    ]
    head_spec = pl.BlockSpec((tm, H_A, dh_a), lambda i: (i, 0, 0))
    out_specs = ([row_spec(mix_m)] * 4 + [row_spec(mix_a), head_spec, head_spec, row_spec(mix_a), row_spec(mix_a)]
                 + [row_spec(H_IDX * D_IDX), row_spec(LANES)])
    return pl.pallas_call(
        functools.partial(_in_proj_kernel, mix_m=mix_m, mix_a=mix_a, dh_m=dh_m, dh_a=dh_a),
        grid=grid,
        in_specs=[row_spec(d), const((1, d)),
                  pl.BlockSpec((d, nw), lambda i: (0, 0), pipeline_mode=pl.Buffered(1)), const((1, LANES)),
                  const((1, dh_a)), const((1, dh_a)),
                  pl.BlockSpec((tm, 8 * LANES), lambda i: (i % tab_tiles, 0))],
        out_specs=out_specs,
        out_shape=out_shapes,
        compiler_params=_cparams(("parallel",)),
        name="in_proj",
    )(x2d, norm_mix, w_r, bias_tail, q_norm, k_norm, tab)


def _mlstm_prompt_kernel(q_ref, k_ref, v_ref, o_ref, grow_ref, gcol_ref, gain_ref,
                         y_ref, c_ref, n_ref, m_ref, cs_ref, ns_ref, ms_ref, *, chunk, d):
    c_idx = pl.program_id(1)
    L = chunk
    row_i = lax.broadcasted_iota(jnp.int32, (L, L), 0)
    col_i = lax.broadcasted_iota(jnp.int32, (L, L), 1)
    tril = col_i <= row_i
    triu = row_i <= col_i

    @pl.when(c_idx == 0)
    def _():
        cs_ref[...] = jnp.zeros_like(cs_ref)
        ns_ref[...] = jnp.zeros_like(ns_ref)
        ms_ref[...] = jnp.zeros_like(ms_ref)

    gr = grow_ref[0, 0]
    gc = gcol_ref[0]
    for hd in range(H_M):
        hs = slice(hd * d, (hd + 1) * d)
        C = cs_ref[hd]
        n = ns_ref[hd, 0:1, :]
        m = ms_ref[hd, 0:1, 0:1]
        q = q_ref[0, :, hs]
        k = k_ref[0, :, hs]
        v = v_ref[0, :, hs]
        o = o_ref[0, :, hs]
        li_r = gr[hd:hd + 1, :]
        lf_r = gr[H_M + hd:H_M + hd + 1, :]
        li_c = gc[:, hd:hd + 1]
        lf_c = gc[:, H_M + hd:H_M + hd + 1]
        b_c = jnp.sum(jnp.where(tril, lf_r, 0.0), axis=1, keepdims=True)
        b_r = jnp.sum(jnp.where(triu, lf_c, 0.0), axis=0, keepdims=True)
        logd = jnp.where(tril, b_c - b_r + li_r, NEG_INF)
        inter = b_c + m
        m_t = jnp.maximum(inter, jnp.max(logd, axis=1, keepdims=True))
        qb = q.astype(BF16)
        kb = k.astype(BF16)
        s = _nt(qb, kb) * jnp.exp(logd - m_t)
        g_inter = jnp.exp(inter - m_t)
        num = g_inter * _nt(qb, C.astype(BF16)) + _mm(s.astype(BF16), v.astype(BF16))
        den = g_inter * jnp.sum(q * n, axis=1, keepdims=True) + jnp.sum(s, axis=1, keepdims=True)
        h = num / jnp.maximum(jnp.abs(den), jnp.exp(-m_t))
        m_new = m_t[L - 1:L, :]
        b_last = b_c[L - 1:L, :]
        g_prev = jnp.exp(b_last + m - m_new)
        w_c = jnp.exp(b_last - b_c + li_c - m_new)
        cs_ref[hd] = g_prev * C + _tn((v * w_c).astype(BF16), kb)
        ns_ref[hd, 0:1, :] = g_prev * n + jnp.sum(k * w_c, axis=0, keepdims=True)
        ms_ref[hd, 0:1, :] = jnp.broadcast_to(m_new, (1, LANES))
        y_ref[0, :, hs] = _sigmoid(o) * _rms(h, gain_ref[:, hs])

    @pl.when(c_idx == pl.num_programs(1) - 1)
    def _():
        c_ref[0] = cs_ref[...]
        for hd in range(H_M):
            n_ref[0, hd:hd + 1, :] = ns_ref[hd, 0:1, :]
            m_ref[0, hd:hd + 1, :] = ms_ref[hd, 0:1, :]


def _mlstm_prompt(mq, mk, mv, mo, grow, gcol, gain, *, chunk):
    B, S, mix_m = mq.shape
    d = mix_m // H_M
    n_chunks = S // chunk
    seq = pl.BlockSpec((1, chunk, mix_m), lambda b, c: (b, c, 0))
    return pl.pallas_call(
        functools.partial(_mlstm_prompt_kernel, chunk=chunk, d=d),
        grid=(B, n_chunks),
        in_specs=[seq, seq, seq, seq,
                  pl.BlockSpec((1, 1, 2 * H_M, chunk), lambda b, c: (b, c, 0, 0)),
                  pl.BlockSpec((1, chunk, 2 * H_M), lambda b, c: (b, c, 0)),
                  pl.BlockSpec((1, mix_m), lambda b, c: (0, 0))],
        out_specs=[seq,
                   pl.BlockSpec((1, H_M, d, d), lambda b, c: (b, 0, 0, 0)),
                   pl.BlockSpec((1, H_M, d), lambda b, c: (b, 0, 0)),
                   pl.BlockSpec((1, H_M, LANES), lambda b, c: (b, 0, 0))],
        out_shape=[jax.ShapeDtypeStruct((B, S, mix_m), F32),
                   jax.ShapeDtypeStruct((B, H_M, d, d), F32),
                   jax.ShapeDtypeStruct((B, H_M, d), F32),
                   jax.ShapeDtypeStruct((B, H_M, LANES), F32)],
        scratch_shapes=[pltpu.VMEM((H_M, d, d), F32), pltpu.VMEM((H_M, 8, d), F32),
                        pltpu.VMEM((H_M, 8, LANES), F32)],
        compiler_params=_cparams(("parallel", "arbitrary")),
        name="mlstm_prompt",
    )(mq, mk, mv, mo, grow, gcol, gain)


def _mlstm_sample_kernel(q_ref, k_ref, v_ref, o_ref, gs_ref, c_ref, n_ref, gain_ref,
                         y_ref, co_ref, no_ref, mo_ref, *, d):
    gs = gs_ref[0]
    eye = (lax.broadcasted_iota(jnp.int32, (d, d), 0) == lax.broadcasted_iota(jnp.int32, (d, d), 1))
    lane = lax.broadcasted_iota(jnp.int32, (1, LANES), 1)
    m_out = jnp.zeros((1, LANES), F32)
    for h in range(H_M):
        sl = slice(h * d, (h + 1) * d)
        q = q_ref[0, :, sl]
        k = k_ref[0, :, sl]
        v = v_ref[0, :, sl]
        o = o_ref[0, :, sl]
        li = gs[:, h:h + 1]
        lf = gs[:, H_M + h:H_M + h + 1]
        m = gs[:, 2 * H_M + h:2 * H_M + h + 1]
        C = c_ref[0, h]
        n = n_ref[0, :, sl]
        inter = lf + m
        m_t = jnp.maximum(inter, li)
        s = jnp.sum(q * k, axis=1, keepdims=True) * jnp.exp(li - m_t)
        g = jnp.exp(inter - m_t)
        q8 = jnp.broadcast_to(q, (8, d)).astype(BF16)
        cq = _nt(q8, C.astype(BF16))[0:1, :]
        num = g * cq + s * v
        den = g * jnp.sum(n * q, axis=1, keepdims=True) + s
        hh = num / jnp.maximum(jnp.abs(den), jnp.exp(-m_t))
        w = jnp.exp(li - m_t)
        v_col = jnp.sum(jnp.where(eye, v, 0.0), axis=1, keepdims=True)
        co_ref[0, h] = g * C + (w * v_col) * k
        no_ref[0, :, sl] = g * n + w * k
        m_out = jnp.where(lane == h, m_t, m_out)
        y_ref[0, :, sl] = _sigmoid(o) * _rms(hh, gain_ref[:, sl])
    mo_ref[0] = m_out


def _mlstm_sample(mq, mk, mv, mo, gs, c_state, n_state, gain):
    Bd, _, mix_m = mq.shape
    d = mix_m // H_M
    row = pl.BlockSpec((1, 1, mix_m), lambda b: (b, 0, 0))
    return pl.pallas_call(
        functools.partial(_mlstm_sample_kernel, d=d),
        grid=(Bd,),
        in_specs=[row, row, row, row,
                  pl.BlockSpec((1, 1, 3 * H_M), lambda b: (b, 0, 0)),
                  pl.BlockSpec((1, H_M, d, d), lambda b: (b, 0, 0, 0)),
                  row,
                  pl.BlockSpec((1, mix_m), lambda b: (0, 0))],
        out_specs=[row,
                   pl.BlockSpec((1, H_M, d, d), lambda b: (b, 0, 0, 0)),
                   row,
                   pl.BlockSpec((1, 1, LANES), lambda b: (b, 0, 0))],
        out_shape=[jax.ShapeDtypeStruct((Bd, 1, mix_m), F32),
                   jax.ShapeDtypeStruct((Bd, H_M, d, d), F32),
                   jax.ShapeDtypeStruct((Bd, 1, mix_m), F32),
                   jax.ShapeDtypeStruct((Bd, 1, LANES), F32)],
        compiler_params=_cparams(("parallel",)),
        name="mlstm_sample",
    )(mq, mk, mv, mo, gs, c_state, n_state, gain)


def _dsa_prompt_kernel(iq_ref, misc_ref, ikt_ref, aq_ref, ak_ref, av_ref, ya_ref, sc_ref, acc_ref,
                       *, tq, w, topk, dh, scale):
    i = pl.program_id(1)
    nk = ((i + 1) * tq + w - 1) // w
    kf = float(topk)
    nsub = w // LANES

    q_pos = i * tq + lax.broadcasted_iota(jnp.int32, (tq, 1), 0)
    lane_w = lax.broadcasted_iota(jnp.int32, (1, w), 1)
    iq = iq_ref[0]
    iq_h = [iq[:, h * D_IDX:(h + 1) * D_IDX] for h in range(H_IDX)]
    misc = misc_ref[0]
    w_h = [misc[:, D_IDX + h:D_IDX + h + 1] for h in range(H_IDX)]

    def score_body(c, carry):
        rmax, rmin = carry
        ikc = ikt_ref[0, c]
        score = jnp.zeros((tq, w), F32)
        for h in range(H_IDX):
            score = score + w_h[h] * jnp.maximum(_mm(iq_h[h], ikc), 0.0)
        valid = (c * w + lane_w) <= q_pos
        sc_ref[c] = jnp.where(valid, score, NEG_INF)
        rmax = jnp.maximum(rmax, jnp.max(jnp.where(valid, score, NEG_INF), axis=1, keepdims=True))
        rmin = jnp.minimum(rmin, jnp.min(jnp.where(valid, score, POS_INF), axis=1, keepdims=True))
        return rmax, rmin

    rmax, rmin = lax.fori_loop(0, nk, score_body,
                               (jnp.full((tq, 1), NEG_INF, F32), jnp.full((tq, 1), POS_INF, F32)))

    ge = lambda x, t: x >= t
    gt = lambda x, t: x > t

    rh = min(tq, LANES)
    groups = [pl.ds(r0, rh) for r0 in range(0, tq, rh)]
    part = lambda a: [a[r0:r0 + rh] for r0 in range(0, tq, rh)]

    def pass_acc(rows, fn, init, combine):
        def body(c, acc):
            x = sc_ref[c, rows, :]
            for j in range(nsub):
                acc = combine(acc, fn(x[:, j * LANES:(j + 1) * LANES]))
            return acc
        return lax.fori_loop(0, nk, body, jnp.full((rh, LANES), init, F32))

    def count_acc(rows, pred, thr):
        thr_b = jnp.broadcast_to(thr, (rh, LANES))
        return pass_acc(rows, lambda x: jnp.where(pred(x, thr_b), 1.0, 0.0), 0.0, jnp.add)

    def count(rows, pred, thr):
        return jnp.sum(count_acc(rows, pred, thr), axis=1, keepdims=True)

    def min_where(rows, pred, thr):
        thr_b = jnp.broadcast_to(thr, (rh, LANES))
        acc = pass_acc(rows, lambda x: jnp.where(pred(x, thr_b), x, POS_INF), POS_INF, jnp.minimum)
        return jnp.min(acc, axis=1, keepdims=True)

    def bis_body(_, carry):
        los, his, clos = carry
        mids = [0.5 * (lo + hi) for lo, hi in zip(los, his)]
        accs = [count_acc(rows, ge, mid) for rows, mid in zip(groups, mids)]
        cms = [jnp.sum(a, axis=1, keepdims=True) for a in accs]
        oks = [cm >= kf for cm in cms]
        return (tuple(jnp.where(ok, mid, lo) for ok, mid, lo in zip(oks, mids, los)),
                tuple(jnp.where(ok, hi, mid) for ok, mid, hi in zip(oks, mids, his)),
                tuple(jnp.where(ok, cm, cl) for ok, cm, cl in zip(oks, cms, clos)))

    los, _, clos = lax.fori_loop(
        0, N_BISECT, bis_body,
        (tuple(part(rmin)), tuple(part(rmax + jnp.abs(rmax) + 1.0)), tuple(part((q_pos + 1).astype(F32)))))

    def finish_rows(rows, qp, rmin_h, lo, c_lo):
        active = (qp + 1) > topk
        unresolved = jnp.max(jnp.where(active & (c_lo != kf), 1.0, 0.0)) > 0.5

        @pl.when(jnp.logical_not(unresolved))
        def _():
            thr = jnp.where(active, lo, rmin_h)

            def body(c, _):
                sc_ref[c, rows, :] = jnp.where(sc_ref[c, rows, :] >= thr, 0.0, NEG_INF)
                return 0
            lax.fori_loop(0, nk, body, 0)

        @pl.when(unresolved)
        def _():
            tau = min_where(rows, ge, lo)
            g = count(rows, gt, tau)

            def undone(tau, g):
                return active & (g >= kf)

            def fix_cond(st):
                return jnp.max(jnp.where(undone(*st), 1.0, 0.0)) > 0.5

            def fix_body(st):
                tau, g = st
                nd = undone(tau, g)
                tau2 = jnp.where(nd, min_where(rows, gt, tau), tau)
                return tau2, jnp.where(nd, count(rows, gt, tau2), g)

            tau, g = lax.while_loop(fix_cond, fix_body, (tau, g))
            tau_b = jnp.broadcast_to(jnp.where(active, tau, rmin_h), (rh, LANES))
            need_b = jnp.broadcast_to(jnp.where(active, kf - g, 1e9), (rh, LANES))
            r_i = lax.broadcasted_iota(jnp.int32, (LANES, 2 * LANES), 0)
            c_i = lax.broadcasted_iota(jnp.int32, (LANES, 2 * LANES), 1)
            tri_ones = ((r_i <= c_i) | (c_i >= LANES)).astype(BF16)

            def body(c, run):
                x = sc_ref[c, rows, :]
                outs = []
                for j in range(nsub):
                    xj = x[:, j * LANES:(j + 1) * LANES]
                    is_eq = xj == tau_b
                    cnt2 = _mm(jnp.where(is_eq, 1.0, 0.0).astype(BF16), tri_ones)
                    sel = (xj > tau_b) | (is_eq & (cnt2[:, :LANES] + run <= need_b))
                    outs.append(jnp.where(sel, 0.0, NEG_INF))
                    run = run + cnt2[:, LANES:]
                sc_ref[c, rows, :] = jnp.concatenate(outs, axis=1)
                return run
            lax.fori_loop(0, nk, body, jnp.zeros((rh, LANES), F32))

    for rows, qp, rmin_h, lo, c_lo in zip(groups, part(q_pos), part(rmin), los, clos):
        finish_rows(rows, qp, rmin_h, lo, c_lo)

    aq = aq_ref[0]
    q_heads = [aq[:, h * dh:(h + 1) * dh] for h in range(H_A)]
    acc_ref[...] = jnp.zeros_like(acc_ref)
    c2 = scale * LOG2E
    ones_blk = jnp.ones((w, dh), BF16)

    def att_body(c, ms):
        k0 = pl.multiple_of(c * w, w)
        bias = sc_ref[c]
        ms_new = []
        for h in range(H_A):
            hs = slice(h * dh, (h + 1) * dh)
            s = _nt(q_heads[h], ak_ref[0, pl.ds(k0, w), hs]) + bias
            m_new = jnp.maximum(ms[h], jnp.max(s, axis=1, keepdims=True))
            m_safe = jnp.where(m_new == NEG_INF, 0.0, m_new)
            alpha = jnp.exp2((ms[h] - m_safe) * c2)
            p = jnp.exp2((s - m_safe) * c2).astype(BF16)
            v_aug = jnp.concatenate([av_ref[0, pl.ds(k0, w), hs], ones_blk], axis=1)
            acc_ref[h] = alpha * acc_ref[h] + _mm(p, v_aug)
            ms_new.append(m_new)
        return tuple(ms_new)

    lax.fori_loop(0, nk, att_body, tuple(jnp.full((tq, 1), NEG_INF, F32) for _ in range(H_A)))
    for h in range(H_A):
        a = acc_ref[h]
        ya_ref[0, :, h * dh:(h + 1) * dh] = a[:, :dh] / a[:, dh:]


def _dsa_prompt(iqb, misc, ikt, aqb, akb, avb, *, tq, w, topk):
    B, S, mix_a = aqb.shape
    dh = mix_a // H_A
    nq = S // tq
    nw = S // w
    return pl.pallas_call(
        functools.partial(_dsa_prompt_kernel, tq=tq, w=w, topk=topk, dh=dh, scale=dh ** -0.5),
        grid=(B, nq),
        in_specs=[pl.BlockSpec((1, tq, H_IDX * D_IDX), lambda b, i: (b, i, 0)),
                  pl.BlockSpec((1, tq, LANES), lambda b, i: (b, i, 0)),
                  pl.BlockSpec((1, nw, D_IDX, w), lambda b, i: (b, 0, 0, 0)),
                  pl.BlockSpec((1, tq, mix_a), lambda b, i: (b, i, 0)),
                  pl.BlockSpec((1, S, mix_a), lambda b, i: (b, 0, 0)),
                  pl.BlockSpec((1, S, mix_a), lambda b, i: (b, 0, 0))],
        out_specs=pl.BlockSpec((1, tq, mix_a), lambda b, i: (b, i, 0)),
        out_shape=jax.ShapeDtypeStruct((B, S, mix_a), F32),
        scratch_shapes=[pltpu.VMEM((nw, tq, w), F32), pltpu.VMEM((H_A, tq, 2 * dh), F32)],
        compiler_params=_cparams(("parallel", "arbitrary")),
        name="dsa_prompt",
    )(iqb, misc, ikt, aqb, akb, avb)


def _dsa_sample_select_kernel(pt_ref, iq_ref, w_ref, ikn_ref, ptv_ref, pool_ref, rows_ref, flag_ref,
                              ikbuf, sem, sc_ref, xn_ref, slot_ref, phys_ref,
                              *, n_pages, page, topk, cw):
    nb = iq_ref.shape[0]
    past = n_pages * page
    kf = float(topk)
    n_cw = past // cw

    def page_copy(bb, p, slot):
        return pltpu.make_async_copy(pool_ref.at[pt_ref[bb, p]],
                                     ikbuf.at[slot, :, pl.ds(p * page, page)],
                                     sem.at[slot])

    def start_all(bb, slot):
        def body(p, _):
            page_copy(bb, p, slot).start()
            return 0
        lax.fori_loop(0, n_pages, body, 0)

    start_all(0, 0)

    def score_body(b, _):
        slot = b % 2

        @pl.when(b + 1 < nb)
        def _():
            start_all(b + 1, 1 - slot)

        def wait_body(p, _):
            page_copy(b, p, slot).wait()
            return 0
        lax.fori_loop(0, n_pages, wait_body, 0)

        iq8 = iq_ref[b]
        w8 = w_ref[b]
        s8 = _mm(iq8, ikbuf[slot].astype(BF16))
        sc_ref[pl.ds(b, 1), :] = jnp.sum(w8 * jnp.maximum(s8, 0.0), axis=0, keepdims=True)
        ikn = ikn_ref[b].astype(BF16).astype(F32)
        sn8 = jnp.sum(iq8.astype(F32) * ikn, axis=1, keepdims=True)
        xn_b = jnp.sum(w8 * jnp.maximum(sn8, 0.0), axis=0, keepdims=True)
        xn_ref[pl.ds(b, 1), :] = jnp.broadcast_to(xn_b, (1, LANES))
        return 0

    lax.fori_loop(0, nb, score_body, 0)

    x = sc_ref[...]
    xn = xn_ref[:, 0:1]

    def cnt(mask_row, mask_new):
        return (jnp.sum(jnp.where(mask_row, 1.0, 0.0), axis=1, keepdims=True)
                + jnp.where(mask_new, 1.0, 0.0))

    rmax = jnp.maximum(jnp.max(x, axis=1, keepdims=True), xn)
    rmin = jnp.minimum(jnp.min(x, axis=1, keepdims=True), xn)
    hi0 = rmax + jnp.abs(rmax) + 1.0

    def bis_body(_, carry):
        lo, hi = carry
        mid = 0.5 * (lo + hi)
        ok = cnt(x >= mid, xn >= mid) >= kf
        return jnp.where(ok, mid, lo), jnp.where(ok, hi, mid)

    lo, _ = lax.fori_loop(0, N_BISECT, bis_body, (rmin, hi0))

    def min_where(mask_row, mask_new):
        return jnp.minimum(jnp.min(jnp.where(mask_row, x, POS_INF), axis=1, keepdims=True),
                           jnp.where(mask_new, xn, POS_INF))

    tau = min_where(x >= lo, xn >= lo)
    g = cnt(x > tau, xn > tau)

    def fix_cond(st):
        tau, g = st
        return jnp.max(jnp.where(g >= kf, 1.0, 0.0)) > 0.5

    def fix_body(st):
        tau, g = st
        tau2 = jnp.where(g >= kf, min_where(x > tau, xn > tau), tau)
        return tau2, cnt(x > tau2, xn > tau2)

    tau, g = lax.while_loop(fix_cond, fix_body, (tau, g))
    need = kf - g

    tri = (lax.broadcasted_iota(jnp.int32, (cw, cw), 0)
           < lax.broadcasted_iota(jnp.int32, (cw, cw), 1)).astype(BF16)

    def excl_prefix(flag):
        outs = []
        run = jnp.zeros((nb, 1), F32)
        for c in range(n_cw):
            f = flag[:, c * cw:(c + 1) * cw]
            outs.append(_mm(f.astype(BF16), tri) + run)
            run = run + jnp.sum(f, axis=1, keepdims=True)
        return jnp.concatenate(outs, axis=1), run

    is_eq = x == tau
    pre_eq, n_eq_past = excl_prefix(jnp.where(is_eq, 1.0, 0.0))
    sel = (x > tau) | (is_eq & (pre_eq < need))
    new_sel = (xn > tau) | ((xn == tau) & (n_eq_past < need))
    slot, _ = excl_prefix(jnp.where(sel, 1.0, 0.0))
    slot_ref[...] = jnp.where(sel, slot, -1.0)

    ptv = ptv_ref[...]
    jrow = lax.broadcasted_iota(jnp.int32, (1, past), 1)
    prow = lax.broadcasted_iota(jnp.int32, (n_pages, 1), 0)
    expand = ((jrow >= prow * page) & (jrow < (prow + 1) * page)).astype(BF16)
    digit_bits = 6
    pt_hi = _mm((ptv >> digit_bits).astype(F32).astype(BF16), expand)
    pt_lo = _mm((ptv & ((1 << digit_bits) - 1)).astype(F32).astype(BF16), expand)
    pidx = lax.broadcasted_iota(jnp.int32, (8, n_pages), 1).astype(F32).astype(BF16)
    pg = _mm(pidx, expand)[0:1, :]
    phys_ref[...] = (pt_hi * (1 << digit_bits) + pt_lo) * page + (jrow.astype(F32) - pg * page)

    slot_col = lax.broadcasted_iota(jnp.int32, (topk, 1), 0).astype(F32)
    lane_b = lax.broadcasted_iota(jnp.int32, (1, LANES), 1)

    def extract_body(b, out):
        srow = slot_ref[pl.ds(b, 1), :]
        frow = phys_ref[pl.ds(b, 1), :]
        acc = jnp.zeros((topk, LANES), F32)
        for c in range(past // LANES):
            cs = slice(c * LANES, (c + 1) * LANES)
            acc = acc + jnp.where(srow[:, cs] == slot_col, frow[:, cs], 0.0)
        return jnp.where(lane_b == b, jnp.sum(acc, axis=1, keepdims=True), out)

    out = lax.fori_loop(0, nb, extract_body, jnp.zeros((topk, LANES), F32))
    rows_ref[...] = out.astype(jnp.int32)
    flag_ref[...] = jnp.broadcast_to(jnp.where(new_sel, 1, 0), (nb, LANES)).astype(jnp.int32)


def _dsa_sample_select(page_table, iq8, w8, ik_new, pool_ik_t, *, topk, cw):
    Bd, n_pages = page_table.shape
    n_pool, d_idx, page = pool_ik_t.shape
    past = n_pages * page
    assert Bd <= LANES and n_pool <= 64 * 256
    full = lambda shp: pl.BlockSpec(shp, lambda i, pt: (0,) * len(shp))
    grid_spec = pltpu.PrefetchScalarGridSpec(
        num_scalar_prefetch=1,
        grid=(1,),
        in_specs=[full((Bd, 8, d_idx)), full((Bd, 8, 1)), full((Bd, 1, d_idx)), full((Bd, n_pages)),
                  pl.BlockSpec(memory_space=pl.ANY)],
        out_specs=[full((topk, LANES)), full((Bd, LANES))],
        scratch_shapes=[pltpu.VMEM((2, d_idx, past), F32),
                        pltpu.SemaphoreType.DMA((2,)),
                        pltpu.VMEM((Bd, past), F32),
                        pltpu.VMEM((Bd, LANES), F32),
                        pltpu.VMEM((Bd, past), F32),
                        pltpu.VMEM((Bd, past), F32)],
    )
    return pl.pallas_call(
        functools.partial(_dsa_sample_select_kernel, n_pages=n_pages, page=page, topk=topk, cw=cw),
        grid_spec=grid_spec,
        out_shape=[jax.ShapeDtypeStruct((topk, LANES), jnp.int32),
                   jax.ShapeDtypeStruct((Bd, LANES), jnp.int32)],
        compiler_params=_cparams(("arbitrary",)),
        name="dsa_sample_select",
    )(page_table, iq8, w8, ik_new, page_table, pool_ik_t)


def _dsa_sample_attend_kernel(rows_ref, flag_ref, aq_ref, knew_ref, vnew_ref, kpool_ref, vpool_ref,
                              ya_ref, kbuf, vbuf, sem, *, topk, dh, scale):
    b = pl.program_id(0)
    nb = pl.num_programs(0)

    def row_copies(bb, t, slot):
        r = rows_ref[bb, t]
        dst = pl.ds(t * H_A, H_A)
        return (pltpu.make_async_copy(kpool_ref.at[r], kbuf.at[slot, dst, :], sem.at[0, slot]),
                pltpu.make_async_copy(vpool_ref.at[r], vbuf.at[slot, dst, :], sem.at[1, slot]))

    def start_all(bb, slot):
        def body(t, _):
            ck, cv = row_copies(bb, t, slot)
            ck.start()
            cv.start()
            return 0
        lax.fori_loop(0, topk, body, 0, unroll=8)

    slot = b % 2

    @pl.when(b == 0)
    def _():
        start_all(0, 0)

    @pl.when(b + 1 < nb)
    def _():
        start_all(b + 1, 1 - slot)

    def wait_body(t, _):
        ck, cv = row_copies(b, t, slot)
        ck.wait()
        cv.wait()
        return 0
    lax.fori_loop(0, topk, wait_body, 0, unroll=8)

    take_new = (lax.broadcasted_iota(jnp.int32, (topk, 1), 0) == topk - 1) & (flag_ref[b] > 0)
    aq = aq_ref[0]
    for h in range(H_A):
        hs = slice(h * dh, (h + 1) * dh)
        kh = kbuf[slot, pl.ds(h, topk, stride=H_A), :]
        vh = vbuf[slot, pl.ds(h, topk, stride=H_A), :]
        kh = jnp.where(take_new, knew_ref[0, h:h + 1, :], kh).astype(BF16)
        vh = jnp.where(take_new, vnew_ref[0, h:h + 1, :], vh).astype(BF16)
        q8 = jnp.broadcast_to(aq[:, hs], (8, dh))
        s = _nt(q8, kh) * scale
        m = jnp.max(s, axis=1, keepdims=True)
        p = jnp.exp(s - m)
        p = p / jnp.sum(p, axis=1, keepdims=True)
        ya_ref[0, :, hs] = _mm(p.astype(BF16), vh)[0:1, :]


def _dsa_sample_attend(rows, flags, aqb, k_new, v_new, pool_k, pool_v, *, topk):
    Bd, _, mix_a = aqb.shape
    dh = pool_k.shape[2]
    new_spec = pl.BlockSpec((1, H_A, dh), lambda b, r, f: (b, 0, 0))
    grid_spec = pltpu.PrefetchScalarGridSpec(
        num_scalar_prefetch=2,
        grid=(Bd,),
        in_specs=[pl.BlockSpec((1, 1, mix_a), lambda b, r, f: (b, 0, 0)),
                  new_spec, new_spec,
                  pl.BlockSpec(memory_space=pl.ANY),
                  pl.BlockSpec(memory_space=pl.ANY)],
        out_specs=pl.BlockSpec((1, 1, mix_a), lambda b, r, f: (b, 0, 0)),
        scratch_shapes=[pltpu.VMEM((2, topk * H_A, dh), F32),
                        pltpu.VMEM((2, topk * H_A, dh), F32),
                        pltpu.SemaphoreType.DMA((2, 2))],
    )
    return pl.pallas_call(
        functools.partial(_dsa_sample_attend_kernel, topk=topk, dh=dh, scale=dh ** -0.5),
        grid_spec=grid_spec,
        out_shape=jax.ShapeDtypeStruct((Bd, 1, mix_a), F32),
        compiler_params=_cparams(("arbitrary",)),
        name="dsa_sample_attend",
    )(rows, flags, aqb, k_new, v_new, pool_k, pool_v)


def _mem_kv_kernel(mem_ref, nm_ref, wk_ref, wv_ref, kn_ref, k_ref, v_ref, *, dh):
    hm = _rms(mem_ref[...], nm_ref[...]).astype(BF16)
    kk = _mm(hm, wk_ref[...])
    vv = _mm(hm, wv_ref[...])
    for h in range(H_C):
        hs = slice(h * dh, (h + 1) * dh)
        k_ref[:, h, :] = _rms(kk[:, hs], kn_ref[...])
        v_ref[:, h, :] = vv[:, hs]


def _mem_kv(mem2d, norm_mem, w_ck, w_cv, ck_norm, *, tm):
    rows, d = mem2d.shape
    dh = d // H_C
    row = pl.BlockSpec((tm, d), lambda i: (i, 0))
    heads = pl.BlockSpec((tm, H_C, dh), lambda i: (i, 0, 0))
    const = lambda shp: pl.BlockSpec(shp, lambda i: (0, 0))
    return pl.pallas_call(
        functools.partial(_mem_kv_kernel, dh=dh),
        grid=(rows // tm,),
        in_specs=[row, const((1, d)), const((d, d)), const((d, d)), const((1, dh))],
        out_specs=[heads, heads],
        out_shape=[jax.ShapeDtypeStruct((rows, H_C, dh), F32)] * 2,
        compiler_params=_cparams(("parallel",)),
        name="mem_kv",
    )(mem2d, norm_mem, w_ck, w_cv, ck_norm)


def _out_cq_kernel(x_ref, ym_ref, ya_ref, wo_ref, nc_ref, wq_ref, qn_ref, x1_ref, qc_ref, *, mix_m, dh):
    upd = (_mm(ym_ref[...].astype(BF16), wo_ref[0:mix_m, :])
           + _mm(ya_ref[...].astype(BF16), wo_ref[mix_m:, :]))
    x1 = x_ref[...] + upd
    x1_ref[...] = x1
    hq = _mm(_rms(x1, nc_ref[...]).astype(BF16), wq_ref[...])
    for h in range(H_C):
        hs = slice(h * dh, (h + 1) * dh)
        qc_ref[:, hs] = _rms(hq[:, hs], qn_ref[...]).astype(BF16)


def _out_cq(x2d, ym, ya, w_out, norm_cross, w_cq, cq_norm, *, tm):
    rows, d = x2d.shape
    mix_m = ym.shape[1]
    mix_a = ya.shape[1]
    dh = d // H_C
    row = lambda wdt: pl.BlockSpec((tm, wdt), lambda i: (i, 0))
    const = lambda shp: pl.BlockSpec(shp, lambda i: (0, 0))
    return pl.pallas_call(
        functools.partial(_out_cq_kernel, mix_m=mix_m, dh=dh),
        grid=(rows // tm,),
        in_specs=[row(d), row(mix_m), row(mix_a), const((mix_m + mix_a, d)), const((1, d)),
                  const((d, d)), const((1, dh))],
        out_specs=[row(d), row(d)],
        out_shape=[jax.ShapeDtypeStruct((rows, d), F32), jax.ShapeDtypeStruct((rows, d), BF16)],
        compiler_params=_cparams(("parallel",)),
        name="out_cq",
    )(x2d, ym, ya, w_out, norm_cross, w_cq, cq_norm)


def _cross_kernel(q_ref, k_hbm, v_hbm, o_ref, kv_buf, sem, *, dh, scale):
    b = pl.program_id(0)
    t = pl.program_id(1)
    nb = pl.num_programs(0)
    slot = b % 2

    def head_copies(bb, sl):
        cps = []
        for h in range(H_C):
            cps.append(pltpu.make_async_copy(k_hbm.at[bb, :, h, :], kv_buf.at[sl, 0, h], sem.at[sl]))
            cps.append(pltpu.make_async_copy(v_hbm.at[bb, :, h, :], kv_buf.at[sl, 1, h], sem.at[sl]))
        return cps

    @pl.when(t == 0)
    def _():
        @pl.when(b == 0)
        def _():
            for cp in head_copies(0, 0):
                cp.start()

        @pl.when(b + 1 < nb)
        def _():
            for cp in head_copies(b + 1, 1 - slot):
                cp.start()

        for cp in head_copies(b, slot):
            cp.wait()

    q = q_ref[0]
    rows = q.shape[0]
    if rows < 8:
        q = jnp.broadcast_to(q, (8, q.shape[1]))
    for h in range(H_C):
        hs = slice(h * dh, (h + 1) * dh)
        kb = kv_buf[slot, 0, h].astype(BF16)
        vb = kv_buf[slot, 1, h].astype(BF16)
        s = _nt(q[:, hs], kb) * scale
        m = jnp.max(s, axis=1, keepdims=True)
        p = jnp.exp(s - m)
        p = p / jnp.sum(p, axis=1, keepdims=True)
        o = _mm(p.astype(BF16), vb)
        o_ref[0, :, hs] = o[0:rows].astype(BF16)


def _cross(qc, mem_k, mem_v, *, tq):
    B, T, d = qc.shape
    M = mem_k.shape[1]
    dh = d // H_C
    return pl.pallas_call(
        functools.partial(_cross_kernel, dh=dh, scale=dh ** -0.5),
        grid=(B, T // tq),
        in_specs=[pl.BlockSpec((1, tq, d), lambda b, t: (b, t, 0)),
                  pl.BlockSpec(memory_space=pl.ANY),
                  pl.BlockSpec(memory_space=pl.ANY)],
        out_specs=pl.BlockSpec((1, tq, d), lambda b, t: (b, t, 0)),
        out_shape=jax.ShapeDtypeStruct((B, T, d), BF16),
        scratch_shapes=[pltpu.VMEM((2, 2, H_C, M, dh), F32), pltpu.SemaphoreType.DMA((2,))],
        compiler_params=_cparams(("arbitrary", "arbitrary")),
        name="cross_attn",
    )(qc, mem_k, mem_v)


def _gelu_tanh(x):
    return 0.5 * x * (1.0 + jnp.tanh(np.sqrt(2.0 / np.pi) * (x + 0.044715 * (x * x * x))))


def _ffn_front(x1_ref, o_ref, wco_ref, nf_ref, x2_ref, hb_ref, acc_ref):
    x2 = x1_ref[0] + _mm(o_ref[0], wco_ref[...])
    x2_ref[...] = x2
    hb_ref[...] = _rms(x2, nf_ref[...]).astype(BF16)
    acc_ref[...] = jnp.zeros_like(acc_ref)


def _ffn_prompt_kernel(x1_ref, o_ref, wco_ref, nf_ref, wua_ref, wug_ref, cwa_ref, cwg_ref,
                       cba_ref, cbg_ref, wd_ref, ha_ref, hg_ref,
                       y_ref, ca_ref, cg_ref, x2_ref, hb_ref, acc_ref, carry_ref, *, tm, rs):
    t = pl.program_id(1)
    j = pl.program_id(2)
    nj = pl.num_programs(2)

    @pl.when(j == 0)
    def _():
        _ffn_front(x1_ref, o_ref, wco_ref, nf_ref, x2_ref, hb_ref, acc_ref)

    @pl.when(t == 0)
    def _():
        carry_ref[j, 0, 6:8, :] = ha_ref[0]
        carry_ref[j, 1, 6:8, :] = hg_ref[0]

    rid = lax.broadcasted_iota(jnp.int32, (rs, 1), 0)

    def conv_part(hb, part, wu_ref, cw_ref, cb_ref):
        u = _mm(hb, wu_ref[...])
        p2 = carry_ref[j, part, 6:7, :]
        p1 = carry_ref[j, part, 7:8, :]
        um1 = jnp.where(rid == 0, p1, pltpu.roll(u, 1, 0))
        um2 = jnp.where(rid == 0, p2, jnp.where(rid == 1, p1, pltpu.roll(u, 2, 0)))
        carry_ref[j, part] = u[rs - 8:rs, :]
        return cb_ref[...] + um2 * cw_ref[0:1, :] + um1 * cw_ref[1:2, :] + u * cw_ref[2:3, :]

    def sub_body(r, _):
        r0 = pl.multiple_of(r * rs, rs)
        hb = hb_ref[pl.ds(r0, rs), :]
        a = conv_part(hb, 0, wua_ref, cwa_ref, cba_ref)
        g = conv_part(hb, 1, wug_ref, cwg_ref, cbg_ref)
        acc_ref[pl.ds(r0, rs), :] += _mm((_gelu_tanh(g) * a).astype(BF16), wd_ref[...])
        return 0

    lax.fori_loop(0, tm // rs, sub_body, 0)
    ca_ref[0, 0] = carry_ref[j, 0, 6:8, :]
    cg_ref[0, 0] = carry_ref[j, 1, 6:8, :]

    @pl.when(j == nj - 1)
    def _():
        y_ref[0] = x2_ref[...] + acc_ref[...]


def _ffn_prompt(x1, o, w_co, norm_ffn, w_up, conv_w, conv_b, w_down, hist, *, tm, tf):
    B, T, d = x1.shape
    d_ff = w_down.shape[0]
    nj = d_ff // tf
    nt = T // tm
    idx3 = lambda b, t, j: (b, t, 0)
    c2 = lambda shp: pl.BlockSpec(shp, lambda b, t, j: (0, 0))
    return pl.pallas_call(
        functools.partial(_ffn_prompt_kernel, tm=tm, rs=min(512, tm)),
        grid=(B, nt, nj),
        in_specs=[pl.BlockSpec((1, tm, d), idx3), pl.BlockSpec((1, tm, d), idx3),
                  pl.BlockSpec((d, d), lambda b, t, j: (0, 0), pipeline_mode=pl.Buffered(1)), c2((1, d)),
                  pl.BlockSpec((d, tf), lambda b, t, j: (0, j)),
                  pl.BlockSpec((d, tf), lambda b, t, j: (0, nj + j)),
                  pl.BlockSpec((CONV_W, tf), lambda b, t, j: (0, j)),
                  pl.BlockSpec((CONV_W, tf), lambda b, t, j: (0, nj + j)),
                  pl.BlockSpec((1, tf), lambda b, t, j: (0, j)),
                  pl.BlockSpec((1, tf), lambda b, t, j: (0, nj + j)),
                  pl.BlockSpec((tf, d), lambda b, t, j: (j, 0)),
                  pl.BlockSpec((1, CONV_W - 1, tf), lambda b, t, j: (b, 0, j)),
                  pl.BlockSpec((1, CONV_W - 1, tf), lambda b, t, j: (b, 0, nj + j))],
        out_specs=[pl.BlockSpec((1, tm, d), idx3),
                   pl.BlockSpec((1, 1, CONV_W - 1, tf), lambda b, t, j: (b, t, 0, j)),
                   pl.BlockSpec((1, 1, CONV_W - 1, tf), lambda b, t, j: (b, t, 0, j))],
        out_shape=[jax.ShapeDtypeStruct((B, T, d), F32),
                   jax.ShapeDtypeStruct((B, nt, CONV_W - 1, d_ff), F32),
                   jax.ShapeDtypeStruct((B, nt, CONV_W - 1, d_ff), F32)],
        scratch_shapes=[pltpu.VMEM((tm, d), F32), pltpu.VMEM((tm, d), BF16), pltpu.VMEM((tm, d), F32),
                        pltpu.VMEM((nj, 2, 8, tf), F32)],
        compiler_params=_cparams(("arbitrary", "arbitrary", "arbitrary")),
        name="ffn_prompt",
    )(x1, o, w_co, norm_ffn, w_up, w_up, conv_w, conv_w, conv_b, conv_b, w_down, hist, hist)


def _ffn_sample_kernel(x1_ref, o_ref, wco_ref, nf_ref, wua_ref, wug_ref, cwa_ref, cwg_ref,
                       cba_ref, cbg_ref, wd_ref, h0a_ref, h0g_ref, h1a_ref, h1g_ref,
                       y_ref, ua_ref, ug_ref, x2_ref, hb_ref, acc_ref):
    j = pl.program_id(0)
    nj = pl.num_programs(0)

    @pl.when(j == 0)
    def _():
        _ffn_front(x1_ref, o_ref, wco_ref, nf_ref, x2_ref, hb_ref, acc_ref)

    hb = hb_ref[...]

    def conv_part(wu_ref, cw_ref, cb_ref, h0_ref, h1_ref, u_out_ref):
        u = _mm(hb, wu_ref[...])
        u_out_ref[...] = u
        return cb_ref[...] + h0_ref[...] * cw_ref[0:1, :] + h1_ref[...] * cw_ref[1:2, :] + u * cw_ref[2:3, :]

    a = conv_part(wua_ref, cwa_ref, cba_ref, h0a_ref, h1a_ref, ua_ref)
    g = conv_part(wug_ref, cwg_ref, cbg_ref, h0g_ref, h1g_ref, ug_ref)
    acc_ref[...] += _mm((_gelu_tanh(g) * a).astype(BF16), wd_ref[...])

    @pl.when(j == nj - 1)
    def _():
        y_ref[0] = x2_ref[...] + acc_ref[...]


def _ffn_sample(x1, o, w_co, norm_ffn, w_up, conv_w, conv_b, w_down, h0, h1, *, tf):
    _, rows, d = x1.shape
    d_ff = w_down.shape[0]
    nj = d_ff // tf
    c2 = lambda shp: pl.BlockSpec(shp, lambda j: (0, 0))
    c3 = lambda shp: pl.BlockSpec(shp, lambda j: (0, 0, 0))
    col_a = lambda r: pl.BlockSpec((r, tf), lambda j: (0, j))
    col_g = lambda r: pl.BlockSpec((r, tf), lambda j: (0, nj + j))
    return pl.pallas_call(
        _ffn_sample_kernel,
        grid=(nj,),
        in_specs=[c3((1, rows, d)), c3((1, rows, d)), c2((d, d)), c2((1, d)),
                  col_a(d), col_g(d), col_a(CONV_W), col_g(CONV_W), col_a(1), col_g(1),
                  pl.BlockSpec((tf, d), lambda j: (j, 0)),
                  col_a(rows), col_g(rows), col_a(rows), col_g(rows)],
        out_specs=[c3((1, rows, d)), col_a(rows), col_a(rows)],
        out_shape=[jax.ShapeDtypeStruct((1, rows, d), F32),
                   jax.ShapeDtypeStruct((rows, d_ff), F32),
                   jax.ShapeDtypeStruct((rows, d_ff), F32)],
        scratch_shapes=[pltpu.VMEM((rows, d), F32), pltpu.VMEM((rows, d), BF16), pltpu.VMEM((rows, d), F32)],
        compiler_params=_cparams(("arbitrary",)),
        name="ffn_sample",
    )(x1, o, w_co, norm_ffn, w_up, w_up, conv_w, conv_w, conv_b, conv_b, w_down, h0, h0, h1, h1)


def _rope_tables(pos, dh_a):
    posf = pos.astype(F32)[:, None]
    half_a = dh_a // 2
    inv_a = ROPE_THETA ** (-jnp.arange(half_a, dtype=F32) / half_a)
    ang_a = posf * inv_a[None, :]
    cos_a, sin_a = jnp.cos(ang_a), jnp.sin(ang_a)
    half_i = D_IDX // 2
    inv_i = ROPE_THETA ** (-jnp.arange(half_i, dtype=F32) / half_i)
    ang_i = posf * inv_i[None, :]
    cos_i, sin_i = jnp.cos(ang_i), jnp.sin(ang_i)
    z = jnp.zeros_like(sin_i)
    one = jnp.ones_like(sin_i)
    return jnp.concatenate([
        cos_a, cos_a, -sin_a, sin_a,
        cos_i, cos_i, cos_i, cos_i,
        z, sin_i, z, sin_i,
        -sin_i, z, -sin_i, z,
        cos_i, cos_i, one, one,
        z, sin_i, z, z,
        -sin_i, z, z, z,
    ], axis=1)


def kernel(x_prompt, x_sample, mem_prompt, cache_k, cache_v, cache_idx_k, cache_mem_k, cache_mem_v,
           state_mlstm_c, state_mlstm_n, state_mlstm_m, state_conv, page_table,
           norm_mix, w_in, b_if, mlstm_norm, q_norm, k_norm, w_out, norm_cross, norm_mem,
           w_cq, w_ck, w_cv, w_co, cq_norm, ck_norm, norm_ffn, w_up, conv_w, conv_b, w_down):
    B, S, D = x_prompt.shape
    Bd, T, _ = x_sample.shape
    assert T == 1 and w_in.shape[0] == 1
    n_pool, page = cache_k.shape[1], cache_k.shape[2]
    n_pages = page_table.shape[1]
    past = n_pages * page
    mix_m = mlstm_norm.shape[1]
    dh_m = mix_m // H_M
    dh_a = q_norm.shape[1]
    mix_a = H_A * dh_a
    d_ff = w_down.shape[1]
    M = mem_prompt.shape[1]
    chunk = min(128, S)
    topk_p = min(TOPK_MAX, S // 4)
    topk_s = min(TOPK_MAX, (past + T) // 4)

    w = w_in[0]
    o_gate = 4 * mix_m
    o_aq = o_gate + 2 * H_M
    o_iq = o_aq + 3 * mix_a
    o_ik = o_iq + H_IDX * D_IDX
    o_iw = o_ik + D_IDX
    tail_pad = LANES - (D_IDX + H_IDX + 2 * H_M)
    w_r = jnp.concatenate([w[:, :o_gate], w[:, o_aq:o_iq], w[:, o_iq:o_ik], w[:, o_ik:o_iw],
                           w[:, o_iw:o_iw + H_IDX], w[:, o_gate:o_aq],
                           jnp.zeros((D, tail_pad), w.dtype)], axis=1).astype(BF16)
    bias_tail = jnp.concatenate([jnp.zeros((D_IDX + H_IDX,), F32), b_if[0].astype(F32),
                                 jnp.zeros((tail_pad,), F32)])[None, :]
    w_out_b = w_out[0].astype(BF16)
    w_cq_b, w_ck_b, w_cv_b, w_co_b = (a[0].astype(BF16) for a in (w_cq, w_ck, w_cv, w_co))
    w_up_b = w_up[0].astype(BF16)
    w_down_b = w_down[0].astype(BF16)
    row = lambda a: a[0][None, :]

    def split_misc(misc):
        ik = misc[:, :D_IDX]
        li = misc[:, D_IDX + H_IDX:D_IDX + H_IDX + H_M]
        lf = misc[:, D_IDX + H_IDX + H_M:D_IDX + H_IDX + 2 * H_M]
        return ik, li, lf

    tm_in = min(512, S)
    tab_p = _rope_tables(jnp.arange(S), dh_a)
    (mq, mk, mv, mo, aqb, ak, av, akb, avb, iqb, misc) = _in_proj(
        x_prompt.reshape(B * S, D), row(norm_mix), w_r, bias_tail, row(q_norm), row(k_norm), tab_p,
        tm=tm_in, tab_tiles=S // tm_in, mix_m=mix_m, mix_a=mix_a)
    ik_p, li_p, lf_p = split_misc(misc)
    r3 = lambda a: a.reshape(B, S, a.shape[-1])
    gcol = jnp.concatenate([li_p, lf_p], axis=-1).reshape(B, S, 2 * H_M)
    grow = gcol.reshape(B, S // chunk, chunk, 2 * H_M).transpose(0, 1, 3, 2)
    y_m, c_p, n_p, m_p = _mlstm_prompt(r3(mq), r3(mk), r3(mv), r3(mo), grow, gcol, row(mlstm_norm), chunk=chunk)

    tq = min(256, S)
    wk = min(512, S)
    ikt = ik_p.astype(BF16).reshape(B, S // wk, wk, D_IDX).transpose(0, 1, 3, 2)
    y_a = _dsa_prompt(r3(iqb), r3(misc), ikt, r3(aqb), r3(akb), r3(avb), tq=tq, w=wk, topk=topk_p)

    mk_p, mv_p = _mem_kv(mem_prompt.reshape(B * M, D), row(norm_mem), w_ck_b, w_cv_b, row(ck_norm),
                         tm=min(256, B * M))
    x1, qc = _out_cq(x_prompt.reshape(B * S, D), y_m.reshape(B * S, mix_m), y_a.reshape(B * S, mix_a),
                     w_out_b, row(norm_cross), w_cq_b, row(cq_norm), tm=min(512, S))
    dh_c = D // H_C
    o_c = _cross(qc.reshape(B, S, D), mk_p.reshape(B, M, H_C, dh_c), mv_p.reshape(B, M, H_C, dh_c), tq=min(512, S))
    tf = d_ff // 2 if (d_ff // 2) % LANES == 0 else d_ff
    xp, conv_a, conv_g = _ffn_prompt(x1.reshape(B, S, D), o_c, w_co_b, row(norm_ffn), w_up_b, conv_w[0],
                                     conv_b[0][None, :], w_down_b,
                                     jnp.zeros((B, CONV_W - 1, 2 * d_ff), F32), tm=min(512, S), tf=tf)
    conv_p = jnp.concatenate([conv_a[:, -1], conv_g[:, -1]], axis=-1)

    tab_s = jnp.broadcast_to(_rope_tables(jnp.full((1,), past, jnp.int32), dh_a), (Bd, 8 * LANES))
    (mq_s, mk_s, mv_s, mo_s, aqb_s, ak_s, av_s, _, _, iqb_s, misc_s) = _in_proj(
        x_sample.reshape(Bd, D), row(norm_mix), w_r, bias_tail, row(q_norm), row(k_norm), tab_s,
        tm=Bd, tab_tiles=1, mix_m=mix_m, mix_a=mix_a)
    ik_s, li_s, lf_s = split_misc(misc_s)
    gs = jnp.concatenate([li_s, lf_s, state_mlstm_m[0].astype(F32)], axis=-1)[:, None, :]
    e1 = lambda a: a[:, None, :]
    y_ms, c_s, n_s, m_s = _mlstm_sample(e1(mq_s), e1(mk_s), e1(mv_s), e1(mo_s), gs,
                                        state_mlstm_c[0], state_mlstm_n[0].reshape(Bd, 1, mix_m),
                                        row(mlstm_norm))

    iq8 = jnp.pad(iqb_s.reshape(Bd, H_IDX, D_IDX), ((0, 0), (0, 8 - H_IDX), (0, 0)))
    w8 = jnp.pad(misc_s[:, D_IDX:D_IDX + H_IDX], ((0, 0), (0, 8 - H_IDX)))[:, :, None]
    assert n_pages <= 256
    rows_t, flags = _dsa_sample_select(page_table, iq8, w8, e1(ik_s), jnp.swapaxes(cache_idx_k[0], 1, 2),
                                       topk=topk_s, cw=min(512, past))
    y_as = _dsa_sample_attend(rows_t[:, :Bd].T, flags[:, 0], e1(aqb_s), ak_s, av_s,
                              cache_k[0].reshape(n_pool * page, H_A, dh_a),
                              cache_v[0].reshape(n_pool * page, H_A, dh_a), topk=topk_s)

    x1_s, qc_s = _out_cq(x_sample.reshape(Bd, D), y_ms.reshape(Bd, mix_m), y_as.reshape(Bd, mix_a),
                         w_out_b, row(norm_cross), w_cq_b, row(cq_norm), tm=Bd)
    o_s = _cross(qc_s.reshape(Bd, 1, D), cache_mem_k[0], cache_mem_v[0], tq=1)
    xs, u_a, u_g = _ffn_sample(x1_s.reshape(1, Bd, D), o_s.reshape(1, Bd, D), w_co_b, row(norm_ffn), w_up_b,
                               conv_w[0], conv_b[0][None, :], w_down_b,
                               state_conv[0, :, 0, :], state_conv[0, :, 1, :], tf=tf)
    conv_s = jnp.stack([state_conv[0, :, 1, :], jnp.concatenate([u_a, u_g], axis=-1)], axis=1)

    lead = lambda a: a[None]
    return (xp, xs.reshape(Bd, 1, D),
            lead(ak.reshape(B, S, H_A, dh_a)), lead(av.reshape(B, S, H_A, dh_a)), lead(ik_p.reshape(B, S, D_IDX)),
            lead(c_p), lead(n_p), lead(m_p[:, :, 0]),
            lead(mk_p.reshape(B, M, H_C, D // H_C)), lead(mv_p.reshape(B, M, H_C, D // H_C)), lead(conv_p),
            lead(ak_s.reshape(Bd, 1, H_A, dh_a)), lead(av_s.reshape(Bd, 1, H_A, dh_a)),
            lead(ik_s.reshape(Bd, 1, D_IDX)),
            lead(c_s), lead(n_s.reshape(Bd, H_M, dh_m)), lead(m_s[:, 0, :H_M]), lead(conv_s))
```

```python
import functools

import jax
import jax.numpy as jnp
import numpy as np
from jax import lax
from jax.experimental import pallas as pl
from jax.experimental.pallas import tpu as pltpu

F32 = jnp.float32
BF16 = jnp.bfloat16

H_M = 4
H_A = 4
H_IDX = 4
D_IDX = 64
H_C = 4
TOPK_MAX = 256
CONV_W = 3
ROPE_THETA = 10000.0
EPS = 1e-6
LOG2E = 1.4426950408889634
NEG_INF = float("-inf")
POS_INF = float("inf")

LANES = 128
VMEM_LIMIT = 56 * 1024 * 1024
N_BISECT = 20


def _cparams(sem):
    return pltpu.CompilerParams(dimension_semantics=sem, vmem_limit_bytes=VMEM_LIMIT)


def _nt(a, b):
    return lax.dot_general(a, b, (((1,), (1,)), ((), ())), preferred_element_type=F32)


def _tn(a, b):
    return lax.dot_general(a, b, (((0,), (0,)), ((), ())), preferred_element_type=F32)


def _mm(a, b):
    return jnp.dot(a, b, preferred_element_type=F32)


def _rms(x, g):
    ms = jnp.mean(x * x, axis=-1, keepdims=True)
    return x * lax.rsqrt(ms + EPS) * g


def _sigmoid(x):
    return 1.0 / (1.0 + jnp.exp(-x))


def _in_proj_kernel(x_ref, nm_ref, w_ref, bias_ref, qn_ref, kn_ref, tab_ref,
                    mq_ref, mk_ref, mv_ref, mo_ref, aqb_ref, ak_ref, av_ref, akb_ref, avb_ref,
                    iqb_ref, misc_ref, *, mix_m, mix_a, dh_m, dh_a):
    h = _rms(x_ref[...], nm_ref[...]).astype(BF16)

    def proj(lo, width):
        return _mm(h, w_ref[:, lo:lo + width])

    o_mq, o_mk, o_mv, o_mo = 0, mix_m, 2 * mix_m, 3 * mix_m
    o_aq = 4 * mix_m
    o_ak = o_aq + mix_a
    o_av = o_ak + mix_a
    o_iq = o_av + mix_a
    o_tail = o_iq + H_IDX * D_IDX

    mq_ref[...] = proj(o_mq, mix_m)
    mk_ref[...] = proj(o_mk, mix_m) * (dh_m ** -0.5)
    mv_ref[...] = proj(o_mv, mix_m)
    mo_ref[...] = proj(o_mo, mix_m)

    t_a = tab_ref[:, 0:LANES]
    t_i = tab_ref[:, LANES:2 * LANES]
    lane = lax.broadcasted_iota(jnp.int32, t_a.shape, 1)
    first_a = lane < dh_a // 2
    r_a = pltpu.roll(t_a, dh_a // 2, 1)
    cos_a = jnp.where(first_a, t_a, r_a)
    sin_a = jnp.where(first_a, -r_a, t_a)
    first_i = (lane & (D_IDX - 1)) < D_IDX // 2
    c_i = jnp.where(first_i, t_i, pltpu.roll(t_i, D_IDX // 2, 1))
    s1_i = jnp.where(first_i, 0.0, t_i)
    s2_i = jnp.where(first_i, -pltpu.roll(t_i, LANES - D_IDX // 2, 1), 0.0)
    in_key = lane < D_IDX
    c_t = jnp.where(in_key, c_i, 1.0)
    s1_t = jnp.where(in_key, s1_i, 0.0)
    s2_t = jnp.where(in_key, s2_i, 0.0)

    def norm_rope(z, g_ref):
        outs = []
        for hh in range(mix_a // dh_a):
            zh = _rms(z[:, hh * dh_a:(hh + 1) * dh_a], g_ref[...])
            outs.append(zh * cos_a + pltpu.roll(zh, dh_a // 2, 1) * sin_a)
        return outs

    aq = norm_rope(proj(o_aq, mix_a), qn_ref)
    aqb_ref[...] = jnp.concatenate(aq, axis=1).astype(BF16)
    ak = norm_rope(proj(o_ak, mix_a), kn_ref)
    av = proj(o_av, mix_a)
    for hh in range(mix_a // dh_a):
        ak_ref[:, hh, :] = ak[hh]
        av_ref[:, hh, :] = av[:, hh * dh_a:(hh + 1) * dh_a]
    akb_ref[...] = jnp.concatenate(ak, axis=1).astype(BF16)
    avb_ref[...] = av.astype(BF16)

    ziq = proj(o_iq, H_IDX * D_IDX)
    cols = []
    for c in range(H_IDX * D_IDX // LANES):
        zc = ziq[:, c * LANES:(c + 1) * LANES]
        cols.append(zc * c_i + pltpu.roll(zc, D_IDX // 2, 1) * s1_i
                    + pltpu.roll(zc, LANES - D_IDX // 2, 1) * s2_i)
    iqb_ref[...] = jnp.concatenate(cols, axis=1).astype(BF16)

    zt = proj(o_tail, LANES) + bias_ref[...]
    zt = zt * c_t + pltpu.roll(zt, D_IDX // 2, 1) * s1_t + pltpu.roll(zt, LANES - D_IDX // 2, 1) * s2_t
    f_lo = D_IDX + H_IDX + H_M
    log_sig = jnp.minimum(zt, 0.0) - jnp.log(1.0 + jnp.exp(-jnp.abs(zt)))
    misc_ref[...] = jnp.where((lane >= f_lo) & (lane < f_lo + H_M), log_sig, zt)


def _in_proj(x2d, norm_mix, w_r, bias_tail, q_norm, k_norm, tab, *, tm, tab_tiles, mix_m, mix_a):
    rows, d = x2d.shape
    dh_m = mix_m // H_M
    dh_a = mix_a // H_A
    nw = w_r.shape[1]
    grid = (rows // tm,)
    row_spec = lambda wdt: pl.BlockSpec((tm, wdt), lambda i: (i, 0))
    const = lambda shp: pl.BlockSpec(shp, lambda i: (0, 0))
    out_shapes = [
        jax.ShapeDtypeStruct((rows, mix_m), F32),
        jax.ShapeDtypeStruct((rows, mix_m), F32),
        jax.ShapeDtypeStruct((rows, mix_m), F32),
        jax.ShapeDtypeStruct((rows, mix_m), F32),
        jax.ShapeDtypeStruct((rows, mix_a), BF16),
        jax.ShapeDtypeStruct((rows, H_A, dh_a), F32),
        jax.ShapeDtypeStruct((rows, H_A, dh_a), F32),
        jax.ShapeDtypeStruct((rows, mix_a), BF16),
        jax.ShapeDtypeStruct((rows, mix_a), BF16),
        jax.ShapeDtypeStruct((rows, H_IDX * D_IDX), BF16),
        jax.ShapeDtypeStruct((rows, LANES), F32),
    ]
    head_spec = pl.BlockSpec((tm, H_A, dh_a), lambda i: (i, 0, 0))
    out_specs = ([row_spec(mix_m)] * 4 + [row_spec(mix_a), head_spec, head_spec, row_spec(mix_a), row_spec(mix_a)]
                 + [row_spec(H_IDX * D_IDX), row_spec(LANES)])
    return pl.pallas_call(
        functools.partial(_in_proj_kernel, mix_m=mix_m, mix_a=mix_a, dh_m=dh_m, dh_a=dh_a),
        grid=grid,
        in_specs=[row_spec(d), const((1, d)),
                  pl.BlockSpec((d, nw), lambda i: (0, 0), pipeline_mode=pl.Buffered(1)), const((1, LANES)),
                  const((1, dh_a)), const((1, dh_a)),
                  pl.BlockSpec((tm, 2 * LANES), lambda i: (i % tab_tiles, 0))],
        out_specs=out_specs,
        out_shape=out_shapes,
        compiler_params=_cparams(("parallel",)),
        name="in_proj",
    )(x2d, norm_mix, w_r, bias_tail, q_norm, k_norm, tab)


def _mlstm_prompt_kernel(q_ref, k_ref, v_ref, o_ref, grow_ref, gcol_ref, gain_ref,
                         y_ref, c_ref, n_ref, m_ref, cs_ref, ns_ref, ms_ref, *, chunk, d, nbp):
    c_idx = pl.program_id(1)
    L = chunk
    row_i = lax.broadcasted_iota(jnp.int32, (L, L), 0)
    col_i = lax.broadcasted_iota(jnp.int32, (L, L), 1)
    tril = col_i <= row_i
    triu = row_i <= col_i

    @pl.when(c_idx == 0)
    def _():
        cs_ref[...] = jnp.zeros_like(cs_ref)
        ns_ref[...] = jnp.zeros_like(ns_ref)
        ms_ref[...] = jnp.zeros_like(ms_ref)

    chains = [(bi, hd) for bi in range(nbp) for hd in range(H_M)]
    tril_b = tril.astype(BF16)
    csum = []
    for bi in range(nbp):
        g = gcol_ref[bi]
        g_hi = g.astype(BF16)
        r1 = g - g_hi.astype(F32)
        g_mid = r1.astype(BF16)
        g_lo = (r1 - g_mid.astype(F32)).astype(BF16)
        csum.append(_mm(tril_b, g_hi) + _mm(tril_b, g_mid) + _mm(tril_b, g_lo))
    st = []
    for bi, hd in chains:
        gr = grow_ref[bi, 0]
        gc = gcol_ref[bi]
        sidx = bi * H_M + hd
        hs = slice(hd * d, (hd + 1) * d)
        m = ms_ref[sidx, 0:1, 0:1]
        li_r = gr[hd:hd + 1, :]
        li_c = gc[:, hd:hd + 1]
        lf_c = gc[:, H_M + hd:H_M + hd + 1]
        b_c = csum[bi][:, H_M + hd:H_M + hd + 1]
        b_r = jnp.sum(jnp.where(triu, lf_c, 0.0), axis=0, keepdims=True)
        logd = jnp.where(tril, b_c - b_r + li_r, NEG_INF)
        inter = b_c + m
        m_t = jnp.maximum(inter, jnp.max(logd, axis=1, keepdims=True))
        st.append(dict(bi=bi, sidx=sidx, hs=hs, m=m, li_c=li_c, b_c=b_c, inter=inter, m_t=m_t,
                       dmat=jnp.exp(logd - m_t)))
    for c in st:
        q = q_ref[c["bi"], :, c["hs"]]
        c["q"] = q
        c["kb"] = k_ref[c["bi"], :, c["hs"]].astype(BF16)
        qb = q.astype(BF16)
        c["C"] = cs_ref[c["sidx"]]
        c["s"] = _nt(qb, c["kb"]) * c["dmat"]
        c["qc"] = _nt(qb, c["C"].astype(BF16))
    for c in st:
        v = v_ref[c["bi"], :, c["hs"]]
        c["v"] = v
        n = ns_ref[c["sidx"], 0:1, :]
        c["n"] = n
        g_inter = jnp.exp(c["inter"] - c["m_t"])
        num = g_inter * c["qc"] + _mm(c["s"].astype(BF16), v.astype(BF16))
        den = g_inter * jnp.sum(c["q"] * n, axis=1, keepdims=True) + jnp.sum(c["s"], axis=1, keepdims=True)
        h = num / jnp.maximum(jnp.abs(den), jnp.exp(-c["m_t"]))
        o = o_ref[c["bi"], :, c["hs"]]
        y_ref[c["bi"], :, c["hs"]] = _sigmoid(o) * _rms(h, gain_ref[:, c["hs"]])
    for c in st:
        m_new = c["m_t"][L - 1:L, :]
        b_last = c["b_c"][L - 1:L, :]
        g_prev = jnp.exp(b_last + c["m"] - m_new)
        w_c = jnp.exp(b_last - c["b_c"] + c["li_c"] - m_new)
        k = k_ref[c["bi"], :, c["hs"]]
        cs_ref[c["sidx"]] = g_prev * c["C"] + _tn((c["v"] * w_c).astype(BF16), c["kb"])
        ns_ref[c["sidx"], 0:1, :] = g_prev * c["n"] + jnp.sum(k * w_c, axis=0, keepdims=True)
        ms_ref[c["sidx"], 0:1, :] = jnp.broadcast_to(m_new, (1, LANES))

    @pl.when(c_idx == pl.num_programs(1) - 1)
    def _():
        for bi in range(nbp):
            for hd in range(H_M):
                sidx = bi * H_M + hd
                c_ref[bi, hd] = cs_ref[sidx]
                n_ref[bi, hd:hd + 1, :] = ns_ref[sidx, 0:1, :]
                m_ref[bi, hd:hd + 1, :] = ms_ref[sidx, 0:1, :]


def _mlstm_prompt(mq, mk, mv, mo, grow, gcol, gain, *, chunk):
    B, S, mix_m = mq.shape
    d = mix_m // H_M
    n_chunks = S // chunk
    nbp = 4 if B % 4 == 0 else (2 if B % 2 == 0 else 1)
    seq = pl.BlockSpec((nbp, chunk, mix_m), lambda b, c: (b, c, 0))
    return pl.pallas_call(
        functools.partial(_mlstm_prompt_kernel, chunk=chunk, d=d, nbp=nbp),
        grid=(B // nbp, n_chunks),
        in_specs=[seq, seq, seq, seq,
                  pl.BlockSpec((nbp, 1, 2 * H_M, chunk), lambda b, c: (b, c, 0, 0)),
                  pl.BlockSpec((nbp, chunk, 2 * H_M), lambda b, c: (b, c, 0)),
                  pl.BlockSpec((1, mix_m), lambda b, c: (0, 0))],
        out_specs=[seq,
                   pl.BlockSpec((nbp, H_M, d, d), lambda b, c: (b, 0, 0, 0)),
                   pl.BlockSpec((nbp, H_M, d), lambda b, c: (b, 0, 0)),
                   pl.BlockSpec((nbp, H_M, LANES), lambda b, c: (b, 0, 0))],
        out_shape=[jax.ShapeDtypeStruct((B, S, mix_m), F32),
                   jax.ShapeDtypeStruct((B, H_M, d, d), F32),
                   jax.ShapeDtypeStruct((B, H_M, d), F32),
                   jax.ShapeDtypeStruct((B, H_M, LANES), F32)],
        scratch_shapes=[pltpu.VMEM((nbp * H_M, d, d), F32), pltpu.VMEM((nbp * H_M, 8, d), F32),
                        pltpu.VMEM((nbp * H_M, 8, LANES), F32)],
        compiler_params=_cparams(("parallel", "arbitrary")),
        name="mlstm_prompt",
    )(mq, mk, mv, mo, grow, gcol, gain)


def _mlstm_sample_kernel(q_ref, k_ref, v_ref, o_ref, gs_ref, c_ref, n_ref, gain_ref,
                         y_ref, co_ref, no_ref, mo_ref, *, d):
    gs = gs_ref[0]
    eye = (lax.broadcasted_iota(jnp.int32, (d, d), 0) == lax.broadcasted_iota(jnp.int32, (d, d), 1))
    lane = lax.broadcasted_iota(jnp.int32, (1, LANES), 1)
    m_out = jnp.zeros((1, LANES), F32)
    for h in range(H_M):
        sl = slice(h * d, (h + 1) * d)
        q = q_ref[0, :, sl]
        k = k_ref[0, :, sl]
        v = v_ref[0, :, sl]
        o = o_ref[0, :, sl]
        li = gs[:, h:h + 1]
        lf = gs[:, H_M + h:H_M + h + 1]
        m = gs[:, 2 * H_M + h:2 * H_M + h + 1]
        C = c_ref[0, h]
        n = n_ref[0, :, sl]
        inter = lf + m
        m_t = jnp.maximum(inter, li)
        s = jnp.sum(q * k, axis=1, keepdims=True) * jnp.exp(li - m_t)
        g = jnp.exp(inter - m_t)
        q8 = jnp.broadcast_to(q, (8, d)).astype(BF16)
        cq = _nt(q8, C.astype(BF16))[0:1, :]
        num = g * cq + s * v
        den = g * jnp.sum(n * q, axis=1, keepdims=True) + s
        hh = num / jnp.maximum(jnp.abs(den), jnp.exp(-m_t))
        w = jnp.exp(li - m_t)
        v_col = jnp.sum(jnp.where(eye, v, 0.0), axis=1, keepdims=True)
        co_ref[0, h] = g * C + (w * v_col) * k
        no_ref[0, :, sl] = g * n + w * k
        m_out = jnp.where(lane == h, m_t, m_out)
        y_ref[0, :, sl] = _sigmoid(o) * _rms(hh, gain_ref[:, sl])
    mo_ref[0] = m_out


def _mlstm_sample(mq, mk, mv, mo, gs, c_state, n_state, gain):
    Bd, _, mix_m = mq.shape
    d = mix_m // H_M
    row = pl.BlockSpec((1, 1, mix_m), lambda b: (b, 0, 0))
    return pl.pallas_call(
        functools.partial(_mlstm_sample_kernel, d=d),
        grid=(Bd,),
        in_specs=[row, row, row, row,
                  pl.BlockSpec((1, 1, 3 * H_M), lambda b: (b, 0, 0)),
                  pl.BlockSpec((1, H_M, d, d), lambda b: (b, 0, 0, 0)),
                  row,
                  pl.BlockSpec((1, mix_m), lambda b: (0, 0))],
        out_specs=[row,
                   pl.BlockSpec((1, H_M, d, d), lambda b: (b, 0, 0, 0)),
                   row,
                   pl.BlockSpec((1, 1, LANES), lambda b: (b, 0, 0))],
        out_shape=[jax.ShapeDtypeStruct((Bd, 1, mix_m), F32),
                   jax.ShapeDtypeStruct((Bd, H_M, d, d), F32),
                   jax.ShapeDtypeStruct((Bd, 1, mix_m), F32),
                   jax.ShapeDtypeStruct((Bd, 1, LANES), F32)],
        compiler_params=_cparams(("parallel",)),
        name="mlstm_sample",
    )(mq, mk, mv, mo, gs, c_state, n_state, gain)


def _dsa_prompt_kernel(iq_ref, misc_ref, ikt_ref, aq_ref, ak_ref, av_ref, ya_ref, sc_ref, acc_ref,
                       *, tq, w, topk, dh, scale):
    i = pl.program_id(1)
    nk = ((i + 1) * tq + w - 1) // w
    kf = float(topk)
    nsub = w // LANES

    q_pos = i * tq + lax.broadcasted_iota(jnp.int32, (tq, 1), 0)
    lane_w = lax.broadcasted_iota(jnp.int32, (1, w), 1)
    iq = iq_ref[0]
    iq_h = [iq[:, h * D_IDX:(h + 1) * D_IDX] for h in range(H_IDX)]
    misc = misc_ref[0]
    w_h = [misc[:, D_IDX + h:D_IDX + h + 1] for h in range(H_IDX)]

    def score_body(c, carry):
        rmax, rmin = carry
        ikc = ikt_ref[0, c]
        score = jnp.zeros((tq, w), F32)
        for h in range(H_IDX):
            score = score + w_h[h] * jnp.maximum(_mm(iq_h[h], ikc), 0.0)
        valid = (c * w + lane_w) <= q_pos
        sc_ref[c] = jnp.where(valid, score, NEG_INF)
        rmax = jnp.maximum(rmax, jnp.max(jnp.where(valid, score, NEG_INF), axis=1, keepdims=True))
        rmin = jnp.minimum(rmin, jnp.min(jnp.where(valid, score, POS_INF), axis=1, keepdims=True))
        return rmax, rmin

    rmax, rmin = lax.fori_loop(0, nk, score_body,
                               (jnp.full((tq, 1), NEG_INF, F32), jnp.full((tq, 1), POS_INF, F32)))

    ge = lambda x, t: x >= t
    gt = lambda x, t: x > t

    rh = min(tq, LANES)
    groups = [pl.ds(r0, rh) for r0 in range(0, tq, rh)]
    part = lambda a: [a[r0:r0 + rh] for r0 in range(0, tq, rh)]

    def pass_acc(rows, fn, init, combine):
        def body(c, acc):
            x = sc_ref[c, rows, :]
            for j in range(nsub):
                acc = combine(acc, fn(x[:, j * LANES:(j + 1) * LANES]))
            return acc
        return lax.fori_loop(0, nk, body, jnp.full((rh, LANES), init, F32))

    def count_acc(rows, pred, thr):
        thr_b = jnp.broadcast_to(thr, (rh, LANES))
        return pass_acc(rows, lambda x: jnp.where(pred(x, thr_b), 1.0, 0.0), 0.0, jnp.add)

    def count(rows, pred, thr):
        return jnp.sum(count_acc(rows, pred, thr), axis=1, keepdims=True)

    def min_where(rows, pred, thr):
        thr_b = jnp.broadcast_to(thr, (rh, LANES))
        acc = pass_acc(rows, lambda x: jnp.where(pred(x, thr_b), x, POS_INF), POS_INF, jnp.minimum)
        return jnp.min(acc, axis=1, keepdims=True)

    def bis_body(_, carry):
        los, his, clos = carry
        mids = [0.5 * (lo + hi) for lo, hi in zip(los, his)]
        accs = [count_acc(rows, ge, mid) for rows, mid in zip(groups, mids)]
        cms = [jnp.sum(a, axis=1, keepdims=True) for a in accs]
        oks = [cm >= kf for cm in cms]
        return (tuple(jnp.where(ok, mid, lo) for ok, mid, lo in zip(oks, mids, los)),
                tuple(jnp.where(ok, hi, mid) for ok, mid, hi in zip(oks, mids, his)),
                tuple(jnp.where(ok, cm, cl) for ok, cm, cl in zip(oks, cms, clos)))

    los, _, clos = lax.fori_loop(
        0, N_BISECT, bis_body,
        (tuple(part(rmin)), tuple(part(rmax + jnp.abs(rmax) + 1.0)), tuple(part((q_pos + 1).astype(F32)))))

    def finish_rows(rows, qp, rmin_h, lo, c_lo):
        active = (qp + 1) > topk
        unresolved = jnp.max(jnp.where(active & (c_lo != kf), 1.0, 0.0)) > 0.5

        @pl.when(jnp.logical_not(unresolved))
        def _():
            thr = jnp.where(active, lo, rmin_h)

            def body(c, _):
                sc_ref[c, rows, :] = jnp.where(sc_ref[c, rows, :] >= thr, 0.0, NEG_INF)
                return 0
            lax.fori_loop(0, nk, body, 0)

        @pl.when(unresolved)
        def _():
            tau = min_where(rows, ge, lo)
            g = count(rows, gt, tau)

            def undone(tau, g):
                return active & (g >= kf)

            def fix_cond(st):
                return jnp.max(jnp.where(undone(*st), 1.0, 0.0)) > 0.5

            def fix_body(st):
                tau, g = st
                nd = undone(tau, g)
                tau2 = jnp.where(nd, min_where(rows, gt, tau), tau)
                return tau2, jnp.where(nd, count(rows, gt, tau2), g)

            tau, g = lax.while_loop(fix_cond, fix_body, (tau, g))
            tau_b = jnp.broadcast_to(jnp.where(active, tau, rmin_h), (rh, LANES))
            need_b = jnp.broadcast_to(jnp.where(active, kf - g, 1e9), (rh, LANES))
            r_i = lax.broadcasted_iota(jnp.int32, (LANES, 2 * LANES), 0)
            c_i = lax.broadcasted_iota(jnp.int32, (LANES, 2 * LANES), 1)
            tri_ones = ((r_i <= c_i) | (c_i >= LANES)).astype(BF16)

            def body(c, run):
                x = sc_ref[c, rows, :]
                outs = []
                for j in range(nsub):
                    xj = x[:, j * LANES:(j + 1) * LANES]
                    is_eq = xj == tau_b
                    cnt2 = _mm(jnp.where(is_eq, 1.0, 0.0).astype(BF16), tri_ones)
                    sel = (xj > tau_b) | (is_eq & (cnt2[:, :LANES] + run <= need_b))
                    outs.append(jnp.where(sel, 0.0, NEG_INF))
                    run = run + cnt2[:, LANES:]
                sc_ref[c, rows, :] = jnp.concatenate(outs, axis=1)
                return run
            lax.fori_loop(0, nk, body, jnp.zeros((rh, LANES), F32))

    for rows, qp, rmin_h, lo, c_lo in zip(groups, part(q_pos), part(rmin), los, clos):
        finish_rows(rows, qp, rmin_h, lo, c_lo)

    aq = aq_ref[0]
    q_heads = [aq[:, h * dh:(h + 1) * dh] for h in range(H_A)]
    acc_ref[...] = jnp.zeros_like(acc_ref)
    c2 = scale * LOG2E
    ones_blk = jnp.ones((w, dh), BF16)

    def att_body(c, ms):
        k0 = pl.multiple_of(c * w, w)
        bias = sc_ref[c]
        ms_new = []
        for h in range(H_A):
            hs = slice(h * dh, (h + 1) * dh)
            s = _nt(q_heads[h], ak_ref[0, pl.ds(k0, w), hs]) + bias
            m_new = jnp.maximum(ms[h], jnp.max(s, axis=1, keepdims=True))
            m_safe = jnp.where(m_new == NEG_INF, 0.0, m_new)
            alpha = jnp.exp2((ms[h] - m_safe) * c2)
            p = jnp.exp2((s - m_safe) * c2).astype(BF16)
            v_aug = jnp.concatenate([av_ref[0, pl.ds(k0, w), hs], ones_blk], axis=1)
            acc_ref[h] = alpha * acc_ref[h] + _mm(p, v_aug)
            ms_new.append(m_new)
        return tuple(ms_new)

    lax.fori_loop(0, nk, att_body, tuple(jnp.full((tq, 1), NEG_INF, F32) for _ in range(H_A)))
    for h in range(H_A):
        a = acc_ref[h]
        ya_ref[0, :, h * dh:(h + 1) * dh] = a[:, :dh] / a[:, dh:]


def _dsa_prompt(iqb, misc, ikt, aqb, akb, avb, *, tq, w, topk):
    B, S, mix_a = aqb.shape
    dh = mix_a // H_A
    nq = S // tq
    nw = S // w
    return pl.pallas_call(
        functools.partial(_dsa_prompt_kernel, tq=tq, w=w, topk=topk, dh=dh, scale=dh ** -0.5),
        grid=(B, nq),
        in_specs=[pl.BlockSpec((1, tq, H_IDX * D_IDX), lambda b, i: (b, i, 0)),
                  pl.BlockSpec((1, tq, LANES), lambda b, i: (b, i, 0)),
                  pl.BlockSpec((1, nw, D_IDX, w), lambda b, i: (b, 0, 0, 0)),
                  pl.BlockSpec((1, tq, mix_a), lambda b, i: (b, i, 0)),
                  pl.BlockSpec((1, S, mix_a), lambda b, i: (b, 0, 0)),
                  pl.BlockSpec((1, S, mix_a), lambda b, i: (b, 0, 0))],
        out_specs=pl.BlockSpec((1, tq, mix_a), lambda b, i: (b, i, 0)),
        out_shape=jax.ShapeDtypeStruct((B, S, mix_a), F32),
        scratch_shapes=[pltpu.VMEM((nw, tq, w), F32), pltpu.VMEM((H_A, tq, 2 * dh), F32)],
        compiler_params=_cparams(("parallel", "arbitrary")),
        name="dsa_prompt",
    )(iqb, misc, ikt, aqb, akb, avb)


def _dsa_sample_select_kernel(pt_ref, iq_ref, w_ref, ikn_ref, ptv_ref, pool_ref, rows_ref, flag_ref,
                              ikbuf, sem, sc_ref, xn_ref, slot_ref, phys_ref,
                              *, n_pages, page, topk, cw):
    nb = iq_ref.shape[0]
    past = n_pages * page
    kf = float(topk)
    n_cw = past // cw

    def page_copy(bb, p, slot):
        return pltpu.make_async_copy(pool_ref.at[pt_ref[bb, p]],
                                     ikbuf.at[slot, :, pl.ds(p * page, page)],
                                     sem.at[slot])

    def start_all(bb, slot):
        def body(p, _):
            page_copy(bb, p, slot).start()
            return 0
        lax.fori_loop(0, n_pages, body, 0)

    start_all(0, 0)

    def score_body(b, _):
        slot = b % 2

        @pl.when(b + 1 < nb)
        def _():
            start_all(b + 1, 1 - slot)

        def wait_body(p, _):
            page_copy(b, p, slot).wait()
            return 0
        lax.fori_loop(0, n_pages, wait_body, 0)

        iq8 = iq_ref[b]
        w8 = w_ref[b]
        s8 = _mm(iq8, ikbuf[slot].astype(BF16))
        sc_ref[pl.ds(b, 1), :] = jnp.sum(w8 * jnp.maximum(s8, 0.0), axis=0, keepdims=True)
        ikn = ikn_ref[b].astype(BF16).astype(F32)
        sn8 = jnp.sum(iq8.astype(F32) * ikn, axis=1, keepdims=True)
        xn_b = jnp.sum(w8 * jnp.maximum(sn8, 0.0), axis=0, keepdims=True)
        xn_ref[pl.ds(b, 1), :] = jnp.broadcast_to(xn_b, (1, LANES))
        return 0

    lax.fori_loop(0, nb, score_body, 0)

    x = sc_ref[...]
    xn = xn_ref[:, 0:1]

    def cnt(mask_row, mask_new):
        return (jnp.sum(jnp.where(mask_row, 1.0, 0.0), axis=1, keepdims=True)
                + jnp.where(mask_new, 1.0, 0.0))

    rmax = jnp.maximum(jnp.max(x, axis=1, keepdims=True), xn)
    rmin = jnp.minimum(jnp.min(x, axis=1, keepdims=True), xn)
    hi0 = rmax + jnp.abs(rmax) + 1.0

    def bis_body(_, carry):
        lo, hi = carry
        mid = 0.5 * (lo + hi)
        ok = cnt(x >= mid, xn >= mid) >= kf
        return jnp.where(ok, mid, lo), jnp.where(ok, hi, mid)

    lo, _ = lax.fori_loop(0, N_BISECT, bis_body, (rmin, hi0))

    def min_where(mask_row, mask_new):
        return jnp.minimum(jnp.min(jnp.where(mask_row, x, POS_INF), axis=1, keepdims=True),
                           jnp.where(mask_new, xn, POS_INF))

    tau = min_where(x >= lo, xn >= lo)
    g = cnt(x > tau, xn > tau)

    def fix_cond(st):
        tau, g = st
        return jnp.max(jnp.where(g >= kf, 1.0, 0.0)) > 0.5

    def fix_body(st):
        tau, g = st
        tau2 = jnp.where(g >= kf, min_where(x > tau, xn > tau), tau)
        return tau2, cnt(x > tau2, xn > tau2)

    tau, g = lax.while_loop(fix_cond, fix_body, (tau, g))
    need = kf - g

    tri = (lax.broadcasted_iota(jnp.int32, (cw, cw), 0)
           < lax.broadcasted_iota(jnp.int32, (cw, cw), 1)).astype(BF16)

    def excl_prefix(flag):
        outs = []
        run = jnp.zeros((nb, 1), F32)
        for c in range(n_cw):
            f = flag[:, c * cw:(c + 1) * cw]
            outs.append(_mm(f.astype(BF16), tri) + run)
            run = run + jnp.sum(f, axis=1, keepdims=True)
        return jnp.concatenate(outs, axis=1), run

    is_eq = x == tau
    pre_eq, n_eq_past = excl_prefix(jnp.where(is_eq, 1.0, 0.0))
    sel = (x > tau) | (is_eq & (pre_eq < need))
    new_sel = (xn > tau) | ((xn == tau) & (n_eq_past < need))
    slot, _ = excl_prefix(jnp.where(sel, 1.0, 0.0))
    slot_ref[...] = jnp.where(sel, slot, -1.0)

    ptv = ptv_ref[...]
    jrow = lax.broadcasted_iota(jnp.int32, (1, past), 1)
    prow = lax.broadcasted_iota(jnp.int32, (n_pages, 1), 0)
    expand = ((jrow >= prow * page) & (jrow < (prow + 1) * page)).astype(BF16)
    digit_bits = 6
    pt_hi = _mm((ptv >> digit_bits).astype(F32).astype(BF16), expand)
    pt_lo = _mm((ptv & ((1 << digit_bits) - 1)).astype(F32).astype(BF16), expand)
    pidx = lax.broadcasted_iota(jnp.int32, (8, n_pages), 1).astype(F32).astype(BF16)
    pg = _mm(pidx, expand)[0:1, :]
    phys_ref[...] = (pt_hi * (1 << digit_bits) + pt_lo) * page + (jrow.astype(F32) - pg * page)

    slot_col = lax.broadcasted_iota(jnp.int32, (topk, 1), 0).astype(F32)
    lane_b = lax.broadcasted_iota(jnp.int32, (1, LANES), 1)

    def extract_body(b, out):
        srow = slot_ref[pl.ds(b, 1), :]
        frow = phys_ref[pl.ds(b, 1), :]
        acc = jnp.zeros((topk, LANES), F32)
        for c in range(past // LANES):
            cs = slice(c * LANES, (c + 1) * LANES)
            acc = acc + jnp.where(srow[:, cs] == slot_col, frow[:, cs], 0.0)
        return jnp.where(lane_b == b, jnp.sum(acc, axis=1, keepdims=True), out)

    out = lax.fori_loop(0, nb, extract_body, jnp.zeros((topk, LANES), F32))
    rows_ref[...] = out.astype(jnp.int32)
    flag_ref[...] = jnp.broadcast_to(jnp.where(new_sel, 1, 0), (nb, LANES)).astype(jnp.int32)


def _dsa_sample_select(page_table, iq8, w8, ik_new, pool_ik_t, *, topk, cw):
    Bd, n_pages = page_table.shape
    n_pool, d_idx, page = pool_ik_t.shape
    past = n_pages * page
    assert Bd <= LANES and n_pool <= 64 * 256
    full = lambda shp: pl.BlockSpec(shp, lambda i, pt: (0,) * len(shp))
    grid_spec = pltpu.PrefetchScalarGridSpec(
        num_scalar_prefetch=1,
        grid=(1,),
        in_specs=[full((Bd, 8, d_idx)), full((Bd, 8, 1)), full((Bd, 1, d_idx)), full((Bd, n_pages)),
                  pl.BlockSpec(memory_space=pl.ANY)],
        out_specs=[full((topk, LANES)), full((Bd, LANES))],
        scratch_shapes=[pltpu.VMEM((2, d_idx, past), F32),
                        pltpu.SemaphoreType.DMA((2,)),
                        pltpu.VMEM((Bd, past), F32),
                        pltpu.VMEM((Bd, LANES), F32),
                        pltpu.VMEM((Bd, past), F32),
                        pltpu.VMEM((Bd, past), F32)],
    )
    return pl.pallas_call(
        functools.partial(_dsa_sample_select_kernel, n_pages=n_pages, page=page, topk=topk, cw=cw),
        grid_spec=grid_spec,
        out_shape=[jax.ShapeDtypeStruct((topk, LANES), jnp.int32),
                   jax.ShapeDtypeStruct((Bd, LANES), jnp.int32)],
        compiler_params=_cparams(("arbitrary",)),
        name="dsa_sample_select",
    )(page_table, iq8, w8, ik_new, page_table, pool_ik_t)


def _dsa_sample_attend_kernel(rows_ref, flag_ref, aq_ref, knew_ref, vnew_ref, kpool_ref, vpool_ref,
                              ya_ref, kbuf, vbuf, sem, *, topk, dh, scale):
    b = pl.program_id(0)
    nb = pl.num_programs(0)

    def row_copies(bb, t, slot):
        r = rows_ref[bb, t]
        dst = pl.ds(t * H_A, H_A)
        return (pltpu.make_async_copy(kpool_ref.at[r], kbuf.at[slot, dst, :], sem.at[0, slot]),
                pltpu.make_async_copy(vpool_ref.at[r], vbuf.at[slot, dst, :], sem.at[1, slot]))

    def start_all(bb, slot):
        def body(t, _):
            ck, cv = row_copies(bb, t, slot)
            ck.start()
            cv.start()
            return 0
        lax.fori_loop(0, topk, body, 0, unroll=8)

    slot = b % 2

    @pl.when(b == 0)
    def _():
        start_all(0, 0)

    @pl.when(b + 1 < nb)
    def _():
        start_all(b + 1, 1 - slot)

    def wait_body(t, _):
        ck, cv = row_copies(b, t, slot)
        ck.wait()
        cv.wait()
        return 0
    lax.fori_loop(0, topk, wait_body, 0, unroll=8)

    take_new = (lax.broadcasted_iota(jnp.int32, (topk, 1), 0) == topk - 1) & (flag_ref[b] > 0)
    aq = aq_ref[0]
    for h in range(H_A):
        hs = slice(h * dh, (h + 1) * dh)
        kh = kbuf[slot, pl.ds(h, topk, stride=H_A), :]
        vh = vbuf[slot, pl.ds(h, topk, stride=H_A), :]
        kh = jnp.where(take_new, knew_ref[0, h:h + 1, :], kh).astype(BF16)
        vh = jnp.where(take_new, vnew_ref[0, h:h + 1, :], vh).astype(BF16)
        q8 = jnp.broadcast_to(aq[:, hs], (8, dh))
        s = _nt(q8, kh) * scale
        m = jnp.max(s, axis=1, keepdims=True)
        p = jnp.exp(s - m)
        p = p / jnp.sum(p, axis=1, keepdims=True)
        ya_ref[0, :, hs] = _mm(p.astype(BF16), vh)[0:1, :]


def _dsa_sample_attend(rows, flags, aqb, k_new, v_new, pool_k, pool_v, *, topk):
    Bd, _, mix_a = aqb.shape
    dh = pool_k.shape[2]
    new_spec = pl.BlockSpec((1, H_A, dh), lambda b, r, f: (b, 0, 0))
    grid_spec = pltpu.PrefetchScalarGridSpec(
        num_scalar_prefetch=2,
        grid=(Bd,),
        in_specs=[pl.BlockSpec((1, 1, mix_a), lambda b, r, f: (b, 0, 0)),
                  new_spec, new_spec,
                  pl.BlockSpec(memory_space=pl.ANY),
                  pl.BlockSpec(memory_space=pl.ANY)],
        out_specs=pl.BlockSpec((1, 1, mix_a), lambda b, r, f: (b, 0, 0)),
        scratch_shapes=[pltpu.VMEM((2, topk * H_A, dh), F32),
                        pltpu.VMEM((2, topk * H_A, dh), F32),
                        pltpu.SemaphoreType.DMA((2, 2))],
    )
    return pl.pallas_call(
        functools.partial(_dsa_sample_attend_kernel, topk=topk, dh=dh, scale=dh ** -0.5),
        grid_spec=grid_spec,
        out_shape=jax.ShapeDtypeStruct((Bd, 1, mix_a), F32),
        compiler_params=_cparams(("arbitrary",)),
        name="dsa_sample_attend",
    )(rows, flags, aqb, k_new, v_new, pool_k, pool_v)


def _mem_kv_kernel(mem_ref, nm_ref, wk_ref, wv_ref, kn_ref, k_ref, v_ref, *, dh):
    hm = _rms(mem_ref[...], nm_ref[...]).astype(BF16)
    kk = _mm(hm, wk_ref[...])
    vv = _mm(hm, wv_ref[...])
    for h in range(H_C):
        hs = slice(h * dh, (h + 1) * dh)
        k_ref[:, h, :] = _rms(kk[:, hs], kn_ref[...])
        v_ref[:, h, :] = vv[:, hs]


def _mem_kv(mem2d, norm_mem, w_ck, w_cv, ck_norm, *, tm):
    rows, d = mem2d.shape
    dh = d // H_C
    row = pl.BlockSpec((tm, d), lambda i: (i, 0))
    heads = pl.BlockSpec((tm, H_C, dh), lambda i: (i, 0, 0))
    const = lambda shp: pl.BlockSpec(shp, lambda i: (0, 0))
    return pl.pallas_call(
        functools.partial(_mem_kv_kernel, dh=dh),
        grid=(rows // tm,),
        in_specs=[row, const((1, d)), const((d, d)), const((d, d)), const((1, dh))],
        out_specs=[heads, heads],
        out_shape=[jax.ShapeDtypeStruct((rows, H_C, dh), F32)] * 2,
        compiler_params=_cparams(("parallel",)),
        name="mem_kv",
    )(mem2d, norm_mem, w_ck, w_cv, ck_norm)


def _out_cq_kernel(x_ref, ym_ref, ya_ref, wo_ref, nc_ref, wq_ref, qn_ref, x1_ref, qc_ref, *, mix_m, dh):
    upd = (_mm(ym_ref[...].astype(BF16), wo_ref[0:mix_m, :])
           + _mm(ya_ref[...].astype(BF16), wo_ref[mix_m:, :]))
    x1 = x_ref[...] + upd
    x1_ref[...] = x1
    hq = _mm(_rms(x1, nc_ref[...]).astype(BF16), wq_ref[...])
    for h in range(H_C):
        hs = slice(h * dh, (h + 1) * dh)
        qc_ref[:, hs] = _rms(hq[:, hs], qn_ref[...]).astype(BF16)


def _out_cq(x2d, ym, ya, w_out, norm_cross, w_cq, cq_norm, *, tm):
    rows, d = x2d.shape
    mix_m = ym.shape[1]
    mix_a = ya.shape[1]
    dh = d // H_C
    row = lambda wdt: pl.BlockSpec((tm, wdt), lambda i: (i, 0))
    const = lambda shp: pl.BlockSpec(shp, lambda i: (0, 0))
    return pl.pallas_call(
        functools.partial(_out_cq_kernel, mix_m=mix_m, dh=dh),
        grid=(rows // tm,),
        in_specs=[row(d), row(mix_m), row(mix_a), const((mix_m + mix_a, d)), const((1, d)),
                  const((d, d)), const((1, dh))],
        out_specs=[row(d), row(d)],
        out_shape=[jax.ShapeDtypeStruct((rows, d), F32), jax.ShapeDtypeStruct((rows, d), BF16)],
        compiler_params=_cparams(("parallel",)),
        name="out_cq",
    )(x2d, ym, ya, w_out, norm_cross, w_cq, cq_norm)


def _cross_kernel(q_ref, k_hbm, v_hbm, o_ref, kv_buf, sem, *, dh, scale):
    b = pl.program_id(0)
    t = pl.program_id(1)
    nb = pl.num_programs(0)
    slot = b % 2

    def head_copies(bb, sl):
        cps = []
        for h in range(H_C):
            cps.append(pltpu.make_async_copy(k_hbm.at[bb, :, h, :], kv_buf.at[sl, 0, h], sem.at[sl]))
            cps.append(pltpu.make_async_copy(v_hbm.at[bb, :, h, :], kv_buf.at[sl, 1, h], sem.at[sl]))
        return cps

    @pl.when(t == 0)
    def _():
        @pl.when(b == 0)
        def _():
            for cp in head_copies(0, 0):
                cp.start()

        @pl.when(b + 1 < nb)
        def _():
            for cp in head_copies(b + 1, 1 - slot):
                cp.start()

        for cp in head_copies(b, slot):
            cp.wait()

    q = q_ref[0]
    rows = q.shape[0]
    if rows < 8:
        q = jnp.broadcast_to(q, (8, q.shape[1]))
    for h in range(H_C):
        hs = slice(h * dh, (h + 1) * dh)
        kb = kv_buf[slot, 0, h].astype(BF16)
        vb = kv_buf[slot, 1, h].astype(BF16)
        s = _nt(q[:, hs], kb) * scale
        m = jnp.max(s, axis=1, keepdims=True)
        p = jnp.exp(s - m)
        p = p / jnp.sum(p, axis=1, keepdims=True)
        o = _mm(p.astype(BF16), vb)
        o_ref[0, :, hs] = o[0:rows].astype(BF16)


def _cross(qc, mem_k, mem_v, *, tq):
    B, T, d = qc.shape
    M = mem_k.shape[1]
    dh = d // H_C
    return pl.pallas_call(
        functools.partial(_cross_kernel, dh=dh, scale=dh ** -0.5),
        grid=(B, T // tq),
        in_specs=[pl.BlockSpec((1, tq, d), lambda b, t: (b, t, 0)),
                  pl.BlockSpec(memory_space=pl.ANY),
                  pl.BlockSpec(memory_space=pl.ANY)],
        out_specs=pl.BlockSpec((1, tq, d), lambda b, t: (b, t, 0)),
        out_shape=jax.ShapeDtypeStruct((B, T, d), BF16),
        scratch_shapes=[pltpu.VMEM((2, 2, H_C, M, dh), F32), pltpu.SemaphoreType.DMA((2,))],
        compiler_params=_cparams(("arbitrary", "arbitrary")),
        name="cross_attn",
    )(qc, mem_k, mem_v)


def _gelu_tanh(x):
    return 0.5 * x * (1.0 + jnp.tanh(np.sqrt(2.0 / np.pi) * (x + 0.044715 * (x * x * x))))


def _ffn_front(x1_ref, o_ref, wco_ref, nf_ref, x2_ref, hb_ref, acc_ref):
    x2 = x1_ref[0] + _mm(o_ref[0], wco_ref[...])
    x2_ref[...] = x2
    hb_ref[...] = _rms(x2, nf_ref[...]).astype(BF16)
    acc_ref[...] = jnp.zeros_like(acc_ref)


def _ffn_prompt_kernel(x1_ref, o_ref, wco_ref, nf_ref, wua_ref, wug_ref, cwa_ref, cwg_ref,
                       cba_ref, cbg_ref, wd_ref, ha_ref, hg_ref,
                       y_ref, ca_ref, cg_ref, x2_ref, hb_ref, acc_ref, carry_ref, *, tm, rs):
    t = pl.program_id(1)
    j = pl.program_id(2)
    nj = pl.num_programs(2)

    @pl.when(j == 0)
    def _():
        _ffn_front(x1_ref, o_ref, wco_ref, nf_ref, x2_ref, hb_ref, acc_ref)

    @pl.when(t == 0)
    def _():
        carry_ref[j, 0, 6:8, :] = ha_ref[0]
        carry_ref[j, 1, 6:8, :] = hg_ref[0]

    rid = lax.broadcasted_iota(jnp.int32, (rs, 1), 0)

    def conv_part(hb, part, wu_ref, cw_ref, cb_ref):
        u = _mm(hb, wu_ref[...])
        p2 = carry_ref[j, part, 6:7, :]
        p1 = carry_ref[j, part, 7:8, :]
        um1 = jnp.where(rid == 0, p1, pltpu.roll(u, 1, 0))
        um2 = jnp.where(rid == 0, p2, jnp.where(rid == 1, p1, pltpu.roll(u, 2, 0)))
        carry_ref[j, part] = u[rs - 8:rs, :]
        return cb_ref[...] + um2 * cw_ref[0:1, :] + um1 * cw_ref[1:2, :] + u * cw_ref[2:3, :]

    def sub_body(r, _):
        r0 = pl.multiple_of(r * rs, rs)
        hb = hb_ref[pl.ds(r0, rs), :]
        a = conv_part(hb, 0, wua_ref, cwa_ref, cba_ref)
        g = conv_part(hb, 1, wug_ref, cwg_ref, cbg_ref)
        acc_ref[pl.ds(r0, rs), :] += _mm((_gelu_tanh(g) * a).astype(BF16), wd_ref[...])
        return 0

    lax.fori_loop(0, tm // rs, sub_body, 0)
    ca_ref[0, 0] = carry_ref[j, 0, 6:8, :]
    cg_ref[0, 0] = carry_ref[j, 1, 6:8, :]

    @pl.when(j == nj - 1)
    def _():
        y_ref[0] = x2_ref[...] + acc_ref[...]


def _ffn_prompt(x1, o, w_co, norm_ffn, w_up, conv_w, conv_b, w_down, hist, *, tm, tf):
    B, T, d = x1.shape
    d_ff = w_down.shape[0]
    nj = d_ff // tf
    nt = T // tm
    idx3 = lambda b, t, j: (b, t, 0)
    c2 = lambda shp: pl.BlockSpec(shp, lambda b, t, j: (0, 0))
    return pl.pallas_call(
        functools.partial(_ffn_prompt_kernel, tm=tm, rs=min(512, tm)),
        grid=(B, nt, nj),
        in_specs=[pl.BlockSpec((1, tm, d), idx3), pl.BlockSpec((1, tm, d), idx3),
                  pl.BlockSpec((d, d), lambda b, t, j: (0, 0), pipeline_mode=pl.Buffered(1)), c2((1, d)),
                  pl.BlockSpec((d, tf), lambda b, t, j: (0, j)),
                  pl.BlockSpec((d, tf), lambda b, t, j: (0, nj + j)),
                  pl.BlockSpec((CONV_W, tf), lambda b, t, j: (0, j)),
                  pl.BlockSpec((CONV_W, tf), lambda b, t, j: (0, nj + j)),
                  pl.BlockSpec((1, tf), lambda b, t, j: (0, j)),
                  pl.BlockSpec((1, tf), lambda b, t, j: (0, nj + j)),
                  pl.BlockSpec((tf, d), lambda b, t, j: (j, 0)),
                  pl.BlockSpec((1, CONV_W - 1, tf), lambda b, t, j: (b, 0, j)),
                  pl.BlockSpec((1, CONV_W - 1, tf), lambda b, t, j: (b, 0, nj + j))],
        out_specs=[pl.BlockSpec((1, tm, d), idx3),
                   pl.BlockSpec((1, 1, CONV_W - 1, tf), lambda b, t, j: (b, t, 0, j)),
                   pl.BlockSpec((1, 1, CONV_W - 1, tf), lambda b, t, j: (b, t, 0, j))],
        out_shape=[jax.ShapeDtypeStruct((B, T, d), F32),
                   jax.ShapeDtypeStruct((B, nt, CONV_W - 1, d_ff), F32),
                   jax.ShapeDtypeStruct((B, nt, CONV_W - 1, d_ff), F32)],
        scratch_shapes=[pltpu.VMEM((tm, d), F32), pltpu.VMEM((tm, d), BF16), pltpu.VMEM((tm, d), F32),
                        pltpu.VMEM((nj, 2, 8, tf), F32)],
        compiler_params=_cparams(("arbitrary", "arbitrary", "arbitrary")),
        name="ffn_prompt",
    )(x1, o, w_co, norm_ffn, w_up, w_up, conv_w, conv_w, conv_b, conv_b, w_down, hist, hist)


def _ffn_sample_kernel(x1_ref, o_ref, wco_ref, nf_ref, wua_ref, wug_ref, cwa_ref, cwg_ref,
                       cba_ref, cbg_ref, wd_ref, h0a_ref, h0g_ref, h1a_ref, h1g_ref,
                       y_ref, ua_ref, ug_ref, x2_ref, hb_ref, acc_ref):
    j = pl.program_id(0)
    nj = pl.num_programs(0)

    @pl.when(j == 0)
    def _():
        _ffn_front(x1_ref, o_ref, wco_ref, nf_ref, x2_ref, hb_ref, acc_ref)

    hb = hb_ref[...]

    def conv_part(wu_ref, cw_ref, cb_ref, h0_ref, h1_ref, u_out_ref):
        u = _mm(hb, wu_ref[...])
        u_out_ref[...] = u
        return cb_ref[...] + h0_ref[...] * cw_ref[0:1, :] + h1_ref[...] * cw_ref[1:2, :] + u * cw_ref[2:3, :]

    a = conv_part(wua_ref, cwa_ref, cba_ref, h0a_ref, h1a_ref, ua_ref)
    g = conv_part(wug_ref, cwg_ref, cbg_ref, h0g_ref, h1g_ref, ug_ref)
    acc_ref[...] += _mm((_gelu_tanh(g) * a).astype(BF16), wd_ref[...])

    @pl.when(j == nj - 1)
    def _():
        y_ref[0] = x2_ref[...] + acc_ref[...]


def _ffn_sample(x1, o, w_co, norm_ffn, w_up, conv_w, conv_b, w_down, h0, h1, *, tf):
    _, rows, d = x1.shape
    d_ff = w_down.shape[0]
    nj = d_ff // tf
    c2 = lambda shp: pl.BlockSpec(shp, lambda j: (0, 0))
    c3 = lambda shp: pl.BlockSpec(shp, lambda j: (0, 0, 0))
    col_a = lambda r: pl.BlockSpec((r, tf), lambda j: (0, j))
    col_g = lambda r: pl.BlockSpec((r, tf), lambda j: (0, nj + j))
    return pl.pallas_call(
        _ffn_sample_kernel,
        grid=(nj,),
        in_specs=[c3((1, rows, d)), c3((1, rows, d)), c2((d, d)), c2((1, d)),
                  col_a(d), col_g(d), col_a(CONV_W), col_g(CONV_W), col_a(1), col_g(1),
                  pl.BlockSpec((tf, d), lambda j: (j, 0)),
                  col_a(rows), col_g(rows), col_a(rows), col_g(rows)],
        out_specs=[c3((1, rows, d)), col_a(rows), col_a(rows)],
        out_shape=[jax.ShapeDtypeStruct((1, rows, d), F32),
                   jax.ShapeDtypeStruct((rows, d_ff), F32),
                   jax.ShapeDtypeStruct((rows, d_ff), F32)],
        scratch_shapes=[pltpu.VMEM((rows, d), F32), pltpu.VMEM((rows, d), BF16), pltpu.VMEM((rows, d), F32)],
        compiler_params=_cparams(("arbitrary",)),
        name="ffn_sample",
    )(x1, o, w_co, norm_ffn, w_up, w_up, conv_w, conv_w, conv_b, conv_b, w_down, h0, h0, h1, h1)


def _rope_tables(pos, dh_a):
    assert dh_a == LANES and 2 * D_IDX == LANES and D_IDX & (D_IDX - 1) == 0
    posf = pos.astype(F32)[:, None]
    half_a = dh_a // 2
    inv_a = ROPE_THETA ** (-jnp.arange(half_a, dtype=F32) / half_a)
    ang_a = posf * inv_a[None, :]
    half_i = D_IDX // 2
    inv_i = ROPE_THETA ** (-jnp.arange(half_i, dtype=F32) / half_i)
    ang_i = posf * inv_i[None, :]
    cos_i, sin_i = jnp.cos(ang_i), jnp.sin(ang_i)
    return jnp.concatenate([jnp.cos(ang_a), jnp.sin(ang_a), cos_i, sin_i, cos_i, sin_i], axis=1)


def kernel(x_prompt, x_sample, mem_prompt, cache_k, cache_v, cache_idx_k, cache_mem_k, cache_mem_v,
           state_mlstm_c, state_mlstm_n, state_mlstm_m, state_conv, page_table,
           norm_mix, w_in, b_if, mlstm_norm, q_norm, k_norm, w_out, norm_cross, norm_mem,
           w_cq, w_ck, w_cv, w_co, cq_norm, ck_norm, norm_ffn, w_up, conv_w, conv_b, w_down):
    B, S, D = x_prompt.shape
    Bd, T, _ = x_sample.shape
    assert T == 1 and w_in.shape[0] == 1
    n_pool, page = cache_k.shape[1], cache_k.shape[2]
    n_pages = page_table.shape[1]
    past = n_pages * page
    mix_m = mlstm_norm.shape[1]
    dh_m = mix_m // H_M
    dh_a = q_norm.shape[1]
    mix_a = H_A * dh_a
    d_ff = w_down.shape[1]
    M = mem_prompt.shape[1]
    chunk = min(128, S)
    topk_p = min(TOPK_MAX, S // 4)
    topk_s = min(TOPK_MAX, (past + T) // 4)

    w = w_in[0]
    o_gate = 4 * mix_m
    o_aq = o_gate + 2 * H_M
    o_iq = o_aq + 3 * mix_a
    o_ik = o_iq + H_IDX * D_IDX
    o_iw = o_ik + D_IDX
    tail_pad = LANES - (D_IDX + H_IDX + 2 * H_M)
    w_r = jnp.concatenate([w[:, :o_gate], w[:, o_aq:o_iq], w[:, o_iq:o_ik], w[:, o_ik:o_iw],
                           w[:, o_iw:o_iw + H_IDX], w[:, o_gate:o_aq],
                           jnp.zeros((D, tail_pad), w.dtype)], axis=1).astype(BF16)
    bias_tail = jnp.concatenate([jnp.zeros((D_IDX + H_IDX,), F32), b_if[0].astype(F32),
                                 jnp.zeros((tail_pad,), F32)])[None, :]
    w_out_b = w_out[0].astype(BF16)
    w_cq_b, w_ck_b, w_cv_b, w_co_b = (a[0].astype(BF16) for a in (w_cq, w_ck, w_cv, w_co))
    w_up_b = w_up[0].astype(BF16)
    w_down_b = w_down[0].astype(BF16)
    row = lambda a: a[0][None, :]

    def split_misc(misc):
        ik = misc[:, :D_IDX]
        li = misc[:, D_IDX + H_IDX:D_IDX + H_IDX + H_M]
        lf = misc[:, D_IDX + H_IDX + H_M:D_IDX + H_IDX + 2 * H_M]
        return ik, li, lf

    tm_in = min(256, S)
    tab_p = _rope_tables(jnp.arange(S), dh_a)
    (mq, mk, mv, mo, aqb, ak, av, akb, avb, iqb, misc) = _in_proj(
        x_prompt.reshape(B * S, D), row(norm_mix), w_r, bias_tail, row(q_norm), row(k_norm), tab_p,
        tm=tm_in, tab_tiles=S // tm_in, mix_m=mix_m, mix_a=mix_a)
    ik_p, li_p, lf_p = split_misc(misc)
    r3 = lambda a: a.reshape(B, S, a.shape[-1])
    gcol = jnp.concatenate([li_p, lf_p], axis=-1).reshape(B, S, 2 * H_M)
    grow = gcol.reshape(B, S // chunk, chunk, 2 * H_M).transpose(0, 1, 3, 2)
    y_m, c_p, n_p, m_p = _mlstm_prompt(r3(mq), r3(mk), r3(mv), r3(mo), grow, gcol, row(mlstm_norm), chunk=chunk)

    tq = min(256, S)
    wk = min(512, S)
    ikt = ik_p.astype(BF16).reshape(B, S // wk, wk, D_IDX).transpose(0, 1, 3, 2)
    y_a = _dsa_prompt(r3(iqb), r3(misc), ikt, r3(aqb), r3(akb), r3(avb), tq=tq, w=wk, topk=topk_p)

    mk_p, mv_p = _mem_kv(mem_prompt.reshape(B * M, D), row(norm_mem), w_ck_b, w_cv_b, row(ck_norm),
                         tm=min(256, B * M))
    x1, qc = _out_cq(x_prompt.reshape(B * S, D), y_m.reshape(B * S, mix_m), y_a.reshape(B * S, mix_a),
                     w_out_b, row(norm_cross), w_cq_b, row(cq_norm), tm=min(512, S))
    dh_c = D // H_C
    o_c = _cross(qc.reshape(B, S, D), mk_p.reshape(B, M, H_C, dh_c), mv_p.reshape(B, M, H_C, dh_c), tq=min(512, S))
    tf = d_ff // 2 if (d_ff // 2) % LANES == 0 else d_ff
    xp, conv_a, conv_g = _ffn_prompt(x1.reshape(B, S, D), o_c, w_co_b, row(norm_ffn), w_up_b, conv_w[0],
                                     conv_b[0][None, :], w_down_b,
                                     jnp.zeros((B, CONV_W - 1, 2 * d_ff), F32), tm=min(512, S), tf=tf)
    conv_p = jnp.concatenate([conv_a[:, -1], conv_g[:, -1]], axis=-1)

    tab_s = jnp.broadcast_to(_rope_tables(jnp.full((1,), past, jnp.int32), dh_a), (Bd, 2 * LANES))
    (mq_s, mk_s, mv_s, mo_s, aqb_s, ak_s, av_s, _, _, iqb_s, misc_s) = _in_proj(
        x_sample.reshape(Bd, D), row(norm_mix), w_r, bias_tail, row(q_norm), row(k_norm), tab_s,
        tm=Bd, tab_tiles=1, mix_m=mix_m, mix_a=mix_a)
    ik_s, li_s, lf_s = split_misc(misc_s)
    gs = jnp.concatenate([li_s, lf_s, state_mlstm_m[0].astype(F32)], axis=-1)[:, None, :]
    e1 = lambda a: a[:, None, :]
    y_ms, c_s, n_s, m_s = _mlstm_sample(e1(mq_s), e1(mk_s), e1(mv_s), e1(mo_s), gs,
                                        state_mlstm_c[0], state_mlstm_n[0].reshape(Bd, 1, mix_m),
                                        row(mlstm_norm))

    iq8 = jnp.pad(iqb_s.reshape(Bd, H_IDX, D_IDX), ((0, 0), (0, 8 - H_IDX), (0, 0)))
    w8 = jnp.pad(misc_s[:, D_IDX:D_IDX + H_IDX], ((0, 0), (0, 8 - H_IDX)))[:, :, None]
    assert n_pages <= 256
    rows_t, flags = _dsa_sample_select(page_table, iq8, w8, e1(ik_s), jnp.swapaxes(cache_idx_k[0], 1, 2),
                                       topk=topk_s, cw=min(512, past))
    y_as = _dsa_sample_attend(rows_t[:, :Bd].T, flags[:, 0], e1(aqb_s), ak_s, av_s,
                              cache_k[0].reshape(n_pool * page, H_A, dh_a),
                              cache_v[0].reshape(n_pool * page, H_A, dh_a), topk=topk_s)

    x1_s, qc_s = _out_cq(x_sample.reshape(Bd, D), y_ms.reshape(Bd, mix_m), y_as.reshape(Bd, mix_a),
                         w_out_b, row(norm_cross), w_cq_b, row(cq_norm), tm=Bd)
    o_s = _cross(qc_s.reshape(Bd, 1, D), cache_mem_k[0], cache_mem_v[0], tq=1)
    xs, u_a, u_g = _ffn_sample(x1_s.reshape(1, Bd, D), o_s.reshape(1, Bd, D), w_co_b, row(norm_ffn), w_up_b,
                               conv_w[0], conv_b[0][None, :], w_down_b,
                               state_conv[0, :, 0, :], state_conv[0, :, 1, :], tf=tf)
    conv_s = jnp.stack([state_conv[0, :, 1, :], jnp.concatenate([u_a, u_g], axis=-1)], axis=1)

    lead = lambda a: a[None]
    return (xp, xs.reshape(Bd, 1, D),
            lead(ak.reshape(B, S, H_A, dh_a)), lead(av.reshape(B, S, H_A, dh_a)), lead(ik_p.reshape(B, S, D_IDX)),
            lead(c_p), lead(n_p), lead(m_p[:, :, 0]),
            lead(mk_p.reshape(B, M, H_C, D // H_C)), lead(mv_p.reshape(B, M, H_C, D // H_C)), lead(conv_p),
            lead(ak_s.reshape(Bd, 1, H_A, dh_a)), lead(av_s.reshape(Bd, 1, H_A, dh_a)),
            lead(ik_s.reshape(Bd, 1, D_IDX)),
            lead(c_s), lead(n_s.reshape(Bd, H_M, dh_m)), lead(m_s[:, 0, :H_M]), lead(conv_s))
```

```python
import functools

import jax
import jax.numpy as jnp
import numpy as np
from jax import lax
from jax.experimental import pallas as pl
from jax.experimental.pallas import tpu as pltpu

F32 = jnp.float32
BF16 = jnp.bfloat16

H_M = 4
H_A = 4
H_IDX = 4
D_IDX = 64
H_C = 4
TOPK_MAX = 256
CONV_W = 3
ROPE_THETA = 10000.0
EPS = 1e-6
LOG2E = 1.4426950408889634
NEG_INF = float("-inf")
POS_INF = float("inf")

LANES = 128
VMEM_LIMIT = 56 * 1024 * 1024
N_BISECT = 20


def _cparams(sem):
    return pltpu.CompilerParams(dimension_semantics=sem, vmem_limit_bytes=VMEM_LIMIT)


def _nt(a, b):
    return lax.dot_general(a, b, (((1,), (1,)), ((), ())), preferred_element_type=F32)


def _tn(a, b):
    return lax.dot_general(a, b, (((0,), (0,)), ((), ())), preferred_element_type=F32)


def _mm(a, b):
    return jnp.dot(a, b, preferred_element_type=F32)


def _rms(x, g):
    ms = jnp.mean(x * x, axis=-1, keepdims=True)
    return x * lax.rsqrt(ms + EPS) * g


def _sigmoid(x):
    return 1.0 / (1.0 + jnp.exp(-x))


def _in_proj_kernel(x_ref, nm_ref, wm_ref, wa_ref, wt_ref, bias_ref, qn_ref, kn_ref, tab_ref,
                    mq_ref, mk_ref, mv_ref, mo_ref, aqb_ref, ak_ref, av_ref, akb_ref, avb_ref,
                    iqb_ref, misc_ref, *, mix_m, mix_a, dh_m, dh_a):
    h = _rms(x_ref[...], nm_ref[...]).astype(BF16)

    o_mq, o_mk, o_mv, o_mo = 0, mix_m, 2 * mix_m, 3 * mix_m
    n_m = 4 * mix_m
    o_aq = n_m
    o_ak = o_aq + mix_a
    o_av = o_ak + mix_a
    o_iq = o_av + mix_a
    o_tail = o_iq + H_IDX * D_IDX

    def proj(lo, width):
        if lo < n_m:
            return _mm(h, wm_ref[:, lo:lo + width])
        if lo < o_tail:
            return _mm(h, wa_ref[:, lo - n_m:lo - n_m + width])
        return _mm(h, wt_ref[...])

    mq_ref[...] = proj(o_mq, mix_m)
    mk_ref[...] = proj(o_mk, mix_m) * (dh_m ** -0.5)
    mv_ref[...] = proj(o_mv, mix_m)
    mo_ref[...] = proj(o_mo, mix_m)

    t_a = tab_ref[:, 0:LANES]
    t_i = tab_ref[:, LANES:2 * LANES]
    lane = lax.broadcasted_iota(jnp.int32, t_a.shape, 1)
    first_a = lane < dh_a // 2
    r_a = pltpu.roll(t_a, dh_a // 2, 1)
    cos_a = jnp.where(first_a, t_a, r_a)
    sin_a = jnp.where(first_a, -r_a, t_a)
    first_i = (lane & (D_IDX - 1)) < D_IDX // 2
    c_i = jnp.where(first_i, t_i, pltpu.roll(t_i, D_IDX // 2, 1))
    s1_i = jnp.where(first_i, 0.0, t_i)
    s2_i = jnp.where(first_i, -pltpu.roll(t_i, LANES - D_IDX // 2, 1), 0.0)
    in_key = lane < D_IDX
    c_t = jnp.where(in_key, c_i, 1.0)
    s1_t = jnp.where(in_key, s1_i, 0.0)
    s2_t = jnp.where(in_key, s2_i, 0.0)

    def norm_rope(z, g_ref):
        outs = []
        for hh in range(mix_a // dh_a):
            zh = _rms(z[:, hh * dh_a:(hh + 1) * dh_a], g_ref[...])
            outs.append(zh * cos_a + pltpu.roll(zh, dh_a // 2, 1) * sin_a)
        return outs

    aq = norm_rope(proj(o_aq, mix_a), qn_ref)
    aqb_ref[...] = jnp.concatenate(aq, axis=1).astype(BF16)
    ak = norm_rope(proj(o_ak, mix_a), kn_ref)
    av = proj(o_av, mix_a)
    for hh in range(mix_a // dh_a):
        ak_ref[:, hh, :] = ak[hh]
        av_ref[:, hh, :] = av[:, hh * dh_a:(hh + 1) * dh_a]
    akb_ref[...] = jnp.concatenate(ak, axis=1).astype(BF16)
    avb_ref[...] = av.astype(BF16)

    ziq = proj(o_iq, H_IDX * D_IDX)
    cols = []
    for c in range(H_IDX * D_IDX // LANES):
        zc = ziq[:, c * LANES:(c + 1) * LANES]
        cols.append(zc * c_i + pltpu.roll(zc, D_IDX // 2, 1) * s1_i
                    + pltpu.roll(zc, LANES - D_IDX // 2, 1) * s2_i)
    iqb_ref[...] = jnp.concatenate(cols, axis=1).astype(BF16)

    zt = proj(o_tail, LANES) + bias_ref[...]
    zt = zt * c_t + pltpu.roll(zt, D_IDX // 2, 1) * s1_t + pltpu.roll(zt, LANES - D_IDX // 2, 1) * s2_t
    f_lo = D_IDX + H_IDX + H_M
    log_sig = jnp.minimum(zt, 0.0) - jnp.log(1.0 + jnp.exp(-jnp.abs(zt)))
    misc_ref[...] = jnp.where((lane >= f_lo) & (lane < f_lo + H_M), log_sig, zt)


def _in_proj(x2d, norm_mix, w_parts, bias_tail, q_norm, k_norm, tab, *, tm, tab_tiles, mix_m, mix_a):
    rows, d = x2d.shape
    dh_m = mix_m // H_M
    dh_a = mix_a // H_A
    once = lambda a: pl.BlockSpec(a.shape, lambda i: (0, 0), pipeline_mode=pl.Buffered(1))
    grid = (rows // tm,)
    row_spec = lambda wdt: pl.BlockSpec((tm, wdt), lambda i: (i, 0))
    const = lambda shp: pl.BlockSpec(shp, lambda i: (0, 0))
    out_shapes = [
        jax.ShapeDtypeStruct((rows, mix_m), F32),
        jax.ShapeDtypeStruct((rows, mix_m), F32),
        jax.ShapeDtypeStruct((rows, mix_m), F32),
        jax.ShapeDtypeStruct((rows, mix_m), F32),
        jax.ShapeDtypeStruct((rows, mix_a), BF16),
        jax.ShapeDtypeStruct((rows, H_A, dh_a), F32),
        jax.ShapeDtypeStruct((rows, H_A, dh_a), F32),
        jax.ShapeDtypeStruct((rows, mix_a), BF16),
        jax.ShapeDtypeStruct((rows, mix_a), BF16),
        jax.ShapeDtypeStruct((rows, H_IDX * D_IDX), BF16),
        jax.ShapeDtypeStruct((rows, LANES), F32),
    ]
    head_spec = pl.BlockSpec((tm, H_A, dh_a), lambda i: (i, 0, 0))
    out_specs = ([row_spec(mix_m)] * 4 + [row_spec(mix_a), head_spec, head_spec, row_spec(mix_a), row_spec(mix_a)]
                 + [row_spec(H_IDX * D_IDX), row_spec(LANES)])
    return pl.pallas_call(
        functools.partial(_in_proj_kernel, mix_m=mix_m, mix_a=mix_a, dh_m=dh_m, dh_a=dh_a),
        grid=grid,
        in_specs=[row_spec(d), const((1, d)), once(w_parts[0]), once(w_parts[1]), once(w_parts[2]),
                  const((1, LANES)),
                  const((1, dh_a)), const((1, dh_a)),
                  pl.BlockSpec((tm, 2 * LANES), lambda i: (i % tab_tiles, 0))],
        out_specs=out_specs,
        out_shape=out_shapes,
        compiler_params=_cparams(("parallel",)),
        name="in_proj",
    )(x2d, norm_mix, *w_parts, bias_tail, q_norm, k_norm, tab)


def _mlstm_prompt_kernel(q_ref, k_ref, v_ref, o_ref, grow_ref, gcol_ref, gain_ref,
                         y_ref, c_ref, n_ref, m_ref, cs_ref, ns_ref, ms_ref, *, chunk, d, nbp):
    c_idx = pl.program_id(1)
    L = chunk
    row_i = lax.broadcasted_iota(jnp.int32, (L, L), 0)
    col_i = lax.broadcasted_iota(jnp.int32, (L, L), 1)
    tril = col_i <= row_i
    triu = row_i <= col_i

    @pl.when(c_idx == 0)
    def _():
        cs_ref[...] = jnp.zeros_like(cs_ref)
        ns_ref[...] = jnp.zeros_like(ns_ref)
        ms_ref[...] = jnp.zeros_like(ms_ref)

    chains = [(bi, hd) for bi in range(nbp) for hd in range(H_M)]
    tril_b = tril.astype(BF16)
    csum = []
    for bi in range(nbp):
        g = gcol_ref[bi]
        g_hi = g.astype(BF16)
        r1 = g - g_hi.astype(F32)
        g_mid = r1.astype(BF16)
        g_lo = (r1 - g_mid.astype(F32)).astype(BF16)
        csum.append(_mm(tril_b, g_hi) + _mm(tril_b, g_mid) + _mm(tril_b, g_lo))
    st = []
    for bi, hd in chains:
        gr = grow_ref[bi, 0]
        gc = gcol_ref[bi]
        sidx = bi * H_M + hd
        hs = slice(hd * d, (hd + 1) * d)
        m = ms_ref[sidx, 0:1, 0:1]
        li_r = gr[hd:hd + 1, :]
        li_c = gc[:, hd:hd + 1]
        lf_c = gc[:, H_M + hd:H_M + hd + 1]
        b_c = csum[bi][:, H_M + hd:H_M + hd + 1]
        b_r = jnp.sum(jnp.where(triu, lf_c, 0.0), axis=0, keepdims=True)
        logd = jnp.where(tril, b_c - b_r + li_r, NEG_INF)
        inter = b_c + m
        m_t = jnp.maximum(inter, jnp.max(logd, axis=1, keepdims=True))
        st.append(dict(bi=bi, sidx=sidx, hs=hs, m=m, li_c=li_c, b_c=b_c, inter=inter, m_t=m_t,
                       dmat=jnp.exp(logd - m_t)))
    for c in st:
        q = q_ref[c["bi"], :, c["hs"]]
        c["q"] = q
        c["kb"] = k_ref[c["bi"], :, c["hs"]].astype(BF16)
        qb = q.astype(BF16)
        c["C"] = cs_ref[c["sidx"]]
        c["n"] = ns_ref[c["sidx"], 0:1, :]
        c["s"] = _nt(qb, c["kb"]) * c["dmat"]
        c_aug = jnp.concatenate([c["C"], jnp.broadcast_to(c["n"], (d, d))], axis=0).astype(BF16)
        c["qc"] = _nt(qb, c_aug)
    ones_ld = jnp.ones((L, d), BF16)
    for c in st:
        v = v_ref[c["bi"], :, c["hs"]]
        c["v"] = v
        g_inter = jnp.exp(c["inter"] - c["m_t"])
        sv = _mm(c["s"].astype(BF16), jnp.concatenate([v.astype(BF16), ones_ld], axis=1))
        num = g_inter * c["qc"][:, :d] + sv[:, :d]
        den = g_inter * c["qc"][:, d:] + sv[:, d:]
        h = num / jnp.maximum(jnp.abs(den), jnp.exp(-c["m_t"]))
        o = o_ref[c["bi"], :, c["hs"]]
        y_ref[c["bi"], :, c["hs"]] = _sigmoid(o) * _rms(h, gain_ref[:, c["hs"]])
    for c in st:
        m_new = c["m_t"][L - 1:L, :]
        b_last = c["b_c"][L - 1:L, :]
        g_prev = jnp.exp(b_last + c["m"] - m_new)
        w_c = jnp.exp(b_last - c["b_c"] + c["li_c"] - m_new)
        k = k_ref[c["bi"], :, c["hs"]]
        cs_ref[c["sidx"]] = g_prev * c["C"] + _tn((c["v"] * w_c).astype(BF16), c["kb"])
        ns_ref[c["sidx"], 0:1, :] = g_prev * c["n"] + jnp.sum(k * w_c, axis=0, keepdims=True)
        ms_ref[c["sidx"], 0:1, :] = jnp.broadcast_to(m_new, (1, LANES))

    @pl.when(c_idx == pl.num_programs(1) - 1)
    def _():
        for bi in range(nbp):
            for hd in range(H_M):
                sidx = bi * H_M + hd
                c_ref[bi, hd] = cs_ref[sidx]
                n_ref[bi, hd:hd + 1, :] = ns_ref[sidx, 0:1, :]
                m_ref[bi, hd:hd + 1, :] = ms_ref[sidx, 0:1, :]


def _mlstm_prompt(mq, mk, mv, mo, grow, gcol, gain, *, chunk):
    B, S, mix_m = mq.shape
    d = mix_m // H_M
    n_chunks = S // chunk
    nbp = 4 if B % 4 == 0 else (2 if B % 2 == 0 else 1)
    seq = pl.BlockSpec((nbp, chunk, mix_m), lambda b, c: (b, c, 0))
    return pl.pallas_call(
        functools.partial(_mlstm_prompt_kernel, chunk=chunk, d=d, nbp=nbp),
        grid=(B // nbp, n_chunks),
        in_specs=[seq, seq, seq, seq,
                  pl.BlockSpec((nbp, 1, 2 * H_M, chunk), lambda b, c: (b, c, 0, 0)),
                  pl.BlockSpec((nbp, chunk, 2 * H_M), lambda b, c: (b, c, 0)),
                  pl.BlockSpec((1, mix_m), lambda b, c: (0, 0))],
        out_specs=[seq,
                   pl.BlockSpec((nbp, H_M, d, d), lambda b, c: (b, 0, 0, 0)),
                   pl.BlockSpec((nbp, H_M, d), lambda b, c: (b, 0, 0)),
                   pl.BlockSpec((nbp, H_M, LANES), lambda b, c: (b, 0, 0))],
        out_shape=[jax.ShapeDtypeStruct((B, S, mix_m), F32),
                   jax.ShapeDtypeStruct((B, H_M, d, d), F32),
                   jax.ShapeDtypeStruct((B, H_M, d), F32),
                   jax.ShapeDtypeStruct((B, H_M, LANES), F32)],
        scratch_shapes=[pltpu.VMEM((nbp * H_M, d, d), F32), pltpu.VMEM((nbp * H_M, 8, d), F32),
                        pltpu.VMEM((nbp * H_M, 8, LANES), F32)],
        compiler_params=_cparams(("parallel", "arbitrary")),
        name="mlstm_prompt",
    )(mq, mk, mv, mo, grow, gcol, gain)


def _mlstm_sample_kernel(q_ref, k_ref, v_ref, o_ref, gs_ref, c_ref, n_ref, gain_ref,
                         y_ref, co_ref, no_ref, mo_ref, *, d):
    gs = gs_ref[0]
    eye = (lax.broadcasted_iota(jnp.int32, (d, d), 0) == lax.broadcasted_iota(jnp.int32, (d, d), 1))
    lane = lax.broadcasted_iota(jnp.int32, (1, LANES), 1)
    m_out = jnp.zeros((1, LANES), F32)
    for h in range(H_M):
        sl = slice(h * d, (h + 1) * d)
        q = q_ref[0, :, sl]
        k = k_ref[0, :, sl]
        v = v_ref[0, :, sl]
        o = o_ref[0, :, sl]
        li = gs[:, h:h + 1]
        lf = gs[:, H_M + h:H_M + h + 1]
        m = gs[:, 2 * H_M + h:2 * H_M + h + 1]
        C = c_ref[0, h]
        n = n_ref[0, :, sl]
        inter = lf + m
        m_t = jnp.maximum(inter, li)
        s = jnp.sum(q * k, axis=1, keepdims=True) * jnp.exp(li - m_t)
        g = jnp.exp(inter - m_t)
        q8 = jnp.broadcast_to(q, (8, d)).astype(BF16)
        cq = _nt(q8, C.astype(BF16))[0:1, :]
        num = g * cq + s * v
        den = g * jnp.sum(n * q, axis=1, keepdims=True) + s
        hh = num / jnp.maximum(jnp.abs(den), jnp.exp(-m_t))
        w = jnp.exp(li - m_t)
        v_col = jnp.sum(jnp.where(eye, v, 0.0), axis=1, keepdims=True)
        co_ref[0, h] = g * C + (w * v_col) * k
        no_ref[0, :, sl] = g * n + w * k
        m_out = jnp.where(lane == h, m_t, m_out)
        y_ref[0, :, sl] = _sigmoid(o) * _rms(hh, gain_ref[:, sl])
    mo_ref[0] = m_out


def _mlstm_sample(mq, mk, mv, mo, gs, c_state, n_state, gain):
    Bd, _, mix_m = mq.shape
    d = mix_m // H_M
    row = pl.BlockSpec((1, 1, mix_m), lambda b: (b, 0, 0))
    return pl.pallas_call(
        functools.partial(_mlstm_sample_kernel, d=d),
        grid=(Bd,),
        in_specs=[row, row, row, row,
                  pl.BlockSpec((1, 1, 3 * H_M), lambda b: (b, 0, 0)),
                  pl.BlockSpec((1, H_M, d, d), lambda b: (b, 0, 0, 0)),
                  row,
                  pl.BlockSpec((1, mix_m), lambda b: (0, 0))],
        out_specs=[row,
                   pl.BlockSpec((1, H_M, d, d), lambda b: (b, 0, 0, 0)),
                   row,
                   pl.BlockSpec((1, 1, LANES), lambda b: (b, 0, 0))],
        out_shape=[jax.ShapeDtypeStruct((Bd, 1, mix_m), F32),
                   jax.ShapeDtypeStruct((Bd, H_M, d, d), F32),
                   jax.ShapeDtypeStruct((Bd, 1, mix_m), F32),
                   jax.ShapeDtypeStruct((Bd, 1, LANES), F32)],
        compiler_params=_cparams(("parallel",)),
        name="mlstm_sample",
    )(mq, mk, mv, mo, gs, c_state, n_state, gain)


def _dsa_prompt_kernel(iq_ref, misc_ref, ikt_ref, aq_ref, ak_ref, av_ref, ya_ref, sc_ref, acc_ref,
                       *, tq, w, topk, dh, scale):
    i = pl.program_id(1)
    nk = ((i + 1) * tq + w - 1) // w
    kf = float(topk)
    nsub = w // LANES

    q_pos = i * tq + lax.broadcasted_iota(jnp.int32, (tq, 1), 0)
    lane_w = lax.broadcasted_iota(jnp.int32, (1, w), 1)
    iq = iq_ref[0]
    iq_h = [iq[:, h * D_IDX:(h + 1) * D_IDX] for h in range(H_IDX)]
    misc = misc_ref[0]
    w_h = [misc[:, D_IDX + h:D_IDX + h + 1] for h in range(H_IDX)]

    def score_body(c, carry, causal_edge):
        rmax, rmin = carry
        ikc = ikt_ref[0, c]
        score = jnp.zeros((tq, w), F32)
        for h in range(H_IDX):
            score = score + w_h[h] * jnp.maximum(_mm(iq_h[h], ikc), 0.0)
        if causal_edge:
            valid = (c * w + lane_w) <= q_pos
            sc_ref[c] = jnp.where(valid, score, NEG_INF)
            s_hi = jnp.where(valid, score, NEG_INF)
            s_lo = jnp.where(valid, score, POS_INF)
        else:
            sc_ref[c] = score
            s_hi = s_lo = score
        rmax = jnp.maximum(rmax, jnp.max(s_hi, axis=1, keepdims=True))
        rmin = jnp.minimum(rmin, jnp.min(s_lo, axis=1, keepdims=True))
        return rmax, rmin

    n_full = (i * tq) // w
    carry0 = (jnp.full((tq, 1), NEG_INF, F32), jnp.full((tq, 1), POS_INF, F32))
    carry0 = lax.fori_loop(0, n_full, functools.partial(score_body, causal_edge=False), carry0)
    rmax, rmin = lax.fori_loop(n_full, nk, functools.partial(score_body, causal_edge=True), carry0)

    ge = lambda x, t: x >= t
    gt = lambda x, t: x > t

    rh = min(tq, LANES)
    groups = [pl.ds(r0, rh) for r0 in range(0, tq, rh)]
    part = lambda a: [a[r0:r0 + rh] for r0 in range(0, tq, rh)]

    def pass_acc(rows, fn, init, combine):
        def body(c, acc):
            x = sc_ref[c, rows, :]
            for j in range(nsub):
                acc = combine(acc, fn(x[:, j * LANES:(j + 1) * LANES]))
            return acc
        return lax.fori_loop(0, nk, body, jnp.full((rh, LANES), init, F32))

    def count_acc(rows, pred, thr):
        thr_b = jnp.broadcast_to(thr, (rh, LANES))
        return pass_acc(rows, lambda x: jnp.where(pred(x, thr_b), 1.0, 0.0), 0.0, jnp.add)

    def count(rows, pred, thr):
        return jnp.sum(count_acc(rows, pred, thr), axis=1, keepdims=True)

    def min_where(rows, pred, thr):
        thr_b = jnp.broadcast_to(thr, (rh, LANES))
        acc = pass_acc(rows, lambda x: jnp.where(pred(x, thr_b), x, POS_INF), POS_INF, jnp.minimum)
        return jnp.min(acc, axis=1, keepdims=True)

    def bis_body(_, carry):
        los, his, clos = carry
        mids = [0.5 * (lo + hi) for lo, hi in zip(los, his)]
        accs = [count_acc(rows, ge, mid) for rows, mid in zip(groups, mids)]
        cms = [jnp.sum(a, axis=1, keepdims=True) for a in accs]
        oks = [cm >= kf for cm in cms]
        return (tuple(jnp.where(ok, mid, lo) for ok, mid, lo in zip(oks, mids, los)),
                tuple(jnp.where(ok, hi, mid) for ok, mid, hi in zip(oks, mids, his)),
                tuple(jnp.where(ok, cm, cl) for ok, cm, cl in zip(oks, cms, clos)))

    los, _, clos = lax.fori_loop(
        0, N_BISECT, bis_body,
        (tuple(part(rmin)), tuple(part(rmax + jnp.abs(rmax) + 1.0)), tuple(part((q_pos + 1).astype(F32)))))

    def finish_rows(rows, qp, rmin_h, lo, c_lo):
        active = (qp + 1) > topk
        unresolved = jnp.max(jnp.where(active & (c_lo != kf), 1.0, 0.0)) > 0.5

        @pl.when(jnp.logical_not(unresolved))
        def _():
            thr = jnp.where(active, lo, rmin_h)

            def body(c, _):
                sc_ref[c, rows, :] = jnp.where(sc_ref[c, rows, :] >= thr, 0.0, NEG_INF)
                return 0
            lax.fori_loop(0, nk, body, 0)

        @pl.when(unresolved)
        def _():
            tau = min_where(rows, ge, lo)
            g = count(rows, gt, tau)

            def undone(tau, g):
                return active & (g >= kf)

            def fix_cond(st):
                return jnp.max(jnp.where(undone(*st), 1.0, 0.0)) > 0.5

            def fix_body(st):
                tau, g = st
                nd = undone(tau, g)
                tau2 = jnp.where(nd, min_where(rows, gt, tau), tau)
                return tau2, jnp.where(nd, count(rows, gt, tau2), g)

            tau, g = lax.while_loop(fix_cond, fix_body, (tau, g))
            tau_b = jnp.broadcast_to(jnp.where(active, tau, rmin_h), (rh, LANES))
            need_b = jnp.broadcast_to(jnp.where(active, kf - g, 1e9), (rh, LANES))
            r_i = lax.broadcasted_iota(jnp.int32, (LANES, 2 * LANES), 0)
            c_i = lax.broadcasted_iota(jnp.int32, (LANES, 2 * LANES), 1)
            tri_ones = ((r_i <= c_i) | (c_i >= LANES)).astype(BF16)

            def body(c, run):
                x = sc_ref[c, rows, :]
                outs = []
                for j in range(nsub):
                    xj = x[:, j * LANES:(j + 1) * LANES]
                    is_eq = xj == tau_b
                    cnt2 = _mm(jnp.where(is_eq, 1.0, 0.0).astype(BF16), tri_ones)
                    sel = (xj > tau_b) | (is_eq & (cnt2[:, :LANES] + run <= need_b))
                    outs.append(jnp.where(sel, 0.0, NEG_INF))
                    run = run + cnt2[:, LANES:]
                sc_ref[c, rows, :] = jnp.concatenate(outs, axis=1)
                return run
            lax.fori_loop(0, nk, body, jnp.zeros((rh, LANES), F32))

    for rows, qp, rmin_h, lo, c_lo in zip(groups, part(q_pos), part(rmin), los, clos):
        finish_rows(rows, qp, rmin_h, lo, c_lo)

    aq = aq_ref[0]
    q_heads = [aq[:, h * dh:(h + 1) * dh] for h in range(H_A)]
    acc_ref[...] = jnp.zeros_like(acc_ref)
    c2 = scale * LOG2E
    ones_blk = jnp.ones((w, dh), BF16)

    def att_body(c, ms):
        k0 = pl.multiple_of(c * w, w)
        bias = sc_ref[c]
        heads = [slice(h * dh, (h + 1) * dh) for h in range(H_A)]
        ss = [_nt(q_heads[h], ak_ref[0, pl.ds(k0, w), heads[h]]) + bias for h in range(H_A)]
        ms_new = [jnp.maximum(ms[h], jnp.max(ss[h], axis=1, keepdims=True)) for h in range(H_A)]
        m_safe = [jnp.where(m == NEG_INF, 0.0, m) for m in ms_new]
        ps = [jnp.exp2((ss[h] - m_safe[h]) * c2).astype(BF16) for h in range(H_A)]
        for h in range(H_A):
            alpha = jnp.exp2((ms[h] - m_safe[h]) * c2)
            v_aug = jnp.concatenate([av_ref[0, pl.ds(k0, w), heads[h]], ones_blk], axis=1)
            acc_ref[h] = alpha * acc_ref[h] + _mm(ps[h], v_aug)
        return tuple(ms_new)

    lax.fori_loop(0, nk, att_body, tuple(jnp.full((tq, 1), NEG_INF, F32) for _ in range(H_A)))
    for h in range(H_A):
        a = acc_ref[h]
        ya_ref[0, :, h * dh:(h + 1) * dh] = a[:, :dh] / a[:, dh:]


def _dsa_prompt(iqb, misc, ikt, aqb, akb, avb, *, tq, w, topk):
    B, S, mix_a = aqb.shape
    dh = mix_a // H_A
    nq = S // tq
    nw = S // w
    return pl.pallas_call(
        functools.partial(_dsa_prompt_kernel, tq=tq, w=w, topk=topk, dh=dh, scale=dh ** -0.5),
        grid=(B, nq),
        in_specs=[pl.BlockSpec((1, tq, H_IDX * D_IDX), lambda b, i: (b, i, 0)),
                  pl.BlockSpec((1, tq, LANES), lambda b, i: (b, i, 0)),
                  pl.BlockSpec((1, nw, D_IDX, w), lambda b, i: (b, 0, 0, 0)),
                  pl.BlockSpec((1, tq, mix_a), lambda b, i: (b, i, 0)),
                  pl.BlockSpec((1, S, mix_a), lambda b, i: (b, 0, 0)),
                  pl.BlockSpec((1, S, mix_a), lambda b, i: (b, 0, 0))],
        out_specs=pl.BlockSpec((1, tq, mix_a), lambda b, i: (b, i, 0)),
        out_shape=jax.ShapeDtypeStruct((B, S, mix_a), F32),
        scratch_shapes=[pltpu.VMEM((nw, tq, w), F32), pltpu.VMEM((H_A, tq, 2 * dh), F32)],
        compiler_params=_cparams(("parallel", "arbitrary")),
        name="dsa_prompt",
    )(iqb, misc, ikt, aqb, akb, avb)


def _dsa_sample_select_kernel(pt_ref, iq_ref, w_ref, ikn_ref, ptv_ref, pool_ref, rows_ref, flag_ref,
                              ikbuf, sem, sc_ref, xn_ref, slot_ref, phys_ref,
                              *, n_pages, page, topk, cw):
    nb = iq_ref.shape[0]
    past = n_pages * page
    kf = float(topk)
    n_cw = past // cw

    def page_copy(bb, p, slot):
        return pltpu.make_async_copy(pool_ref.at[pt_ref[bb, p]],
                                     ikbuf.at[slot, :, pl.ds(p * page, page)],
                                     sem.at[slot])

    def start_all(bb, slot):
        def body(p, _):
            page_copy(bb, p, slot).start()
            return 0
        lax.fori_loop(0, n_pages, body, 0)

    start_all(0, 0)

    def score_body(b, _):
        slot = b % 2

        @pl.when(b + 1 < nb)
        def _():
            start_all(b + 1, 1 - slot)

        def wait_body(p, _):
            page_copy(b, p, slot).wait()
            return 0
        lax.fori_loop(0, n_pages, wait_body, 0)

        iq8 = iq_ref[b]
        w8 = w_ref[b]
        s8 = _mm(iq8, ikbuf[slot].astype(BF16))
        sc_ref[pl.ds(b, 1), :] = jnp.sum(w8 * jnp.maximum(s8, 0.0), axis=0, keepdims=True)
        ikn = ikn_ref[b].astype(BF16).astype(F32)
        sn8 = jnp.sum(iq8.astype(F32) * ikn, axis=1, keepdims=True)
        xn_b = jnp.sum(w8 * jnp.maximum(sn8, 0.0), axis=0, keepdims=True)
        xn_ref[pl.ds(b, 1), :] = jnp.broadcast_to(xn_b, (1, LANES))
        return 0

    lax.fori_loop(0, nb, score_body, 0)

    x = sc_ref[...]
    xn = xn_ref[:, 0:1]

    def cnt(mask_row, mask_new):
        return (jnp.sum(jnp.where(mask_row, 1.0, 0.0), axis=1, keepdims=True)
                + jnp.where(mask_new, 1.0, 0.0))

    rmax = jnp.maximum(jnp.max(x, axis=1, keepdims=True), xn)
    rmin = jnp.minimum(jnp.min(x, axis=1, keepdims=True), xn)
    hi0 = rmax + jnp.abs(rmax) + 1.0

    def bis_body(_, carry):
        lo, hi = carry
        mid = 0.5 * (lo + hi)
        ok = cnt(x >= mid, xn >= mid) >= kf
        return jnp.where(ok, mid, lo), jnp.where(ok, hi, mid)

    lo, _ = lax.fori_loop(0, N_BISECT, bis_body, (rmin, hi0))

    def min_where(mask_row, mask_new):
        return jnp.minimum(jnp.min(jnp.where(mask_row, x, POS_INF), axis=1, keepdims=True),
                           jnp.where(mask_new, xn, POS_INF))

    tau = min_where(x >= lo, xn >= lo)
    g = cnt(x > tau, xn > tau)

    def fix_cond(st):
        tau, g = st
        return jnp.max(jnp.where(g >= kf, 1.0, 0.0)) > 0.5

    def fix_body(st):
        tau, g = st
        tau2 = jnp.where(g >= kf, min_where(x > tau, xn > tau), tau)
        return tau2, cnt(x > tau2, xn > tau2)

    tau, g = lax.while_loop(fix_cond, fix_body, (tau, g))
    need = kf - g

    tri = (lax.broadcasted_iota(jnp.int32, (cw, cw), 0)
           < lax.broadcasted_iota(jnp.int32, (cw, cw), 1)).astype(BF16)

    def excl_prefix(flag):
        outs = []
        run = jnp.zeros((nb, 1), F32)
        for c in range(n_cw):
            f = flag[:, c * cw:(c + 1) * cw]
            outs.append(_mm(f.astype(BF16), tri) + run)
            run = run + jnp.sum(f, axis=1, keepdims=True)
        return jnp.concatenate(outs, axis=1), run

    is_eq = x == tau
    pre_eq, n_eq_past = excl_prefix(jnp.where(is_eq, 1.0, 0.0))
    sel = (x > tau) | (is_eq & (pre_eq < need))
    new_sel = (xn > tau) | ((xn == tau) & (n_eq_past < need))
    slot, _ = excl_prefix(jnp.where(sel, 1.0, 0.0))
    slot_ref[...] = jnp.where(sel, slot, -1.0)

    ptv = ptv_ref[...]
    jrow = lax.broadcasted_iota(jnp.int32, (1, past), 1)
    prow = lax.broadcasted_iota(jnp.int32, (n_pages, 1), 0)
    expand = ((jrow >= prow * page) & (jrow < (prow + 1) * page)).astype(BF16)
    digit_bits = 6
    pt_hi = _mm((ptv >> digit_bits).astype(F32).astype(BF16), expand)
    pt_lo = _mm((ptv & ((1 << digit_bits) - 1)).astype(F32).astype(BF16), expand)
    pidx = lax.broadcasted_iota(jnp.int32, (8, n_pages), 1).astype(F32).astype(BF16)
    pg = _mm(pidx, expand)[0:1, :]
    phys_ref[...] = (pt_hi * (1 << digit_bits) + pt_lo) * page + (jrow.astype(F32) - pg * page)

    slot_col = lax.broadcasted_iota(jnp.int32, (topk, 1), 0).astype(F32)
    lane_b = lax.broadcasted_iota(jnp.int32, (1, LANES), 1)

    def extract_body(b, out):
        srow = slot_ref[pl.ds(b, 1), :]
        frow = phys_ref[pl.ds(b, 1), :]
        acc = jnp.zeros((topk, LANES), F32)
        for c in range(past // LANES):
            cs = slice(c * LANES, (c + 1) * LANES)
            acc = acc + jnp.where(srow[:, cs] == slot_col, frow[:, cs], 0.0)
        return jnp.where(lane_b == b, jnp.sum(acc, axis=1, keepdims=True), out)

    out = lax.fori_loop(0, nb, extract_body, jnp.zeros((topk, LANES), F32))
    rows_ref[...] = out.astype(jnp.int32)
    flag_ref[...] = jnp.broadcast_to(jnp.where(new_sel, 1, 0), (nb, LANES)).astype(jnp.int32)


def _dsa_sample_select(page_table, iq8, w8, ik_new, pool_ik_t, *, topk, cw):
    Bd, n_pages = page_table.shape
    n_pool, d_idx, page = pool_ik_t.shape
    past = n_pages * page
    assert Bd <= LANES and n_pool <= 64 * 256
    full = lambda shp: pl.BlockSpec(shp, lambda i, pt: (0,) * len(shp))
    grid_spec = pltpu.PrefetchScalarGridSpec(
        num_scalar_prefetch=1,
        grid=(1,),
        in_specs=[full((Bd, 8, d_idx)), full((Bd, 8, 1)), full((Bd, 1, d_idx)), full((Bd, n_pages)),
                  pl.BlockSpec(memory_space=pl.ANY)],
        out_specs=[full((topk, LANES)), full((Bd, LANES))],
        scratch_shapes=[pltpu.VMEM((2, d_idx, past), F32),
                        pltpu.SemaphoreType.DMA((2,)),
                        pltpu.VMEM((Bd, past), F32),
                        pltpu.VMEM((Bd, LANES), F32),
                        pltpu.VMEM((Bd, past), F32),
                        pltpu.VMEM((Bd, past), F32)],
    )
    return pl.pallas_call(
        functools.partial(_dsa_sample_select_kernel, n_pages=n_pages, page=page, topk=topk, cw=cw),
        grid_spec=grid_spec,
        out_shape=[jax.ShapeDtypeStruct((topk, LANES), jnp.int32),
                   jax.ShapeDtypeStruct((Bd, LANES), jnp.int32)],
        compiler_params=_cparams(("arbitrary",)),
        name="dsa_sample_select",
    )(page_table, iq8, w8, ik_new, page_table, pool_ik_t)


def _dsa_sample_attend_kernel(rows_ref, flag_ref, aq_ref, knew_ref, vnew_ref, kpool_ref, vpool_ref,
                              ya_ref, kbuf, vbuf, sem, *, topk, dh, scale):
    b = pl.program_id(0)
    nb = pl.num_programs(0)

    def row_copies(bb, t, slot):
        r = rows_ref[bb, t]
        dst = pl.ds(t * H_A, H_A)
        return (pltpu.make_async_copy(kpool_ref.at[r], kbuf.at[slot, dst, :], sem.at[0, slot]),
                pltpu.make_async_copy(vpool_ref.at[r], vbuf.at[slot, dst, :], sem.at[1, slot]))

    def start_all(bb, slot):
        def body(t, _):
            ck, cv = row_copies(bb, t, slot)
            ck.start()
            cv.start()
            return 0
        lax.fori_loop(0, topk, body, 0, unroll=8)

    slot = b % 2

    @pl.when(b == 0)
    def _():
        start_all(0, 0)

    @pl.when(b + 1 < nb)
    def _():
        start_all(b + 1, 1 - slot)

    def wait_body(t, _):
        ck, cv = row_copies(b, t, slot)
        ck.wait()
        cv.wait()
        return 0
    lax.fori_loop(0, topk, wait_body, 0, unroll=8)

    take_new = (lax.broadcasted_iota(jnp.int32, (topk, 1), 0) == topk - 1) & (flag_ref[b] > 0)
    aq = aq_ref[0]
    for h in range(H_A):
        hs = slice(h * dh, (h + 1) * dh)
        kh = kbuf[slot, pl.ds(h, topk, stride=H_A), :]
        vh = vbuf[slot, pl.ds(h, topk, stride=H_A), :]
        kh = jnp.where(take_new, knew_ref[0, h:h + 1, :], kh).astype(BF16)
        vh = jnp.where(take_new, vnew_ref[0, h:h + 1, :], vh).astype(BF16)
        q8 = jnp.broadcast_to(aq[:, hs], (8, dh))
        s = _nt(q8, kh) * scale
        m = jnp.max(s, axis=1, keepdims=True)
        p = jnp.exp(s - m)
        p = p / jnp.sum(p, axis=1, keepdims=True)
        ya_ref[0, :, hs] = _mm(p.astype(BF16), vh)[0:1, :]


def _dsa_sample_attend(rows, flags, aqb, k_new, v_new, pool_k, pool_v, *, topk):
    Bd, _, mix_a = aqb.shape
    dh = pool_k.shape[2]
    new_spec = pl.BlockSpec((1, H_A, dh), lambda b, r, f: (b, 0, 0))
    grid_spec = pltpu.PrefetchScalarGridSpec(
        num_scalar_prefetch=2,
        grid=(Bd,),
        in_specs=[pl.BlockSpec((1, 1, mix_a), lambda b, r, f: (b, 0, 0)),
                  new_spec, new_spec,
                  pl.BlockSpec(memory_space=pl.ANY),
                  pl.BlockSpec(memory_space=pl.ANY)],
        out_specs=pl.BlockSpec((1, 1, mix_a), lambda b, r, f: (b, 0, 0)),
        scratch_shapes=[pltpu.VMEM((2, topk * H_A, dh), F32),
                        pltpu.VMEM((2, topk * H_A, dh), F32),
                        pltpu.SemaphoreType.DMA((2, 2))],
    )
    return pl.pallas_call(
        functools.partial(_dsa_sample_attend_kernel, topk=topk, dh=dh, scale=dh ** -0.5),
        grid_spec=grid_spec,
        out_shape=jax.ShapeDtypeStruct((Bd, 1, mix_a), F32),
        compiler_params=_cparams(("arbitrary",)),
        name="dsa_sample_attend",
    )(rows, flags, aqb, k_new, v_new, pool_k, pool_v)


def _mem_kv_kernel(mem_ref, nm_ref, wk_ref, wv_ref, kn_ref, k_ref, v_ref, *, dh):
    hm = _rms(mem_ref[...], nm_ref[...]).astype(BF16)
    kk = _mm(hm, wk_ref[...])
    vv = _mm(hm, wv_ref[...])
    for h in range(H_C):
        hs = slice(h * dh, (h + 1) * dh)
        k_ref[:, h, :] = _rms(kk[:, hs], kn_ref[...])
        v_ref[:, h, :] = vv[:, hs]


def _mem_kv(mem2d, norm_mem, w_ck, w_cv, ck_norm, *, tm):
    rows, d = mem2d.shape
    dh = d // H_C
    row = pl.BlockSpec((tm, d), lambda i: (i, 0))
    heads = pl.BlockSpec((tm, H_C, dh), lambda i: (i, 0, 0))
    const = lambda shp: pl.BlockSpec(shp, lambda i: (0, 0))
    return pl.pallas_call(
        functools.partial(_mem_kv_kernel, dh=dh),
        grid=(rows // tm,),
        in_specs=[row, const((1, d)), const((d, d)), const((d, d)), const((1, dh))],
        out_specs=[heads, heads],
        out_shape=[jax.ShapeDtypeStruct((rows, H_C, dh), F32)] * 2,
        compiler_params=_cparams(("parallel",)),
        name="mem_kv",
    )(mem2d, norm_mem, w_ck, w_cv, ck_norm)


def _out_cq_kernel(x_ref, ym_ref, ya_ref, wo_ref, nc_ref, wq_ref, qn_ref, x1_ref, qc_ref, *, mix_m, dh):
    upd = (_mm(ym_ref[...].astype(BF16), wo_ref[0:mix_m, :])
           + _mm(ya_ref[...].astype(BF16), wo_ref[mix_m:, :]))
    x1 = x_ref[...] + upd
    x1_ref[...] = x1
    hq = _mm(_rms(x1, nc_ref[...]).astype(BF16), wq_ref[...])
    for h in range(H_C):
        hs = slice(h * dh, (h + 1) * dh)
        qc_ref[:, hs] = _rms(hq[:, hs], qn_ref[...]).astype(BF16)


def _out_cq(x2d, ym, ya, w_out, norm_cross, w_cq, cq_norm, *, tm):
    rows, d = x2d.shape
    mix_m = ym.shape[1]
    mix_a = ya.shape[1]
    dh = d // H_C
    row = lambda wdt: pl.BlockSpec((tm, wdt), lambda i: (i, 0))
    const = lambda shp: pl.BlockSpec(shp, lambda i: (0, 0))
    return pl.pallas_call(
        functools.partial(_out_cq_kernel, mix_m=mix_m, dh=dh),
        grid=(rows // tm,),
        in_specs=[row(d), row(mix_m), row(mix_a), const((mix_m + mix_a, d)), const((1, d)),
                  const((d, d)), const((1, dh))],
        out_specs=[row(d), row(d)],
        out_shape=[jax.ShapeDtypeStruct((rows, d), F32), jax.ShapeDtypeStruct((rows, d), BF16)],
        compiler_params=_cparams(("parallel",)),
        name="out_cq",
    )(x2d, ym, ya, w_out, norm_cross, w_cq, cq_norm)


def _cross_kernel(q_ref, k_hbm, v_hbm, o_ref, kv_buf, sem, *, dh, scale):
    b = pl.program_id(0)
    t = pl.program_id(1)
    nb = pl.num_programs(0)
    slot = b % 2

    def head_copies(bb, sl):
        cps = []
        for h in range(H_C):
            cps.append(pltpu.make_async_copy(k_hbm.at[bb, :, h, :], kv_buf.at[sl, 0, h], sem.at[sl]))
            cps.append(pltpu.make_async_copy(v_hbm.at[bb, :, h, :], kv_buf.at[sl, 1, h], sem.at[sl]))
        return cps

    @pl.when(t == 0)
    def _():
        @pl.when(b == 0)
        def _():
            for cp in head_copies(0, 0):
                cp.start()

        @pl.when(b + 1 < nb)
        def _():
            for cp in head_copies(b + 1, 1 - slot):
                cp.start()

        for cp in head_copies(b, slot):
            cp.wait()

    q = q_ref[0]
    rows = q.shape[0]
    if rows < 8:
        q = jnp.broadcast_to(q, (8, q.shape[1]))
    for h in range(H_C):
        hs = slice(h * dh, (h + 1) * dh)
        kb = kv_buf[slot, 0, h].astype(BF16)
        vb = kv_buf[slot, 1, h].astype(BF16)
        s = _nt(q[:, hs], kb) * scale
        m = jnp.max(s, axis=1, keepdims=True)
        p = jnp.exp(s - m)
        p = p / jnp.sum(p, axis=1, keepdims=True)
        o = _mm(p.astype(BF16), vb)
        o_ref[0, :, hs] = o[0:rows].astype(BF16)


def _cross(qc, mem_k, mem_v, *, tq):
    B, T, d = qc.shape
    M = mem_k.shape[1]
    dh = d // H_C
    return pl.pallas_call(
        functools.partial(_cross_kernel, dh=dh, scale=dh ** -0.5),
        grid=(B, T // tq),
        in_specs=[pl.BlockSpec((1, tq, d), lambda b, t: (b, t, 0)),
                  pl.BlockSpec(memory_space=pl.ANY),
                  pl.BlockSpec(memory_space=pl.ANY)],
        out_specs=pl.BlockSpec((1, tq, d), lambda b, t: (b, t, 0)),
        out_shape=jax.ShapeDtypeStruct((B, T, d), BF16),
        scratch_shapes=[pltpu.VMEM((2, 2, H_C, M, dh), F32), pltpu.SemaphoreType.DMA((2,))],
        compiler_params=_cparams(("arbitrary", "arbitrary")),
        name="cross_attn",
    )(qc, mem_k, mem_v)


def _gelu_tanh(x):
    return 0.5 * x * (1.0 + jnp.tanh(np.sqrt(2.0 / np.pi) * (x + 0.044715 * (x * x * x))))


def _ffn_front(x1_ref, o_ref, wco_ref, nf_ref, x2_ref, hb_ref, acc_ref):
    x2 = x1_ref[0] + _mm(o_ref[0], wco_ref[...])
    x2_ref[...] = x2
    hb_ref[...] = _rms(x2, nf_ref[...]).astype(BF16)
    acc_ref[...] = jnp.zeros_like(acc_ref)


def _ffn_prompt_kernel(x1_ref, o_ref, wco_ref, nf_ref, wua_ref, wug_ref, cwa_ref, cwg_ref,
                       cba_ref, cbg_ref, wd_ref, ha_ref, hg_ref,
                       y_ref, ca_ref, cg_ref, x2_ref, hb_ref, acc_ref, carry_ref, *, tm, rs):
    t = pl.program_id(1)
    j = pl.program_id(2)
    nj = pl.num_programs(2)

    @pl.when(j == 0)
    def _():
        _ffn_front(x1_ref, o_ref, wco_ref, nf_ref, x2_ref, hb_ref, acc_ref)

    @pl.when(t == 0)
    def _():
        carry_ref[j, 0, 6:8, :] = ha_ref[0]
        carry_ref[j, 1, 6:8, :] = hg_ref[0]

    rid = lax.broadcasted_iota(jnp.int32, (rs, 1), 0)

    def conv_part(hb, part, wu_ref, cw_ref, cb_ref):
        u = _mm(hb, wu_ref[...])
        p2 = carry_ref[j, part, 6:7, :]
        p1 = carry_ref[j, part, 7:8, :]
        um1 = jnp.where(rid == 0, p1, pltpu.roll(u, 1, 0))
        um2 = jnp.where(rid == 0, p2, jnp.where(rid == 1, p1, pltpu.roll(u, 2, 0)))
        carry_ref[j, part] = u[rs - 8:rs, :]
        return cb_ref[...] + um2 * cw_ref[0:1, :] + um1 * cw_ref[1:2, :] + u * cw_ref[2:3, :]

    def sub_body(r, _):
        r0 = pl.multiple_of(r * rs, rs)
        hb = hb_ref[pl.ds(r0, rs), :]
        a = conv_part(hb, 0, wua_ref, cwa_ref, cba_ref)
        g = conv_part(hb, 1, wug_ref, cwg_ref, cbg_ref)
        acc_ref[pl.ds(r0, rs), :] += _mm((_gelu_tanh(g) * a).astype(BF16), wd_ref[...])
        return 0

    lax.fori_loop(0, tm // rs, sub_body, 0)
    ca_ref[0, 0] = carry_ref[j, 0, 6:8, :]
    cg_ref[0, 0] = carry_ref[j, 1, 6:8, :]

    @pl.when(j == nj - 1)
    def _():
        y_ref[0] = x2_ref[...] + acc_ref[...]


def _ffn_prompt(x1, o, w_co, norm_ffn, w_up, conv_w, conv_b, w_down, hist, *, tm, tf):
    B, T, d = x1.shape
    d_ff = w_down.shape[0]
    nj = d_ff // tf
    nt = T // tm
    idx3 = lambda b, t, j: (b, t, 0)
    c2 = lambda shp: pl.BlockSpec(shp, lambda b, t, j: (0, 0))
    return pl.pallas_call(
        functools.partial(_ffn_prompt_kernel, tm=tm, rs=min(512, tm)),
        grid=(B, nt, nj),
        in_specs=[pl.BlockSpec((1, tm, d), idx3), pl.BlockSpec((1, tm, d), idx3),
                  pl.BlockSpec((d, d), lambda b, t, j: (0, 0), pipeline_mode=pl.Buffered(1)), c2((1, d)),
                  pl.BlockSpec((d, tf), lambda b, t, j: (0, j)),
                  pl.BlockSpec((d, tf), lambda b, t, j: (0, nj + j)),
                  pl.BlockSpec((CONV_W, tf), lambda b, t, j: (0, j)),
                  pl.BlockSpec((CONV_W, tf), lambda b, t, j: (0, nj + j)),
                  pl.BlockSpec((1, tf), lambda b, t, j: (0, j)),
                  pl.BlockSpec((1, tf), lambda b, t, j: (0, nj + j)),
                  pl.BlockSpec((tf, d), lambda b, t, j: (j, 0)),
                  pl.BlockSpec((1, CONV_W - 1, tf), lambda b, t, j: (b, 0, j)),
                  pl.BlockSpec((1, CONV_W - 1, tf), lambda b, t, j: (b, 0, nj + j))],
        out_specs=[pl.BlockSpec((1, tm, d), idx3),
                   pl.BlockSpec((1, 1, CONV_W - 1, tf), lambda b, t, j: (b, t, 0, j)),
                   pl.BlockSpec((1, 1, CONV_W - 1, tf), lambda b, t, j: (b, t, 0, j))],
        out_shape=[jax.ShapeDtypeStruct((B, T, d), F32),
                   jax.ShapeDtypeStruct((B, nt, CONV_W - 1, d_ff), F32),
                   jax.ShapeDtypeStruct((B, nt, CONV_W - 1, d_ff), F32)],
        scratch_shapes=[pltpu.VMEM((tm, d), F32), pltpu.VMEM((tm, d), BF16), pltpu.VMEM((tm, d), F32),
                        pltpu.VMEM((nj, 2, 8, tf), F32)],
        compiler_params=_cparams(("arbitrary", "arbitrary", "arbitrary")),
        name="ffn_prompt",
    )(x1, o, w_co, norm_ffn, w_up, w_up, conv_w, conv_w, conv_b, conv_b, w_down, hist, hist)


def _ffn_sample_kernel(x1_ref, o_ref, wco_ref, nf_ref, wua_ref, wug_ref, cwa_ref, cwg_ref,
                       cba_ref, cbg_ref, wd_ref, h0a_ref, h0g_ref, h1a_ref, h1g_ref,
                       y_ref, ua_ref, ug_ref, x2_ref, hb_ref, acc_ref):
    j = pl.program_id(0)
    nj = pl.num_programs(0)

    @pl.when(j == 0)
    def _():
        _ffn_front(x1_ref, o_ref, wco_ref, nf_ref, x2_ref, hb_ref, acc_ref)

    hb = hb_ref[...]

    def conv_part(wu_ref, cw_ref, cb_ref, h0_ref, h1_ref, u_out_ref):
        u = _mm(hb, wu_ref[...])
        u_out_ref[...] = u
        return cb_ref[...] + h0_ref[...] * cw_ref[0:1, :] + h1_ref[...] * cw_ref[1:2, :] + u * cw_ref[2:3, :]

    a = conv_part(wua_ref, cwa_ref, cba_ref, h0a_ref, h1a_ref, ua_ref)
    g = conv_part(wug_ref, cwg_ref, cbg_ref, h0g_ref, h1g_ref, ug_ref)
    acc_ref[...] += _mm((_gelu_tanh(g) * a).astype(BF16), wd_ref[...])

    @pl.when(j == nj - 1)
    def _():
        y_ref[0] = x2_ref[...] + acc_ref[...]


def _ffn_sample(x1, o, w_co, norm_ffn, w_up, conv_w, conv_b, w_down, h0, h1, *, tf):
    _, rows, d = x1.shape
    d_ff = w_down.shape[0]
    nj = d_ff // tf
    c2 = lambda shp: pl.BlockSpec(shp, lambda j: (0, 0))
    c3 = lambda shp: pl.BlockSpec(shp, lambda j: (0, 0, 0))
    col_a = lambda r: pl.BlockSpec((r, tf), lambda j: (0, j))
    col_g = lambda r: pl.BlockSpec((r, tf), lambda j: (0, nj + j))
    return pl.pallas_call(
        _ffn_sample_kernel,
        grid=(nj,),
        in_specs=[c3((1, rows, d)), c3((1, rows, d)), c2((d, d)), c2((1, d)),
                  col_a(d), col_g(d), col_a(CONV_W), col_g(CONV_W), col_a(1), col_g(1),
                  pl.BlockSpec((tf, d), lambda j: (j, 0)),
                  col_a(rows), col_g(rows), col_a(rows), col_g(rows)],
        out_specs=[c3((1, rows, d)), col_a(rows), col_a(rows)],
        out_shape=[jax.ShapeDtypeStruct((1, rows, d), F32),
                   jax.ShapeDtypeStruct((rows, d_ff), F32),
                   jax.ShapeDtypeStruct((rows, d_ff), F32)],
        scratch_shapes=[pltpu.VMEM((rows, d), F32), pltpu.VMEM((rows, d), BF16), pltpu.VMEM((rows, d), F32)],
        compiler_params=_cparams(("arbitrary",)),
        name="ffn_sample",
    )(x1, o, w_co, norm_ffn, w_up, w_up, conv_w, conv_w, conv_b, conv_b, w_down, h0, h0, h1, h1)


def _rope_tables(pos, dh_a):
    assert dh_a == LANES and 2 * D_IDX == LANES and D_IDX & (D_IDX - 1) == 0
    posf = pos.astype(F32)[:, None]
    half_a = dh_a // 2
    inv_a = ROPE_THETA ** (-jnp.arange(half_a, dtype=F32) / half_a)
    ang_a = posf * inv_a[None, :]
    half_i = D_IDX // 2
    inv_i = ROPE_THETA ** (-jnp.arange(half_i, dtype=F32) / half_i)
    ang_i = posf * inv_i[None, :]
    cos_i, sin_i = jnp.cos(ang_i), jnp.sin(ang_i)
    return jnp.concatenate([jnp.cos(ang_a), jnp.sin(ang_a), cos_i, sin_i, cos_i, sin_i], axis=1)


def kernel(x_prompt, x_sample, mem_prompt, cache_k, cache_v, cache_idx_k, cache_mem_k, cache_mem_v,
           state_mlstm_c, state_mlstm_n, state_mlstm_m, state_conv, page_table,
           norm_mix, w_in, b_if, mlstm_norm, q_norm, k_norm, w_out, norm_cross, norm_mem,
           w_cq, w_ck, w_cv, w_co, cq_norm, ck_norm, norm_ffn, w_up, conv_w, conv_b, w_down):
    B, S, D = x_prompt.shape
    Bd, T, _ = x_sample.shape
    assert T == 1 and w_in.shape[0] == 1
    n_pool, page = cache_k.shape[1], cache_k.shape[2]
    n_pages = page_table.shape[1]
    past = n_pages * page
    mix_m = mlstm_norm.shape[1]
    dh_m = mix_m // H_M
    dh_a = q_norm.shape[1]
    mix_a = H_A * dh_a
    d_ff = w_down.shape[1]
    M = mem_prompt.shape[1]
    chunk = min(128, S)
    topk_p = min(TOPK_MAX, S // 4)
    topk_s = min(TOPK_MAX, (past + T) // 4)

    w = w_in[0]
    o_gate = 4 * mix_m
    o_aq = o_gate + 2 * H_M
    o_iq = o_aq + 3 * mix_a
    o_ik = o_iq + H_IDX * D_IDX
    o_iw = o_ik + D_IDX
    tail_pad = LANES - (D_IDX + H_IDX + 2 * H_M)
    w_r = (w[:, :o_gate].astype(BF16), w[:, o_aq:o_ik].astype(BF16),
           jnp.concatenate([w[:, o_ik:o_iw + H_IDX], w[:, o_gate:o_aq],
                            jnp.zeros((D, tail_pad), w.dtype)], axis=1).astype(BF16))
    bias_tail = jnp.concatenate([jnp.zeros((D_IDX + H_IDX,), F32), b_if[0].astype(F32),
                                 jnp.zeros((tail_pad,), F32)])[None, :]
    w_out_b = w_out[0].astype(BF16)
    w_cq_b, w_ck_b, w_cv_b, w_co_b = (a[0].astype(BF16) for a in (w_cq, w_ck, w_cv, w_co))
    w_up_b = w_up[0].astype(BF16)
    w_down_b = w_down[0].astype(BF16)
    row = lambda a: a[0][None, :]

    def split_misc(misc):
        ik = misc[:, :D_IDX]
        li = misc[:, D_IDX + H_IDX:D_IDX + H_IDX + H_M]
        lf = misc[:, D_IDX + H_IDX + H_M:D_IDX + H_IDX + 2 * H_M]
        return ik, li, lf

    tm_in = min(256, S)
    tab_p = _rope_tables(jnp.arange(S), dh_a)
    (mq, mk, mv, mo, aqb, ak, av, akb, avb, iqb, misc) = _in_proj(
        x_prompt.reshape(B * S, D), row(norm_mix), w_r, bias_tail, row(q_norm), row(k_norm), tab_p,
        tm=tm_in, tab_tiles=S // tm_in, mix_m=mix_m, mix_a=mix_a)
    ik_p, li_p, lf_p = split_misc(misc)
    r3 = lambda a: a.reshape(B, S, a.shape[-1])
    gcol = misc[:, D_IDX + H_IDX:D_IDX + H_IDX + 2 * H_M].reshape(B, S, 2 * H_M)
    grow = gcol.reshape(B, S // chunk, chunk, 2 * H_M).transpose(0, 1, 3, 2)
    y_m, c_p, n_p, m_p = _mlstm_prompt(r3(mq), r3(mk), r3(mv), r3(mo), grow, gcol, row(mlstm_norm), chunk=chunk)

    tq = min(256, S)
    wk = min(512, S)
    ikt = ik_p.astype(BF16).reshape(B, S // wk, wk, D_IDX).transpose(0, 1, 3, 2)
    y_a = _dsa_prompt(r3(iqb), r3(misc), ikt, r3(aqb), r3(akb), r3(avb), tq=tq, w=wk, topk=topk_p)

    mk_p, mv_p = _mem_kv(mem_prompt.reshape(B * M, D), row(norm_mem), w_ck_b, w_cv_b, row(ck_norm),
                         tm=min(256, B * M))
    x1, qc = _out_cq(x_prompt.reshape(B * S, D), y_m.reshape(B * S, mix_m), y_a.reshape(B * S, mix_a),
                     w_out_b, row(norm_cross), w_cq_b, row(cq_norm), tm=min(512, S))
    dh_c = D // H_C
    o_c = _cross(qc.reshape(B, S, D), mk_p.reshape(B, M, H_C, dh_c), mv_p.reshape(B, M, H_C, dh_c), tq=min(512, S))
    tf = d_ff // 2 if (d_ff // 2) % LANES == 0 else d_ff
    xp, conv_a, conv_g = _ffn_prompt(x1.reshape(B, S, D), o_c, w_co_b, row(norm_ffn), w_up_b, conv_w[0],
                                     conv_b[0][None, :], w_down_b,
                                     jnp.zeros((B, CONV_W - 1, 2 * d_ff), F32), tm=min(512, S), tf=tf)
    conv_p = jnp.concatenate([conv_a[:, -1], conv_g[:, -1]], axis=-1)

    tab_s = jnp.broadcast_to(_rope_tables(jnp.full((1,), past, jnp.int32), dh_a), (Bd, 2 * LANES))
    (mq_s, mk_s, mv_s, mo_s, aqb_s, ak_s, av_s, _, _, iqb_s, misc_s) = _in_proj(
        x_sample.reshape(Bd, D), row(norm_mix), w_r, bias_tail, row(q_norm), row(k_norm), tab_s,
        tm=Bd, tab_tiles=1, mix_m=mix_m, mix_a=mix_a)
    ik_s, li_s, lf_s = split_misc(misc_s)
    gs = jnp.concatenate([li_s, lf_s, state_mlstm_m[0].astype(F32)], axis=-1)[:, None, :]
    e1 = lambda a: a[:, None, :]
    y_ms, c_s, n_s, m_s = _mlstm_sample(e1(mq_s), e1(mk_s), e1(mv_s), e1(mo_s), gs,
                                        state_mlstm_c[0], state_mlstm_n[0].reshape(Bd, 1, mix_m),
                                        row(mlstm_norm))

    iq8 = jnp.pad(iqb_s.reshape(Bd, H_IDX, D_IDX), ((0, 0), (0, 8 - H_IDX), (0, 0)))
    w8 = jnp.pad(misc_s[:, D_IDX:D_IDX + H_IDX], ((0, 0), (0, 8 - H_IDX)))[:, :, None]
    assert n_pages <= 256
    rows_t, flags = _dsa_sample_select(page_table, iq8, w8, e1(ik_s), jnp.swapaxes(cache_idx_k[0], 1, 2),
                                       topk=topk_s, cw=min(512, past))
    y_as = _dsa_sample_attend(rows_t[:, :Bd].T, flags[:, 0], e1(aqb_s), ak_s, av_s,
                              cache_k[0].reshape(n_pool * page, H_A, dh_a),
                              cache_v[0].reshape(n_pool * page, H_A, dh_a), topk=topk_s)

    x1_s, qc_s = _out_cq(x_sample.reshape(Bd, D), y_ms.reshape(Bd, mix_m), y_as.reshape(Bd, mix_a),
                         w_out_b, row(norm_cross), w_cq_b, row(cq_norm), tm=Bd)
    o_s = _cross(qc_s.reshape(Bd, 1, D), cache_mem_k[0], cache_mem_v[0], tq=1)
    xs, u_a, u_g = _ffn_sample(x1_s.reshape(1, Bd, D), o_s.reshape(1, Bd, D), w_co_b, row(norm_ffn), w_up_b,
                               conv_w[0], conv_b[0][None, :], w_down_b,
                               state_conv[0, :, 0, :], state_conv[0, :, 1, :], tf=tf)
    conv_s = jnp.stack([state_conv[0, :, 1, :], jnp.concatenate([u_a, u_g], axis=-1)], axis=1)

    lead = lambda a: a[None]
    return (xp, xs.reshape(Bd, 1, D),
            lead(ak.reshape(B, S, H_A, dh_a)), lead(av.reshape(B, S, H_A, dh_a)), lead(ik_p.reshape(B, S, D_IDX)),
            lead(c_p), lead(n_p), lead(m_p[:, :, 0]),
            lead(mk_p.reshape(B, M, H_C, D // H_C)), lead(mv_p.reshape(B, M, H_C, D // H_C)), lead(conv_p),
            lead(ak_s.reshape(Bd, 1, H_A, dh_a)), lead(av_s.reshape(Bd, 1, H_A, dh_a)),
            lead(ik_s.reshape(Bd, 1, D_IDX)),
            lead(c_s), lead(n_s.reshape(Bd, H_M, dh_m)), lead(m_s[:, 0, :H_M]), lead(conv_s))
```

```python
import functools

import jax
import jax.numpy as jnp
import numpy as np
from jax import lax
from jax.experimental import pallas as pl
from jax.experimental.pallas import tpu as pltpu

F32 = jnp.float32
BF16 = jnp.bfloat16

H_M = 4
H_A = 4
H_IDX = 4
D_IDX = 64
H_C = 4
TOPK_MAX = 256
CONV_W = 3
ROPE_THETA = 10000.0
EPS = 1e-6
LOG2E = 1.4426950408889634
NEG_INF = float("-inf")
POS_INF = float("inf")

LANES = 128
VMEM_LIMIT = 56 * 1024 * 1024
N_BISECT = 20
N_BISECT16 = 10
N_BISECT32 = 10
BF16_GAP = 2.0 ** -7
TINY = 1e-30


def _cparams(sem):
    return pltpu.CompilerParams(dimension_semantics=sem, vmem_limit_bytes=VMEM_LIMIT)


def _nt(a, b):
    return lax.dot_general(a, b, (((1,), (1,)), ((), ())), preferred_element_type=F32)


def _tn(a, b):
    return lax.dot_general(a, b, (((0,), (0,)), ((), ())), preferred_element_type=F32)


def _mm(a, b):
    return jnp.dot(a, b, preferred_element_type=F32)


def _rms(x, g):
    ms = jnp.mean(x * x, axis=-1, keepdims=True)
    return x * lax.rsqrt(ms + EPS) * g


def _sigmoid(x):
    return 1.0 / (1.0 + jnp.exp(-x))


def _in_proj_kernel(x_ref, nm_ref, wm_ref, wa_ref, wt_ref, bias_ref, qn_ref, kn_ref, tab_ref,
                    mq_ref, mk_ref, mv_ref, mo_ref, aqb_ref, ak_ref, av_ref, akb_ref, avb_ref,
                    iqb_ref, misc_ref, *, mix_m, mix_a, dh_m, dh_a):
    h = _rms(x_ref[...], nm_ref[...]).astype(BF16)

    o_mq, o_mk, o_mv, o_mo = 0, mix_m, 2 * mix_m, 3 * mix_m
    n_m = 4 * mix_m
    o_aq = n_m
    o_ak = o_aq + mix_a
    o_av = o_ak + mix_a
    o_iq = o_av + mix_a
    o_tail = o_iq + H_IDX * D_IDX

    def proj(lo, width):
        if lo < n_m:
            return _mm(h, wm_ref[:, lo:lo + width])
        if lo < o_tail:
            return _mm(h, wa_ref[:, lo - n_m:lo - n_m + width])
        return _mm(h, wt_ref[...])

    mq_ref[...] = proj(o_mq, mix_m)
    mk_ref[...] = proj(o_mk, mix_m) * (dh_m ** -0.5)
    mv_ref[...] = proj(o_mv, mix_m)
    mo_ref[...] = proj(o_mo, mix_m)

    t_a = tab_ref[:, 0:LANES]
    t_i = tab_ref[:, LANES:2 * LANES]
    lane = lax.broadcasted_iota(jnp.int32, t_a.shape, 1)
    first_a = lane < dh_a // 2
    r_a = pltpu.roll(t_a, dh_a // 2, 1)
    cos_a = jnp.where(first_a, t_a, r_a)
    sin_a = jnp.where(first_a, -r_a, t_a)
    first_i = (lane & (D_IDX - 1)) < D_IDX // 2
    c_i = jnp.where(first_i, t_i, pltpu.roll(t_i, D_IDX // 2, 1))
    s1_i = jnp.where(first_i, 0.0, t_i)
    s2_i = jnp.where(first_i, -pltpu.roll(t_i, LANES - D_IDX // 2, 1), 0.0)
    in_key = lane < D_IDX
    c_t = jnp.where(in_key, c_i, 1.0)
    s1_t = jnp.where(in_key, s1_i, 0.0)
    s2_t = jnp.where(in_key, s2_i, 0.0)

    def norm_rope(z, g_ref):
        outs = []
        for hh in range(mix_a // dh_a):
            zh = _rms(z[:, hh * dh_a:(hh + 1) * dh_a], g_ref[...])
            outs.append(zh * cos_a + pltpu.roll(zh, dh_a // 2, 1) * sin_a)
        return outs

    aq = norm_rope(proj(o_aq, mix_a), qn_ref)
    aqb_ref[...] = jnp.concatenate(aq, axis=1).astype(BF16)
    ak = norm_rope(proj(o_ak, mix_a), kn_ref)
    av = proj(o_av, mix_a)
    for hh in range(mix_a // dh_a):
        ak_ref[:, hh, :] = ak[hh]
        av_ref[:, hh, :] = av[:, hh * dh_a:(hh + 1) * dh_a]
    akb_ref[...] = jnp.concatenate(ak, axis=1).astype(BF16)
    avb_ref[...] = av.astype(BF16)

    ziq = proj(o_iq, H_IDX * D_IDX)
    cols = []
    for c in range(H_IDX * D_IDX // LANES):
        zc = ziq[:, c * LANES:(c + 1) * LANES]
        cols.append(zc * c_i + pltpu.roll(zc, D_IDX // 2, 1) * s1_i
                    + pltpu.roll(zc, LANES - D_IDX // 2, 1) * s2_i)
    iqb_ref[...] = jnp.concatenate(cols, axis=1).astype(BF16)

    zt = proj(o_tail, LANES) + bias_ref[...]
    zt = zt * c_t + pltpu.roll(zt, D_IDX // 2, 1) * s1_t + pltpu.roll(zt, LANES - D_IDX // 2, 1) * s2_t
    f_lo = D_IDX + H_IDX + H_M
    log_sig = jnp.minimum(zt, 0.0) - jnp.log(1.0 + jnp.exp(-jnp.abs(zt)))
    misc_ref[...] = jnp.where((lane >= f_lo) & (lane < f_lo + H_M), log_sig, zt)


def _in_proj(x2d, norm_mix, w_parts, bias_tail, q_norm, k_norm, tab, *, tm, tab_tiles, mix_m, mix_a):
    rows, d = x2d.shape
    dh_m = mix_m // H_M
    dh_a = mix_a // H_A
    once = lambda a: pl.BlockSpec(a.shape, lambda i: (0, 0), pipeline_mode=pl.Buffered(1))
    grid = (rows // tm,)
    row_spec = lambda wdt: pl.BlockSpec((tm, wdt), lambda i: (i, 0))
    const = lambda shp: pl.BlockSpec(shp, lambda i: (0, 0))
    out_shapes = [
        jax.ShapeDtypeStruct((rows, mix_m), F32),
        jax.ShapeDtypeStruct((rows, mix_m), F32),
        jax.ShapeDtypeStruct((rows, mix_m), F32),
        jax.ShapeDtypeStruct((rows, mix_m), F32),
        jax.ShapeDtypeStruct((rows, mix_a), BF16),
        jax.ShapeDtypeStruct((rows, H_A, dh_a), F32),
        jax.ShapeDtypeStruct((rows, H_A, dh_a), F32),
        jax.ShapeDtypeStruct((rows, mix_a), BF16),
        jax.ShapeDtypeStruct((rows, mix_a), BF16),
        jax.ShapeDtypeStruct((rows, H_IDX * D_IDX), BF16),
        jax.ShapeDtypeStruct((rows, LANES), F32),
    ]
    head_spec = pl.BlockSpec((tm, H_A, dh_a), lambda i: (i, 0, 0))
    out_specs = ([row_spec(mix_m)] * 4 + [row_spec(mix_a), head_spec, head_spec, row_spec(mix_a), row_spec(mix_a)]
                 + [row_spec(H_IDX * D_IDX), row_spec(LANES)])
    return pl.pallas_call(
        functools.partial(_in_proj_kernel, mix_m=mix_m, mix_a=mix_a, dh_m=dh_m, dh_a=dh_a),
        grid=grid,
        in_specs=[row_spec(d), const((1, d)), once(w_parts[0]), once(w_parts[1]), once(w_parts[2]),
                  const((1, LANES)),
                  const((1, dh_a)), const((1, dh_a)),
                  pl.BlockSpec((tm, 2 * LANES), lambda i: (i % tab_tiles, 0))],
        out_specs=out_specs,
        out_shape=out_shapes,
        compiler_params=_cparams(("parallel",)),
        name="in_proj",
    )(x2d, norm_mix, *w_parts, bias_tail, q_norm, k_norm, tab)


def _mlstm_prompt_kernel(q_ref, k_ref, v_ref, o_ref, grow_ref, gcol_ref, gain_ref,
                         y_ref, c_ref, n_ref, m_ref, cs_ref, ns_ref, ms_ref, *, chunk, d, nbp):
    c_idx = pl.program_id(1)
    L = chunk
    row_i = lax.broadcasted_iota(jnp.int32, (L, L), 0)
    col_i = lax.broadcasted_iota(jnp.int32, (L, L), 1)
    tril = col_i <= row_i
    triu = row_i <= col_i

    @pl.when(c_idx == 0)
    def _():
        cs_ref[...] = jnp.zeros_like(cs_ref)
        ns_ref[...] = jnp.zeros_like(ns_ref)
        ms_ref[...] = jnp.zeros_like(ms_ref)

    chains = [(bi, hd) for bi in range(nbp) for hd in range(H_M)]
    tril_b = tril.astype(BF16)
    csum = []
    for bi in range(nbp):
        g = gcol_ref[bi]
        g_hi = g.astype(BF16)
        r1 = g - g_hi.astype(F32)
        g_mid = r1.astype(BF16)
        g_lo = (r1 - g_mid.astype(F32)).astype(BF16)
        csum.append(_mm(tril_b, g_hi) + _mm(tril_b, g_mid) + _mm(tril_b, g_lo))
    st = []
    for bi, hd in chains:
        gr = grow_ref[bi, 0]
        gc = gcol_ref[bi]
        sidx = bi * H_M + hd
        hs = slice(hd * d, (hd + 1) * d)
        m = ms_ref[sidx, 0:1, 0:1]
        li_r = gr[hd:hd + 1, :]
        li_c = gc[:, hd:hd + 1]
        lf_c = gc[:, H_M + hd:H_M + hd + 1]
        b_c = csum[bi][:, H_M + hd:H_M + hd + 1]
        b_r = jnp.sum(jnp.where(triu, lf_c, 0.0), axis=0, keepdims=True)
        logd = jnp.where(tril, b_c - b_r + li_r, NEG_INF)
        inter = b_c + m
        m_t = jnp.maximum(inter, jnp.max(logd, axis=1, keepdims=True))
        st.append(dict(bi=bi, sidx=sidx, hs=hs, m=m, li_c=li_c, b_c=b_c, inter=inter, m_t=m_t,
                       dmat=jnp.exp(logd - m_t)))
    for c in st:
        q = q_ref[c["bi"], :, c["hs"]]
        c["q"] = q
        c["kb"] = k_ref[c["bi"], :, c["hs"]].astype(BF16)
        qb = q.astype(BF16)
        c["C"] = cs_ref[c["sidx"]]
        c["n"] = ns_ref[c["sidx"], 0:1, :]
        c["s"] = _nt(qb, c["kb"]) * c["dmat"]
        c_aug = jnp.concatenate([c["C"], jnp.broadcast_to(c["n"], (d, d))], axis=0).astype(BF16)
        c["qc"] = _nt(qb, c_aug)
    ones_ld = jnp.ones((L, d), BF16)
    for c in st:
        v = v_ref[c["bi"], :, c["hs"]]
        c["v"] = v
        g_inter = jnp.exp(c["inter"] - c["m_t"])
        sv = _mm(c["s"].astype(BF16), jnp.concatenate([v.astype(BF16), ones_ld], axis=1))
        num = g_inter * c["qc"][:, :d] + sv[:, :d]
        den = g_inter * c["qc"][:, d:] + sv[:, d:]
        h = num / jnp.maximum(jnp.abs(den), jnp.exp(-c["m_t"]))
        o = o_ref[c["bi"], :, c["hs"]]
        y_ref[c["bi"], :, c["hs"]] = _sigmoid(o) * _rms(h, gain_ref[:, c["hs"]])
    for c in st:
        m_new = c["m_t"][L - 1:L, :]
        b_last = c["b_c"][L - 1:L, :]
        g_prev = jnp.exp(b_last + c["m"] - m_new)
        w_c = jnp.exp(b_last - c["b_c"] + c["li_c"] - m_new)
        k = k_ref[c["bi"], :, c["hs"]]
        cs_ref[c["sidx"]] = g_prev * c["C"] + _tn((c["v"] * w_c).astype(BF16), c["kb"])
        ns_ref[c["sidx"], 0:1, :] = g_prev * c["n"] + jnp.sum(k * w_c, axis=0, keepdims=True)
        ms_ref[c["sidx"], 0:1, :] = jnp.broadcast_to(m_new, (1, LANES))

    @pl.when(c_idx == pl.num_programs(1) - 1)
    def _():
        for bi in range(nbp):
            for hd in range(H_M):
                sidx = bi * H_M + hd
                c_ref[bi, hd] = cs_ref[sidx]
                n_ref[bi, hd:hd + 1, :] = ns_ref[sidx, 0:1, :]
                m_ref[bi, hd:hd + 1, :] = ms_ref[sidx, 0:1, :]


def _mlstm_prompt(mq, mk, mv, mo, grow, gcol, gain, *, chunk):
    B, S, mix_m = mq.shape
    d = mix_m // H_M
    n_chunks = S // chunk
    nbp = 4 if B % 4 == 0 else (2 if B % 2 == 0 else 1)
    seq = pl.BlockSpec((nbp, chunk, mix_m), lambda b, c: (b, c, 0))
    return pl.pallas_call(
        functools.partial(_mlstm_prompt_kernel, chunk=chunk, d=d, nbp=nbp),
        grid=(B // nbp, n_chunks),
        in_specs=[seq, seq, seq, seq,
                  pl.BlockSpec((nbp, 1, 2 * H_M, chunk), lambda b, c: (b, c, 0, 0)),
                  pl.BlockSpec((nbp, chunk, 2 * H_M), lambda b, c: (b, c, 0)),
                  pl.BlockSpec((1, mix_m), lambda b, c: (0, 0))],
        out_specs=[seq,
                   pl.BlockSpec((nbp, H_M, d, d), lambda b, c: (b, 0, 0, 0)),
                   pl.BlockSpec((nbp, H_M, d), lambda b, c: (b, 0, 0)),
                   pl.BlockSpec((nbp, H_M, LANES), lambda b, c: (b, 0, 0))],
        out_shape=[jax.ShapeDtypeStruct((B, S, mix_m), F32),
                   jax.ShapeDtypeStruct((B, H_M, d, d), F32),
                   jax.ShapeDtypeStruct((B, H_M, d), F32),
                   jax.ShapeDtypeStruct((B, H_M, LANES), F32)],
        scratch_shapes=[pltpu.VMEM((nbp * H_M, d, d), F32), pltpu.VMEM((nbp * H_M, 8, d), F32),
                        pltpu.VMEM((nbp * H_M, 8, LANES), F32)],
        compiler_params=_cparams(("parallel", "arbitrary")),
        name="mlstm_prompt",
    )(mq, mk, mv, mo, grow, gcol, gain)


def _mlstm_sample_kernel(q_ref, k_ref, v_ref, o_ref, gs_ref, c_ref, n_ref, gain_ref,
                         y_ref, co_ref, no_ref, mo_ref, *, d):
    gs = gs_ref[0]
    eye = (lax.broadcasted_iota(jnp.int32, (d, d), 0) == lax.broadcasted_iota(jnp.int32, (d, d), 1))
    lane = lax.broadcasted_iota(jnp.int32, (1, LANES), 1)
    m_out = jnp.zeros((1, LANES), F32)
    for h in range(H_M):
        sl = slice(h * d, (h + 1) * d)
        q = q_ref[0, :, sl]
        k = k_ref[0, :, sl]
        v = v_ref[0, :, sl]
        o = o_ref[0, :, sl]
        li = gs[:, h:h + 1]
        lf = gs[:, H_M + h:H_M + h + 1]
        m = gs[:, 2 * H_M + h:2 * H_M + h + 1]
        C = c_ref[0, h]
        n = n_ref[0, :, sl]
        inter = lf + m
        m_t = jnp.maximum(inter, li)
        s = jnp.sum(q * k, axis=1, keepdims=True) * jnp.exp(li - m_t)
        g = jnp.exp(inter - m_t)
        q8 = jnp.broadcast_to(q, (8, d)).astype(BF16)
        cq = _nt(q8, C.astype(BF16))[0:1, :]
        num = g * cq + s * v
        den = g * jnp.sum(n * q, axis=1, keepdims=True) + s
        hh = num / jnp.maximum(jnp.abs(den), jnp.exp(-m_t))
        w = jnp.exp(li - m_t)
        v_col = jnp.sum(jnp.where(eye, v, 0.0), axis=1, keepdims=True)
        co_ref[0, h] = g * C + (w * v_col) * k
        no_ref[0, :, sl] = g * n + w * k
        m_out = jnp.where(lane == h, m_t, m_out)
        y_ref[0, :, sl] = _sigmoid(o) * _rms(hh, gain_ref[:, sl])
    mo_ref[0] = m_out


def _mlstm_sample(mq, mk, mv, mo, gs, c_state, n_state, gain):
    Bd, _, mix_m = mq.shape
    d = mix_m // H_M
    row = pl.BlockSpec((1, 1, mix_m), lambda b: (b, 0, 0))
    return pl.pallas_call(
        functools.partial(_mlstm_sample_kernel, d=d),
        grid=(Bd,),
        in_specs=[row, row, row, row,
                  pl.BlockSpec((1, 1, 3 * H_M), lambda b: (b, 0, 0)),
                  pl.BlockSpec((1, H_M, d, d), lambda b: (b, 0, 0, 0)),
                  row,
                  pl.BlockSpec((1, mix_m), lambda b: (0, 0))],
        out_specs=[row,
                   pl.BlockSpec((1, H_M, d, d), lambda b: (b, 0, 0, 0)),
                   row,
                   pl.BlockSpec((1, 1, LANES), lambda b: (b, 0, 0))],
        out_shape=[jax.ShapeDtypeStruct((Bd, 1, mix_m), F32),
                   jax.ShapeDtypeStruct((Bd, H_M, d, d), F32),
                   jax.ShapeDtypeStruct((Bd, 1, mix_m), F32),
                   jax.ShapeDtypeStruct((Bd, 1, LANES), F32)],
        compiler_params=_cparams(("parallel",)),
        name="mlstm_sample",
    )(mq, mk, mv, mo, gs, c_state, n_state, gain)


def _dsa_prompt_kernel(iq_ref, misc_ref, ikt_ref, aq_ref, ak_ref, av_ref, ya_ref, sc_ref, acc_ref, sc16_ref,
                       *, tq, w, topk, dh, scale):
    i = pl.program_id(1)
    nk = ((i + 1) * tq + w - 1) // w
    kf = float(topk)
    nsub = w // LANES

    q_pos = i * tq + lax.broadcasted_iota(jnp.int32, (tq, 1), 0)
    lane_w = lax.broadcasted_iota(jnp.int32, (1, w), 1)
    iq = iq_ref[0]
    iq_h = [iq[:, h * D_IDX:(h + 1) * D_IDX] for h in range(H_IDX)]
    misc = misc_ref[0]
    w_h = [misc[:, D_IDX + h:D_IDX + h + 1] for h in range(H_IDX)]

    def score_body(c, carry, causal_edge):
        rmax, rmin = carry
        ikc = ikt_ref[0, c]
        score = jnp.zeros((tq, w), F32)
        for h in range(H_IDX):
            score = score + w_h[h] * jnp.maximum(_mm(iq_h[h], ikc), 0.0)
        if causal_edge:
            valid = (c * w + lane_w) <= q_pos
            s_hi = jnp.where(valid, score, NEG_INF)
            s_lo = jnp.where(valid, score, POS_INF)
        else:
            s_hi = s_lo = score
        sc_ref[c] = s_hi
        sc16_ref[c] = s_hi.astype(BF16)
        rmax = jnp.maximum(rmax, jnp.max(s_hi, axis=1, keepdims=True))
        rmin = jnp.minimum(rmin, jnp.min(s_lo, axis=1, keepdims=True))
        return rmax, rmin

    n_full = (i * tq) // w
    carry0 = (jnp.full((tq, 1), NEG_INF, F32), jnp.full((tq, 1), POS_INF, F32))
    carry0 = lax.fori_loop(0, n_full, functools.partial(score_body, causal_edge=False), carry0)
    rmax, rmin = lax.fori_loop(n_full, nk, functools.partial(score_body, causal_edge=True), carry0)

    ge = lambda x, t: x >= t
    gt = lambda x, t: x > t

    rh = min(tq, LANES)
    groups = [pl.ds(r0, rh) for r0 in range(0, tq, rh)]
    part = lambda a: [a[r0:r0 + rh] for r0 in range(0, tq, rh)]

    def pass_acc(rows, fn, init, combine):
        def body(c, acc):
            x = sc_ref[c, rows, :]
            for j in range(nsub):
                acc = combine(acc, fn(x[:, j * LANES:(j + 1) * LANES]))
            return acc
        return lax.fori_loop(0, nk, body, jnp.full((rh, LANES), init, F32))

    def count_acc(rows, pred, thr):
        thr_b = jnp.broadcast_to(thr, (rh, LANES))
        return pass_acc(rows, lambda x: jnp.where(pred(x, thr_b), 1.0, 0.0), 0.0, jnp.add)

    def count(rows, pred, thr):
        return jnp.sum(count_acc(rows, pred, thr), axis=1, keepdims=True)

    def min_where(rows, pred, thr):
        thr_b = jnp.broadcast_to(thr, (rh, LANES))
        acc = pass_acc(rows, lambda x: jnp.where(pred(x, thr_b), x, POS_INF), POS_INF, jnp.minimum)
        return jnp.min(acc, axis=1, keepdims=True)

    def bis_body(_, carry):
        los, his, clos = carry
        mids = [0.5 * (lo + hi) for lo, hi in zip(los, his)]
        accs = [count_acc(rows, ge, mid) for rows, mid in zip(groups, mids)]
        cms = [jnp.sum(a, axis=1, keepdims=True) for a in accs]
        oks = [cm >= kf for cm in cms]
        return (tuple(jnp.where(ok, mid, lo) for ok, mid, lo in zip(oks, mids, los)),
                tuple(jnp.where(ok, hi, mid) for ok, mid, hi in zip(oks, mids, his)),
                tuple(jnp.where(ok, cm, cl) for ok, cm, cl in zip(oks, cms, clos)))

    to16 = lambda a: a.astype(BF16).astype(F32)

    def count16(rows, thr):
        thr_b = jnp.broadcast_to(thr, (rh, LANES)).astype(BF16)
        one, zero = jnp.ones((), BF16), jnp.zeros((), BF16)

        def body(c, acc):
            x = sc16_ref[c, rows, :]
            for j in range(nsub):
                acc = acc + jnp.where(x[:, j * LANES:(j + 1) * LANES] >= thr_b, one, zero)
            return acc
        acc = lax.fori_loop(0, nk, body, jnp.zeros((rh, LANES), BF16))
        return jnp.sum(acc.astype(F32), axis=1, keepdims=True)

    def bis16_body(_, carry):
        los, his = carry
        mids = [to16(0.5 * (lo + hi)) for lo, hi in zip(los, his)]
        oks = [count16(rows, mid) >= kf for rows, mid in zip(groups, mids)]
        return (tuple(jnp.where(ok, mid, lo) for ok, mid, lo in zip(oks, mids, los)),
                tuple(jnp.where(ok, hi, mid) for ok, mid, hi in zip(oks, mids, his)))

    assert nsub * sc_ref.shape[0] <= 256
    los, his = lax.fori_loop(0, N_BISECT16, bis16_body,
                             (tuple(part(to16(rmin))), tuple(part(to16(rmax + jnp.abs(rmax) + 1.0)))))
    los = tuple(lo - (jnp.abs(lo) * BF16_GAP + TINY) for lo in los)
    los, _, clos = lax.fori_loop(0, N_BISECT32, bis_body,
                                 (los, his, tuple(part((q_pos + 1).astype(F32)))))

    def finish_rows(rows, qp, rmin_h, lo, c_lo):
        active = (qp + 1) > topk
        unresolved = jnp.max(jnp.where(active & (c_lo != kf), 1.0, 0.0)) > 0.5

        @pl.when(jnp.logical_not(unresolved))
        def _():
            thr = jnp.where(active, lo, rmin_h)

            def body(c, _):
                sc_ref[c, rows, :] = jnp.where(sc_ref[c, rows, :] >= thr, 0.0, NEG_INF)
                return 0
            lax.fori_loop(0, nk, body, 0)

        @pl.when(unresolved)
        def _():
            tau = min_where(rows, ge, lo)
            g = count(rows, gt, tau)

            def undone(tau, g):
                return active & (g >= kf)

            def fix_cond(st):
                return jnp.max(jnp.where(undone(*st), 1.0, 0.0)) > 0.5

            def fix_body(st):
                tau, g = st
                nd = undone(tau, g)
                tau2 = jnp.where(nd, min_where(rows, gt, tau), tau)
                return tau2, jnp.where(nd, count(rows, gt, tau2), g)

            tau, g = lax.while_loop(fix_cond, fix_body, (tau, g))
            tau_b = jnp.broadcast_to(jnp.where(active, tau, rmin_h), (rh, LANES))
            need_b = jnp.broadcast_to(jnp.where(active, kf - g, 1e9), (rh, LANES))
            r_i = lax.broadcasted_iota(jnp.int32, (LANES, 2 * LANES), 0)
            c_i = lax.broadcasted_iota(jnp.int32, (LANES, 2 * LANES), 1)
            tri_ones = ((r_i <= c_i) | (c_i >= LANES)).astype(BF16)

            def body(c, run):
                x = sc_ref[c, rows, :]
                outs = []
                for j in range(nsub):
                    xj = x[:, j * LANES:(j + 1) * LANES]
                    is_eq = xj == tau_b
                    cnt2 = _mm(jnp.where(is_eq, 1.0, 0.0).astype(BF16), tri_ones)
                    sel = (xj > tau_b) | (is_eq & (cnt2[:, :LANES] + run <= need_b))
                    outs.append(jnp.where(sel, 0.0, NEG_INF))
                    run = run + cnt2[:, LANES:]
                sc_ref[c, rows, :] = jnp.concatenate(outs, axis=1)
                return run
            lax.fori_loop(0, nk, body, jnp.zeros((rh, LANES), F32))

    for rows, qp, rmin_h, lo, c_lo in zip(groups, part(q_pos), part(rmin), los, clos):
        finish_rows(rows, qp, rmin_h, lo, c_lo)

    aq = aq_ref[0]
    q_heads = [aq[:, h * dh:(h + 1) * dh] for h in range(H_A)]
    acc_ref[...] = jnp.zeros_like(acc_ref)
    c2 = scale * LOG2E
    ones_blk = jnp.ones((w, dh), BF16)

    def att_body(c, ms):
        k0 = pl.multiple_of(c * w, w)
        bias = sc_ref[c]
        heads = [slice(h * dh, (h + 1) * dh) for h in range(H_A)]
        ss = [_nt(q_heads[h], ak_ref[0, pl.ds(k0, w), heads[h]]) + bias for h in range(H_A)]
        ms_new = [jnp.maximum(ms[h], jnp.max(ss[h], axis=1, keepdims=True)) for h in range(H_A)]
        m_safe = [jnp.where(m == NEG_INF, 0.0, m) for m in ms_new]
        ps = [jnp.exp2((ss[h] - m_safe[h]) * c2).astype(BF16) for h in range(H_A)]
        for h in range(H_A):
            alpha = jnp.exp2((ms[h] - m_safe[h]) * c2)
            v_aug = jnp.concatenate([av_ref[0, pl.ds(k0, w), heads[h]], ones_blk], axis=1)
            acc_ref[h] = alpha * acc_ref[h] + _mm(ps[h], v_aug)
        return tuple(ms_new)

    lax.fori_loop(0, nk, att_body, tuple(jnp.full((tq, 1), NEG_INF, F32) for _ in range(H_A)))
    for h in range(H_A):
        a = acc_ref[h]
        ya_ref[0, :, h * dh:(h + 1) * dh] = a[:, :dh] / a[:, dh:]


def _dsa_prompt(iqb, misc, ikt, aqb, akb, avb, *, tq, w, topk):
    B, S, mix_a = aqb.shape
    dh = mix_a // H_A
    nq = S // tq
    nw = S // w
    return pl.pallas_call(
        functools.partial(_dsa_prompt_kernel, tq=tq, w=w, topk=topk, dh=dh, scale=dh ** -0.5),
        grid=(B, nq),
        in_specs=[pl.BlockSpec((1, tq, H_IDX * D_IDX), lambda b, i: (b, i, 0)),
                  pl.BlockSpec((1, tq, LANES), lambda b, i: (b, i, 0)),
                  pl.BlockSpec((1, nw, D_IDX, w), lambda b, i: (b, 0, 0, 0)),
                  pl.BlockSpec((1, tq, mix_a), lambda b, i: (b, i, 0)),
                  pl.BlockSpec((1, S, mix_a), lambda b, i: (b, 0, 0)),
                  pl.BlockSpec((1, S, mix_a), lambda b, i: (b, 0, 0))],
        out_specs=pl.BlockSpec((1, tq, mix_a), lambda b, i: (b, i, 0)),
        out_shape=jax.ShapeDtypeStruct((B, S, mix_a), F32),
        scratch_shapes=[pltpu.VMEM((nw, tq, w), F32), pltpu.VMEM((H_A, tq, 2 * dh), F32),
                        pltpu.VMEM((nw, tq, w), BF16)],
        compiler_params=_cparams(("parallel", "arbitrary")),
        name="dsa_prompt",
    )(iqb, misc, ikt, aqb, akb, avb)


def _dsa_sample_select_kernel(pt_ref, iq_ref, w_ref, ikn_ref, ptv_ref, pool_ref, rows_ref, flag_ref,
                              ikbuf, sem, sc_ref, xn_ref, slot_ref, phys_ref,
                              *, n_pages, page, topk, cw):
    nb = iq_ref.shape[0]
    past = n_pages * page
    kf = float(topk)
    n_cw = past // cw

    def page_copy(bb, p, slot):
        return pltpu.make_async_copy(pool_ref.at[pt_ref[bb, p]],
                                     ikbuf.at[slot, :, pl.ds(p * page, page)],
                                     sem.at[slot])

    def start_all(bb, slot):
        def body(p, _):
            page_copy(bb, p, slot).start()
            return 0
        lax.fori_loop(0, n_pages, body, 0)

    start_all(0, 0)

    def score_body(b, _):
        slot = b % 2

        @pl.when(b + 1 < nb)
        def _():
            start_all(b + 1, 1 - slot)

        def wait_body(p, _):
            page_copy(b, p, slot).wait()
            return 0
        lax.fori_loop(0, n_pages, wait_body, 0)

        iq8 = iq_ref[b]
        w8 = w_ref[b]
        s8 = _mm(iq8, ikbuf[slot].astype(BF16))
        sc_ref[pl.ds(b, 1), :] = jnp.sum(w8 * jnp.maximum(s8, 0.0), axis=0, keepdims=True)
        ikn = ikn_ref[b].astype(BF16).astype(F32)
        sn8 = jnp.sum(iq8.astype(F32) * ikn, axis=1, keepdims=True)
        xn_b = jnp.sum(w8 * jnp.maximum(sn8, 0.0), axis=0, keepdims=True)
        xn_ref[pl.ds(b, 1), :] = jnp.broadcast_to(xn_b, (1, LANES))
        return 0

    lax.fori_loop(0, nb, score_body, 0)

    x = sc_ref[...]
    xn = xn_ref[:, 0:1]

    def cnt(mask_row, mask_new):
        return (jnp.sum(jnp.where(mask_row, 1.0, 0.0), axis=1, keepdims=True)
                + jnp.where(mask_new, 1.0, 0.0))

    rmax = jnp.maximum(jnp.max(x, axis=1, keepdims=True), xn)
    rmin = jnp.minimum(jnp.min(x, axis=1, keepdims=True), xn)
    hi0 = rmax + jnp.abs(rmax) + 1.0

    def bis_body(_, carry):
        lo, hi = carry
        mid = 0.5 * (lo + hi)
        ok = cnt(x >= mid, xn >= mid) >= kf
        return jnp.where(ok, mid, lo), jnp.where(ok, hi, mid)

    lo, _ = lax.fori_loop(0, N_BISECT, bis_body, (rmin, hi0))

    def min_where(mask_row, mask_new):
        return jnp.minimum(jnp.min(jnp.where(mask_row, x, POS_INF), axis=1, keepdims=True),
                           jnp.where(mask_new, xn, POS_INF))

    tau = min_where(x >= lo, xn >= lo)
    g = cnt(x > tau, xn > tau)

    def fix_cond(st):
        tau, g = st
        return jnp.max(jnp.where(g >= kf, 1.0, 0.0)) > 0.5

    def fix_body(st):
        tau, g = st
        tau2 = jnp.where(g >= kf, min_where(x > tau, xn > tau), tau)
        return tau2, cnt(x > tau2, xn > tau2)

    tau, g = lax.while_loop(fix_cond, fix_body, (tau, g))
    need = kf - g

    tri = (lax.broadcasted_iota(jnp.int32, (cw, cw), 0)
           < lax.broadcasted_iota(jnp.int32, (cw, cw), 1)).astype(BF16)

    def excl_prefix(flag):
        outs = []
        run = jnp.zeros((nb, 1), F32)
        for c in range(n_cw):
            f = flag[:, c * cw:(c + 1) * cw]
            outs.append(_mm(f.astype(BF16), tri) + run)
            run = run + jnp.sum(f, axis=1, keepdims=True)
        return jnp.concatenate(outs, axis=1), run

    is_eq = x == tau
    pre_eq, n_eq_past = excl_prefix(jnp.where(is_eq, 1.0, 0.0))
    sel = (x > tau) | (is_eq & (pre_eq < need))
    new_sel = (xn > tau) | ((xn == tau) & (n_eq_past < need))
    slot, _ = excl_prefix(jnp.where(sel, 1.0, 0.0))
    slot_ref[...] = jnp.where(sel, slot, -1.0)

    ptv = ptv_ref[...]
    jrow = lax.broadcasted_iota(jnp.int32, (1, past), 1)
    prow = lax.broadcasted_iota(jnp.int32, (n_pages, 1), 0)
    expand = ((jrow >= prow * page) & (jrow < (prow + 1) * page)).astype(BF16)
    digit_bits = 6
    pt_hi = _mm((ptv >> digit_bits).astype(F32).astype(BF16), expand)
    pt_lo = _mm((ptv & ((1 << digit_bits) - 1)).astype(F32).astype(BF16), expand)
    pidx = lax.broadcasted_iota(jnp.int32, (8, n_pages), 1).astype(F32).astype(BF16)
    pg = _mm(pidx, expand)[0:1, :]
    phys_ref[...] = (pt_hi * (1 << digit_bits) + pt_lo) * page + (jrow.astype(F32) - pg * page)

    slot_col = lax.broadcasted_iota(jnp.int32, (topk, 1), 0).astype(F32)
    lane_b = lax.broadcasted_iota(jnp.int32, (1, LANES), 1)

    def extract_body(b, out):
        srow = slot_ref[pl.ds(b, 1), :]
        frow = phys_ref[pl.ds(b, 1), :]
        acc = jnp.zeros((topk, LANES), F32)
        for c in range(past // LANES):
            cs = slice(c * LANES, (c + 1) * LANES)
            acc = acc + jnp.where(srow[:, cs] == slot_col, frow[:, cs], 0.0)
        return jnp.where(lane_b == b, jnp.sum(acc, axis=1, keepdims=True), out)

    out = lax.fori_loop(0, nb, extract_body, jnp.zeros((topk, LANES), F32))
    rows_ref[...] = out.astype(jnp.int32)
    flag_ref[...] = jnp.broadcast_to(jnp.where(new_sel, 1, 0), (nb, LANES)).astype(jnp.int32)


def _dsa_sample_select(page_table, iq8, w8, ik_new, pool_ik_t, *, topk, cw):
    Bd, n_pages = page_table.shape
    n_pool, d_idx, page = pool_ik_t.shape
    past = n_pages * page
    assert Bd <= LANES and n_pool <= 64 * 256
    full = lambda shp: pl.BlockSpec(shp, lambda i, pt: (0,) * len(shp))
    grid_spec = pltpu.PrefetchScalarGridSpec(
        num_scalar_prefetch=1,
        grid=(1,),
        in_specs=[full((Bd, 8, d_idx)), full((Bd, 8, 1)), full((Bd, 1, d_idx)), full((Bd, n_pages)),
                  pl.BlockSpec(memory_space=pl.ANY)],
        out_specs=[full((topk, LANES)), full((Bd, LANES))],
        scratch_shapes=[pltpu.VMEM((2, d_idx, past), F32),
                        pltpu.SemaphoreType.DMA((2,)),
                        pltpu.VMEM((Bd, past), F32),
                        pltpu.VMEM((Bd, LANES), F32),
                        pltpu.VMEM((Bd, past), F32),
                        pltpu.VMEM((Bd, past), F32)],
    )
    return pl.pallas_call(
        functools.partial(_dsa_sample_select_kernel, n_pages=n_pages, page=page, topk=topk, cw=cw),
        grid_spec=grid_spec,
        out_shape=[jax.ShapeDtypeStruct((topk, LANES), jnp.int32),
                   jax.ShapeDtypeStruct((Bd, LANES), jnp.int32)],
        compiler_params=_cparams(("arbitrary",)),
        name="dsa_sample_select",
    )(page_table, iq8, w8, ik_new, page_table, pool_ik_t)


def _dsa_sample_attend_kernel(rows_ref, flag_ref, aq_ref, knew_ref, vnew_ref, kpool_ref, vpool_ref,
                              ya_ref, kbuf, vbuf, sem, *, topk, dh, scale):
    b = pl.program_id(0)
    nb = pl.num_programs(0)

    def row_copies(bb, t, slot):
        r = rows_ref[bb, t]
        dst = pl.ds(t * H_A, H_A)
        return (pltpu.make_async_copy(kpool_ref.at[r], kbuf.at[slot, dst, :], sem.at[0, slot]),
                pltpu.make_async_copy(vpool_ref.at[r], vbuf.at[slot, dst, :], sem.at[1, slot]))

    def start_all(bb, slot):
        def body(t, _):
            ck, cv = row_copies(bb, t, slot)
            ck.start(priority=0)
            cv.start(priority=1)
            return 0
        lax.fori_loop(0, topk, body, 0, unroll=8)

    slot = b % 2

    @pl.when(b == 0)
    def _():
        start_all(0, 0)

    @pl.when(b + 1 < nb)
    def _():
        start_all(b + 1, 1 - slot)

    def wait_body(t, _):
        ck, cv = row_copies(b, t, slot)
        ck.wait()
        cv.wait()
        return 0
    lax.fori_loop(0, topk, wait_body, 0, unroll=8)

    take_new = (lax.broadcasted_iota(jnp.int32, (topk, 1), 0) == topk - 1) & (flag_ref[b] > 0)
    aq = aq_ref[0]
    for h in range(H_A):
        hs = slice(h * dh, (h + 1) * dh)
        kh = kbuf[slot, pl.ds(h, topk, stride=H_A), :]
        vh = vbuf[slot, pl.ds(h, topk, stride=H_A), :]
        kh = jnp.where(take_new, knew_ref[0, h:h + 1, :], kh).astype(BF16)
        vh = jnp.where(take_new, vnew_ref[0, h:h + 1, :], vh).astype(BF16)
        q8 = jnp.broadcast_to(aq[:, hs], (8, dh))
        s = _nt(q8, kh) * scale
        m = jnp.max(s, axis=1, keepdims=True)
        p = jnp.exp(s - m)
        p = p / jnp.sum(p, axis=1, keepdims=True)
        ya_ref[0, :, hs] = _mm(p.astype(BF16), vh)[0:1, :]


def _dsa_sample_attend(rows, flags, aqb, k_new, v_new, pool_k, pool_v, *, topk):
    Bd, _, mix_a = aqb.shape
    dh = pool_k.shape[2]
    new_spec = pl.BlockSpec((1, H_A, dh), lambda b, r, f: (b, 0, 0))
    grid_spec = pltpu.PrefetchScalarGridSpec(
        num_scalar_prefetch=2,
        grid=(Bd,),
        in_specs=[pl.BlockSpec((1, 1, mix_a), lambda b, r, f: (b, 0, 0)),
                  new_spec, new_spec,
                  pl.BlockSpec(memory_space=pl.ANY),
                  pl.BlockSpec(memory_space=pl.ANY)],
        out_specs=pl.BlockSpec((1, 1, mix_a), lambda b, r, f: (b, 0, 0)),
        scratch_shapes=[pltpu.VMEM((2, topk * H_A, dh), F32),
                        pltpu.VMEM((2, topk * H_A, dh), F32),
                        pltpu.SemaphoreType.DMA((2, 2))],
    )
    return pl.pallas_call(
        functools.partial(_dsa_sample_attend_kernel, topk=topk, dh=dh, scale=dh ** -0.5),
        grid_spec=grid_spec,
        out_shape=jax.ShapeDtypeStruct((Bd, 1, mix_a), F32),
        compiler_params=_cparams(("arbitrary",)),
        name="dsa_sample_attend",
    )(rows, flags, aqb, k_new, v_new, pool_k, pool_v)


def _mem_kv_kernel(mem_ref, nm_ref, wk_ref, wv_ref, kn_ref, k_ref, v_ref, *, dh):
    hm = _rms(mem_ref[...], nm_ref[...]).astype(BF16)
    kk = _mm(hm, wk_ref[...])
    vv = _mm(hm, wv_ref[...])
    for h in range(H_C):
        hs = slice(h * dh, (h + 1) * dh)
        k_ref[:, h, :] = _rms(kk[:, hs], kn_ref[...])
        v_ref[:, h, :] = vv[:, hs]


def _mem_kv(mem2d, norm_mem, w_ck, w_cv, ck_norm, *, tm):
    rows, d = mem2d.shape
    dh = d // H_C
    row = pl.BlockSpec((tm, d), lambda i: (i, 0))
    heads = pl.BlockSpec((tm, H_C, dh), lambda i: (i, 0, 0))
    const = lambda shp: pl.BlockSpec(shp, lambda i: (0, 0))
    return pl.pallas_call(
        functools.partial(_mem_kv_kernel, dh=dh),
        grid=(rows // tm,),
        in_specs=[row, const((1, d)), const((d, d)), const((d, d)), const((1, dh))],
        out_specs=[heads, heads],
        out_shape=[jax.ShapeDtypeStruct((rows, H_C, dh), F32)] * 2,
        compiler_params=_cparams(("parallel",)),
        name="mem_kv",
    )(mem2d, norm_mem, w_ck, w_cv, ck_norm)


def _out_cq_kernel(x_ref, ym_ref, ya_ref, wo_ref, nc_ref, wq_ref, qn_ref, x1_ref, qc_ref, *, mix_m, dh):
    upd = (_mm(ym_ref[...].astype(BF16), wo_ref[0:mix_m, :])
           + _mm(ya_ref[...].astype(BF16), wo_ref[mix_m:, :]))
    x1 = x_ref[...] + upd
    x1_ref[...] = x1
    hq = _mm(_rms(x1, nc_ref[...]).astype(BF16), wq_ref[...])
    for h in range(H_C):
        hs = slice(h * dh, (h + 1) * dh)
        qc_ref[:, hs] = _rms(hq[:, hs], qn_ref[...]).astype(BF16)


def _out_cq(x2d, ym, ya, w_out, norm_cross, w_cq, cq_norm, *, tm):
    rows, d = x2d.shape
    mix_m = ym.shape[1]
    mix_a = ya.shape[1]
    dh = d // H_C
    row = lambda wdt: pl.BlockSpec((tm, wdt), lambda i: (i, 0))
    const = lambda shp: pl.BlockSpec(shp, lambda i: (0, 0))
    return pl.pallas_call(
        functools.partial(_out_cq_kernel, mix_m=mix_m, dh=dh),
        grid=(rows // tm,),
        in_specs=[row(d), row(mix_m), row(mix_a), const((mix_m + mix_a, d)), const((1, d)),
                  const((d, d)), const((1, dh))],
        out_specs=[row(d), row(d)],
        out_shape=[jax.ShapeDtypeStruct((rows, d), F32), jax.ShapeDtypeStruct((rows, d), BF16)],
        compiler_params=_cparams(("parallel",)),
        name="out_cq",
    )(x2d, ym, ya, w_out, norm_cross, w_cq, cq_norm)


def _cross_kernel(q_ref, k_hbm, v_hbm, o_ref, kv_buf, sem, *, dh, scale):
    b = pl.program_id(0)
    t = pl.program_id(1)
    nb = pl.num_programs(0)
    slot = b % 2

    def head_copies(bb, sl):
        cps = []
        for h in range(H_C):
            cps.append(pltpu.make_async_copy(k_hbm.at[bb, :, h, :], kv_buf.at[sl, 0, h], sem.at[sl]))
            cps.append(pltpu.make_async_copy(v_hbm.at[bb, :, h, :], kv_buf.at[sl, 1, h], sem.at[sl]))
        return cps

    @pl.when(t == 0)
    def _():
        @pl.when(b == 0)
        def _():
            for cp in head_copies(0, 0):
                cp.start()

        @pl.when(b + 1 < nb)
        def _():
            for cp in head_copies(b + 1, 1 - slot):
                cp.start()

        for cp in head_copies(b, slot):
            cp.wait()

    q = q_ref[0]
    rows = q.shape[0]
    if rows < 8:
        q = jnp.broadcast_to(q, (8, q.shape[1]))
    for h in range(H_C):
        hs = slice(h * dh, (h + 1) * dh)
        kb = kv_buf[slot, 0, h].astype(BF16)
        vb = kv_buf[slot, 1, h].astype(BF16)
        s = _nt(q[:, hs], kb) * scale
        m = jnp.max(s, axis=1, keepdims=True)
        p = jnp.exp(s - m)
        p = p / jnp.sum(p, axis=1, keepdims=True)
        o = _mm(p.astype(BF16), vb)
        o_ref[0, :, hs] = o[0:rows].astype(BF16)


def _cross(qc, mem_k, mem_v, *, tq):
    B, T, d = qc.shape
    M = mem_k.shape[1]
    dh = d // H_C
    return pl.pallas_call(
        functools.partial(_cross_kernel, dh=dh, scale=dh ** -0.5),
        grid=(B, T // tq),
        in_specs=[pl.BlockSpec((1, tq, d), lambda b, t: (b, t, 0)),
                  pl.BlockSpec(memory_space=pl.ANY),
                  pl.BlockSpec(memory_space=pl.ANY)],
        out_specs=pl.BlockSpec((1, tq, d), lambda b, t: (b, t, 0)),
        out_shape=jax.ShapeDtypeStruct((B, T, d), BF16),
        scratch_shapes=[pltpu.VMEM((2, 2, H_C, M, dh), F32), pltpu.SemaphoreType.DMA((2,))],
        compiler_params=_cparams(("arbitrary", "arbitrary")),
        name="cross_attn",
    )(qc, mem_k, mem_v)


def _gelu_tanh(x):
    return 0.5 * x * (1.0 + jnp.tanh(np.sqrt(2.0 / np.pi) * (x + 0.044715 * (x * x * x))))


def _ffn_front(x1_ref, o_ref, wco_ref, nf_ref, x2_ref, hb_ref, acc_ref):
    x2 = x1_ref[0] + _mm(o_ref[0], wco_ref[...])
    x2_ref[...] = x2
    hb_ref[...] = _rms(x2, nf_ref[...]).astype(BF16)
    acc_ref[...] = jnp.zeros_like(acc_ref)


def _ffn_prompt_kernel(x1_ref, o_ref, wco_ref, nf_ref, wua_ref, wug_ref, cwa_ref, cwg_ref,
                       cba_ref, cbg_ref, wd_ref, ha_ref, hg_ref,
                       y_ref, ca_ref, cg_ref, x2_ref, hb_ref, acc_ref, carry_ref, *, tm, rs):
    t = pl.program_id(1)
    j = pl.program_id(2)
    nj = pl.num_programs(2)

    @pl.when(j == 0)
    def _():
        _ffn_front(x1_ref, o_ref, wco_ref, nf_ref, x2_ref, hb_ref, acc_ref)

    @pl.when(t == 0)
    def _():
        carry_ref[j, 0, 6:8, :] = ha_ref[0]
        carry_ref[j, 1, 6:8, :] = hg_ref[0]

    rid = lax.broadcasted_iota(jnp.int32, (rs, 1), 0)

    def conv_part(hb, part, wu_ref, cw_ref, cb_ref):
        u = _mm(hb, wu_ref[...])
        p2 = carry_ref[j, part, 6:7, :]
        p1 = carry_ref[j, part, 7:8, :]
        um1 = jnp.where(rid == 0, p1, pltpu.roll(u, 1, 0))
        um2 = jnp.where(rid == 0, p2, jnp.where(rid == 1, p1, pltpu.roll(u, 2, 0)))
        carry_ref[j, part] = u[rs - 8:rs, :]
        return cb_ref[...] + um2 * cw_ref[0:1, :] + um1 * cw_ref[1:2, :] + u * cw_ref[2:3, :]

    def sub_body(r, _):
        r0 = pl.multiple_of(r * rs, rs)
        hb = hb_ref[pl.ds(r0, rs), :]
        a = conv_part(hb, 0, wua_ref, cwa_ref, cba_ref)
        g = conv_part(hb, 1, wug_ref, cwg_ref, cbg_ref)
        acc_ref[pl.ds(r0, rs), :] += _mm((_gelu_tanh(g) * a).astype(BF16), wd_ref[...])
        return 0

    lax.fori_loop(0, tm // rs, sub_body, 0)
    ca_ref[0, 0] = carry_ref[j, 0, 6:8, :]
    cg_ref[0, 0] = carry_ref[j, 1, 6:8, :]

    @pl.when(j == nj - 1)
    def _():
        y_ref[0] = x2_ref[...] + acc_ref[...]


def _ffn_prompt(x1, o, w_co, norm_ffn, w_up, conv_w, conv_b, w_down, hist, *, tm, tf):
    B, T, d = x1.shape
    d_ff = w_down.shape[0]
    nj = d_ff // tf
    nt = T // tm
    idx3 = lambda b, t, j: (b, t, 0)
    c2 = lambda shp: pl.BlockSpec(shp, lambda b, t, j: (0, 0))
    return pl.pallas_call(
        functools.partial(_ffn_prompt_kernel, tm=tm, rs=min(512, tm)),
        grid=(B, nt, nj),
        in_specs=[pl.BlockSpec((1, tm, d), idx3), pl.BlockSpec((1, tm, d), idx3),
                  pl.BlockSpec((d, d), lambda b, t, j: (0, 0), pipeline_mode=pl.Buffered(1)), c2((1, d)),
                  pl.BlockSpec((d, tf), lambda b, t, j: (0, j)),
                  pl.BlockSpec((d, tf), lambda b, t, j: (0, nj + j)),
                  pl.BlockSpec((CONV_W, tf), lambda b, t, j: (0, j)),
                  pl.BlockSpec((CONV_W, tf), lambda b, t, j: (0, nj + j)),
                  pl.BlockSpec((1, tf), lambda b, t, j: (0, j)),
                  pl.BlockSpec((1, tf), lambda b, t, j: (0, nj + j)),
                  pl.BlockSpec((tf, d), lambda b, t, j: (j, 0)),
                  pl.BlockSpec((1, CONV_W - 1, tf), lambda b, t, j: (b, 0, j)),
                  pl.BlockSpec((1, CONV_W - 1, tf), lambda b, t, j: (b, 0, nj + j))],
        out_specs=[pl.BlockSpec((1, tm, d), idx3),
                   pl.BlockSpec((1, 1, CONV_W - 1, tf), lambda b, t, j: (b, t, 0, j)),
                   pl.BlockSpec((1, 1, CONV_W - 1, tf), lambda b, t, j: (b, t, 0, j))],
        out_shape=[jax.ShapeDtypeStruct((B, T, d), F32),
                   jax.ShapeDtypeStruct((B, nt, CONV_W - 1, d_ff), F32),
                   jax.ShapeDtypeStruct((B, nt, CONV_W - 1, d_ff), F32)],
        scratch_shapes=[pltpu.VMEM((tm, d), F32), pltpu.VMEM((tm, d), BF16), pltpu.VMEM((tm, d), F32),
                        pltpu.VMEM((nj, 2, 8, tf), F32)],
        compiler_params=_cparams(("arbitrary", "arbitrary", "arbitrary")),
        name="ffn_prompt",
    )(x1, o, w_co, norm_ffn, w_up, w_up, conv_w, conv_w, conv_b, conv_b, w_down, hist, hist)


def _ffn_sample_kernel(x1_ref, o_ref, wco_ref, nf_ref, wua_ref, wug_ref, cwa_ref, cwg_ref,
                       cba_ref, cbg_ref, wd_ref, h0a_ref, h0g_ref, h1a_ref, h1g_ref,
                       y_ref, ua_ref, ug_ref, x2_ref, hb_ref, acc_ref):
    j = pl.program_id(0)
    nj = pl.num_programs(0)

    @pl.when(j == 0)
    def _():
        _ffn_front(x1_ref, o_ref, wco_ref, nf_ref, x2_ref, hb_ref, acc_ref)

    hb = hb_ref[...]

    def conv_part(wu_ref, cw_ref, cb_ref, h0_ref, h1_ref, u_out_ref):
        u = _mm(hb, wu_ref[...])
        u_out_ref[...] = u
        return cb_ref[...] + h0_ref[...] * cw_ref[0:1, :] + h1_ref[...] * cw_ref[1:2, :] + u * cw_ref[2:3, :]

    a = conv_part(wua_ref, cwa_ref, cba_ref, h0a_ref, h1a_ref, ua_ref)
    g = conv_part(wug_ref, cwg_ref, cbg_ref, h0g_ref, h1g_ref, ug_ref)
    acc_ref[...] += _mm((_gelu_tanh(g) * a).astype(BF16), wd_ref[...])

    @pl.when(j == nj - 1)
    def _():
        y_ref[0] = x2_ref[...] + acc_ref[...]


def _ffn_sample(x1, o, w_co, norm_ffn, w_up, conv_w, conv_b, w_down, h0, h1, *, tf):
    _, rows, d = x1.shape
    d_ff = w_down.shape[0]
    nj = d_ff // tf
    c2 = lambda shp: pl.BlockSpec(shp, lambda j: (0, 0))
    c3 = lambda shp: pl.BlockSpec(shp, lambda j: (0, 0, 0))
    col_a = lambda r: pl.BlockSpec((r, tf), lambda j: (0, j))
    col_g = lambda r: pl.BlockSpec((r, tf), lambda j: (0, nj + j))
    return pl.pallas_call(
        _ffn_sample_kernel,
        grid=(nj,),
        in_specs=[c3((1, rows, d)), c3((1, rows, d)), c2((d, d)), c2((1, d)),
                  col_a(d), col_g(d), col_a(CONV_W), col_g(CONV_W), col_a(1), col_g(1),
                  pl.BlockSpec((tf, d), lambda j: (j, 0)),
                  col_a(rows), col_g(rows), col_a(rows), col_g(rows)],
        out_specs=[c3((1, rows, d)), col_a(rows), col_a(rows)],
        out_shape=[jax.ShapeDtypeStruct((1, rows, d), F32),
                   jax.ShapeDtypeStruct((rows, d_ff), F32),
                   jax.ShapeDtypeStruct((rows, d_ff), F32)],
        scratch_shapes=[pltpu.VMEM((rows, d), F32), pltpu.VMEM((rows, d), BF16), pltpu.VMEM((rows, d), F32)],
        compiler_params=_cparams(("arbitrary",)),
        name="ffn_sample",
    )(x1, o, w_co, norm_ffn, w_up, w_up, conv_w, conv_w, conv_b, conv_b, w_down, h0, h0, h1, h1)


def _rope_tables(pos, dh_a):
    assert dh_a == LANES and 2 * D_IDX == LANES and D_IDX & (D_IDX - 1) == 0
    posf = pos.astype(F32)[:, None]
    half_a = dh_a // 2
    inv_a = ROPE_THETA ** (-jnp.arange(half_a, dtype=F32) / half_a)
    ang_a = posf * inv_a[None, :]
    half_i = D_IDX // 2
    inv_i = ROPE_THETA ** (-jnp.arange(half_i, dtype=F32) / half_i)
    ang_i = posf * inv_i[None, :]
    cos_i, sin_i = jnp.cos(ang_i), jnp.sin(ang_i)
    return jnp.concatenate([jnp.cos(ang_a), jnp.sin(ang_a), cos_i, sin_i, cos_i, sin_i], axis=1)


def kernel(x_prompt, x_sample, mem_prompt, cache_k, cache_v, cache_idx_k, cache_mem_k, cache_mem_v,
           state_mlstm_c, state_mlstm_n, state_mlstm_m, state_conv, page_table,
           norm_mix, w_in, b_if, mlstm_norm, q_norm, k_norm, w_out, norm_cross, norm_mem,
           w_cq, w_ck, w_cv, w_co, cq_norm, ck_norm, norm_ffn, w_up, conv_w, conv_b, w_down):
    B, S, D = x_prompt.shape
    Bd, T, _ = x_sample.shape
    assert T == 1 and w_in.shape[0] == 1
    n_pool, page = cache_k.shape[1], cache_k.shape[2]
    n_pages = page_table.shape[1]
    past = n_pages * page
    mix_m = mlstm_norm.shape[1]
    dh_m = mix_m // H_M
    dh_a = q_norm.shape[1]
    mix_a = H_A * dh_a
    d_ff = w_down.shape[1]
    M = mem_prompt.shape[1]
    chunk = min(128, S)
    topk_p = min(TOPK_MAX, S // 4)
    topk_s = min(TOPK_MAX, (past + T) // 4)

    w = w_in[0]
    o_gate = 4 * mix_m
    o_aq = o_gate + 2 * H_M
    o_iq = o_aq + 3 * mix_a
    o_ik = o_iq + H_IDX * D_IDX
    o_iw = o_ik + D_IDX
    tail_pad = LANES - (D_IDX + H_IDX + 2 * H_M)
    w_r = (w[:, :o_gate].astype(BF16), w[:, o_aq:o_ik].astype(BF16),
           jnp.concatenate([w[:, o_ik:o_iw + H_IDX], w[:, o_gate:o_aq],
                            jnp.zeros((D, tail_pad), w.dtype)], axis=1).astype(BF16))
    bias_tail = jnp.concatenate([jnp.zeros((D_IDX + H_IDX,), F32), b_if[0].astype(F32),
                                 jnp.zeros((tail_pad,), F32)])[None, :]
    w_out_b = w_out[0].astype(BF16)
    w_cq_b, w_ck_b, w_cv_b, w_co_b = (a[0].astype(BF16) for a in (w_cq, w_ck, w_cv, w_co))
    w_up_b = w_up[0].astype(BF16)
    w_down_b = w_down[0].astype(BF16)
    row = lambda a: a[0][None, :]

    def split_misc(misc):
        ik = misc[:, :D_IDX]
        li = misc[:, D_IDX + H_IDX:D_IDX + H_IDX + H_M]
        lf = misc[:, D_IDX + H_IDX + H_M:D_IDX + H_IDX + 2 * H_M]
        return ik, li, lf

    tm_in = min(256, S)
    tab_p = _rope_tables(jnp.arange(S), dh_a)
    (mq, mk, mv, mo, aqb, ak, av, akb, avb, iqb, misc) = _in_proj(
        x_prompt.reshape(B * S, D), row(norm_mix), w_r, bias_tail, row(q_norm), row(k_norm), tab_p,
        tm=tm_in, tab_tiles=S // tm_in, mix_m=mix_m, mix_a=mix_a)
    ik_p, li_p, lf_p = split_misc(misc)
    r3 = lambda a: a.reshape(B, S, a.shape[-1])
    gcol = misc[:, D_IDX + H_IDX:D_IDX + H_IDX + 2 * H_M].reshape(B, S, 2 * H_M)
    grow = gcol.reshape(B, S // chunk, chunk, 2 * H_M).transpose(0, 1, 3, 2)
    y_m, c_p, n_p, m_p = _mlstm_prompt(r3(mq), r3(mk), r3(mv), r3(mo), grow, gcol, row(mlstm_norm), chunk=chunk)

    tq = min(256, S)
    wk = min(512, S)
    ikt = ik_p.astype(BF16).reshape(B, S // wk, wk, D_IDX).transpose(0, 1, 3, 2)
    y_a = _dsa_prompt(r3(iqb), r3(misc), ikt, r3(aqb), r3(akb), r3(avb), tq=tq, w=wk, topk=topk_p)

    mk_p, mv_p = _mem_kv(mem_prompt.reshape(B * M, D), row(norm_mem), w_ck_b, w_cv_b, row(ck_norm),
                         tm=min(256, B * M))
    x1, qc = _out_cq(x_prompt.reshape(B * S, D), y_m.reshape(B * S, mix_m), y_a.reshape(B * S, mix_a),
                     w_out_b, row(norm_cross), w_cq_b, row(cq_norm), tm=min(512, S))
    dh_c = D // H_C
    o_c = _cross(qc.reshape(B, S, D), mk_p.reshape(B, M, H_C, dh_c), mv_p.reshape(B, M, H_C, dh_c), tq=min(512, S))
    tf = d_ff // 2 if (d_ff // 2) % LANES == 0 else d_ff
    xp, conv_a, conv_g = _ffn_prompt(x1.reshape(B, S, D), o_c, w_co_b, row(norm_ffn), w_up_b, conv_w[0],
                                     conv_b[0][None, :], w_down_b,
                                     jnp.zeros((B, CONV_W - 1, 2 * d_ff), F32), tm=min(512, S), tf=tf)
    conv_p = jnp.concatenate([conv_a[:, -1], conv_g[:, -1]], axis=-1)

    tab_s = jnp.broadcast_to(_rope_tables(jnp.full((1,), past, jnp.int32), dh_a), (Bd, 2 * LANES))
    (mq_s, mk_s, mv_s, mo_s, aqb_s, ak_s, av_s, _, _, iqb_s, misc_s) = _in_proj(
        x_sample.reshape(Bd, D), row(norm_mix), w_r, bias_tail, row(q_norm), row(k_norm), tab_s,
        tm=Bd, tab_tiles=1, mix_m=mix_m, mix_a=mix_a)
    ik_s, li_s, lf_s = split_misc(misc_s)
    gs = jnp.concatenate([li_s, lf_s, state_mlstm_m[0].astype(F32)], axis=-1)[:, None, :]
    e1 = lambda a: a[:, None, :]
    y_ms, c_s, n_s, m_s = _mlstm_sample(e1(mq_s), e1(mk_s), e1(mv_s), e1(mo_s), gs,
                                        state_mlstm_c[0], state_mlstm_n[0].reshape(Bd, 1, mix_m),
                                        row(mlstm_norm))

    iq8 = jnp.pad(iqb_s.reshape(Bd, H_IDX, D_IDX), ((0, 0), (0, 8 - H_IDX), (0, 0)))
    w8 = jnp.pad(misc_s[:, D_IDX:D_IDX + H_IDX], ((0, 0), (0, 8 - H_IDX)))[:, :, None]
    assert n_pages <= 256
    rows_t, flags = _dsa_sample_select(page_table, iq8, w8, e1(ik_s), jnp.swapaxes(cache_idx_k[0], 1, 2),
                                       topk=topk_s, cw=min(512, past))
    y_as = _dsa_sample_attend(rows_t[:, :Bd].T, flags[:, 0], e1(aqb_s), ak_s, av_s,
                              cache_k[0].reshape(n_pool * page, H_A, dh_a),
                              cache_v[0].reshape(n_pool * page, H_A, dh_a), topk=topk_s)

    x1_s, qc_s = _out_cq(x_sample.reshape(Bd, D), y_ms.reshape(Bd, mix_m), y_as.reshape(Bd, mix_a),
                         w_out_b, row(norm_cross), w_cq_b, row(cq_norm), tm=Bd)
    o_s = _cross(qc_s.reshape(Bd, 1, D), cache_mem_k[0], cache_mem_v[0], tq=1)
    xs, u_a, u_g = _ffn_sample(x1_s.reshape(1, Bd, D), o_s.reshape(1, Bd, D), w_co_b, row(norm_ffn), w_up_b,
                               conv_w[0], conv_b[0][None, :], w_down_b,
                               state_conv[0, :, 0, :], state_conv[0, :, 1, :], tf=tf)
    conv_s = jnp.stack([state_conv[0, :, 1, :], jnp.concatenate([u_a, u_g], axis=-1)], axis=1)

    lead = lambda a: a[None]
    return (xp, xs.reshape(Bd, 1, D),
            lead(ak.reshape(B, S, H_A, dh_a)), lead(av.reshape(B, S, H_A, dh_a)), lead(ik_p.reshape(B, S, D_IDX)),
            lead(c_p), lead(n_p), lead(m_p[:, :, 0]),
            lead(mk_p.reshape(B, M, H_C, D // H_C)), lead(mv_p.reshape(B, M, H_C, D // H_C)), lead(conv_p),
            lead(ak_s.reshape(Bd, 1, H_A, dh_a)), lead(av_s.reshape(Bd, 1, H_A, dh_a)),
            lead(ik_s.reshape(Bd, 1, D_IDX)),
            lead(c_s), lead(n_s.reshape(Bd, H_M, dh_m)), lead(m_s[:, 0, :H_M]), lead(conv_s))
```

```python
import functools

import jax
import jax.numpy as jnp
import numpy as np
from jax import lax
from jax.experimental import pallas as pl
from jax.experimental.pallas import tpu as pltpu

F32 = jnp.float32
BF16 = jnp.bfloat16

H_M = 4
H_A = 4
H_IDX = 4
D_IDX = 64
H_C = 4
TOPK_MAX = 256
CONV_W = 3
ROPE_THETA = 10000.0
EPS = 1e-6
LOG2E = 1.4426950408889634
NEG_INF = float("-inf")
POS_INF = float("inf")

LANES = 128
VMEM_LIMIT = 56 * 1024 * 1024
N_BISECT = 20


def _cparams(sem):
    return pltpu.CompilerParams(dimension_semantics=sem, vmem_limit_bytes=VMEM_LIMIT)


def _nt(a, b):
    return lax.dot_general(a, b, (((1,), (1,)), ((), ())), preferred_element_type=F32)


def _tn(a, b):
    return lax.dot_general(a, b, (((0,), (0,)), ((), ())), preferred_element_type=F32)


def _mm(a, b):
    return jnp.dot(a, b, preferred_element_type=F32)


def _rms(x, g):
    ms = jnp.mean(x * x, axis=-1, keepdims=True)
    return x * lax.rsqrt(ms + EPS) * g


def _sigmoid(x):
    return 1.0 / (1.0 + jnp.exp(-x))


def _in_proj_kernel(x_ref, nm_ref, wm_ref, wa_ref, wt_ref, bias_ref, qn_ref, kn_ref, tab_ref,
                    mq_ref, mk_ref, mv_ref, mo_ref, aqb_ref, ak_ref, av_ref, akb_ref, avb_ref,
                    iqb_ref, misc_ref, *, mix_m, mix_a, dh_m, dh_a):
    h = _rms(x_ref[...], nm_ref[...]).astype(BF16)

    o_mq, o_mk, o_mv, o_mo = 0, mix_m, 2 * mix_m, 3 * mix_m
    n_m = 4 * mix_m
    o_aq = n_m
    o_ak = o_aq + mix_a
    o_av = o_ak + mix_a
    o_iq = o_av + mix_a
    o_tail = o_iq + H_IDX * D_IDX

    def proj(lo, width):
        if lo < n_m:
            return _mm(h, wm_ref[:, lo:lo + width])
        if lo < o_tail:
            return _mm(h, wa_ref[:, lo - n_m:lo - n_m + width])
        return _mm(h, wt_ref[...])

    mq_ref[...] = proj(o_mq, mix_m)
    mk_ref[...] = proj(o_mk, mix_m) * (dh_m ** -0.5)
    mv_ref[...] = proj(o_mv, mix_m)
    mo_ref[...] = proj(o_mo, mix_m)

    t_a = tab_ref[:, 0:LANES]
    t_i = tab_ref[:, LANES:2 * LANES]
    lane = lax.broadcasted_iota(jnp.int32, t_a.shape, 1)
    first_a = lane < dh_a // 2
    r_a = pltpu.roll(t_a, dh_a // 2, 1)
    cos_a = jnp.where(first_a, t_a, r_a)
    sin_a = jnp.where(first_a, -r_a, t_a)
    first_i = (lane & (D_IDX - 1)) < D_IDX // 2
    c_i = jnp.where(first_i, t_i, pltpu.roll(t_i, D_IDX // 2, 1))
    s1_i = jnp.where(first_i, 0.0, t_i)
    s2_i = jnp.where(first_i, -pltpu.roll(t_i, LANES - D_IDX // 2, 1), 0.0)
    in_key = lane < D_IDX
    c_t = jnp.where(in_key, c_i, 1.0)
    s1_t = jnp.where(in_key, s1_i, 0.0)
    s2_t = jnp.where(in_key, s2_i, 0.0)

    def norm_rope(z, g_ref):
        outs = []
        for hh in range(mix_a // dh_a):
            zh = _rms(z[:, hh * dh_a:(hh + 1) * dh_a], g_ref[...])
            outs.append(zh * cos_a + pltpu.roll(zh, dh_a // 2, 1) * sin_a)
        return outs

    aq = norm_rope(proj(o_aq, mix_a), qn_ref)
    aqb_ref[...] = jnp.concatenate(aq, axis=1).astype(BF16)
    ak = norm_rope(proj(o_ak, mix_a), kn_ref)
    av = proj(o_av, mix_a)
    for hh in range(mix_a // dh_a):
        ak_ref[:, hh, :] = ak[hh]
        av_ref[:, hh, :] = av[:, hh * dh_a:(hh + 1) * dh_a]
    akb_ref[...] = jnp.concatenate(ak, axis=1).astype(BF16)
    avb_ref[...] = av.astype(BF16)

    ziq = proj(o_iq, H_IDX * D_IDX)
    cols = []
    for c in range(H_IDX * D_IDX // LANES):
        zc = ziq[:, c * LANES:(c + 1) * LANES]
        cols.append(zc * c_i + pltpu.roll(zc, D_IDX // 2, 1) * s1_i
                    + pltpu.roll(zc, LANES - D_IDX // 2, 1) * s2_i)
    iqb_ref[...] = jnp.concatenate(cols, axis=1).astype(BF16)

    zt = proj(o_tail, LANES) + bias_ref[...]
    zt = zt * c_t + pltpu.roll(zt, D_IDX // 2, 1) * s1_t + pltpu.roll(zt, LANES - D_IDX // 2, 1) * s2_t
    f_lo = D_IDX + H_IDX + H_M
    log_sig = jnp.minimum(zt, 0.0) - jnp.log(1.0 + jnp.exp(-jnp.abs(zt)))
    misc_ref[...] = jnp.where((lane >= f_lo) & (lane < f_lo + H_M), log_sig, zt)


def _in_proj(x2d, norm_mix, w_parts, bias_tail, q_norm, k_norm, tab, *, tm, tab_tiles, mix_m, mix_a):
    rows, d = x2d.shape
    dh_m = mix_m // H_M
    dh_a = mix_a // H_A
    once = lambda a: pl.BlockSpec(a.shape, lambda i: (0, 0), pipeline_mode=pl.Buffered(1))
    grid = (rows // tm,)
    row_spec = lambda wdt: pl.BlockSpec((tm, wdt), lambda i: (i, 0))
    const = lambda shp: pl.BlockSpec(shp, lambda i: (0, 0))
    out_shapes = [
        jax.ShapeDtypeStruct((rows, mix_m), F32),
        jax.ShapeDtypeStruct((rows, mix_m), F32),
        jax.ShapeDtypeStruct((rows, mix_m), F32),
        jax.ShapeDtypeStruct((rows, mix_m), F32),
        jax.ShapeDtypeStruct((rows, mix_a), BF16),
        jax.ShapeDtypeStruct((rows, H_A, dh_a), F32),
        jax.ShapeDtypeStruct((rows, H_A, dh_a), F32),
        jax.ShapeDtypeStruct((rows, mix_a), BF16),
        jax.ShapeDtypeStruct((rows, mix_a), BF16),
        jax.ShapeDtypeStruct((rows, H_IDX * D_IDX), BF16),
        jax.ShapeDtypeStruct((rows, LANES), F32),
    ]
    head_spec = pl.BlockSpec((tm, H_A, dh_a), lambda i: (i, 0, 0))
    out_specs = ([row_spec(mix_m)] * 4 + [row_spec(mix_a), head_spec, head_spec, row_spec(mix_a), row_spec(mix_a)]
                 + [row_spec(H_IDX * D_IDX), row_spec(LANES)])
    return pl.pallas_call(
        functools.partial(_in_proj_kernel, mix_m=mix_m, mix_a=mix_a, dh_m=dh_m, dh_a=dh_a),
        grid=grid,
        in_specs=[row_spec(d), const((1, d)), once(w_parts[0]), once(w_parts[1]), once(w_parts[2]),
                  const((1, LANES)),
                  const((1, dh_a)), const((1, dh_a)),
                  pl.BlockSpec((tm, 2 * LANES), lambda i: (i % tab_tiles, 0))],
        out_specs=out_specs,
        out_shape=out_shapes,
        compiler_params=_cparams(("parallel",)),
        name="in_proj",
    )(x2d, norm_mix, *w_parts, bias_tail, q_norm, k_norm, tab)


def _mlstm_prompt_kernel(q_ref, k_ref, v_ref, o_ref, grow_ref, gcol_ref, gain_ref,
                         y_ref, c_ref, n_ref, m_ref, cs_ref, ns_ref, ms_ref, *, chunk, d, nbp):
    c_idx = pl.program_id(1)
    L = chunk
    row_i = lax.broadcasted_iota(jnp.int32, (L, L), 0)
    col_i = lax.broadcasted_iota(jnp.int32, (L, L), 1)
    tril = col_i <= row_i
    triu = row_i <= col_i

    @pl.when(c_idx == 0)
    def _():
        cs_ref[...] = jnp.zeros_like(cs_ref)
        ns_ref[...] = jnp.zeros_like(ns_ref)
        ms_ref[...] = jnp.zeros_like(ms_ref)

    chains = [(bi, hd) for bi in range(nbp) for hd in range(H_M)]
    tril_b = tril.astype(BF16)
    csum = []
    for bi in range(nbp):
        g = gcol_ref[bi]
        g_hi = g.astype(BF16)
        r1 = g - g_hi.astype(F32)
        g_mid = r1.astype(BF16)
        g_lo = (r1 - g_mid.astype(F32)).astype(BF16)
        csum.append(_mm(tril_b, g_hi) + _mm(tril_b, g_mid) + _mm(tril_b, g_lo))
    st = []
    for bi, hd in chains:
        gr = grow_ref[bi, 0]
        gc = gcol_ref[bi]
        sidx = bi * H_M + hd
        hs = slice(hd * d, (hd + 1) * d)
        m = ms_ref[sidx, 0:1, 0:1]
        li_r = gr[hd:hd + 1, :]
        li_c = gc[:, hd:hd + 1]
        lf_c = gc[:, H_M + hd:H_M + hd + 1]
        b_c = csum[bi][:, H_M + hd:H_M + hd + 1]
        b_r = jnp.sum(jnp.where(triu, lf_c, 0.0), axis=0, keepdims=True)
        logd = jnp.where(tril, b_c - b_r + li_r, NEG_INF)
        inter = b_c + m
        m_t = jnp.maximum(inter, jnp.max(logd, axis=1, keepdims=True))
        st.append(dict(bi=bi, sidx=sidx, hs=hs, m=m, li_c=li_c, b_c=b_c, inter=inter, m_t=m_t,
                       dmat=jnp.exp(logd - m_t)))
    for c in st:
        q = q_ref[c["bi"], :, c["hs"]]
        c["q"] = q
        c["kb"] = k_ref[c["bi"], :, c["hs"]].astype(BF16)
        qb = q.astype(BF16)
        c["C"] = cs_ref[c["sidx"]]
        c["n"] = ns_ref[c["sidx"], 0:1, :]
        c["s"] = _nt(qb, c["kb"]) * c["dmat"]
        c_aug = jnp.concatenate([c["C"], jnp.broadcast_to(c["n"], (d, d))], axis=0).astype(BF16)
        c["qc"] = _nt(qb, c_aug)
    ones_ld = jnp.ones((L, d), BF16)
    for c in st:
        v = v_ref[c["bi"], :, c["hs"]]
        c["v"] = v
        g_inter = jnp.exp(c["inter"] - c["m_t"])
        sv = _mm(c["s"].astype(BF16), jnp.concatenate([v.astype(BF16), ones_ld], axis=1))
        num = g_inter * c["qc"][:, :d] + sv[:, :d]
        den = g_inter * c["qc"][:, d:] + sv[:, d:]
        h = num / jnp.maximum(jnp.abs(den), jnp.exp(-c["m_t"]))
        o = o_ref[c["bi"], :, c["hs"]]
        y_ref[c["bi"], :, c["hs"]] = _sigmoid(o) * _rms(h, gain_ref[:, c["hs"]])
    for c in st:
        m_new = c["m_t"][L - 1:L, :]
        b_last = c["b_c"][L - 1:L, :]
        g_prev = jnp.exp(b_last + c["m"] - m_new)
        w_c = jnp.exp(b_last - c["b_c"] + c["li_c"] - m_new)
        k = k_ref[c["bi"], :, c["hs"]]
        cs_ref[c["sidx"]] = g_prev * c["C"] + _tn((c["v"] * w_c).astype(BF16), c["kb"])
        ns_ref[c["sidx"], 0:1, :] = g_prev * c["n"] + jnp.sum(k * w_c, axis=0, keepdims=True)
        ms_ref[c["sidx"], 0:1, :] = jnp.broadcast_to(m_new, (1, LANES))

    @pl.when(c_idx == pl.num_programs(1) - 1)
    def _():
        for bi in range(nbp):
            for hd in range(H_M):
                sidx = bi * H_M + hd
                c_ref[bi, hd] = cs_ref[sidx]
                n_ref[bi, hd:hd + 1, :] = ns_ref[sidx, 0:1, :]
                m_ref[bi, hd:hd + 1, :] = ms_ref[sidx, 0:1, :]


def _mlstm_prompt(mq, mk, mv, mo, grow, gcol, gain, *, chunk):
    B, S, mix_m = mq.shape
    d = mix_m // H_M
    n_chunks = S // chunk
    nbp = 4 if B % 4 == 0 else (2 if B % 2 == 0 else 1)
    seq = pl.BlockSpec((nbp, chunk, mix_m), lambda b, c: (b, c, 0))
    return pl.pallas_call(
        functools.partial(_mlstm_prompt_kernel, chunk=chunk, d=d, nbp=nbp),
        grid=(B // nbp, n_chunks),
        in_specs=[seq, seq, seq, seq,
                  pl.BlockSpec((nbp, 1, 2 * H_M, chunk), lambda b, c: (b, c, 0, 0)),
                  pl.BlockSpec((nbp, chunk, 2 * H_M), lambda b, c: (b, c, 0)),
                  pl.BlockSpec((1, mix_m), lambda b, c: (0, 0))],
        out_specs=[seq,
                   pl.BlockSpec((nbp, H_M, d, d), lambda b, c: (b, 0, 0, 0)),
                   pl.BlockSpec((nbp, H_M, d), lambda b, c: (b, 0, 0)),
                   pl.BlockSpec((nbp, H_M, LANES), lambda b, c: (b, 0, 0))],
        out_shape=[jax.ShapeDtypeStruct((B, S, mix_m), F32),
                   jax.ShapeDtypeStruct((B, H_M, d, d), F32),
                   jax.ShapeDtypeStruct((B, H_M, d), F32),
                   jax.ShapeDtypeStruct((B, H_M, LANES), F32)],
        scratch_shapes=[pltpu.VMEM((nbp * H_M, d, d), F32), pltpu.VMEM((nbp * H_M, 8, d), F32),
                        pltpu.VMEM((nbp * H_M, 8, LANES), F32)],
        compiler_params=_cparams(("parallel", "arbitrary")),
        name="mlstm_prompt",
    )(mq, mk, mv, mo, grow, gcol, gain)


def _mlstm_sample_kernel(q_ref, k_ref, v_ref, o_ref, gs_ref, c_ref, n_ref, gain_ref,
                         y_ref, co_ref, no_ref, mo_ref, *, d):
    gs = gs_ref[0]
    eye = (lax.broadcasted_iota(jnp.int32, (d, d), 0) == lax.broadcasted_iota(jnp.int32, (d, d), 1))
    lane = lax.broadcasted_iota(jnp.int32, (1, LANES), 1)
    m_out = jnp.zeros((1, LANES), F32)
    for h in range(H_M):
        sl = slice(h * d, (h + 1) * d)
        q = q_ref[0, :, sl]
        k = k_ref[0, :, sl]
        v = v_ref[0, :, sl]
        o = o_ref[0, :, sl]
        li = gs[:, h:h + 1]
        lf = gs[:, H_M + h:H_M + h + 1]
        m = gs[:, 2 * H_M + h:2 * H_M + h + 1]
        C = c_ref[0, h]
        n = n_ref[0, :, sl]
        inter = lf + m
        m_t = jnp.maximum(inter, li)
        s = jnp.sum(q * k, axis=1, keepdims=True) * jnp.exp(li - m_t)
        g = jnp.exp(inter - m_t)
        q8 = jnp.broadcast_to(q, (8, d)).astype(BF16)
        cq = _nt(q8, C.astype(BF16))[0:1, :]
        num = g * cq + s * v
        den = g * jnp.sum(n * q, axis=1, keepdims=True) + s
        hh = num / jnp.maximum(jnp.abs(den), jnp.exp(-m_t))
        w = jnp.exp(li - m_t)
        v_col = jnp.sum(jnp.where(eye, v, 0.0), axis=1, keepdims=True)
        co_ref[0, h] = g * C + (w * v_col) * k
        no_ref[0, :, sl] = g * n + w * k
        m_out = jnp.where(lane == h, m_t, m_out)
        y_ref[0, :, sl] = _sigmoid(o) * _rms(hh, gain_ref[:, sl])
    mo_ref[0] = m_out


def _mlstm_sample(mq, mk, mv, mo, gs, c_state, n_state, gain):
    Bd, _, mix_m = mq.shape
    d = mix_m // H_M
    row = pl.BlockSpec((1, 1, mix_m), lambda b: (b, 0, 0))
    return pl.pallas_call(
        functools.partial(_mlstm_sample_kernel, d=d),
        grid=(Bd,),
        in_specs=[row, row, row, row,
                  pl.BlockSpec((1, 1, 3 * H_M), lambda b: (b, 0, 0)),
                  pl.BlockSpec((1, H_M, d, d), lambda b: (b, 0, 0, 0)),
                  row,
                  pl.BlockSpec((1, mix_m), lambda b: (0, 0))],
        out_specs=[row,
                   pl.BlockSpec((1, H_M, d, d), lambda b: (b, 0, 0, 0)),
                   row,
                   pl.BlockSpec((1, 1, LANES), lambda b: (b, 0, 0))],
        out_shape=[jax.ShapeDtypeStruct((Bd, 1, mix_m), F32),
                   jax.ShapeDtypeStruct((Bd, H_M, d, d), F32),
                   jax.ShapeDtypeStruct((Bd, 1, mix_m), F32),
                   jax.ShapeDtypeStruct((Bd, 1, LANES), F32)],
        compiler_params=_cparams(("parallel",)),
        name="mlstm_sample",
    )(mq, mk, mv, mo, gs, c_state, n_state, gain)


def _dsa_prompt_kernel(iq_ref, misc_ref, ikt_ref, aq_ref, ak_ref, av_ref, ya_ref, sc_ref, acc_ref,
                       *, tq, w, topk, dh, scale):
    i = pl.program_id(1)
    nk = ((i + 1) * tq + w - 1) // w
    kf = float(topk)
    nsub = w // LANES

    q_pos = i * tq + lax.broadcasted_iota(jnp.int32, (tq, 1), 0)
    lane_w = lax.broadcasted_iota(jnp.int32, (1, w), 1)
    iq = iq_ref[0]
    iq_h = [iq[:, h * D_IDX:(h + 1) * D_IDX] for h in range(H_IDX)]
    misc = misc_ref[0]
    w_h = [misc[:, D_IDX + h:D_IDX + h + 1] for h in range(H_IDX)]

    def score_body(c, carry, causal_edge):
        rmax, rmin = carry
        ikc = ikt_ref[0, c]
        score = jnp.zeros((tq, w), F32)
        for h in range(H_IDX):
            score = score + w_h[h] * jnp.maximum(_mm(iq_h[h], ikc), 0.0)
        if causal_edge:
            valid = (c * w + lane_w) <= q_pos
            sc_ref[c] = jnp.where(valid, score, NEG_INF)
            s_hi = jnp.where(valid, score, NEG_INF)
            s_lo = jnp.where(valid, score, POS_INF)
        else:
            sc_ref[c] = score
            s_hi = s_lo = score
        rmax = jnp.maximum(rmax, jnp.max(s_hi, axis=1, keepdims=True))
        rmin = jnp.minimum(rmin, jnp.min(s_lo, axis=1, keepdims=True))
        return rmax, rmin

    n_full = (i * tq) // w
    carry0 = (jnp.full((tq, 1), NEG_INF, F32), jnp.full((tq, 1), POS_INF, F32))
    carry0 = lax.fori_loop(0, n_full, functools.partial(score_body, causal_edge=False), carry0)
    rmax, rmin = lax.fori_loop(n_full, nk, functools.partial(score_body, causal_edge=True), carry0)

    ge = lambda x, t: x >= t
    gt = lambda x, t: x > t

    rh = min(tq, LANES)
    groups = [pl.ds(r0, rh) for r0 in range(0, tq, rh)]
    part = lambda a: [a[r0:r0 + rh] for r0 in range(0, tq, rh)]

    def pass_acc(rows, fn, init, combine):
        def body(c, acc):
            x = sc_ref[c, rows, :]
            for j in range(nsub):
                acc = combine(acc, fn(x[:, j * LANES:(j + 1) * LANES]))
            return acc
        return lax.fori_loop(0, nk, body, jnp.full((rh, LANES), init, F32))

    def count_acc(rows, pred, thr):
        thr_b = jnp.broadcast_to(thr, (rh, LANES))
        return pass_acc(rows, lambda x: jnp.where(pred(x, thr_b), 1.0, 0.0), 0.0, jnp.add)

    def count(rows, pred, thr):
        return jnp.sum(count_acc(rows, pred, thr), axis=1, keepdims=True)

    def min_where(rows, pred, thr):
        thr_b = jnp.broadcast_to(thr, (rh, LANES))
        acc = pass_acc(rows, lambda x: jnp.where(pred(x, thr_b), x, POS_INF), POS_INF, jnp.minimum)
        return jnp.min(acc, axis=1, keepdims=True)

    def bis_body(_, carry):
        los, his, clos = carry
        mids = [0.5 * (lo + hi) for lo, hi in zip(los, his)]
        accs = [count_acc(rows, ge, mid) for rows, mid in zip(groups, mids)]
        cms = [jnp.sum(a, axis=1, keepdims=True) for a in accs]
        oks = [cm >= kf for cm in cms]
        return (tuple(jnp.where(ok, mid, lo) for ok, mid, lo in zip(oks, mids, los)),
                tuple(jnp.where(ok, hi, mid) for ok, mid, hi in zip(oks, mids, his)),
                tuple(jnp.where(ok, cm, cl) for ok, cm, cl in zip(oks, cms, clos)))

    los, _, clos = lax.fori_loop(
        0, N_BISECT, bis_body,
        (tuple(part(rmin)), tuple(part(rmax + jnp.abs(rmax) + 1.0)), tuple(part((q_pos + 1).astype(F32)))))

    def finish_rows(rows, qp, rmin_h, lo, c_lo):
        active = (qp + 1) > topk
        unresolved = jnp.max(jnp.where(active & (c_lo != kf), 1.0, 0.0)) > 0.5

        @pl.when(jnp.logical_not(unresolved))
        def _():
            thr = jnp.where(active, lo, rmin_h)

            def body(c, _):
                sc_ref[c, rows, :] = jnp.where(sc_ref[c, rows, :] >= thr, 0.0, NEG_INF)
                return 0
            lax.fori_loop(0, nk, body, 0)

        @pl.when(unresolved)
        def _():
            tau = min_where(rows, ge, lo)
            g = count(rows, gt, tau)

            def undone(tau, g):
                return active & (g >= kf)

            def fix_cond(st):
                return jnp.max(jnp.where(undone(*st), 1.0, 0.0)) > 0.5

            def fix_body(st):
                tau, g = st
                nd = undone(tau, g)
                tau2 = jnp.where(nd, min_where(rows, gt, tau), tau)
                return tau2, jnp.where(nd, count(rows, gt, tau2), g)

            tau, g = lax.while_loop(fix_cond, fix_body, (tau, g))
            tau_b = jnp.broadcast_to(jnp.where(active, tau, rmin_h), (rh, LANES))
            need_b = jnp.broadcast_to(jnp.where(active, kf - g, 1e9), (rh, LANES))
            r_i = lax.broadcasted_iota(jnp.int32, (LANES, 2 * LANES), 0)
            c_i = lax.broadcasted_iota(jnp.int32, (LANES, 2 * LANES), 1)
            tri_ones = ((r_i <= c_i) | (c_i >= LANES)).astype(BF16)

            def body(c, run):
                x = sc_ref[c, rows, :]
                outs = []
                for j in range(nsub):
                    xj = x[:, j * LANES:(j + 1) * LANES]
                    is_eq = xj == tau_b
                    cnt2 = _mm(jnp.where(is_eq, 1.0, 0.0).astype(BF16), tri_ones)
                    sel = (xj > tau_b) | (is_eq & (cnt2[:, :LANES] + run <= need_b))
                    outs.append(jnp.where(sel, 0.0, NEG_INF))
                    run = run + cnt2[:, LANES:]
                sc_ref[c, rows, :] = jnp.concatenate(outs, axis=1)
                return run
            lax.fori_loop(0, nk, body, jnp.zeros((rh, LANES), F32))

    for rows, qp, rmin_h, lo, c_lo in zip(groups, part(q_pos), part(rmin), los, clos):
        finish_rows(rows, qp, rmin_h, lo, c_lo)

    aq = aq_ref[0]
    q_heads = [aq[:, h * dh:(h + 1) * dh] for h in range(H_A)]
    acc_ref[...] = jnp.zeros_like(acc_ref)
    c2 = scale * LOG2E
    ones_blk = jnp.ones((w, dh), BF16)

    def att_body(c, ms):
        k0 = pl.multiple_of(c * w, w)
        bias = sc_ref[c]
        heads = [slice(h * dh, (h + 1) * dh) for h in range(H_A)]
        ss = [_nt(q_heads[h], ak_ref[0, pl.ds(k0, w), heads[h]]) + bias for h in range(H_A)]
        ms_new = [jnp.maximum(ms[h], jnp.max(ss[h], axis=1, keepdims=True)) for h in range(H_A)]
        m_safe = [jnp.where(m == NEG_INF, 0.0, m) for m in ms_new]
        ps = [jnp.exp2((ss[h] - m_safe[h]) * c2).astype(BF16) for h in range(H_A)]
        for h in range(H_A):
            alpha = jnp.exp2((ms[h] - m_safe[h]) * c2)
            v_aug = jnp.concatenate([av_ref[0, pl.ds(k0, w), heads[h]], ones_blk], axis=1)
            acc_ref[h] = alpha * acc_ref[h] + _mm(ps[h], v_aug)
        return tuple(ms_new)

    lax.fori_loop(0, nk, att_body, tuple(jnp.full((tq, 1), NEG_INF, F32) for _ in range(H_A)))
    for h in range(H_A):
        a = acc_ref[h]
        ya_ref[0, :, h * dh:(h + 1) * dh] = a[:, :dh] / a[:, dh:]


def _dsa_prompt(iqb, misc, ikt, aqb, akb, avb, *, tq, w, topk):
    B, S, mix_a = aqb.shape
    dh = mix_a // H_A
    nq = S // tq
    nw = S // w
    return pl.pallas_call(
        functools.partial(_dsa_prompt_kernel, tq=tq, w=w, topk=topk, dh=dh, scale=dh ** -0.5),
        grid=(B, nq),
        in_specs=[pl.BlockSpec((1, tq, H_IDX * D_IDX), lambda b, i: (b, i, 0)),
                  pl.BlockSpec((1, tq, LANES), lambda b, i: (b, i, 0)),
                  pl.BlockSpec((1, nw, D_IDX, w), lambda b, i: (b, 0, 0, 0)),
                  pl.BlockSpec((1, tq, mix_a), lambda b, i: (b, i, 0)),
                  pl.BlockSpec((1, S, mix_a), lambda b, i: (b, 0, 0)),
                  pl.BlockSpec((1, S, mix_a), lambda b, i: (b, 0, 0))],
        out_specs=pl.BlockSpec((1, tq, mix_a), lambda b, i: (b, i, 0)),
        out_shape=jax.ShapeDtypeStruct((B, S, mix_a), F32),
        scratch_shapes=[pltpu.VMEM((nw, tq, w), F32), pltpu.VMEM((H_A, tq, 2 * dh), F32)],
        compiler_params=_cparams(("parallel", "arbitrary")),
        name="dsa_prompt",
    )(iqb, misc, ikt, aqb, akb, avb)


def _dsa_sample_select_kernel(pt_ref, iq_ref, w_ref, ikn_ref, ptv_ref, pool_ref, rows_ref, flag_ref,
                              ikbuf, sem, sc_ref, xn_ref, slot_ref, phys_ref,
                              *, n_pages, page, topk, cw):
    nb = iq_ref.shape[0]
    past = n_pages * page
    kf = float(topk)
    n_cw = past // cw

    def page_copy(bb, p, slot):
        return pltpu.make_async_copy(pool_ref.at[pt_ref[bb, p]],
                                     ikbuf.at[slot, :, pl.ds(p * page, page)],
                                     sem.at[slot])

    def start_all(bb, slot):
        def body(p, _):
            page_copy(bb, p, slot).start()
            return 0
        lax.fori_loop(0, n_pages, body, 0)

    start_all(0, 0)

    def score_body(b, _):
        slot = b % 2

        @pl.when(b + 1 < nb)
        def _():
            start_all(b + 1, 1 - slot)

        def wait_body(p, _):
            page_copy(b, p, slot).wait()
            return 0
        lax.fori_loop(0, n_pages, wait_body, 0)

        iq8 = iq_ref[b]
        w8 = w_ref[b]
        s8 = _mm(iq8, ikbuf[slot].astype(BF16))
        sc_ref[pl.ds(b, 1), :] = jnp.sum(w8 * jnp.maximum(s8, 0.0), axis=0, keepdims=True)
        ikn = ikn_ref[b].astype(BF16).astype(F32)
        sn8 = jnp.sum(iq8.astype(F32) * ikn, axis=1, keepdims=True)
        xn_b = jnp.sum(w8 * jnp.maximum(sn8, 0.0), axis=0, keepdims=True)
        xn_ref[pl.ds(b, 1), :] = jnp.broadcast_to(xn_b, (1, LANES))
        return 0

    lax.fori_loop(0, nb, score_body, 0)

    x = sc_ref[...]
    xn = xn_ref[:, 0:1]

    def cnt(mask_row, mask_new):
        return (jnp.sum(jnp.where(mask_row, 1.0, 0.0), axis=1, keepdims=True)
                + jnp.where(mask_new, 1.0, 0.0))

    rmax = jnp.maximum(jnp.max(x, axis=1, keepdims=True), xn)
    rmin = jnp.minimum(jnp.min(x, axis=1, keepdims=True), xn)
    hi0 = rmax + jnp.abs(rmax) + 1.0

    def bis_body(_, carry):
        lo, hi = carry
        mid = 0.5 * (lo + hi)
        ok = cnt(x >= mid, xn >= mid) >= kf
        return jnp.where(ok, mid, lo), jnp.where(ok, hi, mid)

    lo, _ = lax.fori_loop(0, N_BISECT, bis_body, (rmin, hi0))

    def min_where(mask_row, mask_new):
        return jnp.minimum(jnp.min(jnp.where(mask_row, x, POS_INF), axis=1, keepdims=True),
                           jnp.where(mask_new, xn, POS_INF))

    tau = min_where(x >= lo, xn >= lo)
    g = cnt(x > tau, xn > tau)

    def fix_cond(st):
        tau, g = st
        return jnp.max(jnp.where(g >= kf, 1.0, 0.0)) > 0.5

    def fix_body(st):
        tau, g = st
        tau2 = jnp.where(g >= kf, min_where(x > tau, xn > tau), tau)
        return tau2, cnt(x > tau2, xn > tau2)

    tau, g = lax.while_loop(fix_cond, fix_body, (tau, g))
    need = kf - g

    tri = (lax.broadcasted_iota(jnp.int32, (cw, cw), 0)
           < lax.broadcasted_iota(jnp.int32, (cw, cw), 1)).astype(BF16)

    def excl_prefix(flag):
        outs = []
        run = jnp.zeros((nb, 1), F32)
        for c in range(n_cw):
            f = flag[:, c * cw:(c + 1) * cw]
            outs.append(_mm(f.astype(BF16), tri) + run)
            run = run + jnp.sum(f, axis=1, keepdims=True)
        return jnp.concatenate(outs, axis=1), run

    is_eq = x == tau
    pre_eq, n_eq_past = excl_prefix(jnp.where(is_eq, 1.0, 0.0))
    sel = (x > tau) | (is_eq & (pre_eq < need))
    new_sel = (xn > tau) | ((xn == tau) & (n_eq_past < need))
    slot, _ = excl_prefix(jnp.where(sel, 1.0, 0.0))
    slot_ref[...] = jnp.where(sel, slot, -1.0)

    ptv = ptv_ref[...]
    jrow = lax.broadcasted_iota(jnp.int32, (1, past), 1)
    prow = lax.broadcasted_iota(jnp.int32, (n_pages, 1), 0)
    expand = ((jrow >= prow * page) & (jrow < (prow + 1) * page)).astype(BF16)
    digit_bits = 6
    pt_hi = _mm((ptv >> digit_bits).astype(F32).astype(BF16), expand)
    pt_lo = _mm((ptv & ((1 << digit_bits) - 1)).astype(F32).astype(BF16), expand)
    pidx = lax.broadcasted_iota(jnp.int32, (8, n_pages), 1).astype(F32).astype(BF16)
    pg = _mm(pidx, expand)[0:1, :]
    phys_ref[...] = (pt_hi * (1 << digit_bits) + pt_lo) * page + (jrow.astype(F32) - pg * page)

    slot_col = lax.broadcasted_iota(jnp.int32, (topk, 1), 0).astype(F32)
    lane_b = lax.broadcasted_iota(jnp.int32, (1, LANES), 1)

    def extract_body(b, out):
        srow = slot_ref[pl.ds(b, 1), :]
        frow = phys_ref[pl.ds(b, 1), :]
        acc = jnp.zeros((topk, LANES), F32)
        for c in range(past // LANES):
            cs = slice(c * LANES, (c + 1) * LANES)
            acc = acc + jnp.where(srow[:, cs] == slot_col, frow[:, cs], 0.0)
        return jnp.where(lane_b == b, jnp.sum(acc, axis=1, keepdims=True), out)

    out = lax.fori_loop(0, nb, extract_body, jnp.zeros((topk, LANES), F32))
    rows_ref[...] = out.astype(jnp.int32)
    flag_ref[...] = jnp.broadcast_to(jnp.where(new_sel, 1, 0), (nb, LANES)).astype(jnp.int32)


def _dsa_sample_select(page_table, iq8, w8, ik_new, pool_ik_t, *, topk, cw):
    Bd, n_pages = page_table.shape
    n_pool, d_idx, page = pool_ik_t.shape
    past = n_pages * page
    assert Bd <= LANES and n_pool <= 64 * 256
    full = lambda shp: pl.BlockSpec(shp, lambda i, pt: (0,) * len(shp))
    grid_spec = pltpu.PrefetchScalarGridSpec(
        num_scalar_prefetch=1,
        grid=(1,),
        in_specs=[full((Bd, 8, d_idx)), full((Bd, 8, 1)), full((Bd, 1, d_idx)), full((Bd, n_pages)),
                  pl.BlockSpec(memory_space=pl.ANY)],
        out_specs=[full((topk, LANES)), full((Bd, LANES))],
        scratch_shapes=[pltpu.VMEM((2, d_idx, past), F32),
                        pltpu.SemaphoreType.DMA((2,)),
                        pltpu.VMEM((Bd, past), F32),
                        pltpu.VMEM((Bd, LANES), F32),
                        pltpu.VMEM((Bd, past), F32),
                        pltpu.VMEM((Bd, past), F32)],
    )
    return pl.pallas_call(
        functools.partial(_dsa_sample_select_kernel, n_pages=n_pages, page=page, topk=topk, cw=cw),
        grid_spec=grid_spec,
        out_shape=[jax.ShapeDtypeStruct((topk, LANES), jnp.int32),
                   jax.ShapeDtypeStruct((Bd, LANES), jnp.int32)],
        compiler_params=_cparams(("arbitrary",)),
        name="dsa_sample_select",
    )(page_table, iq8, w8, ik_new, page_table, pool_ik_t)


def _dsa_sample_attend_kernel(rows_ref, flag_ref, aq_ref, knew_ref, vnew_ref, kpool_ref, vpool_ref,
                              ya_ref, kbuf, vbuf, sem, *, topk, dh, scale):
    b = pl.program_id(0)
    nb = pl.num_programs(0)

    def row_copies(bb, t, slot):
        r = rows_ref[bb, t]
        dst = pl.ds(t * H_A, H_A)
        return (pltpu.make_async_copy(kpool_ref.at[r], kbuf.at[slot, dst, :], sem.at[0, slot]),
                pltpu.make_async_copy(vpool_ref.at[r], vbuf.at[slot, dst, :], sem.at[1, slot]))

    def start_all(bb, slot):
        def body(t, _):
            ck, cv = row_copies(bb, t, slot)
            ck.start()
            cv.start()
            return 0
        lax.fori_loop(0, topk, body, 0, unroll=8)

    slot = b % 2

    @pl.when(b == 0)
    def _():
        start_all(0, 0)

    @pl.when(b + 1 < nb)
    def _():
        start_all(b + 1, 1 - slot)

    def wait_body(t, _):
        ck, cv = row_copies(b, t, slot)
        ck.wait()
        cv.wait()
        return 0
    lax.fori_loop(0, topk, wait_body, 0, unroll=8)

    take_new = (lax.broadcasted_iota(jnp.int32, (topk, 1), 0) == topk - 1) & (flag_ref[b] > 0)
    aq = aq_ref[0]
    for h in range(H_A):
        hs = slice(h * dh, (h + 1) * dh)
        kh = kbuf[slot, pl.ds(h, topk, stride=H_A), :]
        vh = vbuf[slot, pl.ds(h, topk, stride=H_A), :]
        kh = jnp.where(take_new, knew_ref[0, h:h + 1, :], kh).astype(BF16)
        vh = jnp.where(take_new, vnew_ref[0, h:h + 1, :], vh).astype(BF16)
        q8 = jnp.broadcast_to(aq[:, hs], (8, dh))
        s = _nt(q8, kh) * scale
        m = jnp.max(s, axis=1, keepdims=True)
        p = jnp.exp(s - m)
        p = p / jnp.sum(p, axis=1, keepdims=True)
        ya_ref[0, :, hs] = _mm(p.astype(BF16), vh)[0:1, :]


def _dsa_sample_attend(rows, flags, aqb, k_new, v_new, pool_k, pool_v, *, topk):
    Bd, _, mix_a = aqb.shape
    dh = pool_k.shape[2]
    new_spec = pl.BlockSpec((1, H_A, dh), lambda b, r, f: (b, 0, 0))
    grid_spec = pltpu.PrefetchScalarGridSpec(
        num_scalar_prefetch=2,
        grid=(Bd,),
        in_specs=[pl.BlockSpec((1, 1, mix_a), lambda b, r, f: (b, 0, 0)),
                  new_spec, new_spec,
                  pl.BlockSpec(memory_space=pl.ANY),
                  pl.BlockSpec(memory_space=pl.ANY)],
        out_specs=pl.BlockSpec((1, 1, mix_a), lambda b, r, f: (b, 0, 0)),
        scratch_shapes=[pltpu.VMEM((2, topk * H_A, dh), F32),
                        pltpu.VMEM((2, topk * H_A, dh), F32),
                        pltpu.SemaphoreType.DMA((2, 2))],
    )
    return pl.pallas_call(
        functools.partial(_dsa_sample_attend_kernel, topk=topk, dh=dh, scale=dh ** -0.5),
        grid_spec=grid_spec,
        out_shape=jax.ShapeDtypeStruct((Bd, 1, mix_a), F32),
        compiler_params=_cparams(("arbitrary",)),
        name="dsa_sample_attend",
    )(rows, flags, aqb, k_new, v_new, pool_k, pool_v)


def _mem_kv_kernel(mem_ref, nm_ref, wk_ref, wv_ref, kn_ref, k_ref, v_ref, *, dh):
    hm = _rms(mem_ref[...], nm_ref[...]).astype(BF16)
    kk = _mm(hm, wk_ref[...])
    vv = _mm(hm, wv_ref[...])
    for h in range(H_C):
        hs = slice(h * dh, (h + 1) * dh)
        k_ref[:, h, :] = _rms(kk[:, hs], kn_ref[...])
        v_ref[:, h, :] = vv[:, hs]


def _mem_kv(mem2d, norm_mem, w_ck, w_cv, ck_norm, *, tm):
    rows, d = mem2d.shape
    dh = d // H_C
    row = pl.BlockSpec((tm, d), lambda i: (i, 0))
    heads = pl.BlockSpec((tm, H_C, dh), lambda i: (i, 0, 0))
    const = lambda shp: pl.BlockSpec(shp, lambda i: (0, 0))
    return pl.pallas_call(
        functools.partial(_mem_kv_kernel, dh=dh),
        grid=(rows // tm,),
        in_specs=[row, const((1, d)), const((d, d)), const((d, d)), const((1, dh))],
        out_specs=[heads, heads],
        out_shape=[jax.ShapeDtypeStruct((rows, H_C, dh), F32)] * 2,
        compiler_params=_cparams(("parallel",)),
        name="mem_kv",
    )(mem2d, norm_mem, w_ck, w_cv, ck_norm)


def _out_cq_kernel(x_ref, ym_ref, ya_ref, wo_ref, nc_ref, wq_ref, qn_ref, x1_ref, qc_ref, *, mix_m, dh):
    upd = (_mm(ym_ref[...].astype(BF16), wo_ref[0:mix_m, :])
           + _mm(ya_ref[...].astype(BF16), wo_ref[mix_m:, :]))
    x1 = x_ref[...] + upd
    x1_ref[...] = x1
    hq = _mm(_rms(x1, nc_ref[...]).astype(BF16), wq_ref[...])
    for h in range(H_C):
        hs = slice(h * dh, (h + 1) * dh)
        qc_ref[:, hs] = _rms(hq[:, hs], qn_ref[...]).astype(BF16)


def _out_cq(x2d, ym, ya, w_out, norm_cross, w_cq, cq_norm, *, tm):
    rows, d = x2d.shape
    mix_m = ym.shape[1]
    mix_a = ya.shape[1]
    dh = d // H_C
    row = lambda wdt: pl.BlockSpec((tm, wdt), lambda i: (i, 0))
    const = lambda shp: pl.BlockSpec(shp, lambda i: (0, 0))
    return pl.pallas_call(
        functools.partial(_out_cq_kernel, mix_m=mix_m, dh=dh),
        grid=(rows // tm,),
        in_specs=[row(d), row(mix_m), row(mix_a), const((mix_m + mix_a, d)), const((1, d)),
                  const((d, d)), const((1, dh))],
        out_specs=[row(d), row(d)],
        out_shape=[jax.ShapeDtypeStruct((rows, d), F32), jax.ShapeDtypeStruct((rows, d), BF16)],
        compiler_params=_cparams(("parallel",)),
        name="out_cq",
    )(x2d, ym, ya, w_out, norm_cross, w_cq, cq_norm)


def _cross_kernel(q_ref, k_hbm, v_hbm, o_ref, kv_buf, sem, *, dh, scale):
    b = pl.program_id(0)
    t = pl.program_id(1)
    nb = pl.num_programs(0)
    slot = b % 2

    def head_copies(bb, sl):
        cps = []
        for h in range(H_C):
            cps.append(pltpu.make_async_copy(k_hbm.at[bb, :, h, :], kv_buf.at[sl, 0, h], sem.at[sl]))
            cps.append(pltpu.make_async_copy(v_hbm.at[bb, :, h, :], kv_buf.at[sl, 1, h], sem.at[sl]))
        return cps

    @pl.when(t == 0)
    def _():
        @pl.when(b == 0)
        def _():
            for cp in head_copies(0, 0):
                cp.start()

        @pl.when(b + 1 < nb)
        def _():
            for cp in head_copies(b + 1, 1 - slot):
                cp.start()

        for cp in head_copies(b, slot):
            cp.wait()

    q = q_ref[0]
    rows = q.shape[0]
    if rows < 8:
        q = jnp.broadcast_to(q, (8, q.shape[1]))
    for h in range(H_C):
        hs = slice(h * dh, (h + 1) * dh)
        kb = kv_buf[slot, 0, h].astype(BF16)
        vb = kv_buf[slot, 1, h].astype(BF16)
        s = _nt(q[:, hs], kb) * scale
        m = jnp.max(s, axis=1, keepdims=True)
        p = jnp.exp(s - m)
        p = p / jnp.sum(p, axis=1, keepdims=True)
        o = _mm(p.astype(BF16), vb)
        o_ref[0, :, hs] = o[0:rows].astype(BF16)


def _cross(qc, mem_k, mem_v, *, tq):
    B, T, d = qc.shape
    M = mem_k.shape[1]
    dh = d // H_C
    return pl.pallas_call(
        functools.partial(_cross_kernel, dh=dh, scale=dh ** -0.5),
        grid=(B, T // tq),
        in_specs=[pl.BlockSpec((1, tq, d), lambda b, t: (b, t, 0)),
                  pl.BlockSpec(memory_space=pl.ANY),
                  pl.BlockSpec(memory_space=pl.ANY)],
        out_specs=pl.BlockSpec((1, tq, d), lambda b, t: (b, t, 0)),
        out_shape=jax.ShapeDtypeStruct((B, T, d), BF16),
        scratch_shapes=[pltpu.VMEM((2, 2, H_C, M, dh), F32), pltpu.SemaphoreType.DMA((2,))],
        compiler_params=_cparams(("arbitrary", "arbitrary")),
        name="cross_attn",
    )(qc, mem_k, mem_v)


def _gelu_tanh(x):
    return 0.5 * x * (1.0 + jnp.tanh(np.sqrt(2.0 / np.pi) * (x + 0.044715 * (x * x * x))))


def _ffn_front(x1_ref, o_ref, wco_ref, nf_ref, x2_ref, hb_ref, acc_ref):
    x2 = x1_ref[0] + _mm(o_ref[0], wco_ref[...])
    x2_ref[...] = x2
    hb_ref[...] = _rms(x2, nf_ref[...]).astype(BF16)
    acc_ref[...] = jnp.zeros_like(acc_ref)


def _ffn_prompt_kernel(x1_ref, o_ref, wco_ref, nf_ref, wua_ref, wug_ref, cwa_ref, cwg_ref,
                       cba_ref, cbg_ref, wd_ref, ha_ref, hg_ref,
                       y_ref, ca_ref, cg_ref, x2_ref, hb_ref, acc_ref, carry_ref, *, tm, rs):
    t = pl.program_id(1)
    j = pl.program_id(2)
    nj = pl.num_programs(2)

    @pl.when(j == 0)
    def _():
        _ffn_front(x1_ref, o_ref, wco_ref, nf_ref, x2_ref, hb_ref, acc_ref)

    @pl.when(t == 0)
    def _():
        carry_ref[j, 0, 6:8, :] = ha_ref[0]
        carry_ref[j, 1, 6:8, :] = hg_ref[0]

    rid = lax.broadcasted_iota(jnp.int32, (rs, 1), 0)

    def conv_part(hb, part, wu_ref, cw_ref, cb_ref):
        u = _mm(hb, wu_ref[...])
        p2 = carry_ref[j, part, 6:7, :]
        p1 = carry_ref[j, part, 7:8, :]
        um1 = jnp.where(rid == 0, p1, pltpu.roll(u, 1, 0))
        um2 = jnp.where(rid == 0, p2, jnp.where(rid == 1, p1, pltpu.roll(u, 2, 0)))
        carry_ref[j, part] = u[rs - 8:rs, :]
        return cb_ref[...] + um2 * cw_ref[0:1, :] + um1 * cw_ref[1:2, :] + u * cw_ref[2:3, :]

    def sub_body(r, _):
        r0 = pl.multiple_of(r * rs, rs)
        hb = hb_ref[pl.ds(r0, rs), :]
        a = conv_part(hb, 0, wua_ref, cwa_ref, cba_ref)
        g = conv_part(hb, 1, wug_ref, cwg_ref, cbg_ref)
        acc_ref[pl.ds(r0, rs), :] += _mm((_gelu_tanh(g) * a).astype(BF16), wd_ref[...])
        return 0

    lax.fori_loop(0, tm // rs, sub_body, 0)
    ca_ref[0, 0] = carry_ref[j, 0, 6:8, :]
    cg_ref[0, 0] = carry_ref[j, 1, 6:8, :]

    @pl.when(j == nj - 1)
    def _():
        y_ref[0] = x2_ref[...] + acc_ref[...]


def _ffn_prompt(x1, o, w_co, norm_ffn, w_up, conv_w, conv_b, w_down, hist, *, tm, tf):
    B, T, d = x1.shape
    d_ff = w_down.shape[0]
    nj = d_ff // tf
    nt = T // tm
    idx3 = lambda b, t, j: (b, t, 0)
    c2 = lambda shp: pl.BlockSpec(shp, lambda b, t, j: (0, 0))
    return pl.pallas_call(
        functools.partial(_ffn_prompt_kernel, tm=tm, rs=min(512, tm)),
        grid=(B, nt, nj),
        in_specs=[pl.BlockSpec((1, tm, d), idx3), pl.BlockSpec((1, tm, d), idx3),
                  pl.BlockSpec((d, d), lambda b, t, j: (0, 0), pipeline_mode=pl.Buffered(1)), c2((1, d)),
                  pl.BlockSpec((d, tf), lambda b, t, j: (0, j)),
                  pl.BlockSpec((d, tf), lambda b, t, j: (0, nj + j)),
                  pl.BlockSpec((CONV_W, tf), lambda b, t, j: (0, j)),
                  pl.BlockSpec((CONV_W, tf), lambda b, t, j: (0, nj + j)),
                  pl.BlockSpec((1, tf), lambda b, t, j: (0, j)),
                  pl.BlockSpec((1, tf), lambda b, t, j: (0, nj + j)),
                  pl.BlockSpec((tf, d), lambda b, t, j: (j, 0)),
                  pl.BlockSpec((1, CONV_W - 1, tf), lambda b, t, j: (b, 0, j)),
                  pl.BlockSpec((1, CONV_W - 1, tf), lambda b, t, j: (b, 0, nj + j))],
        out_specs=[pl.BlockSpec((1, tm, d), idx3),
                   pl.BlockSpec((1, 1, CONV_W - 1, tf), lambda b, t, j: (b, t, 0, j)),
                   pl.BlockSpec((1, 1, CONV_W - 1, tf), lambda b, t, j: (b, t, 0, j))],
        out_shape=[jax.ShapeDtypeStruct((B, T, d), F32),
                   jax.ShapeDtypeStruct((B, nt, CONV_W - 1, d_ff), F32),
                   jax.ShapeDtypeStruct((B, nt, CONV_W - 1, d_ff), F32)],
        scratch_shapes=[pltpu.VMEM((tm, d), F32), pltpu.VMEM((tm, d), BF16), pltpu.VMEM((tm, d), F32),
                        pltpu.VMEM((nj, 2, 8, tf), F32)],
        compiler_params=_cparams(("arbitrary", "arbitrary", "arbitrary")),
        name="ffn_prompt",
    )(x1, o, w_co, norm_ffn, w_up, w_up, conv_w, conv_w, conv_b, conv_b, w_down, hist, hist)


def _ffn_sample_kernel(x1_ref, o_ref, wco_ref, nf_ref, wua_ref, wug_ref, cwa_ref, cwg_ref,
                       cba_ref, cbg_ref, wd_ref, h0a_ref, h0g_ref, h1a_ref, h1g_ref,
                       y_ref, ua_ref, ug_ref, x2_ref, hb_ref, acc_ref):
    j = pl.program_id(0)
    nj = pl.num_programs(0)

    @pl.when(j == 0)
    def _():
        _ffn_front(x1_ref, o_ref, wco_ref, nf_ref, x2_ref, hb_ref, acc_ref)

    hb = hb_ref[...]

    def conv_part(wu_ref, cw_ref, cb_ref, h0_ref, h1_ref, u_out_ref):
        u = _mm(hb, wu_ref[...])
        u_out_ref[...] = u
        return cb_ref[...] + h0_ref[...] * cw_ref[0:1, :] + h1_ref[...] * cw_ref[1:2, :] + u * cw_ref[2:3, :]

    a = conv_part(wua_ref, cwa_ref, cba_ref, h0a_ref, h1a_ref, ua_ref)
    g = conv_part(wug_ref, cwg_ref, cbg_ref, h0g_ref, h1g_ref, ug_ref)
    acc_ref[...] += _mm((_gelu_tanh(g) * a).astype(BF16), wd_ref[...])

    @pl.when(j == nj - 1)
    def _():
        y_ref[0] = x2_ref[...] + acc_ref[...]


def _ffn_sample(x1, o, w_co, norm_ffn, w_up, conv_w, conv_b, w_down, h0, h1, *, tf):
    _, rows, d = x1.shape
    d_ff = w_down.shape[0]
    nj = d_ff // tf
    c2 = lambda shp: pl.BlockSpec(shp, lambda j: (0, 0))
    c3 = lambda shp: pl.BlockSpec(shp, lambda j: (0, 0, 0))
    col_a = lambda r: pl.BlockSpec((r, tf), lambda j: (0, j))
    col_g = lambda r: pl.BlockSpec((r, tf), lambda j: (0, nj + j))
    return pl.pallas_call(
        _ffn_sample_kernel,
        grid=(nj,),
        in_specs=[c3((1, rows, d)), c3((1, rows, d)), c2((d, d)), c2((1, d)),
                  col_a(d), col_g(d), col_a(CONV_W), col_g(CONV_W), col_a(1), col_g(1),
                  pl.BlockSpec((tf, d), lambda j: (j, 0)),
                  col_a(rows), col_g(rows), col_a(rows), col_g(rows)],
        out_specs=[c3((1, rows, d)), col_a(rows), col_a(rows)],
        out_shape=[jax.ShapeDtypeStruct((1, rows, d), F32),
                   jax.ShapeDtypeStruct((rows, d_ff), F32),
                   jax.ShapeDtypeStruct((rows, d_ff), F32)],
        scratch_shapes=[pltpu.VMEM((rows, d), F32), pltpu.VMEM((rows, d), BF16), pltpu.VMEM((rows, d), F32)],
        compiler_params=_cparams(("arbitrary",)),
        name="ffn_sample",
    )(x1, o, w_co, norm_ffn, w_up, w_up, conv_w, conv_w, conv_b, conv_b, w_down, h0, h0, h1, h1)


def _rope_tables(pos, dh_a):
    assert dh_a == LANES and 2 * D_IDX == LANES and D_IDX & (D_IDX - 1) == 0
    posf = pos.astype(F32)[:, None]
    half_a = dh_a // 2
    inv_a = ROPE_THETA ** (-jnp.arange(half_a, dtype=F32) / half_a)
    ang_a = posf * inv_a[None, :]
    half_i = D_IDX // 2
    inv_i = ROPE_THETA ** (-jnp.arange(half_i, dtype=F32) / half_i)
    ang_i = posf * inv_i[None, :]
    cos_i, sin_i = jnp.cos(ang_i), jnp.sin(ang_i)
    return jnp.concatenate([jnp.cos(ang_a), jnp.sin(ang_a), cos_i, sin_i, cos_i, sin_i], axis=1)


def kernel(x_prompt, x_sample, mem_prompt, cache_k, cache_v, cache_idx_k, cache_mem_k, cache_mem_v,
           state_mlstm_c, state_mlstm_n, state_mlstm_m, state_conv, page_table,
           norm_mix, w_in, b_if, mlstm_norm, q_norm, k_norm, w_out, norm_cross, norm_mem,
           w_cq, w_ck, w_cv, w_co, cq_norm, ck_norm, norm_ffn, w_up, conv_w, conv_b, w_down):
    B, S, D = x_prompt.shape
    Bd, T, _ = x_sample.shape
    assert T == 1 and w_in.shape[0] == 1
    n_pool, page = cache_k.shape[1], cache_k.shape[2]
    n_pages = page_table.shape[1]
    past = n_pages * page
    mix_m = mlstm_norm.shape[1]
    dh_m = mix_m // H_M
    dh_a = q_norm.shape[1]
    mix_a = H_A * dh_a
    d_ff = w_down.shape[1]
    M = mem_prompt.shape[1]
    chunk = min(128, S)
    topk_p = min(TOPK_MAX, S // 4)
    topk_s = min(TOPK_MAX, (past + T) // 4)

    w = w_in[0]
    o_gate = 4 * mix_m
    o_aq = o_gate + 2 * H_M
    o_iq = o_aq + 3 * mix_a
    o_ik = o_iq + H_IDX * D_IDX
    o_iw = o_ik + D_IDX
    tail_pad = LANES - (D_IDX + H_IDX + 2 * H_M)
    w_r = (w[:, :o_gate].astype(BF16), w[:, o_aq:o_ik].astype(BF16),
           jnp.concatenate([w[:, o_ik:o_iw + H_IDX], w[:, o_gate:o_aq],
                            jnp.zeros((D, tail_pad), w.dtype)], axis=1).astype(BF16))
    bias_tail = jnp.concatenate([jnp.zeros((D_IDX + H_IDX,), F32), b_if[0].astype(F32),
                                 jnp.zeros((tail_pad,), F32)])[None, :]
    w_out_b = w_out[0].astype(BF16)
    w_cq_b, w_ck_b, w_cv_b, w_co_b = (a[0].astype(BF16) for a in (w_cq, w_ck, w_cv, w_co))
    w_up_b = w_up[0].astype(BF16)
    w_down_b = w_down[0].astype(BF16)
    row = lambda a: a[0][None, :]

    def split_misc(misc):
        ik = misc[:, :D_IDX]
        li = misc[:, D_IDX + H_IDX:D_IDX + H_IDX + H_M]
        lf = misc[:, D_IDX + H_IDX + H_M:D_IDX + H_IDX + 2 * H_M]
        return ik, li, lf

    tm_in = min(256, S)
    tab_p = _rope_tables(jnp.arange(S), dh_a)
    (mq, mk, mv, mo, aqb, ak, av, akb, avb, iqb, misc) = _in_proj(
        x_prompt.reshape(B * S, D), row(norm_mix), w_r, bias_tail, row(q_norm), row(k_norm), tab_p,
        tm=tm_in, tab_tiles=S // tm_in, mix_m=mix_m, mix_a=mix_a)
    ik_p, li_p, lf_p = split_misc(misc)
    r3 = lambda a: a.reshape(B, S, a.shape[-1])
    gcol = misc[:, D_IDX + H_IDX:D_IDX + H_IDX + 2 * H_M].reshape(B, S, 2 * H_M)
    grow = gcol.reshape(B, S // chunk, chunk, 2 * H_M).transpose(0, 1, 3, 2)
    y_m, c_p, n_p, m_p = _mlstm_prompt(r3(mq), r3(mk), r3(mv), r3(mo), grow, gcol, row(mlstm_norm), chunk=chunk)

    tq = min(512, S)
    wk = min(512, S)
    ikt = ik_p.astype(BF16).reshape(B, S // wk, wk, D_IDX).transpose(0, 1, 3, 2)
    y_a = _dsa_prompt(r3(iqb), r3(misc), ikt, r3(aqb), r3(akb), r3(avb), tq=tq, w=wk, topk=topk_p)

    mk_p, mv_p = _mem_kv(mem_prompt.reshape(B * M, D), row(norm_mem), w_ck_b, w_cv_b, row(ck_norm),
                         tm=min(256, B * M))
    x1, qc = _out_cq(x_prompt.reshape(B * S, D), y_m.reshape(B * S, mix_m), y_a.reshape(B * S, mix_a),
                     w_out_b, row(norm_cross), w_cq_b, row(cq_norm), tm=min(512, S))
    dh_c = D // H_C
    o_c = _cross(qc.reshape(B, S, D), mk_p.reshape(B, M, H_C, dh_c), mv_p.reshape(B, M, H_C, dh_c), tq=min(512, S))
    tf = d_ff // 2 if (d_ff // 2) % LANES == 0 else d_ff
    xp, conv_a, conv_g = _ffn_prompt(x1.reshape(B, S, D), o_c, w_co_b, row(norm_ffn), w_up_b, conv_w[0],
                                     conv_b[0][None, :], w_down_b,
                                     jnp.zeros((B, CONV_W - 1, 2 * d_ff), F32), tm=min(512, S), tf=tf)
    conv_p = jnp.concatenate([conv_a[:, -1], conv_g[:, -1]], axis=-1)

    tab_s = jnp.broadcast_to(_rope_tables(jnp.full((1,), past, jnp.int32), dh_a), (Bd, 2 * LANES))
    (mq_s, mk_s, mv_s, mo_s, aqb_s, ak_s, av_s, _, _, iqb_s, misc_s) = _in_proj(
        x_sample.reshape(Bd, D), row(norm_mix), w_r, bias_tail, row(q_norm), row(k_norm), tab_s,
        tm=Bd, tab_tiles=1, mix_m=mix_m, mix_a=mix_a)
    ik_s, li_s, lf_s = split_misc(misc_s)
    gs = jnp.concatenate([li_s, lf_s, state_mlstm_m[0].astype(F32)], axis=-1)[:, None, :]
    e1 = lambda a: a[:, None, :]
    y_ms, c_s, n_s, m_s = _mlstm_sample(e1(mq_s), e1(mk_s), e1(mv_s), e1(mo_s), gs,
                                        state_mlstm_c[0], state_mlstm_n[0].reshape(Bd, 1, mix_m),
                                        row(mlstm_norm))

    iq8 = jnp.pad(iqb_s.reshape(Bd, H_IDX, D_IDX), ((0, 0), (0, 8 - H_IDX), (0, 0)))
    w8 = jnp.pad(misc_s[:, D_IDX:D_IDX + H_IDX], ((0, 0), (0, 8 - H_IDX)))[:, :, None]
    assert n_pages <= 256
    rows_t, flags = _dsa_sample_select(page_table, iq8, w8, e1(ik_s), jnp.swapaxes(cache_idx_k[0], 1, 2),
                                       topk=topk_s, cw=min(512, past))
    y_as = _dsa_sample_attend(rows_t[:, :Bd].T, flags[:, 0], e1(aqb_s), ak_s, av_s,
                              cache_k[0].reshape(n_pool * page, H_A, dh_a),
                              cache_v[0].reshape(n_pool * page, H_A, dh_a), topk=topk_s)

    x1_s, qc_s = _out_cq(x_sample.reshape(Bd, D), y_ms.reshape(Bd, mix_m), y_as.reshape(Bd, mix_a),
                         w_out_b, row(norm_cross), w_cq_b, row(cq_norm), tm=Bd)
    o_s = _cross(qc_s.reshape(Bd, 1, D), cache_mem_k[0], cache_mem_v[0], tq=1)
    xs, u_a, u_g = _ffn_sample(x1_s.reshape(1, Bd, D), o_s.reshape(1, Bd, D), w_co_b, row(norm_ffn), w_up_b,
                               conv_w[0], conv_b[0][None, :], w_down_b,
                               state_conv[0, :, 0, :], state_conv[0, :, 1, :], tf=tf)
    conv_s = jnp.stack([state_conv[0, :, 1, :], jnp.concatenate([u_a, u_g], axis=-1)], axis=1)

    lead = lambda a: a[None]
    return (xp, xs.reshape(Bd, 1, D),
            lead(ak.reshape(B, S, H_A, dh_a)), lead(av.reshape(B, S, H_A, dh_a)), lead(ik_p.reshape(B, S, D_IDX)),
            lead(c_p), lead(n_p), lead(m_p[:, :, 0]),
            lead(mk_p.reshape(B, M, H_C, D // H_C)), lead(mv_p.reshape(B, M, H_C, D // H_C)), lead(conv_p),
            lead(ak_s.reshape(Bd, 1, H_A, dh_a)), lead(av_s.reshape(Bd, 1, H_A, dh_a)),
            lead(ik_s.reshape(Bd, 1, D_IDX)),
            lead(c_s), lead(n_s.reshape(Bd, H_M, dh_m)), lead(m_s[:, 0, :H_M]), lead(conv_s))
```

```python
import functools

import jax
import jax.numpy as jnp
import numpy as np
from jax import lax
from jax.experimental import pallas as pl
from jax.experimental.pallas import tpu as pltpu

F32 = jnp.float32
BF16 = jnp.bfloat16

H_M = 4
H_A = 4
H_IDX = 4
D_IDX = 64
H_C = 4
TOPK_MAX = 256
CONV_W = 3
ROPE_THETA = 10000.0
EPS = 1e-6
LOG2E = 1.4426950408889634
NEG_INF = float("-inf")
POS_INF = float("inf")

LANES = 128
VMEM_LIMIT = 56 * 1024 * 1024
N_BISECT = 20


def _cparams(sem):
    return pltpu.CompilerParams(dimension_semantics=sem, vmem_limit_bytes=VMEM_LIMIT)


def _nt(a, b):
    return lax.dot_general(a, b, (((1,), (1,)), ((), ())), preferred_element_type=F32)


def _tn(a, b):
    return lax.dot_general(a, b, (((0,), (0,)), ((), ())), preferred_element_type=F32)


def _mm(a, b):
    return jnp.dot(a, b, preferred_element_type=F32)


def _rms(x, g):
    ms = jnp.mean(x * x, axis=-1, keepdims=True)
    return x * lax.rsqrt(ms + EPS) * g


def _sigmoid(x):
    return 1.0 / (1.0 + jnp.exp(-x))


def _in_proj_kernel(x_ref, nm_ref, wm_ref, wa_ref, wt_ref, bias_ref, qn_ref, kn_ref, tab_ref,
                    mq_ref, mk_ref, mv_ref, mo_ref, aqb_ref, ak_ref, av_ref, akb_ref, avb_ref,
                    iqb_ref, misc_ref, *, mix_m, mix_a, dh_m, dh_a):
    h = _rms(x_ref[...], nm_ref[...]).astype(BF16)

    o_mq, o_mk, o_mv, o_mo = 0, mix_m, 2 * mix_m, 3 * mix_m
    n_m = 4 * mix_m
    o_aq = n_m
    o_ak = o_aq + mix_a
    o_av = o_ak + mix_a
    o_iq = o_av + mix_a
    o_tail = o_iq + H_IDX * D_IDX

    def proj(lo, width):
        if lo < n_m:
            return _mm(h, wm_ref[:, lo:lo + width])
        if lo < o_tail:
            return _mm(h, wa_ref[:, lo - n_m:lo - n_m + width])
        return _mm(h, wt_ref[...])

    mq_ref[...] = proj(o_mq, mix_m)
    mk_ref[...] = proj(o_mk, mix_m) * (dh_m ** -0.5)
    mv_ref[...] = proj(o_mv, mix_m)
    mo_ref[...] = proj(o_mo, mix_m)

    t_a = tab_ref[:, 0:LANES]
    t_i = tab_ref[:, LANES:2 * LANES]
    lane = lax.broadcasted_iota(jnp.int32, t_a.shape, 1)
    first_a = lane < dh_a // 2
    r_a = pltpu.roll(t_a, dh_a // 2, 1)
    cos_a = jnp.where(first_a, t_a, r_a)
    sin_a = jnp.where(first_a, -r_a, t_a)
    first_i = (lane & (D_IDX - 1)) < D_IDX // 2
    c_i = jnp.where(first_i, t_i, pltpu.roll(t_i, D_IDX // 2, 1))
    s1_i = jnp.where(first_i, 0.0, t_i)
    s2_i = jnp.where(first_i, -pltpu.roll(t_i, LANES - D_IDX // 2, 1), 0.0)
    in_key = lane < D_IDX
    c_t = jnp.where(in_key, c_i, 1.0)
    s1_t = jnp.where(in_key, s1_i, 0.0)
    s2_t = jnp.where(in_key, s2_i, 0.0)

    def norm_rope(z, g_ref):
        outs = []
        for hh in range(mix_a // dh_a):
            zh = _rms(z[:, hh * dh_a:(hh + 1) * dh_a], g_ref[...])
            outs.append(zh * cos_a + pltpu.roll(zh, dh_a // 2, 1) * sin_a)
        return outs

    aq = norm_rope(proj(o_aq, mix_a), qn_ref)
    aqb_ref[...] = jnp.concatenate(aq, axis=1).astype(BF16)
    ak = norm_rope(proj(o_ak, mix_a), kn_ref)
    av = proj(o_av, mix_a)
    for hh in range(mix_a // dh_a):
        ak_ref[:, hh, :] = ak[hh]
        av_ref[:, hh, :] = av[:, hh * dh_a:(hh + 1) * dh_a]
    akb_ref[...] = jnp.concatenate(ak, axis=1).astype(BF16)
    avb_ref[...] = av.astype(BF16)

    ziq = proj(o_iq, H_IDX * D_IDX)
    cols = []
    for c in range(H_IDX * D_IDX // LANES):
        zc = ziq[:, c * LANES:(c + 1) * LANES]
        cols.append(zc * c_i + pltpu.roll(zc, D_IDX // 2, 1) * s1_i
                    + pltpu.roll(zc, LANES - D_IDX // 2, 1) * s2_i)
    iqb_ref[...] = jnp.concatenate(cols, axis=1).astype(BF16)

    zt = proj(o_tail, LANES) + bias_ref[...]
    zt = zt * c_t + pltpu.roll(zt, D_IDX // 2, 1) * s1_t + pltpu.roll(zt, LANES - D_IDX // 2, 1) * s2_t
    f_lo = D_IDX + H_IDX + H_M
    log_sig = jnp.minimum(zt, 0.0) - jnp.log(1.0 + jnp.exp(-jnp.abs(zt)))
    misc_ref[...] = jnp.where((lane >= f_lo) & (lane < f_lo + H_M), log_sig, zt)


def _in_proj(x2d, norm_mix, w_parts, bias_tail, q_norm, k_norm, tab, *, tm, tab_tiles, mix_m, mix_a):
    rows, d = x2d.shape
    dh_m = mix_m // H_M
    dh_a = mix_a // H_A
    once = lambda a: pl.BlockSpec(a.shape, lambda i: (0, 0), pipeline_mode=pl.Buffered(1))
    grid = (rows // tm,)
    row_spec = lambda wdt: pl.BlockSpec((tm, wdt), lambda i: (i, 0))
    const = lambda shp: pl.BlockSpec(shp, lambda i: (0, 0))
    out_shapes = [
        jax.ShapeDtypeStruct((rows, mix_m), F32),
        jax.ShapeDtypeStruct((rows, mix_m), F32),
        jax.ShapeDtypeStruct((rows, mix_m), F32),
        jax.ShapeDtypeStruct((rows, mix_m), F32),
        jax.ShapeDtypeStruct((rows, mix_a), BF16),
        jax.ShapeDtypeStruct((rows, H_A, dh_a), F32),
        jax.ShapeDtypeStruct((rows, H_A, dh_a), F32),
        jax.ShapeDtypeStruct((rows, mix_a), BF16),
        jax.ShapeDtypeStruct((rows, mix_a), BF16),
        jax.ShapeDtypeStruct((rows, H_IDX * D_IDX), BF16),
        jax.ShapeDtypeStruct((rows, LANES), F32),
    ]
    head_spec = pl.BlockSpec((tm, H_A, dh_a), lambda i: (i, 0, 0))
    out_specs = ([row_spec(mix_m)] * 4 + [row_spec(mix_a), head_spec, head_spec, row_spec(mix_a), row_spec(mix_a)]
                 + [row_spec(H_IDX * D_IDX), row_spec(LANES)])
    return pl.pallas_call(
        functools.partial(_in_proj_kernel, mix_m=mix_m, mix_a=mix_a, dh_m=dh_m, dh_a=dh_a),
        grid=grid,
        in_specs=[row_spec(d), const((1, d)), once(w_parts[0]), once(w_parts[1]), once(w_parts[2]),
                  const((1, LANES)),
                  const((1, dh_a)), const((1, dh_a)),
                  pl.BlockSpec((tm, 2 * LANES), lambda i: (i % tab_tiles, 0))],
        out_specs=out_specs,
        out_shape=out_shapes,
        compiler_params=_cparams(("parallel",)),
        name="in_proj",
    )(x2d, norm_mix, *w_parts, bias_tail, q_norm, k_norm, tab)


def _mlstm_prompt_kernel(q_ref, k_ref, v_ref, o_ref, grow_ref, gcol_ref, gain_ref,
                         y_ref, c_ref, n_ref, m_ref, cs_ref, ns_ref, ms_ref, *, chunk, d, nbp):
    c_idx = pl.program_id(1)
    L = chunk
    row_i = lax.broadcasted_iota(jnp.int32, (L, L), 0)
    col_i = lax.broadcasted_iota(jnp.int32, (L, L), 1)
    tril = col_i <= row_i
    triu = row_i <= col_i

    @pl.when(c_idx == 0)
    def _():
        cs_ref[...] = jnp.zeros_like(cs_ref)
        ns_ref[...] = jnp.zeros_like(ns_ref)
        ms_ref[...] = jnp.zeros_like(ms_ref)

    chains = [(bi, hd) for bi in range(nbp) for hd in range(H_M)]
    tril_b = tril.astype(BF16)
    csum = []
    for bi in range(nbp):
        g = gcol_ref[bi]
        g_hi = g.astype(BF16)
        r1 = g - g_hi.astype(F32)
        g_mid = r1.astype(BF16)
        g_lo = (r1 - g_mid.astype(F32)).astype(BF16)
        csum.append(_mm(tril_b, g_hi) + _mm(tril_b, g_mid) + _mm(tril_b, g_lo))
    st = []
    for bi, hd in chains:
        gr = grow_ref[bi, 0]
        gc = gcol_ref[bi]
        sidx = bi * H_M + hd
        hs = slice(hd * d, (hd + 1) * d)
        m = ms_ref[sidx, 0:1, 0:1]
        li_r = gr[hd:hd + 1, :]
        li_c = gc[:, hd:hd + 1]
        lf_c = gc[:, H_M + hd:H_M + hd + 1]
        b_c = csum[bi][:, H_M + hd:H_M + hd + 1]
        b_r = jnp.sum(jnp.where(triu, lf_c, 0.0), axis=0, keepdims=True)
        logd = jnp.where(tril, b_c - b_r + li_r, NEG_INF)
        inter = b_c + m
        m_t = jnp.maximum(inter, jnp.max(logd, axis=1, keepdims=True))
        st.append(dict(bi=bi, sidx=sidx, hs=hs, m=m, li_c=li_c, b_c=b_c, inter=inter, m_t=m_t,
                       dmat=jnp.exp(logd - m_t)))
    for c in st:
        q = q_ref[c["bi"], :, c["hs"]]
        c["q"] = q
        c["kb"] = k_ref[c["bi"], :, c["hs"]].astype(BF16)
        qb = q.astype(BF16)
        c["C"] = cs_ref[c["sidx"]]
        c["n"] = ns_ref[c["sidx"], 0:1, :]
        c["s"] = _nt(qb, c["kb"]) * c["dmat"]
        c_aug = jnp.concatenate([c["C"], jnp.broadcast_to(c["n"], (d, d))], axis=0).astype(BF16)
        c["qc"] = _nt(qb, c_aug)
    ones_ld = jnp.ones((L, d), BF16)
    for c in st:
        v = v_ref[c["bi"], :, c["hs"]]
        c["v"] = v
        g_inter = jnp.exp(c["inter"] - c["m_t"])
        sv = _mm(c["s"].astype(BF16), jnp.concatenate([v.astype(BF16), ones_ld], axis=1))
        num = g_inter * c["qc"][:, :d] + sv[:, :d]
        den = g_inter * c["qc"][:, d:] + sv[:, d:]
        h = num / jnp.maximum(jnp.abs(den), jnp.exp(-c["m_t"]))
        o = o_ref[c["bi"], :, c["hs"]]
        y_ref[c["bi"], :, c["hs"]] = _sigmoid(o) * _rms(h, gain_ref[:, c["hs"]])
    for c in st:
        m_new = c["m_t"][L - 1:L, :]
        b_last = c["b_c"][L - 1:L, :]
        g_prev = jnp.exp(b_last + c["m"] - m_new)
        w_c = jnp.exp(b_last - c["b_c"] + c["li_c"] - m_new)
        k = k_ref[c["bi"], :, c["hs"]]
        cs_ref[c["sidx"]] = g_prev * c["C"] + _tn((c["v"] * w_c).astype(BF16), c["kb"])
        ns_ref[c["sidx"], 0:1, :] = g_prev * c["n"] + jnp.sum(k * w_c, axis=0, keepdims=True)
        ms_ref[c["sidx"], 0:1, :] = jnp.broadcast_to(m_new, (1, LANES))

    @pl.when(c_idx == pl.num_programs(1) - 1)
    def _():
        for bi in range(nbp):
            for hd in range(H_M):
                sidx = bi * H_M + hd
                c_ref[bi, hd] = cs_ref[sidx]
                n_ref[bi, hd:hd + 1, :] = ns_ref[sidx, 0:1, :]
                m_ref[bi, hd:hd + 1, :] = ms_ref[sidx, 0:1, :]


def _mlstm_prompt(mq, mk, mv, mo, grow, gcol, gain, *, chunk):
    B, S, mix_m = mq.shape
    d = mix_m // H_M
    n_chunks = S // chunk
    nbp = 4 if B % 4 == 0 else (2 if B % 2 == 0 else 1)
    seq = pl.BlockSpec((nbp, chunk, mix_m), lambda b, c: (b, c, 0))
    return pl.pallas_call(
        functools.partial(_mlstm_prompt_kernel, chunk=chunk, d=d, nbp=nbp),
        grid=(B // nbp, n_chunks),
        in_specs=[seq, seq, seq, seq,
                  pl.BlockSpec((nbp, 1, 2 * H_M, chunk), lambda b, c: (b, c, 0, 0)),
                  pl.BlockSpec((nbp, chunk, 2 * H_M), lambda b, c: (b, c, 0)),
                  pl.BlockSpec((1, mix_m), lambda b, c: (0, 0))],
        out_specs=[seq,
                   pl.BlockSpec((nbp, H_M, d, d), lambda b, c: (b, 0, 0, 0)),
                   pl.BlockSpec((nbp, H_M, d), lambda b, c: (b, 0, 0)),
                   pl.BlockSpec((nbp, H_M, LANES), lambda b, c: (b, 0, 0))],
        out_shape=[jax.ShapeDtypeStruct((B, S, mix_m), F32),
                   jax.ShapeDtypeStruct((B, H_M, d, d), F32),
                   jax.ShapeDtypeStruct((B, H_M, d), F32),
                   jax.ShapeDtypeStruct((B, H_M, LANES), F32)],
        scratch_shapes=[pltpu.VMEM((nbp * H_M, d, d), F32), pltpu.VMEM((nbp * H_M, 8, d), F32),
                        pltpu.VMEM((nbp * H_M, 8, LANES), F32)],
        compiler_params=_cparams(("parallel", "arbitrary")),
        name="mlstm_prompt",
    )(mq, mk, mv, mo, grow, gcol, gain)


def _mlstm_sample_kernel(q_ref, k_ref, v_ref, o_ref, gs_ref, c_ref, n_ref, gain_ref,
                         y_ref, co_ref, no_ref, mo_ref, *, d):
    gs = gs_ref[0]
    eye = (lax.broadcasted_iota(jnp.int32, (d, d), 0) == lax.broadcasted_iota(jnp.int32, (d, d), 1))
    lane = lax.broadcasted_iota(jnp.int32, (1, LANES), 1)
    m_out = jnp.zeros((1, LANES), F32)
    for h in range(H_M):
        sl = slice(h * d, (h + 1) * d)
        q = q_ref[0, :, sl]
        k = k_ref[0, :, sl]
        v = v_ref[0, :, sl]
        o = o_ref[0, :, sl]
        li = gs[:, h:h + 1]
        lf = gs[:, H_M + h:H_M + h + 1]
        m = gs[:, 2 * H_M + h:2 * H_M + h + 1]
        C = c_ref[0, h]
        n = n_ref[0, :, sl]
        inter = lf + m
        m_t = jnp.maximum(inter, li)
        s = jnp.sum(q * k, axis=1, keepdims=True) * jnp.exp(li - m_t)
        g = jnp.exp(inter - m_t)
        q8 = jnp.broadcast_to(q, (8, d)).astype(BF16)
        cq = _nt(q8, C.astype(BF16))[0:1, :]
        num = g * cq + s * v
        den = g * jnp.sum(n * q, axis=1, keepdims=True) + s
        hh = num / jnp.maximum(jnp.abs(den), jnp.exp(-m_t))
        w = jnp.exp(li - m_t)
        v_col = jnp.sum(jnp.where(eye, v, 0.0), axis=1, keepdims=True)
        co_ref[0, h] = g * C + (w * v_col) * k
        no_ref[0, :, sl] = g * n + w * k
        m_out = jnp.where(lane == h, m_t, m_out)
        y_ref[0, :, sl] = _sigmoid(o) * _rms(hh, gain_ref[:, sl])
    mo_ref[0] = m_out


def _mlstm_sample(mq, mk, mv, mo, gs, c_state, n_state, gain):
    Bd, _, mix_m = mq.shape
    d = mix_m // H_M
    row = pl.BlockSpec((1, 1, mix_m), lambda b: (b, 0, 0))
    return pl.pallas_call(
        functools.partial(_mlstm_sample_kernel, d=d),
        grid=(Bd,),
        in_specs=[row, row, row, row,
                  pl.BlockSpec((1, 1, 3 * H_M), lambda b: (b, 0, 0)),
                  pl.BlockSpec((1, H_M, d, d), lambda b: (b, 0, 0, 0)),
                  row,
                  pl.BlockSpec((1, mix_m), lambda b: (0, 0))],
        out_specs=[row,
                   pl.BlockSpec((1, H_M, d, d), lambda b: (b, 0, 0, 0)),
                   row,
                   pl.BlockSpec((1, 1, LANES), lambda b: (b, 0, 0))],
        out_shape=[jax.ShapeDtypeStruct((Bd, 1, mix_m), F32),
                   jax.ShapeDtypeStruct((Bd, H_M, d, d), F32),
                   jax.ShapeDtypeStruct((Bd, 1, mix_m), F32),
                   jax.ShapeDtypeStruct((Bd, 1, LANES), F32)],
        compiler_params=_cparams(("parallel",)),
        name="mlstm_sample",
    )(mq, mk, mv, mo, gs, c_state, n_state, gain)


def _dsa_prompt_kernel(iq_ref, misc_ref, ikt_ref, aq_ref, ak_ref, av_ref, ya_ref, sc_ref, acc_ref,
                       *, tq, w, topk, dh, scale):
    i = pl.program_id(1)
    nk = ((i + 1) * tq + w - 1) // w
    kf = float(topk)
    nsub = w // LANES

    q_pos = i * tq + lax.broadcasted_iota(jnp.int32, (tq, 1), 0)
    lane_w = lax.broadcasted_iota(jnp.int32, (1, w), 1)
    iq = iq_ref[0]
    iq_h = [iq[:, h * D_IDX:(h + 1) * D_IDX] for h in range(H_IDX)]
    misc = misc_ref[0]
    w_h = [misc[:, D_IDX + h:D_IDX + h + 1] for h in range(H_IDX)]

    def score_body(c, carry, causal_edge):
        rmax, rmin = carry
        ikc = ikt_ref[0, c]
        score = jnp.zeros((tq, w), F32)
        for h in range(H_IDX):
            score = score + w_h[h] * jnp.maximum(_mm(iq_h[h], ikc), 0.0)
        if causal_edge:
            valid = (c * w + lane_w) <= q_pos
            sc_ref[c] = jnp.where(valid, score, NEG_INF)
            s_hi = jnp.where(valid, score, NEG_INF)
            s_lo = jnp.where(valid, score, POS_INF)
        else:
            sc_ref[c] = score
            s_hi = s_lo = score
        rmax = jnp.maximum(rmax, jnp.max(s_hi, axis=1, keepdims=True))
        rmin = jnp.minimum(rmin, jnp.min(s_lo, axis=1, keepdims=True))
        return rmax, rmin

    n_full = (i * tq) // w
    carry0 = (jnp.full((tq, 1), NEG_INF, F32), jnp.full((tq, 1), POS_INF, F32))
    carry0 = lax.fori_loop(0, n_full, functools.partial(score_body, causal_edge=False), carry0)
    rmax, rmin = lax.fori_loop(n_full, nk, functools.partial(score_body, causal_edge=True), carry0)

    ge = lambda x, t: x >= t
    gt = lambda x, t: x > t

    rh = min(tq, LANES)
    groups = [pl.ds(r0, rh) for r0 in range(0, tq, rh)]
    part = lambda a: [a[r0:r0 + rh] for r0 in range(0, tq, rh)]

    def pass_acc(rows, fn, init, combine):
        def body(c, acc):
            x = sc_ref[c, rows, :]
            for j in range(nsub):
                acc = combine(acc, fn(x[:, j * LANES:(j + 1) * LANES]))
            return acc
        return lax.fori_loop(0, nk, body, jnp.full((rh, LANES), init, F32))

    def count_acc(rows, pred, thr):
        thr_b = jnp.broadcast_to(thr, (rh, LANES))
        return pass_acc(rows, lambda x: jnp.where(pred(x, thr_b), 1.0, 0.0), 0.0, jnp.add)

    def count(rows, pred, thr):
        return jnp.sum(count_acc(rows, pred, thr), axis=1, keepdims=True)

    def min_where(rows, pred, thr):
        thr_b = jnp.broadcast_to(thr, (rh, LANES))
        acc = pass_acc(rows, lambda x: jnp.where(pred(x, thr_b), x, POS_INF), POS_INF, jnp.minimum)
        return jnp.min(acc, axis=1, keepdims=True)

    def bis_body(_, carry):
        los, his, clos = carry
        mids = [0.5 * (lo + hi) for lo, hi in zip(los, his)]
        accs = [count_acc(rows, ge, mid) for rows, mid in zip(groups, mids)]
        cms = [jnp.sum(a, axis=1, keepdims=True) for a in accs]
        oks = [cm >= kf for cm in cms]
        return (tuple(jnp.where(ok, mid, lo) for ok, mid, lo in zip(oks, mids, los)),
                tuple(jnp.where(ok, hi, mid) for ok, mid, hi in zip(oks, mids, his)),
                tuple(jnp.where(ok, cm, cl) for ok, cm, cl in zip(oks, cms, clos)))

    los, _, clos = lax.fori_loop(
        0, N_BISECT, bis_body,
        (tuple(part(rmin)), tuple(part(rmax + jnp.abs(rmax) + 1.0)), tuple(part((q_pos + 1).astype(F32)))))

    def finish_rows(rows, qp, rmin_h, lo, c_lo):
        active = (qp + 1) > topk
        unresolved = jnp.max(jnp.where(active & (c_lo != kf), 1.0, 0.0)) > 0.5

        @pl.when(jnp.logical_not(unresolved))
        def _():
            thr = jnp.where(active, lo, rmin_h)

            def body(c, _):
                sc_ref[c, rows, :] = jnp.where(sc_ref[c, rows, :] >= thr, 0.0, NEG_INF)
                return 0
            lax.fori_loop(0, nk, body, 0)

        @pl.when(unresolved)
        def _():
            tau = min_where(rows, ge, lo)
            g = count(rows, gt, tau)

            def undone(tau, g):
                return active & (g >= kf)

            def fix_cond(st):
                return jnp.max(jnp.where(undone(*st), 1.0, 0.0)) > 0.5

            def fix_body(st):
                tau, g = st
                nd = undone(tau, g)
                tau2 = jnp.where(nd, min_where(rows, gt, tau), tau)
                return tau2, jnp.where(nd, count(rows, gt, tau2), g)

            tau, g = lax.while_loop(fix_cond, fix_body, (tau, g))
            tau_b = jnp.broadcast_to(jnp.where(active, tau, rmin_h), (rh, LANES))
            need_b = jnp.broadcast_to(jnp.where(active, kf - g, 1e9), (rh, LANES))
            r_i = lax.broadcasted_iota(jnp.int32, (LANES, 2 * LANES), 0)
            c_i = lax.broadcasted_iota(jnp.int32, (LANES, 2 * LANES), 1)
            tri_ones = ((r_i <= c_i) | (c_i >= LANES)).astype(BF16)

            def body(c, run):
                x = sc_ref[c, rows, :]
                outs = []
                for j in range(nsub):
                    xj = x[:, j * LANES:(j + 1) * LANES]
                    is_eq = xj == tau_b
                    cnt2 = _mm(jnp.where(is_eq, 1.0, 0.0).astype(BF16), tri_ones)
                    sel = (xj > tau_b) | (is_eq & (cnt2[:, :LANES] + run <= need_b))
                    outs.append(jnp.where(sel, 0.0, NEG_INF))
                    run = run + cnt2[:, LANES:]
                sc_ref[c, rows, :] = jnp.concatenate(outs, axis=1)
                return run
            lax.fori_loop(0, nk, body, jnp.zeros((rh, LANES), F32))

    for rows, qp, rmin_h, lo, c_lo in zip(groups, part(q_pos), part(rmin), los, clos):
        finish_rows(rows, qp, rmin_h, lo, c_lo)

    aq = aq_ref[0]
    q_heads = [aq[:, h * dh:(h + 1) * dh] for h in range(H_A)]
    acc_ref[...] = jnp.zeros_like(acc_ref)
    c2 = scale * LOG2E
    ones_blk = jnp.ones((w, dh), BF16)

    def att_body(c, ms):
        k0 = pl.multiple_of(c * w, w)
        bias = sc_ref[c]
        heads = [slice(h * dh, (h + 1) * dh) for h in range(H_A)]
        ss = [_nt(q_heads[h], ak_ref[0, pl.ds(k0, w), heads[h]]) + bias for h in range(H_A)]
        ms_new = [jnp.maximum(ms[h], jnp.max(ss[h], axis=1, keepdims=True)) for h in range(H_A)]
        m_safe = [jnp.where(m == NEG_INF, 0.0, m) for m in ms_new]
        ps = [jnp.exp2((ss[h] - m_safe[h]) * c2).astype(BF16) for h in range(H_A)]
        for h in range(H_A):
            alpha = jnp.exp2((ms[h] - m_safe[h]) * c2)
            v_aug = jnp.concatenate([av_ref[0, pl.ds(k0, w), heads[h]], ones_blk], axis=1)
            acc_ref[h] = alpha * acc_ref[h] + _mm(ps[h], v_aug)
        return tuple(ms_new)

    lax.fori_loop(0, nk, att_body, tuple(jnp.full((tq, 1), NEG_INF, F32) for _ in range(H_A)))
    for h in range(H_A):
        a = acc_ref[h]
        ya_ref[0, :, h * dh:(h + 1) * dh] = a[:, :dh] / a[:, dh:]


def _dsa_prompt(iqb, misc, ikt, aqb, akb, avb, *, tq, w, topk):
    B, S, mix_a = aqb.shape
    dh = mix_a // H_A
    nq = S // tq
    nw = S // w
    return pl.pallas_call(
        functools.partial(_dsa_prompt_kernel, tq=tq, w=w, topk=topk, dh=dh, scale=dh ** -0.5),
        grid=(B, nq),
        in_specs=[pl.BlockSpec((1, tq, H_IDX * D_IDX), lambda b, i: (b, i, 0)),
                  pl.BlockSpec((1, tq, LANES), lambda b, i: (b, i, 0)),
                  pl.BlockSpec((1, nw, D_IDX, w), lambda b, i: (b, 0, 0, 0)),
                  pl.BlockSpec((1, tq, mix_a), lambda b, i: (b, i, 0)),
                  pl.BlockSpec((1, S, mix_a), lambda b, i: (b, 0, 0)),
                  pl.BlockSpec((1, S, mix_a), lambda b, i: (b, 0, 0))],
        out_specs=pl.BlockSpec((1, tq, mix_a), lambda b, i: (b, i, 0)),
        out_shape=jax.ShapeDtypeStruct((B, S, mix_a), F32),
        scratch_shapes=[pltpu.VMEM((nw, tq, w), F32), pltpu.VMEM((H_A, tq, 2 * dh), F32)],
        compiler_params=_cparams(("parallel", "arbitrary")),
        name="dsa_prompt",
    )(iqb, misc, ikt, aqb, akb, avb)


def _dsa_sample_select_kernel(pt_ref, iq_ref, w_ref, ikn_ref, ptv_ref, pool_ref, rows_ref, flag_ref,
                              ikbuf, sem, sc_ref, xn_ref, slot_ref, phys_ref,
                              *, n_pages, page, topk, cw):
    nb = iq_ref.shape[0]
    past = n_pages * page
    kf = float(topk)
    n_cw = past // cw

    def page_copy(bb, p, slot):
        return pltpu.make_async_copy(pool_ref.at[pt_ref[bb, p]],
                                     ikbuf.at[slot, :, pl.ds(p * page, page)],
                                     sem.at[slot])

    def start_all(bb, slot):
        def body(p, _):
            page_copy(bb, p, slot).start()
            return 0
        lax.fori_loop(0, n_pages, body, 0)

    start_all(0, 0)

    def score_body(b, _):
        slot = b % 2

        @pl.when(b + 1 < nb)
        def _():
            start_all(b + 1, 1 - slot)

        def wait_body(p, _):
            page_copy(b, p, slot).wait()
            return 0
        lax.fori_loop(0, n_pages, wait_body, 0)

        iq8 = iq_ref[b]
        w8 = w_ref[b]
        s8 = _mm(iq8, ikbuf[slot].astype(BF16))
        sc_ref[pl.ds(b, 1), :] = jnp.sum(w8 * jnp.maximum(s8, 0.0), axis=0, keepdims=True)
        ikn = ikn_ref[b].astype(BF16).astype(F32)
        sn8 = jnp.sum(iq8.astype(F32) * ikn, axis=1, keepdims=True)
        xn_b = jnp.sum(w8 * jnp.maximum(sn8, 0.0), axis=0, keepdims=True)
        xn_ref[pl.ds(b, 1), :] = jnp.broadcast_to(xn_b, (1, LANES))
        return 0

    lax.fori_loop(0, nb, score_body, 0)

    x = sc_ref[...]
    xn = xn_ref[:, 0:1]

    def cnt(mask_row, mask_new):
        return (jnp.sum(jnp.where(mask_row, 1.0, 0.0), axis=1, keepdims=True)
                + jnp.where(mask_new, 1.0, 0.0))

    rmax = jnp.maximum(jnp.max(x, axis=1, keepdims=True), xn)
    rmin = jnp.minimum(jnp.min(x, axis=1, keepdims=True), xn)
    hi0 = rmax + jnp.abs(rmax) + 1.0

    def bis_body(_, carry):
        lo, hi = carry
        mid = 0.5 * (lo + hi)
        ok = cnt(x >= mid, xn >= mid) >= kf
        return jnp.where(ok, mid, lo), jnp.where(ok, hi, mid)

    lo, _ = lax.fori_loop(0, N_BISECT, bis_body, (rmin, hi0))

    def min_where(mask_row, mask_new):
        return jnp.minimum(jnp.min(jnp.where(mask_row, x, POS_INF), axis=1, keepdims=True),
                           jnp.where(mask_new, xn, POS_INF))

    tau = min_where(x >= lo, xn >= lo)
    g = cnt(x > tau, xn > tau)

    def fix_cond(st):
        tau, g = st
        return jnp.max(jnp.where(g >= kf, 1.0, 0.0)) > 0.5

    def fix_body(st):
        tau, g = st
        tau2 = jnp.where(g >= kf, min_where(x > tau, xn > tau), tau)
        return tau2, cnt(x > tau2, xn > tau2)

    tau, g = lax.while_loop(fix_cond, fix_body, (tau, g))
    need = kf - g

    tri = (lax.broadcasted_iota(jnp.int32, (cw, cw), 0)
           < lax.broadcasted_iota(jnp.int32, (cw, cw), 1)).astype(BF16)

    def excl_prefix(flag):
        outs = []
        run = jnp.zeros((nb, 1), F32)
        for c in range(n_cw):
            f = flag[:, c * cw:(c + 1) * cw]
            outs.append(_mm(f.astype(BF16), tri) + run)
            run = run + jnp.sum(f, axis=1, keepdims=True)
        return jnp.concatenate(outs, axis=1), run

    is_eq = x == tau
    pre_eq, n_eq_past = excl_prefix(jnp.where(is_eq, 1.0, 0.0))
    sel = (x > tau) | (is_eq & (pre_eq < need))
    new_sel = (xn > tau) | ((xn == tau) & (n_eq_past < need))
    slot, _ = excl_prefix(jnp.where(sel, 1.0, 0.0))
    slot_ref[...] = jnp.where(sel, slot, -1.0)

    ptv = ptv_ref[...]
    jrow = lax.broadcasted_iota(jnp.int32, (1, past), 1)
    prow = lax.broadcasted_iota(jnp.int32, (n_pages, 1), 0)
    expand = ((jrow >= prow * page) & (jrow < (prow + 1) * page)).astype(BF16)
    digit_bits = 6
    pt_hi = _mm((ptv >> digit_bits).astype(F32).astype(BF16), expand)
    pt_lo = _mm((ptv & ((1 << digit_bits) - 1)).astype(F32).astype(BF16), expand)
    pidx = lax.broadcasted_iota(jnp.int32, (8, n_pages), 1).astype(F32).astype(BF16)
    pg = _mm(pidx, expand)[0:1, :]
    phys_ref[...] = (pt_hi * (1 << digit_bits) + pt_lo) * page + (jrow.astype(F32) - pg * page)

    slot_col = lax.broadcasted_iota(jnp.int32, (topk, 1), 0).astype(F32)
    lane_b = lax.broadcasted_iota(jnp.int32, (1, LANES), 1)

    def extract_body(b, out):
        srow = slot_ref[pl.ds(b, 1), :]
        frow = phys_ref[pl.ds(b, 1), :]
        acc = jnp.zeros((topk, LANES), F32)
        for c in range(past // LANES):
            cs = slice(c * LANES, (c + 1) * LANES)
            acc = acc + jnp.where(srow[:, cs] == slot_col, frow[:, cs], 0.0)
        return jnp.where(lane_b == b, jnp.sum(acc, axis=1, keepdims=True), out)

    out = lax.fori_loop(0, nb, extract_body, jnp.zeros((topk, LANES), F32))
    rows_ref[...] = out.astype(jnp.int32)
    flag_ref[...] = jnp.broadcast_to(jnp.where(new_sel, 1, 0), (nb, LANES)).astype(jnp.int32)


def _dsa_sample_select(page_table, iq8, w8, ik_new, pool_ik_t, *, topk, cw):
    Bd, n_pages = page_table.shape
    n_pool, d_idx, page = pool_ik_t.shape
    past = n_pages * page
    assert Bd <= LANES and n_pool <= 64 * 256
    full = lambda shp: pl.BlockSpec(shp, lambda i, pt: (0,) * len(shp))
    grid_spec = pltpu.PrefetchScalarGridSpec(
        num_scalar_prefetch=1,
        grid=(1,),
        in_specs=[full((Bd, 8, d_idx)), full((Bd, 8, 1)), full((Bd, 1, d_idx)), full((Bd, n_pages)),
                  pl.BlockSpec(memory_space=pl.ANY)],
        out_specs=[full((topk, LANES)), full((Bd, LANES))],
        scratch_shapes=[pltpu.VMEM((2, d_idx, past), F32),
                        pltpu.SemaphoreType.DMA((2,)),
                        pltpu.VMEM((Bd, past), F32),
                        pltpu.VMEM((Bd, LANES), F32),
                        pltpu.VMEM((Bd, past), F32),
                        pltpu.VMEM((Bd, past), F32)],
    )
    return pl.pallas_call(
        functools.partial(_dsa_sample_select_kernel, n_pages=n_pages, page=page, topk=topk, cw=cw),
        grid_spec=grid_spec,
        out_shape=[jax.ShapeDtypeStruct((topk, LANES), jnp.int32),
                   jax.ShapeDtypeStruct((Bd, LANES), jnp.int32)],
        compiler_params=_cparams(("arbitrary",)),
        name="dsa_sample_select",
    )(page_table, iq8, w8, ik_new, page_table, pool_ik_t)


def _dsa_sample_attend_kernel(rows_ref, flag_ref, aq_ref, knew_ref, vnew_ref, kpool_ref, vpool_ref,
                              ya_ref, kbuf, vbuf, sem, *, topk, dh, scale):
    b = pl.program_id(0)
    nb = pl.num_programs(0)

    def row_copies(bb, t, slot):
        r = rows_ref[bb, t]
        dst = pl.ds(t * H_A, H_A)
        return (pltpu.make_async_copy(kpool_ref.at[r], kbuf.at[slot, dst, :], sem.at[0, slot]),
                pltpu.make_async_copy(vpool_ref.at[r], vbuf.at[slot, dst, :], sem.at[1, slot]))

    def start_all(bb, slot):
        def body(t, _):
            ck, cv = row_copies(bb, t, slot)
            ck.start()
            cv.start()
            return 0
        lax.fori_loop(0, topk, body, 0, unroll=8)

    slot = b % 2

    @pl.when(b == 0)
    def _():
        start_all(0, 0)

    @pl.when(b + 1 < nb)
    def _():
        start_all(b + 1, 1 - slot)

    def wait_body(t, _):
        ck, cv = row_copies(b, t, slot)
        ck.wait()
        cv.wait()
        return 0
    lax.fori_loop(0, topk, wait_body, 0, unroll=8)

    take_new = (lax.broadcasted_iota(jnp.int32, (topk, 1), 0) == topk - 1) & (flag_ref[b] > 0)
    aq = aq_ref[0]
    for h in range(H_A):
        hs = slice(h * dh, (h + 1) * dh)
        kh = kbuf[slot, pl.ds(h, topk, stride=H_A), :]
        vh = vbuf[slot, pl.ds(h, topk, stride=H_A), :]
        kh = jnp.where(take_new, knew_ref[0, h:h + 1, :], kh).astype(BF16)
        vh = jnp.where(take_new, vnew_ref[0, h:h + 1, :], vh).astype(BF16)
        q8 = jnp.broadcast_to(aq[:, hs], (8, dh))
        s = _nt(q8, kh) * scale
        m = jnp.max(s, axis=1, keepdims=True)
        p = jnp.exp(s - m)
        p = p / jnp.sum(p, axis=1, keepdims=True)
        ya_ref[0, :, hs] = _mm(p.astype(BF16), vh)[0:1, :]


def _dsa_sample_attend(rows, flags, aqb, k_new, v_new, pool_k, pool_v, *, topk):
    Bd, _, mix_a = aqb.shape
    dh = pool_k.shape[2]
    new_spec = pl.BlockSpec((1, H_A, dh), lambda b, r, f: (b, 0, 0))
    grid_spec = pltpu.PrefetchScalarGridSpec(
        num_scalar_prefetch=2,
        grid=(Bd,),
        in_specs=[pl.BlockSpec((1, 1, mix_a), lambda b, r, f: (b, 0, 0)),
                  new_spec, new_spec,
                  pl.BlockSpec(memory_space=pl.ANY),
                  pl.BlockSpec(memory_space=pl.ANY)],
        out_specs=pl.BlockSpec((1, 1, mix_a), lambda b, r, f: (b, 0, 0)),
        scratch_shapes=[pltpu.VMEM((2, topk * H_A, dh), F32),
                        pltpu.VMEM((2, topk * H_A, dh), F32),
                        pltpu.SemaphoreType.DMA((2, 2))],
    )
    return pl.pallas_call(
        functools.partial(_dsa_sample_attend_kernel, topk=topk, dh=dh, scale=dh ** -0.5),
        grid_spec=grid_spec,
        out_shape=jax.ShapeDtypeStruct((Bd, 1, mix_a), F32),
        compiler_params=_cparams(("arbitrary",)),
        name="dsa_sample_attend",
    )(rows, flags, aqb, k_new, v_new, pool_k, pool_v)


def _mem_kv_kernel(mem_ref, nm_ref, wk_ref, wv_ref, kn_ref, k_ref, v_ref, *, dh):
    hm = _rms(mem_ref[...], nm_ref[...]).astype(BF16)
    kk = _mm(hm, wk_ref[...])
    vv = _mm(hm, wv_ref[...])
    for h in range(H_C):
        hs = slice(h * dh, (h + 1) * dh)
        k_ref[:, h, :] = _rms(kk[:, hs], kn_ref[...])
        v_ref[:, h, :] = vv[:, hs]


def _mem_kv(mem2d, norm_mem, w_ck, w_cv, ck_norm, *, tm):
    rows, d = mem2d.shape
    dh = d // H_C
    row = pl.BlockSpec((tm, d), lambda i: (i, 0))
    heads = pl.BlockSpec((tm, H_C, dh), lambda i: (i, 0, 0))
    const = lambda shp: pl.BlockSpec(shp, lambda i: (0, 0))
    return pl.pallas_call(
        functools.partial(_mem_kv_kernel, dh=dh),
        grid=(rows // tm,),
        in_specs=[row, const((1, d)), const((d, d)), const((d, d)), const((1, dh))],
        out_specs=[heads, heads],
        out_shape=[jax.ShapeDtypeStruct((rows, H_C, dh), F32)] * 2,
        compiler_params=_cparams(("parallel",)),
        name="mem_kv",
    )(mem2d, norm_mem, w_ck, w_cv, ck_norm)


def _out_cq_kernel(x_ref, ym_ref, ya_ref, wo_ref, nc_ref, wq_ref, qn_ref, x1_ref, qc_ref, *, mix_m, dh):
    upd = (_mm(ym_ref[...].astype(BF16), wo_ref[0:mix_m, :])
           + _mm(ya_ref[...].astype(BF16), wo_ref[mix_m:, :]))
    x1 = x_ref[...] + upd
    x1_ref[...] = x1
    hq = _mm(_rms(x1, nc_ref[...]).astype(BF16), wq_ref[...])
    for h in range(H_C):
        hs = slice(h * dh, (h + 1) * dh)
        qc_ref[:, hs] = _rms(hq[:, hs], qn_ref[...]).astype(BF16)


def _out_cq(x2d, ym, ya, w_out, norm_cross, w_cq, cq_norm, *, tm):
    rows, d = x2d.shape
    mix_m = ym.shape[1]
    mix_a = ya.shape[1]
    dh = d // H_C
    row = lambda wdt: pl.BlockSpec((tm, wdt), lambda i: (i, 0))
    const = lambda shp: pl.BlockSpec(shp, lambda i: (0, 0))
    return pl.pallas_call(
        functools.partial(_out_cq_kernel, mix_m=mix_m, dh=dh),
        grid=(rows // tm,),
        in_specs=[row(d), row(mix_m), row(mix_a), const((mix_m + mix_a, d)), const((1, d)),
                  const((d, d)), const((1, dh))],
        out_specs=[row(d), row(d)],
        out_shape=[jax.ShapeDtypeStruct((rows, d), F32), jax.ShapeDtypeStruct((rows, d), BF16)],
        compiler_params=_cparams(("parallel",)),
        name="out_cq",
    )(x2d, ym, ya, w_out, norm_cross, w_cq, cq_norm)


def _cross_kernel(q_ref, k_hbm, v_hbm, o_ref, kv_buf, sem, *, dh, scale):
    b = pl.program_id(0)
    t = pl.program_id(1)
    nb = pl.num_programs(0)
    slot = b % 2

    def head_copies(bb, sl):
        cps = []
        for h in range(H_C):
            cps.append(pltpu.make_async_copy(k_hbm.at[bb, :, h, :], kv_buf.at[sl, 0, h], sem.at[sl]))
            cps.append(pltpu.make_async_copy(v_hbm.at[bb, :, h, :], kv_buf.at[sl, 1, h], sem.at[sl]))
        return cps

    @pl.when(t == 0)
    def _():
        @pl.when(b == 0)
        def _():
            for cp in head_copies(0, 0):
                cp.start()

        @pl.when(b + 1 < nb)
        def _():
            for cp in head_copies(b + 1, 1 - slot):
                cp.start()

        for cp in head_copies(b, slot):
            cp.wait()

    q = q_ref[0]
    rows = q.shape[0]
    if rows < 8:
        q = jnp.broadcast_to(q, (8, q.shape[1]))
    for h in range(H_C):
        hs = slice(h * dh, (h + 1) * dh)
        kb = kv_buf[slot, 0, h].astype(BF16)
        vb = kv_buf[slot, 1, h].astype(BF16)
        s = _nt(q[:, hs], kb) * scale
        m = jnp.max(s, axis=1, keepdims=True)
        p = jnp.exp(s - m)
        p = p / jnp.sum(p, axis=1, keepdims=True)
        o = _mm(p.astype(BF16), vb)
        o_ref[0, :, hs] = o[0:rows].astype(BF16)


def _cross(qc, mem_k, mem_v, *, tq):
    B, T, d = qc.shape
    M = mem_k.shape[1]
    dh = d // H_C
    return pl.pallas_call(
        functools.partial(_cross_kernel, dh=dh, scale=dh ** -0.5),
        grid=(B, T // tq),
        in_specs=[pl.BlockSpec((1, tq, d), lambda b, t: (b, t, 0)),
                  pl.BlockSpec(memory_space=pl.ANY),
                  pl.BlockSpec(memory_space=pl.ANY)],
        out_specs=pl.BlockSpec((1, tq, d), lambda b, t: (b, t, 0)),
        out_shape=jax.ShapeDtypeStruct((B, T, d), BF16),
        scratch_shapes=[pltpu.VMEM((2, 2, H_C, M, dh), F32), pltpu.SemaphoreType.DMA((2,))],
        compiler_params=_cparams(("arbitrary", "arbitrary")),
        name="cross_attn",
    )(qc, mem_k, mem_v)


def _gelu_tanh(x):
    return 0.5 * x * (1.0 + jnp.tanh(np.sqrt(2.0 / np.pi) * (x + 0.044715 * (x * x * x))))


def _ffn_front(x1_ref, o_ref, wco_ref, nf_ref, x2_ref, hb_ref, acc_ref):
    x2 = x1_ref[0] + _mm(o_ref[0], wco_ref[...])
    x2_ref[...] = x2
    hb_ref[...] = _rms(x2, nf_ref[...]).astype(BF16)
    acc_ref[...] = jnp.zeros_like(acc_ref)


def _ffn_prompt_kernel(x1_ref, o_ref, wco_ref, nf_ref, wua_ref, wug_ref, cwa_ref, cwg_ref,
                       cba_ref, cbg_ref, wd_ref, ha_ref, hg_ref,
                       y_ref, ca_ref, cg_ref, x2_ref, hb_ref, acc_ref, carry_ref, *, tm, rs):
    t = pl.program_id(1)
    j = pl.program_id(2)
    nj = pl.num_programs(2)

    @pl.when(j == 0)
    def _():
        _ffn_front(x1_ref, o_ref, wco_ref, nf_ref, x2_ref, hb_ref, acc_ref)

    @pl.when(t == 0)
    def _():
        carry_ref[j, 0, 6:8, :] = ha_ref[0]
        carry_ref[j, 1, 6:8, :] = hg_ref[0]

    rid = lax.broadcasted_iota(jnp.int32, (rs, 1), 0)

    def conv_part(hb, part, wu_ref, cw_ref, cb_ref):
        u = _mm(hb, wu_ref[...])
        p2 = carry_ref[j, part, 6:7, :]
        p1 = carry_ref[j, part, 7:8, :]
        um1 = jnp.where(rid == 0, p1, pltpu.roll(u, 1, 0))
        um2 = jnp.where(rid == 0, p2, jnp.where(rid == 1, p1, pltpu.roll(u, 2, 0)))
        carry_ref[j, part] = u[rs - 8:rs, :]
        return cb_ref[...] + um2 * cw_ref[0:1, :] + um1 * cw_ref[1:2, :] + u * cw_ref[2:3, :]

    def sub_body(r, _):
        r0 = pl.multiple_of(r * rs, rs)
        hb = hb_ref[pl.ds(r0, rs), :]
        a = conv_part(hb, 0, wua_ref, cwa_ref, cba_ref)
        g = conv_part(hb, 1, wug_ref, cwg_ref, cbg_ref)
        acc_ref[pl.ds(r0, rs), :] += _mm((_gelu_tanh(g) * a).astype(BF16), wd_ref[...])
        return 0

    lax.fori_loop(0, tm // rs, sub_body, 0)
    ca_ref[0, 0] = carry_ref[j, 0, 6:8, :]
    cg_ref[0, 0] = carry_ref[j, 1, 6:8, :]

    @pl.when(j == nj - 1)
    def _():
        y_ref[0] = x2_ref[...] + acc_ref[...]


def _ffn_prompt(x1, o, w_co, norm_ffn, w_up, conv_w, conv_b, w_down, hist, *, tm, tf):
    B, T, d = x1.shape
    d_ff = w_down.shape[0]
    nj = d_ff // tf
    nt = T // tm
    idx3 = lambda b, t, j: (b, t, 0)
    c2 = lambda shp: pl.BlockSpec(shp, lambda b, t, j: (0, 0))
    return pl.pallas_call(
        functools.partial(_ffn_prompt_kernel, tm=tm, rs=min(512, tm)),
        grid=(B, nt, nj),
        in_specs=[pl.BlockSpec((1, tm, d), idx3), pl.BlockSpec((1, tm, d), idx3),
                  pl.BlockSpec((d, d), lambda b, t, j: (0, 0), pipeline_mode=pl.Buffered(1)), c2((1, d)),
                  pl.BlockSpec((d, tf), lambda b, t, j: (0, j)),
                  pl.BlockSpec((d, tf), lambda b, t, j: (0, nj + j)),
                  pl.BlockSpec((CONV_W, tf), lambda b, t, j: (0, j)),
                  pl.BlockSpec((CONV_W, tf), lambda b, t, j: (0, nj + j)),
                  pl.BlockSpec((1, tf), lambda b, t, j: (0, j)),
                  pl.BlockSpec((1, tf), lambda b, t, j: (0, nj + j)),
                  pl.BlockSpec((tf, d), lambda b, t, j: (j, 0)),
                  pl.BlockSpec((1, CONV_W - 1, tf), lambda b, t, j: (b, 0, j)),
                  pl.BlockSpec((1, CONV_W - 1, tf), lambda b, t, j: (b, 0, nj + j))],
        out_specs=[pl.BlockSpec((1, tm, d), idx3),
                   pl.BlockSpec((1, 1, CONV_W - 1, tf), lambda b, t, j: (b, t, 0, j)),
                   pl.BlockSpec((1, 1, CONV_W - 1, tf), lambda b, t, j: (b, t, 0, j))],
        out_shape=[jax.ShapeDtypeStruct((B, T, d), F32),
                   jax.ShapeDtypeStruct((B, nt, CONV_W - 1, d_ff), F32),
                   jax.ShapeDtypeStruct((B, nt, CONV_W - 1, d_ff), F32)],
        scratch_shapes=[pltpu.VMEM((tm, d), F32), pltpu.VMEM((tm, d), BF16), pltpu.VMEM((tm, d), F32),
                        pltpu.VMEM((nj, 2, 8, tf), F32)],
        compiler_params=_cparams(("arbitrary", "arbitrary", "arbitrary")),
        name="ffn_prompt",
    )(x1, o, w_co, norm_ffn, w_up, w_up, conv_w, conv_w, conv_b, conv_b, w_down, hist, hist)


def _ffn_sample_kernel(x1_ref, o_ref, wco_ref, nf_ref, wua_ref, wug_ref, cwa_ref, cwg_ref,
                       cba_ref, cbg_ref, wd_ref, h0a_ref, h0g_ref, h1a_ref, h1g_ref,
                       y_ref, ua_ref, ug_ref, x2_ref, hb_ref, acc_ref):
    j = pl.program_id(0)
    nj = pl.num_programs(0)

    @pl.when(j == 0)
    def _():
        _ffn_front(x1_ref, o_ref, wco_ref, nf_ref, x2_ref, hb_ref, acc_ref)

    hb = hb_ref[...]

    def conv_part(wu_ref, cw_ref, cb_ref, h0_ref, h1_ref, u_out_ref):
        u = _mm(hb, wu_ref[...])
        u_out_ref[...] = u
        return cb_ref[...] + h0_ref[...] * cw_ref[0:1, :] + h1_ref[...] * cw_ref[1:2, :] + u * cw_ref[2:3, :]

    a = conv_part(wua_ref, cwa_ref, cba_ref, h0a_ref, h1a_ref, ua_ref)
    g = conv_part(wug_ref, cwg_ref, cbg_ref, h0g_ref, h1g_ref, ug_ref)
    acc_ref[...] += _mm((_gelu_tanh(g) * a).astype(BF16), wd_ref[...])

    @pl.when(j == nj - 1)
    def _():
        y_ref[0] = x2_ref[...] + acc_ref[...]


def _ffn_sample(x1, o, w_co, norm_ffn, w_up, conv_w, conv_b, w_down, h0, h1, *, tf):
    _, rows, d = x1.shape
    d_ff = w_down.shape[0]
    nj = d_ff // tf
    c2 = lambda shp: pl.BlockSpec(shp, lambda j: (0, 0))
    c3 = lambda shp: pl.BlockSpec(shp, lambda j: (0, 0, 0))
    col_a = lambda r: pl.BlockSpec((r, tf), lambda j: (0, j))
    col_g = lambda r: pl.BlockSpec((r, tf), lambda j: (0, nj + j))
    return pl.pallas_call(
        _ffn_sample_kernel,
        grid=(nj,),
        in_specs=[c3((1, rows, d)), c3((1, rows, d)), c2((d, d)), c2((1, d)),
                  col_a(d), col_g(d), col_a(CONV_W), col_g(CONV_W), col_a(1), col_g(1),
                  pl.BlockSpec((tf, d), lambda j: (j, 0)),
                  col_a(rows), col_g(rows), col_a(rows), col_g(rows)],
        out_specs=[c3((1, rows, d)), col_a(rows), col_a(rows)],
        out_shape=[jax.ShapeDtypeStruct((1, rows, d), F32),
                   jax.ShapeDtypeStruct((rows, d_ff), F32),
                   jax.ShapeDtypeStruct((rows, d_ff), F32)],
        scratch_shapes=[pltpu.VMEM((rows, d), F32), pltpu.VMEM((rows, d), BF16), pltpu.VMEM((rows, d), F32)],
        compiler_params=_cparams(("arbitrary",)),
        name="ffn_sample",
    )(x1, o, w_co, norm_ffn, w_up, w_up, conv_w, conv_w, conv_b, conv_b, w_down, h0, h0, h1, h1)


def _tiles(seq, mem_rows, d_ff):
    half_ff = d_ff // 2
    return dict(
        in_rows=min(256, seq),
        dsa_q=min(512, seq),
        dsa_k=min(512, seq),
        mem_rows=min(256, mem_rows),
        out_rows=min(1024, seq),
        cross_q=min(1024, seq),
        ffn_rows=min(512, seq),
        ffn_cols=half_ff if half_ff % LANES == 0 else d_ff,
    )


def _rope_tables(pos, dh_a):
    assert dh_a == LANES and 2 * D_IDX == LANES and D_IDX & (D_IDX - 1) == 0
    posf = pos.astype(F32)[:, None]
    half_a = dh_a // 2
    inv_a = ROPE_THETA ** (-jnp.arange(half_a, dtype=F32) / half_a)
    ang_a = posf * inv_a[None, :]
    half_i = D_IDX // 2
    inv_i = ROPE_THETA ** (-jnp.arange(half_i, dtype=F32) / half_i)
    ang_i = posf * inv_i[None, :]
    cos_i, sin_i = jnp.cos(ang_i), jnp.sin(ang_i)
    return jnp.concatenate([jnp.cos(ang_a), jnp.sin(ang_a), cos_i, sin_i, cos_i, sin_i], axis=1)


def kernel(x_prompt, x_sample, mem_prompt, cache_k, cache_v, cache_idx_k, cache_mem_k, cache_mem_v,
           state_mlstm_c, state_mlstm_n, state_mlstm_m, state_conv, page_table,
           norm_mix, w_in, b_if, mlstm_norm, q_norm, k_norm, w_out, norm_cross, norm_mem,
           w_cq, w_ck, w_cv, w_co, cq_norm, ck_norm, norm_ffn, w_up, conv_w, conv_b, w_down):
    B, S, D = x_prompt.shape
    Bd, T, _ = x_sample.shape
    assert T == 1 and w_in.shape[0] == 1
    n_pool, page = cache_k.shape[1], cache_k.shape[2]
    n_pages = page_table.shape[1]
    past = n_pages * page
    mix_m = mlstm_norm.shape[1]
    dh_m = mix_m // H_M
    dh_a = q_norm.shape[1]
    mix_a = H_A * dh_a
    d_ff = w_down.shape[1]
    M = mem_prompt.shape[1]
    chunk = min(128, S)
    topk_p = min(TOPK_MAX, S // 4)
    topk_s = min(TOPK_MAX, (past + T) // 4)

    w = w_in[0]
    o_gate = 4 * mix_m
    o_aq = o_gate + 2 * H_M
    o_iq = o_aq + 3 * mix_a
    o_ik = o_iq + H_IDX * D_IDX
    o_iw = o_ik + D_IDX
    tail_pad = LANES - (D_IDX + H_IDX + 2 * H_M)
    w_r = (w[:, :o_gate].astype(BF16), w[:, o_aq:o_ik].astype(BF16),
           jnp.concatenate([w[:, o_ik:o_iw + H_IDX], w[:, o_gate:o_aq],
                            jnp.zeros((D, tail_pad), w.dtype)], axis=1).astype(BF16))
    bias_tail = jnp.concatenate([jnp.zeros((D_IDX + H_IDX,), F32), b_if[0].astype(F32),
                                 jnp.zeros((tail_pad,), F32)])[None, :]
    w_out_b = w_out[0].astype(BF16)
    w_cq_b, w_ck_b, w_cv_b, w_co_b = (a[0].astype(BF16) for a in (w_cq, w_ck, w_cv, w_co))
    w_up_b = w_up[0].astype(BF16)
    w_down_b = w_down[0].astype(BF16)
    row = lambda a: a[0][None, :]

    def split_misc(misc):
        ik = misc[:, :D_IDX]
        li = misc[:, D_IDX + H_IDX:D_IDX + H_IDX + H_M]
        lf = misc[:, D_IDX + H_IDX + H_M:D_IDX + H_IDX + 2 * H_M]
        return ik, li, lf

    tiles = _tiles(S, B * M, d_ff)
    tm_in = tiles["in_rows"]
    tab_p = _rope_tables(jnp.arange(S), dh_a)
    (mq, mk, mv, mo, aqb, ak, av, akb, avb, iqb, misc) = _in_proj(
        x_prompt.reshape(B * S, D), row(norm_mix), w_r, bias_tail, row(q_norm), row(k_norm), tab_p,
        tm=tm_in, tab_tiles=S // tm_in, mix_m=mix_m, mix_a=mix_a)
    ik_p = misc[:, :D_IDX]
    r3 = lambda a: a.reshape(B, S, a.shape[-1])
    gcol = misc[:, D_IDX + H_IDX:D_IDX + H_IDX + 2 * H_M].reshape(B, S, 2 * H_M)
    grow = gcol.reshape(B, S // chunk, chunk, 2 * H_M).transpose(0, 1, 3, 2)
    y_m, c_p, n_p, m_p = _mlstm_prompt(r3(mq), r3(mk), r3(mv), r3(mo), grow, gcol, row(mlstm_norm), chunk=chunk)

    wk = tiles["dsa_k"]
    ikt = ik_p.astype(BF16).reshape(B, S // wk, wk, D_IDX).transpose(0, 1, 3, 2)
    y_a = _dsa_prompt(r3(iqb), r3(misc), ikt, r3(aqb), r3(akb), r3(avb), tq=tiles["dsa_q"], w=wk, topk=topk_p)

    mk_p, mv_p = _mem_kv(mem_prompt.reshape(B * M, D), row(norm_mem), w_ck_b, w_cv_b, row(ck_norm),
                         tm=tiles["mem_rows"])
    x1, qc = _out_cq(x_prompt.reshape(B * S, D), y_m.reshape(B * S, mix_m), y_a.reshape(B * S, mix_a),
                     w_out_b, row(norm_cross), w_cq_b, row(cq_norm), tm=tiles["out_rows"])
    dh_c = D // H_C
    o_c = _cross(qc.reshape(B, S, D), mk_p.reshape(B, M, H_C, dh_c), mv_p.reshape(B, M, H_C, dh_c),
                 tq=tiles["cross_q"])
    tf = tiles["ffn_cols"]
    xp, conv_a, conv_g = _ffn_prompt(x1.reshape(B, S, D), o_c, w_co_b, row(norm_ffn), w_up_b, conv_w[0],
                                     conv_b[0][None, :], w_down_b,
                                     jnp.zeros((B, CONV_W - 1, 2 * d_ff), F32), tm=tiles["ffn_rows"], tf=tf)
    conv_p = jnp.concatenate([conv_a[:, -1], conv_g[:, -1]], axis=-1)

    tab_s = jnp.broadcast_to(_rope_tables(jnp.full((1,), past, jnp.int32), dh_a), (Bd, 2 * LANES))
    (mq_s, mk_s, mv_s, mo_s, aqb_s, ak_s, av_s, _, _, iqb_s, misc_s) = _in_proj(
        x_sample.reshape(Bd, D), row(norm_mix), w_r, bias_tail, row(q_norm), row(k_norm), tab_s,
        tm=Bd, tab_tiles=1, mix_m=mix_m, mix_a=mix_a)
    ik_s, li_s, lf_s = split_misc(misc_s)
    gs = jnp.concatenate([li_s, lf_s, state_mlstm_m[0].astype(F32)], axis=-1)[:, None, :]
    e1 = lambda a: a[:, None, :]
    y_ms, c_s, n_s, m_s = _mlstm_sample(e1(mq_s), e1(mk_s), e1(mv_s), e1(mo_s), gs,
                                        state_mlstm_c[0], state_mlstm_n[0].reshape(Bd, 1, mix_m),
                                        row(mlstm_norm))

    iq8 = jnp.pad(iqb_s.reshape(Bd, H_IDX, D_IDX), ((0, 0), (0, 8 - H_IDX), (0, 0)))
    w8 = jnp.pad(misc_s[:, D_IDX:D_IDX + H_IDX], ((0, 0), (0, 8 - H_IDX)))[:, :, None]
    assert n_pages <= 256
    rows_t, flags = _dsa_sample_select(page_table, iq8, w8, e1(ik_s), jnp.swapaxes(cache_idx_k[0], 1, 2),
                                       topk=topk_s, cw=min(512, past))
    y_as = _dsa_sample_attend(rows_t[:, :Bd].T, flags[:, 0], e1(aqb_s), ak_s, av_s,
                              cache_k[0].reshape(n_pool * page, H_A, dh_a),
                              cache_v[0].reshape(n_pool * page, H_A, dh_a), topk=topk_s)

    x1_s, qc_s = _out_cq(x_sample.reshape(Bd, D), y_ms.reshape(Bd, mix_m), y_as.reshape(Bd, mix_a),
                         w_out_b, row(norm_cross), w_cq_b, row(cq_norm), tm=Bd)
    o_s = _cross(qc_s.reshape(Bd, 1, D), cache_mem_k[0], cache_mem_v[0], tq=1)
    xs, u_a, u_g = _ffn_sample(x1_s.reshape(1, Bd, D), o_s.reshape(1, Bd, D), w_co_b, row(norm_ffn), w_up_b,
                               conv_w[0], conv_b[0][None, :], w_down_b,
                               state_conv[0, :, 0, :], state_conv[0, :, 1, :], tf=tf)
    conv_s = jnp.stack([state_conv[0, :, 1, :], jnp.concatenate([u_a, u_g], axis=-1)], axis=1)

    lead = lambda a: a[None]
    return (xp, xs.reshape(Bd, 1, D),
            lead(ak.reshape(B, S, H_A, dh_a)), lead(av.reshape(B, S, H_A, dh_a)), lead(ik_p.reshape(B, S, D_IDX)),
            lead(c_p), lead(n_p), lead(m_p[:, :, 0]),
            lead(mk_p.reshape(B, M, H_C, D // H_C)), lead(mv_p.reshape(B, M, H_C, D // H_C)), lead(conv_p),
            lead(ak_s.reshape(Bd, 1, H_A, dh_a)), lead(av_s.reshape(Bd, 1, H_A, dh_a)),
            lead(ik_s.reshape(Bd, 1, D_IDX)),
            lead(c_s), lead(n_s.reshape(Bd, H_M, dh_m)), lead(m_s[:, 0, :H_M]), lead(conv_s))
```

```python
import functools

import jax
import jax.numpy as jnp
import numpy as np
from jax import lax
from jax.experimental import pallas as pl
from jax.experimental.pallas import tpu as pltpu

F32 = jnp.float32
BF16 = jnp.bfloat16

H_M = 4
H_A = 4
H_IDX = 4
D_IDX = 64
H_C = 4
TOPK_MAX = 256
CONV_W = 3
ROPE_THETA = 10000.0
EPS = 1e-6
LOG2E = 1.4426950408889634
NEG_INF = float("-inf")
POS_INF = float("inf")

LANES = 128
VMEM_LIMIT = 56 * 1024 * 1024
N_BISECT = 20


def _cparams(sem):
    return pltpu.CompilerParams(dimension_semantics=sem, vmem_limit_bytes=VMEM_LIMIT)


def _nt(a, b):
    return lax.dot_general(a, b, (((1,), (1,)), ((), ())), preferred_element_type=F32)


def _tn(a, b):
    return lax.dot_general(a, b, (((0,), (0,)), ((), ())), preferred_element_type=F32)


def _mm(a, b):
    return jnp.dot(a, b, preferred_element_type=F32)


def _rms(x, g):
    ms = jnp.mean(x * x, axis=-1, keepdims=True)
    return x * lax.rsqrt(ms + EPS) * g


def _sigmoid(x):
    return 1.0 / (1.0 + jnp.exp(-x))


def _in_proj_kernel(x_ref, nm_ref, wm_ref, wa_ref, wt_ref, bias_ref, qn_ref, kn_ref, tab_ref,
                    mq_ref, mk_ref, mv_ref, mo_ref, aqb_ref, ak_ref, av_ref, akb_ref, avb_ref,
                    iqb_ref, misc_ref, *, mix_m, mix_a, dh_m, dh_a):
    h = _rms(x_ref[...], nm_ref[...]).astype(BF16)

    o_mq, o_mk, o_mv, o_mo = 0, mix_m, 2 * mix_m, 3 * mix_m
    n_m = 4 * mix_m
    o_aq = n_m
    o_ak = o_aq + mix_a
    o_av = o_ak + mix_a
    o_iq = o_av + mix_a
    o_tail = o_iq + H_IDX * D_IDX

    def proj(lo, width):
        if lo < n_m:
            return _mm(h, wm_ref[:, lo:lo + width])
        if lo < o_tail:
            return _mm(h, wa_ref[:, lo - n_m:lo - n_m + width])
        return _mm(h, wt_ref[...])

    mq_ref[...] = proj(o_mq, mix_m)
    mk_ref[...] = proj(o_mk, mix_m) * (dh_m ** -0.5)
    mv_ref[...] = proj(o_mv, mix_m)
    mo_ref[...] = proj(o_mo, mix_m)

    t_a = tab_ref[:, 0:LANES]
    t_i = tab_ref[:, LANES:2 * LANES]
    lane = lax.broadcasted_iota(jnp.int32, t_a.shape, 1)
    first_a = lane < dh_a // 2
    r_a = pltpu.roll(t_a, dh_a // 2, 1)
    cos_a = jnp.where(first_a, t_a, r_a)
    sin_a = jnp.where(first_a, -r_a, t_a)
    first_i = (lane & (D_IDX - 1)) < D_IDX // 2
    c_i = jnp.where(first_i, t_i, pltpu.roll(t_i, D_IDX // 2, 1))
    s1_i = jnp.where(first_i, 0.0, t_i)
    s2_i = jnp.where(first_i, -pltpu.roll(t_i, LANES - D_IDX // 2, 1), 0.0)
    in_key = lane < D_IDX
    c_t = jnp.where(in_key, c_i, 1.0)
    s1_t = jnp.where(in_key, s1_i, 0.0)
    s2_t = jnp.where(in_key, s2_i, 0.0)

    def norm_rope(z, g_ref):
        outs = []
        for hh in range(mix_a // dh_a):
            zh = _rms(z[:, hh * dh_a:(hh + 1) * dh_a], g_ref[...])
            outs.append(zh * cos_a + pltpu.roll(zh, dh_a // 2, 1) * sin_a)
        return outs

    aq = norm_rope(proj(o_aq, mix_a), qn_ref)
    aqb_ref[...] = jnp.concatenate(aq, axis=1).astype(BF16)
    ak = norm_rope(proj(o_ak, mix_a), kn_ref)
    av = proj(o_av, mix_a)
    for hh in range(mix_a // dh_a):
        ak_ref[:, hh, :] = ak[hh]
        av_ref[:, hh, :] = av[:, hh * dh_a:(hh + 1) * dh_a]
    akb_ref[...] = jnp.concatenate(ak, axis=1).astype(BF16)
    avb_ref[...] = av.astype(BF16)

    ziq = proj(o_iq, H_IDX * D_IDX)
    cols = []
    for c in range(H_IDX * D_IDX // LANES):
        zc = ziq[:, c * LANES:(c + 1) * LANES]
        cols.append(zc * c_i + pltpu.roll(zc, D_IDX // 2, 1) * s1_i
                    + pltpu.roll(zc, LANES - D_IDX // 2, 1) * s2_i)
    iqb_ref[...] = jnp.concatenate(cols, axis=1).astype(BF16)

    zt = proj(o_tail, LANES) + bias_ref[...]
    zt = zt * c_t + pltpu.roll(zt, D_IDX // 2, 1) * s1_t + pltpu.roll(zt, LANES - D_IDX // 2, 1) * s2_t
    f_lo = D_IDX + H_IDX + H_M
    log_sig = jnp.minimum(zt, 0.0) - jnp.log(1.0 + jnp.exp(-jnp.abs(zt)))
    misc_ref[...] = jnp.where((lane >= f_lo) & (lane < f_lo + H_M), log_sig, zt)


def _in_proj(x2d, norm_mix, w_parts, bias_tail, q_norm, k_norm, tab, *, tm, tab_tiles, mix_m, mix_a):
    rows, d = x2d.shape
    dh_m = mix_m // H_M
    dh_a = mix_a // H_A
    once = lambda a: pl.BlockSpec(a.shape, lambda i: (0, 0), pipeline_mode=pl.Buffered(1))
    grid = (rows // tm,)
    row_spec = lambda wdt: pl.BlockSpec((tm, wdt), lambda i: (i, 0))
    const = lambda shp: pl.BlockSpec(shp, lambda i: (0, 0))
    out_shapes = [
        jax.ShapeDtypeStruct((rows, mix_m), F32),
        jax.ShapeDtypeStruct((rows, mix_m), F32),
        jax.ShapeDtypeStruct((rows, mix_m), F32),
        jax.ShapeDtypeStruct((rows, mix_m), F32),
        jax.ShapeDtypeStruct((rows, mix_a), BF16),
        jax.ShapeDtypeStruct((rows, H_A, dh_a), F32),
        jax.ShapeDtypeStruct((rows, H_A, dh_a), F32),
        jax.ShapeDtypeStruct((rows, mix_a), BF16),
        jax.ShapeDtypeStruct((rows, mix_a), BF16),
        jax.ShapeDtypeStruct((rows, H_IDX * D_IDX), BF16),
        jax.ShapeDtypeStruct((rows, LANES), F32),
    ]
    head_spec = pl.BlockSpec((tm, H_A, dh_a), lambda i: (i, 0, 0))
    out_specs = ([row_spec(mix_m)] * 4 + [row_spec(mix_a), head_spec, head_spec, row_spec(mix_a), row_spec(mix_a)]
                 + [row_spec(H_IDX * D_IDX), row_spec(LANES)])
    return pl.pallas_call(
        functools.partial(_in_proj_kernel, mix_m=mix_m, mix_a=mix_a, dh_m=dh_m, dh_a=dh_a),
        grid=grid,
        in_specs=[row_spec(d), const((1, d)), once(w_parts[0]), once(w_parts[1]), once(w_parts[2]),
                  const((1, LANES)),
                  const((1, dh_a)), const((1, dh_a)),
                  pl.BlockSpec((tm, 2 * LANES), lambda i: (i % tab_tiles, 0))],
        out_specs=out_specs,
        out_shape=out_shapes,
        compiler_params=_cparams(("parallel",)),
        name="in_proj",
    )(x2d, norm_mix, *w_parts, bias_tail, q_norm, k_norm, tab)


def _mlstm_prompt_kernel(q_ref, k_ref, v_ref, o_ref, grow_ref, gcol_ref, gain_ref,
                         y_ref, c_ref, n_ref, m_ref, cs_ref, ns_ref, ms_ref, *, chunk, d, nbp):
    c_idx = pl.program_id(1)
    L = chunk
    row_i = lax.broadcasted_iota(jnp.int32, (L, L), 0)
    col_i = lax.broadcasted_iota(jnp.int32, (L, L), 1)
    tril = col_i <= row_i
    triu = row_i <= col_i

    @pl.when(c_idx == 0)
    def _():
        cs_ref[...] = jnp.zeros_like(cs_ref)
        ns_ref[...] = jnp.zeros_like(ns_ref)
        ms_ref[...] = jnp.zeros_like(ms_ref)

    chains = [(bi, hd) for bi in range(nbp) for hd in range(H_M)]
    tril_b = tril.astype(BF16)
    csum = []
    for bi in range(nbp):
        g = gcol_ref[bi]
        g_hi = g.astype(BF16)
        r1 = g - g_hi.astype(F32)
        g_mid = r1.astype(BF16)
        g_lo = (r1 - g_mid.astype(F32)).astype(BF16)
        csum.append(_mm(tril_b, g_hi) + _mm(tril_b, g_mid) + _mm(tril_b, g_lo))
    st = []
    for bi, hd in chains:
        gr = grow_ref[bi, 0]
        gc = gcol_ref[bi]
        sidx = bi * H_M + hd
        hs = slice(hd * d, (hd + 1) * d)
        m = ms_ref[sidx, 0:1, 0:1]
        li_r = gr[hd:hd + 1, :]
        li_c = gc[:, hd:hd + 1]
        lf_c = gc[:, H_M + hd:H_M + hd + 1]
        b_c = csum[bi][:, H_M + hd:H_M + hd + 1]
        b_r = jnp.sum(jnp.where(triu, lf_c, 0.0), axis=0, keepdims=True)
        logd = jnp.where(tril, b_c - b_r + li_r, NEG_INF)
        inter = b_c + m
        m_t = jnp.maximum(inter, jnp.max(logd, axis=1, keepdims=True))
        st.append(dict(bi=bi, sidx=sidx, hs=hs, m=m, li_c=li_c, b_c=b_c, inter=inter, m_t=m_t,
                       dmat=jnp.exp(logd - m_t)))
    for c in st:
        q = q_ref[c["bi"], :, c["hs"]]
        c["q"] = q
        c["kb"] = k_ref[c["bi"], :, c["hs"]].astype(BF16)
        qb = q.astype(BF16)
        c["C"] = cs_ref[c["sidx"]]
        c["n"] = ns_ref[c["sidx"], 0:1, :]
        c["s"] = _nt(qb, c["kb"]) * c["dmat"]
        c_aug = jnp.concatenate([c["C"], jnp.broadcast_to(c["n"], (d, d))], axis=0).astype(BF16)
        c["qc"] = _nt(qb, c_aug)
    ones_ld = jnp.ones((L, d), BF16)
    for c in st:
        v = v_ref[c["bi"], :, c["hs"]]
        c["v"] = v
        g_inter = jnp.exp(c["inter"] - c["m_t"])
        sv = _mm(c["s"].astype(BF16), jnp.concatenate([v.astype(BF16), ones_ld], axis=1))
        num = g_inter * c["qc"][:, :d] + sv[:, :d]
        den = g_inter * c["qc"][:, d:] + sv[:, d:]
        h = num / jnp.maximum(jnp.abs(den), jnp.exp(-c["m_t"]))
        o = o_ref[c["bi"], :, c["hs"]]
        y_ref[c["bi"], :, c["hs"]] = _sigmoid(o) * _rms(h, gain_ref[:, c["hs"]])
    for c in st:
        m_new = c["m_t"][L - 1:L, :]
        b_last = c["b_c"][L - 1:L, :]
        g_prev = jnp.exp(b_last + c["m"] - m_new)
        w_c = jnp.exp(b_last - c["b_c"] + c["li_c"] - m_new)
        k = k_ref[c["bi"], :, c["hs"]]
        cs_ref[c["sidx"]] = g_prev * c["C"] + _tn((c["v"] * w_c).astype(BF16), c["kb"])
        ns_ref[c["sidx"], 0:1, :] = g_prev * c["n"] + jnp.sum(k * w_c, axis=0, keepdims=True)
        ms_ref[c["sidx"], 0:1, :] = jnp.broadcast_to(m_new, (1, LANES))

    @pl.when(c_idx == pl.num_programs(1) - 1)
    def _():
        for bi in range(nbp):
            for hd in range(H_M):
                sidx = bi * H_M + hd
                c_ref[bi, hd] = cs_ref[sidx]
                n_ref[bi, hd:hd + 1, :] = ns_ref[sidx, 0:1, :]
                m_ref[bi, hd:hd + 1, :] = ms_ref[sidx, 0:1, :]


def _mlstm_prompt(mq, mk, mv, mo, grow, gcol, gain, *, chunk):
    B, S, mix_m = mq.shape
    d = mix_m // H_M
    n_chunks = S // chunk
    nbp = 4 if B % 4 == 0 else (2 if B % 2 == 0 else 1)
    seq = pl.BlockSpec((nbp, chunk, mix_m), lambda b, c: (b, c, 0))
    return pl.pallas_call(
        functools.partial(_mlstm_prompt_kernel, chunk=chunk, d=d, nbp=nbp),
        grid=(B // nbp, n_chunks),
        in_specs=[seq, seq, seq, seq,
                  pl.BlockSpec((nbp, 1, 2 * H_M, chunk), lambda b, c: (b, c, 0, 0)),
                  pl.BlockSpec((nbp, chunk, 2 * H_M), lambda b, c: (b, c, 0)),
                  pl.BlockSpec((1, mix_m), lambda b, c: (0, 0))],
        out_specs=[seq,
                   pl.BlockSpec((nbp, H_M, d, d), lambda b, c: (b, 0, 0, 0)),
                   pl.BlockSpec((nbp, H_M, d), lambda b, c: (b, 0, 0)),
                   pl.BlockSpec((nbp, H_M, LANES), lambda b, c: (b, 0, 0))],
        out_shape=[jax.ShapeDtypeStruct((B, S, mix_m), F32),
                   jax.ShapeDtypeStruct((B, H_M, d, d), F32),
                   jax.ShapeDtypeStruct((B, H_M, d), F32),
                   jax.ShapeDtypeStruct((B, H_M, LANES), F32)],
        scratch_shapes=[pltpu.VMEM((nbp * H_M, d, d), F32), pltpu.VMEM((nbp * H_M, 8, d), F32),
                        pltpu.VMEM((nbp * H_M, 8, LANES), F32)],
        compiler_params=_cparams(("parallel", "arbitrary")),
        name="mlstm_prompt",
    )(mq, mk, mv, mo, grow, gcol, gain)


def _mlstm_sample_kernel(q_ref, k_ref, v_ref, o_ref, gs_ref, c_ref, n_ref, gain_ref,
                         y_ref, co_ref, no_ref, mo_ref, *, d):
    gs = gs_ref[0]
    eye = (lax.broadcasted_iota(jnp.int32, (d, d), 0) == lax.broadcasted_iota(jnp.int32, (d, d), 1))
    lane = lax.broadcasted_iota(jnp.int32, (1, LANES), 1)
    m_out = jnp.zeros((1, LANES), F32)
    for h in range(H_M):
        sl = slice(h * d, (h + 1) * d)
        q = q_ref[0, :, sl]
        k = k_ref[0, :, sl]
        v = v_ref[0, :, sl]
        o = o_ref[0, :, sl]
        li = gs[:, h:h + 1]
        lf = gs[:, H_M + h:H_M + h + 1]
        m = gs[:, 2 * H_M + h:2 * H_M + h + 1]
        C = c_ref[0, h]
        n = n_ref[0, :, sl]
        inter = lf + m
        m_t = jnp.maximum(inter, li)
        s = jnp.sum(q * k, axis=1, keepdims=True) * jnp.exp(li - m_t)
        g = jnp.exp(inter - m_t)
        q8 = jnp.broadcast_to(q, (8, d)).astype(BF16)
        cq = _nt(q8, C.astype(BF16))[0:1, :]
        num = g * cq + s * v
        den = g * jnp.sum(n * q, axis=1, keepdims=True) + s
        hh = num / jnp.maximum(jnp.abs(den), jnp.exp(-m_t))
        w = jnp.exp(li - m_t)
        v_col = jnp.sum(jnp.where(eye, v, 0.0), axis=1, keepdims=True)
        co_ref[0, h] = g * C + (w * v_col) * k
        no_ref[0, :, sl] = g * n + w * k
        m_out = jnp.where(lane == h, m_t, m_out)
        y_ref[0, :, sl] = _sigmoid(o) * _rms(hh, gain_ref[:, sl])
    mo_ref[0] = m_out


def _mlstm_sample(mq, mk, mv, mo, gs, c_state, n_state, gain):
    Bd, _, mix_m = mq.shape
    d = mix_m // H_M
    row = pl.BlockSpec((1, 1, mix_m), lambda b: (b, 0, 0))
    return pl.pallas_call(
        functools.partial(_mlstm_sample_kernel, d=d),
        grid=(Bd,),
        in_specs=[row, row, row, row,
                  pl.BlockSpec((1, 1, 3 * H_M), lambda b: (b, 0, 0)),
                  pl.BlockSpec((1, H_M, d, d), lambda b: (b, 0, 0, 0)),
                  row,
                  pl.BlockSpec((1, mix_m), lambda b: (0, 0))],
        out_specs=[row,
                   pl.BlockSpec((1, H_M, d, d), lambda b: (b, 0, 0, 0)),
                   row,
                   pl.BlockSpec((1, 1, LANES), lambda b: (b, 0, 0))],
        out_shape=[jax.ShapeDtypeStruct((Bd, 1, mix_m), F32),
                   jax.ShapeDtypeStruct((Bd, H_M, d, d), F32),
                   jax.ShapeDtypeStruct((Bd, 1, mix_m), F32),
                   jax.ShapeDtypeStruct((Bd, 1, LANES), F32)],
        compiler_params=_cparams(("parallel",)),
        name="mlstm_sample",
    )(mq, mk, mv, mo, gs, c_state, n_state, gain)


def _dsa_prompt_kernel(iq_ref, misc_ref, ikt_ref, aq_ref, ak_ref, av_ref, ya_ref, sc_ref, acc_ref,
                       *, tq, w, topk, dh, scale):
    i = pl.program_id(1)
    nk = ((i + 1) * tq + w - 1) // w
    kf = float(topk)
    nsub = w // LANES

    q_pos = i * tq + lax.broadcasted_iota(jnp.int32, (tq, 1), 0)
    lane_w = lax.broadcasted_iota(jnp.int32, (1, w), 1)
    iq = iq_ref[0]
    iq_h = [iq[:, h * D_IDX:(h + 1) * D_IDX] for h in range(H_IDX)]
    misc = misc_ref[0]
    w_h = [misc[:, D_IDX + h:D_IDX + h + 1] for h in range(H_IDX)]

    def score_body(c, carry, causal_edge):
        rmax, rmin = carry
        ikc = ikt_ref[0, c]
        score = jnp.zeros((tq, w), F32)
        for h in range(H_IDX):
            score = score + w_h[h] * jnp.maximum(_mm(iq_h[h], ikc), 0.0)
        if causal_edge:
            valid = (c * w + lane_w) <= q_pos
            sc_ref[c] = jnp.where(valid, score, NEG_INF)
            s_hi = jnp.where(valid, score, NEG_INF)
            s_lo = jnp.where(valid, score, POS_INF)
        else:
            sc_ref[c] = score
            s_hi = s_lo = score
        rmax = jnp.maximum(rmax, jnp.max(s_hi, axis=1, keepdims=True))
        rmin = jnp.minimum(rmin, jnp.min(s_lo, axis=1, keepdims=True))
        return rmax, rmin

    n_full = (i * tq) // w
    carry0 = (jnp.full((tq, 1), NEG_INF, F32), jnp.full((tq, 1), POS_INF, F32))
    carry0 = lax.fori_loop(0, n_full, functools.partial(score_body, causal_edge=False), carry0)
    rmax, rmin = lax.fori_loop(n_full, nk, functools.partial(score_body, causal_edge=True), carry0)

    ge = lambda x, t: x >= t
    gt = lambda x, t: x > t

    rh = min(tq, LANES)
    groups = [pl.ds(r0, rh) for r0 in range(0, tq, rh)]
    part = lambda a: [a[r0:r0 + rh] for r0 in range(0, tq, rh)]

    def pass_acc(rows, fn, init, combine):
        def body(c, acc):
            x = sc_ref[c, rows, :]
            for j in range(nsub):
                acc = combine(acc, fn(x[:, j * LANES:(j + 1) * LANES]))
            return acc
        return lax.fori_loop(0, nk, body, jnp.full((rh, LANES), init, F32))

    def count_acc(rows, pred, thr):
        thr_b = jnp.broadcast_to(thr, (rh, LANES))
        return pass_acc(rows, lambda x: jnp.where(pred(x, thr_b), 1.0, 0.0), 0.0, jnp.add)

    def count(rows, pred, thr):
        return jnp.sum(count_acc(rows, pred, thr), axis=1, keepdims=True)

    def min_where(rows, pred, thr):
        thr_b = jnp.broadcast_to(thr, (rh, LANES))
        acc = pass_acc(rows, lambda x: jnp.where(pred(x, thr_b), x, POS_INF), POS_INF, jnp.minimum)
        return jnp.min(acc, axis=1, keepdims=True)

    def bis_body(_, carry):
        los, his, clos = carry
        mids = [0.5 * (lo + hi) for lo, hi in zip(los, his)]
        accs = [count_acc(rows, ge, mid) for rows, mid in zip(groups, mids)]
        cms = [jnp.sum(a, axis=1, keepdims=True) for a in accs]
        oks = [cm >= kf for cm in cms]
        return (tuple(jnp.where(ok, mid, lo) for ok, mid, lo in zip(oks, mids, los)),
                tuple(jnp.where(ok, hi, mid) for ok, mid, hi in zip(oks, mids, his)),
                tuple(jnp.where(ok, cm, cl) for ok, cm, cl in zip(oks, cms, clos)))

    los, _, clos = lax.fori_loop(
        0, N_BISECT, bis_body,
        (tuple(part(rmin)), tuple(part(rmax + jnp.abs(rmax) + 1.0)), tuple(part((q_pos + 1).astype(F32)))))

    def finish_rows(rows, qp, rmin_h, lo, c_lo):
        active = (qp + 1) > topk
        unresolved = jnp.max(jnp.where(active & (c_lo != kf), 1.0, 0.0)) > 0.5

        @pl.when(jnp.logical_not(unresolved))
        def _():
            thr = jnp.where(active, lo, rmin_h)

            def body(c, _):
                sc_ref[c, rows, :] = jnp.where(sc_ref[c, rows, :] >= thr, 0.0, NEG_INF)
                return 0
            lax.fori_loop(0, nk, body, 0)

        @pl.when(unresolved)
        def _():
            tau = min_where(rows, ge, lo)
            g = count(rows, gt, tau)

            def undone(tau, g):
                return active & (g >= kf)

            def fix_cond(st):
                return jnp.max(jnp.where(undone(*st), 1.0, 0.0)) > 0.5

            def fix_body(st):
                tau, g = st
                nd = undone(tau, g)
                tau2 = jnp.where(nd, min_where(rows, gt, tau), tau)
                return tau2, jnp.where(nd, count(rows, gt, tau2), g)

            tau, g = lax.while_loop(fix_cond, fix_body, (tau, g))
            tau_b = jnp.broadcast_to(jnp.where(active, tau, rmin_h), (rh, LANES))
            need_b = jnp.broadcast_to(jnp.where(active, kf - g, 1e9), (rh, LANES))
            r_i = lax.broadcasted_iota(jnp.int32, (LANES, 2 * LANES), 0)
            c_i = lax.broadcasted_iota(jnp.int32, (LANES, 2 * LANES), 1)
            tri_ones = ((r_i <= c_i) | (c_i >= LANES)).astype(BF16)

            def body(c, run):
                x = sc_ref[c, rows, :]
                outs = []
                for j in range(nsub):
                    xj = x[:, j * LANES:(j + 1) * LANES]
                    is_eq = xj == tau_b
                    cnt2 = _mm(jnp.where(is_eq, 1.0, 0.0).astype(BF16), tri_ones)
                    sel = (xj > tau_b) | (is_eq & (cnt2[:, :LANES] + run <= need_b))
                    outs.append(jnp.where(sel, 0.0, NEG_INF))
                    run = run + cnt2[:, LANES:]
                sc_ref[c, rows, :] = jnp.concatenate(outs, axis=1)
                return run
            lax.fori_loop(0, nk, body, jnp.zeros((rh, LANES), F32))

    for rows, qp, rmin_h, lo, c_lo in zip(groups, part(q_pos), part(rmin), los, clos):
        finish_rows(rows, qp, rmin_h, lo, c_lo)

    aq = aq_ref[0]
    q_heads = [aq[:, h * dh:(h + 1) * dh] for h in range(H_A)]
    acc_ref[...] = jnp.zeros_like(acc_ref)
    c2 = scale * LOG2E
    ones_blk = jnp.ones((w, dh), BF16)

    def att_body(c, ms):
        k0 = pl.multiple_of(c * w, w)
        bias = sc_ref[c]
        heads = [slice(h * dh, (h + 1) * dh) for h in range(H_A)]
        ss = [_nt(q_heads[h], ak_ref[0, pl.ds(k0, w), heads[h]]) + bias for h in range(H_A)]
        ms_new = [jnp.maximum(ms[h], jnp.max(ss[h], axis=1, keepdims=True)) for h in range(H_A)]
        m_safe = [jnp.where(m == NEG_INF, 0.0, m) for m in ms_new]
        ps = [jnp.exp2((ss[h] - m_safe[h]) * c2).astype(BF16) for h in range(H_A)]
        for h in range(H_A):
            alpha = jnp.exp2((ms[h] - m_safe[h]) * c2)
            v_aug = jnp.concatenate([av_ref[0, pl.ds(k0, w), heads[h]], ones_blk], axis=1)
            acc_ref[h] = alpha * acc_ref[h] + _mm(ps[h], v_aug)
        return tuple(ms_new)

    lax.fori_loop(0, nk, att_body, tuple(jnp.full((tq, 1), NEG_INF, F32) for _ in range(H_A)))
    for h in range(H_A):
        a = acc_ref[h]
        ya_ref[0, :, h * dh:(h + 1) * dh] = a[:, :dh] / a[:, dh:]


def _dsa_prompt(iqb, misc, ikt, aqb, akb, avb, *, tq, w, topk):
    B, S, mix_a = aqb.shape
    dh = mix_a // H_A
    nq = S // tq
    nw = S // w
    return pl.pallas_call(
        functools.partial(_dsa_prompt_kernel, tq=tq, w=w, topk=topk, dh=dh, scale=dh ** -0.5),
        grid=(B, nq),
        in_specs=[pl.BlockSpec((1, tq, H_IDX * D_IDX), lambda b, i: (b, i, 0)),
                  pl.BlockSpec((1, tq, LANES), lambda b, i: (b, i, 0)),
                  pl.BlockSpec((1, nw, D_IDX, w), lambda b, i: (b, 0, 0, 0)),
                  pl.BlockSpec((1, tq, mix_a), lambda b, i: (b, i, 0)),
                  pl.BlockSpec((1, S, mix_a), lambda b, i: (b, 0, 0)),
                  pl.BlockSpec((1, S, mix_a), lambda b, i: (b, 0, 0))],
        out_specs=pl.BlockSpec((1, tq, mix_a), lambda b, i: (b, i, 0)),
        out_shape=jax.ShapeDtypeStruct((B, S, mix_a), F32),
        scratch_shapes=[pltpu.VMEM((nw, tq, w), F32), pltpu.VMEM((H_A, tq, 2 * dh), F32)],
        compiler_params=_cparams(("parallel", "arbitrary")),
        name="dsa_prompt",
    )(iqb, misc, ikt, aqb, akb, avb)


def _dsa_sample_select_kernel(pt_ref, iq_ref, w_ref, ikn_ref, ptv_ref, pool_ref, rows_ref, flag_ref,
                              ikbuf, sem, sc_ref, xn_ref, slot_ref, phys_ref,
                              *, n_pages, page, topk, cw):
    nb = iq_ref.shape[0]
    past = n_pages * page
    kf = float(topk)
    n_cw = past // cw

    def page_copy(bb, p, slot):
        return pltpu.make_async_copy(pool_ref.at[pt_ref[bb, p]],
                                     ikbuf.at[slot, :, pl.ds(p * page, page)],
                                     sem.at[slot])

    def start_all(bb, slot):
        def body(p, _):
            page_copy(bb, p, slot).start()
            return 0
        lax.fori_loop(0, n_pages, body, 0)

    start_all(0, 0)

    def score_body(b, _):
        slot = b % 2

        @pl.when(b + 1 < nb)
        def _():
            start_all(b + 1, 1 - slot)

        def wait_body(p, _):
            page_copy(b, p, slot).wait()
            return 0
        lax.fori_loop(0, n_pages, wait_body, 0)

        iq8 = iq_ref[b]
        w8 = w_ref[b]
        s8 = _mm(iq8, ikbuf[slot].astype(BF16))
        sc_ref[pl.ds(b, 1), :] = jnp.sum(w8 * jnp.maximum(s8, 0.0), axis=0, keepdims=True)
        ikn = ikn_ref[b].astype(BF16).astype(F32)
        sn8 = jnp.sum(iq8.astype(F32) * ikn, axis=1, keepdims=True)
        xn_b = jnp.sum(w8 * jnp.maximum(sn8, 0.0), axis=0, keepdims=True)
        xn_ref[pl.ds(b, 1), :] = jnp.broadcast_to(xn_b, (1, LANES))
        return 0

    lax.fori_loop(0, nb, score_body, 0)

    x = sc_ref[...]
    xn = xn_ref[:, 0:1]

    def cnt(mask_row, mask_new):
        return (jnp.sum(jnp.where(mask_row, 1.0, 0.0), axis=1, keepdims=True)
                + jnp.where(mask_new, 1.0, 0.0))

    rmax = jnp.maximum(jnp.max(x, axis=1, keepdims=True), xn)
    rmin = jnp.minimum(jnp.min(x, axis=1, keepdims=True), xn)
    hi0 = rmax + jnp.abs(rmax) + 1.0

    def bis_body(_, carry):
        lo, hi = carry
        mid = 0.5 * (lo + hi)
        ok = cnt(x >= mid, xn >= mid) >= kf
        return jnp.where(ok, mid, lo), jnp.where(ok, hi, mid)

    lo, _ = lax.fori_loop(0, N_BISECT, bis_body, (rmin, hi0))

    def min_where(mask_row, mask_new):
        return jnp.minimum(jnp.min(jnp.where(mask_row, x, POS_INF), axis=1, keepdims=True),
                           jnp.where(mask_new, xn, POS_INF))

    tau = min_where(x >= lo, xn >= lo)
    g = cnt(x > tau, xn > tau)

    def fix_cond(st):
        tau, g = st
        return jnp.max(jnp.where(g >= kf, 1.0, 0.0)) > 0.5

    def fix_body(st):
        tau, g = st
        tau2 = jnp.where(g >= kf, min_where(x > tau, xn > tau), tau)
        return tau2, cnt(x > tau2, xn > tau2)

    tau, g = lax.while_loop(fix_cond, fix_body, (tau, g))
    need = kf - g

    tri = (lax.broadcasted_iota(jnp.int32, (cw, cw), 0)
           < lax.broadcasted_iota(jnp.int32, (cw, cw), 1)).astype(BF16)

    def excl_prefix(flag):
        outs = []
        run = jnp.zeros((nb, 1), F32)
        for c in range(n_cw):
            f = flag[:, c * cw:(c + 1) * cw]
            outs.append(_mm(f.astype(BF16), tri) + run)
            run = run + jnp.sum(f, axis=1, keepdims=True)
        return jnp.concatenate(outs, axis=1), run

    is_eq = x == tau
    pre_eq, n_eq_past = excl_prefix(jnp.where(is_eq, 1.0, 0.0))
    sel = (x > tau) | (is_eq & (pre_eq < need))
    new_sel = (xn > tau) | ((xn == tau) & (n_eq_past < need))
    slot, _ = excl_prefix(jnp.where(sel, 1.0, 0.0))
    slot_ref[...] = jnp.where(sel, slot, -1.0)

    ptv = ptv_ref[...]
    jrow = lax.broadcasted_iota(jnp.int32, (1, past), 1)
    prow = lax.broadcasted_iota(jnp.int32, (n_pages, 1), 0)
    expand = ((jrow >= prow * page) & (jrow < (prow + 1) * page)).astype(BF16)
    digit_bits = 6
    pt_hi = _mm((ptv >> digit_bits).astype(F32).astype(BF16), expand)
    pt_lo = _mm((ptv & ((1 << digit_bits) - 1)).astype(F32).astype(BF16), expand)
    pidx = lax.broadcasted_iota(jnp.int32, (8, n_pages), 1).astype(F32).astype(BF16)
    pg = _mm(pidx, expand)[0:1, :]
    phys_ref[...] = (pt_hi * (1 << digit_bits) + pt_lo) * page + (jrow.astype(F32) - pg * page)

    slot_col = lax.broadcasted_iota(jnp.int32, (topk, 1), 0).astype(F32)
    lane_b = lax.broadcasted_iota(jnp.int32, (1, LANES), 1)

    def extract_body(b, out):
        srow = slot_ref[pl.ds(b, 1), :]
        frow = phys_ref[pl.ds(b, 1), :]
        acc = jnp.zeros((topk, LANES), F32)
        for c in range(past // LANES):
            cs = slice(c * LANES, (c + 1) * LANES)
            acc = acc + jnp.where(srow[:, cs] == slot_col, frow[:, cs], 0.0)
        return jnp.where(lane_b == b, jnp.sum(acc, axis=1, keepdims=True), out)

    out = lax.fori_loop(0, nb, extract_body, jnp.zeros((topk, LANES), F32))
    rows_ref[...] = out.astype(jnp.int32)
    flag_ref[...] = jnp.broadcast_to(jnp.where(new_sel, 1, 0), (nb, LANES)).astype(jnp.int32)


def _dsa_sample_select(page_table, iq8, w8, ik_new, pool_ik_t, *, topk, cw):
    Bd, n_pages = page_table.shape
    n_pool, d_idx, page = pool_ik_t.shape
    past = n_pages * page
    assert Bd <= LANES and n_pool <= 64 * 256
    full = lambda shp: pl.BlockSpec(shp, lambda i, pt: (0,) * len(shp))
    grid_spec = pltpu.PrefetchScalarGridSpec(
        num_scalar_prefetch=1,
        grid=(1,),
        in_specs=[full((Bd, 8, d_idx)), full((Bd, 8, 1)), full((Bd, 1, d_idx)), full((Bd, n_pages)),
                  pl.BlockSpec(memory_space=pl.ANY)],
        out_specs=[full((topk, LANES)), full((Bd, LANES))],
        scratch_shapes=[pltpu.VMEM((2, d_idx, past), F32),
                        pltpu.SemaphoreType.DMA((2,)),
                        pltpu.VMEM((Bd, past), F32),
                        pltpu.VMEM((Bd, LANES), F32),
                        pltpu.VMEM((Bd, past), F32),
                        pltpu.VMEM((Bd, past), F32)],
    )
    return pl.pallas_call(
        functools.partial(_dsa_sample_select_kernel, n_pages=n_pages, page=page, topk=topk, cw=cw),
        grid_spec=grid_spec,
        out_shape=[jax.ShapeDtypeStruct((topk, LANES), jnp.int32),
                   jax.ShapeDtypeStruct((Bd, LANES), jnp.int32)],
        compiler_params=_cparams(("arbitrary",)),
        name="dsa_sample_select",
    )(page_table, iq8, w8, ik_new, page_table, pool_ik_t)


def _dsa_sample_attend_kernel(rows_ref, flag_ref, aq_ref, knew_ref, vnew_ref, kpool_ref, vpool_ref,
                              ya_ref, kbuf, vbuf, sem, *, topk, dh, scale):
    b = pl.program_id(0)
    nb = pl.num_programs(0)

    def row_copies(bb, t, slot):
        r = rows_ref[bb, t]
        dst = pl.ds(t * H_A, H_A)
        return (pltpu.make_async_copy(kpool_ref.at[r], kbuf.at[slot, dst, :], sem.at[0, slot]),
                pltpu.make_async_copy(vpool_ref.at[r], vbuf.at[slot, dst, :], sem.at[1, slot]))

    def start_all(bb, slot):
        def body(t, _):
            ck, cv = row_copies(bb, t, slot)
            ck.start()
            cv.start()
            return 0
        lax.fori_loop(0, topk, body, 0, unroll=16)

    slot = b % 2

    @pl.when(b == 0)
    def _():
        start_all(0, 0)

    @pl.when(b + 1 < nb)
    def _():
        start_all(b + 1, 1 - slot)

    def wait_body(t, _):
        ck, cv = row_copies(b, t, slot)
        ck.wait()
        cv.wait()
        return 0
    lax.fori_loop(0, topk, wait_body, 0, unroll=16)

    take_new = (lax.broadcasted_iota(jnp.int32, (topk, 1), 0) == topk - 1) & (flag_ref[b] > 0)
    aq = aq_ref[0]
    for h in range(H_A):
        hs = slice(h * dh, (h + 1) * dh)
        kh = kbuf[slot, pl.ds(h, topk, stride=H_A), :]
        vh = vbuf[slot, pl.ds(h, topk, stride=H_A), :]
        kh = jnp.where(take_new, knew_ref[0, h:h + 1, :], kh).astype(BF16)
        vh = jnp.where(take_new, vnew_ref[0, h:h + 1, :], vh).astype(BF16)
        q8 = jnp.broadcast_to(aq[:, hs], (8, dh))
        s = _nt(q8, kh) * scale
        m = jnp.max(s, axis=1, keepdims=True)
        p = jnp.exp(s - m)
        p = p / jnp.sum(p, axis=1, keepdims=True)
        ya_ref[0, :, hs] = _mm(p.astype(BF16), vh)[0:1, :]


def _dsa_sample_attend(rows, flags, aqb, k_new, v_new, pool_k, pool_v, *, topk):
    Bd, _, mix_a = aqb.shape
    dh = pool_k.shape[2]
    new_spec = pl.BlockSpec((1, H_A, dh), lambda b, r, f: (b, 0, 0))
    grid_spec = pltpu.PrefetchScalarGridSpec(
        num_scalar_prefetch=2,
        grid=(Bd,),
        in_specs=[pl.BlockSpec((1, 1, mix_a), lambda b, r, f: (b, 0, 0)),
                  new_spec, new_spec,
                  pl.BlockSpec(memory_space=pl.ANY),
                  pl.BlockSpec(memory_space=pl.ANY)],
        out_specs=pl.BlockSpec((1, 1, mix_a), lambda b, r, f: (b, 0, 0)),
        scratch_shapes=[pltpu.VMEM((2, topk * H_A, dh), F32),
                        pltpu.VMEM((2, topk * H_A, dh), F32),
                        pltpu.SemaphoreType.DMA((2, 2))],
    )
    return pl.pallas_call(
        functools.partial(_dsa_sample_attend_kernel, topk=topk, dh=dh, scale=dh ** -0.5),
        grid_spec=grid_spec,
        out_shape=jax.ShapeDtypeStruct((Bd, 1, mix_a), F32),
        compiler_params=_cparams(("arbitrary",)),
        name="dsa_sample_attend",
    )(rows, flags, aqb, k_new, v_new, pool_k, pool_v)


def _mem_kv_kernel(mem_ref, nm_ref, wk_ref, wv_ref, kn_ref, k_ref, v_ref, *, dh):
    hm = _rms(mem_ref[...], nm_ref[...]).astype(BF16)
    kk = _mm(hm, wk_ref[...])
    vv = _mm(hm, wv_ref[...])
    for h in range(H_C):
        hs = slice(h * dh, (h + 1) * dh)
        k_ref[:, h, :] = _rms(kk[:, hs], kn_ref[...])
        v_ref[:, h, :] = vv[:, hs]


def _mem_kv(mem2d, norm_mem, w_ck, w_cv, ck_norm, *, tm):
    rows, d = mem2d.shape
    dh = d // H_C
    row = pl.BlockSpec((tm, d), lambda i: (i, 0))
    heads = pl.BlockSpec((tm, H_C, dh), lambda i: (i, 0, 0))
    const = lambda shp: pl.BlockSpec(shp, lambda i: (0, 0))
    return pl.pallas_call(
        functools.partial(_mem_kv_kernel, dh=dh),
        grid=(rows // tm,),
        in_specs=[row, const((1, d)), const((d, d)), const((d, d)), const((1, dh))],
        out_specs=[heads, heads],
        out_shape=[jax.ShapeDtypeStruct((rows, H_C, dh), F32)] * 2,
        compiler_params=_cparams(("parallel",)),
        name="mem_kv",
    )(mem2d, norm_mem, w_ck, w_cv, ck_norm)


def _out_cq_kernel(x_ref, ym_ref, ya_ref, wo_ref, nc_ref, wq_ref, qn_ref, x1_ref, qc_ref, *, mix_m, dh):
    upd = (_mm(ym_ref[...].astype(BF16), wo_ref[0:mix_m, :])
           + _mm(ya_ref[...].astype(BF16), wo_ref[mix_m:, :]))
    x1 = x_ref[...] + upd
    x1_ref[...] = x1
    hq = _mm(_rms(x1, nc_ref[...]).astype(BF16), wq_ref[...])
    for h in range(H_C):
        hs = slice(h * dh, (h + 1) * dh)
        qc_ref[:, hs] = _rms(hq[:, hs], qn_ref[...]).astype(BF16)


def _out_cq(x2d, ym, ya, w_out, norm_cross, w_cq, cq_norm, *, tm):
    rows, d = x2d.shape
    mix_m = ym.shape[1]
    mix_a = ya.shape[1]
    dh = d // H_C
    row = lambda wdt: pl.BlockSpec((tm, wdt), lambda i: (i, 0))
    const = lambda shp: pl.BlockSpec(shp, lambda i: (0, 0))
    return pl.pallas_call(
        functools.partial(_out_cq_kernel, mix_m=mix_m, dh=dh),
        grid=(rows // tm,),
        in_specs=[row(d), row(mix_m), row(mix_a), const((mix_m + mix_a, d)), const((1, d)),
                  const((d, d)), const((1, dh))],
        out_specs=[row(d), row(d)],
        out_shape=[jax.ShapeDtypeStruct((rows, d), F32), jax.ShapeDtypeStruct((rows, d), BF16)],
        compiler_params=_cparams(("parallel",)),
        name="out_cq",
    )(x2d, ym, ya, w_out, norm_cross, w_cq, cq_norm)


def _cross_kernel(q_ref, k_hbm, v_hbm, o_ref, kv_buf, sem, *, dh, scale):
    b = pl.program_id(0)
    t = pl.program_id(1)
    nb = pl.num_programs(0)
    slot = b % 2

    def head_copies(bb, sl):
        cps = []
        for h in range(H_C):
            cps.append(pltpu.make_async_copy(k_hbm.at[bb, :, h, :], kv_buf.at[sl, 0, h], sem.at[sl]))
            cps.append(pltpu.make_async_copy(v_hbm.at[bb, :, h, :], kv_buf.at[sl, 1, h], sem.at[sl]))
        return cps

    @pl.when(t == 0)
    def _():
        @pl.when(b == 0)
        def _():
            for cp in head_copies(0, 0):
                cp.start()

        @pl.when(b + 1 < nb)
        def _():
            for cp in head_copies(b + 1, 1 - slot):
                cp.start()

        for cp in head_copies(b, slot):
            cp.wait()

    q = q_ref[0]
    rows = q.shape[0]
    if rows < 8:
        q = jnp.broadcast_to(q, (8, q.shape[1]))
    for h in range(H_C):
        hs = slice(h * dh, (h + 1) * dh)
        kb = kv_buf[slot, 0, h].astype(BF16)
        vb = kv_buf[slot, 1, h].astype(BF16)
        s = _nt(q[:, hs], kb) * scale
        m = jnp.max(s, axis=1, keepdims=True)
        p = jnp.exp(s - m)
        p = p / jnp.sum(p, axis=1, keepdims=True)
        o = _mm(p.astype(BF16), vb)
        o_ref[0, :, hs] = o[0:rows].astype(BF16)


def _cross(qc, mem_k, mem_v, *, tq):
    B, T, d = qc.shape
    M = mem_k.shape[1]
    dh = d // H_C
    return pl.pallas_call(
        functools.partial(_cross_kernel, dh=dh, scale=dh ** -0.5),
        grid=(B, T // tq),
        in_specs=[pl.BlockSpec((1, tq, d), lambda b, t: (b, t, 0)),
                  pl.BlockSpec(memory_space=pl.ANY),
                  pl.BlockSpec(memory_space=pl.ANY)],
        out_specs=pl.BlockSpec((1, tq, d), lambda b, t: (b, t, 0)),
        out_shape=jax.ShapeDtypeStruct((B, T, d), BF16),
        scratch_shapes=[pltpu.VMEM((2, 2, H_C, M, dh), F32), pltpu.SemaphoreType.DMA((2,))],
        compiler_params=_cparams(("arbitrary", "arbitrary")),
        name="cross_attn",
    )(qc, mem_k, mem_v)


def _gelu_tanh(x):
    return 0.5 * x * (1.0 + jnp.tanh(np.sqrt(2.0 / np.pi) * (x + 0.044715 * (x * x * x))))


def _ffn_front(x1_ref, o_ref, wco_ref, nf_ref, x2_ref, hb_ref, acc_ref):
    x2 = x1_ref[0] + _mm(o_ref[0], wco_ref[...])
    x2_ref[...] = x2
    hb_ref[...] = _rms(x2, nf_ref[...]).astype(BF16)
    acc_ref[...] = jnp.zeros_like(acc_ref)


def _ffn_prompt_kernel(x1_ref, o_ref, wco_ref, nf_ref, wua_ref, wug_ref, cwa_ref, cwg_ref,
                       cba_ref, cbg_ref, wd_ref, ha_ref, hg_ref,
                       y_ref, ca_ref, cg_ref, x2_ref, hb_ref, acc_ref, carry_ref, *, tm, rs):
    t = pl.program_id(1)
    j = pl.program_id(2)
    nj = pl.num_programs(2)

    @pl.when(j == 0)
    def _():
        _ffn_front(x1_ref, o_ref, wco_ref, nf_ref, x2_ref, hb_ref, acc_ref)

    @pl.when(t == 0)
    def _():
        carry_ref[j, 0, 6:8, :] = ha_ref[0]
        carry_ref[j, 1, 6:8, :] = hg_ref[0]

    rid = lax.broadcasted_iota(jnp.int32, (rs, 1), 0)

    def conv_part(hb, part, wu_ref, cw_ref, cb_ref):
        u = _mm(hb, wu_ref[...])
        p2 = carry_ref[j, part, 6:7, :]
        p1 = carry_ref[j, part, 7:8, :]
        um1 = jnp.where(rid == 0, p1, pltpu.roll(u, 1, 0))
        um2 = jnp.where(rid == 0, p2, jnp.where(rid == 1, p1, pltpu.roll(u, 2, 0)))
        carry_ref[j, part] = u[rs - 8:rs, :]
        return cb_ref[...] + um2 * cw_ref[0:1, :] + um1 * cw_ref[1:2, :] + u * cw_ref[2:3, :]

    def sub_body(r, _):
        r0 = pl.multiple_of(r * rs, rs)
        hb = hb_ref[pl.ds(r0, rs), :]
        a = conv_part(hb, 0, wua_ref, cwa_ref, cba_ref)
        g = conv_part(hb, 1, wug_ref, cwg_ref, cbg_ref)
        acc_ref[pl.ds(r0, rs), :] += _mm((_gelu_tanh(g) * a).astype(BF16), wd_ref[...])
        return 0

    lax.fori_loop(0, tm // rs, sub_body, 0)
    ca_ref[0, 0] = carry_ref[j, 0, 6:8, :]
    cg_ref[0, 0] = carry_ref[j, 1, 6:8, :]

    @pl.when(j == nj - 1)
    def _():
        y_ref[0] = x2_ref[...] + acc_ref[...]


def _ffn_prompt(x1, o, w_co, norm_ffn, w_up, conv_w, conv_b, w_down, hist, *, tm, tf):
    B, T, d = x1.shape
    d_ff = w_down.shape[0]
    nj = d_ff // tf
    nt = T // tm
    idx3 = lambda b, t, j: (b, t, 0)
    c2 = lambda shp: pl.BlockSpec(shp, lambda b, t, j: (0, 0))
    return pl.pallas_call(
        functools.partial(_ffn_prompt_kernel, tm=tm, rs=min(512, tm)),
        grid=(B, nt, nj),
        in_specs=[pl.BlockSpec((1, tm, d), idx3), pl.BlockSpec((1, tm, d), idx3),
                  pl.BlockSpec((d, d), lambda b, t, j: (0, 0), pipeline_mode=pl.Buffered(1)), c2((1, d)),
                  pl.BlockSpec((d, tf), lambda b, t, j: (0, j)),
                  pl.BlockSpec((d, tf), lambda b, t, j: (0, nj + j)),
                  pl.BlockSpec((CONV_W, tf), lambda b, t, j: (0, j)),
                  pl.BlockSpec((CONV_W, tf), lambda b, t, j: (0, nj + j)),
                  pl.BlockSpec((1, tf), lambda b, t, j: (0, j)),
                  pl.BlockSpec((1, tf), lambda b, t, j: (0, nj + j)),
                  pl.BlockSpec((tf, d), lambda b, t, j: (j, 0)),
                  pl.BlockSpec((1, CONV_W - 1, tf), lambda b, t, j: (b, 0, j)),
                  pl.BlockSpec((1, CONV_W - 1, tf), lambda b, t, j: (b, 0, nj + j))],
        out_specs=[pl.BlockSpec((1, tm, d), idx3),
                   pl.BlockSpec((1, 1, CONV_W - 1, tf), lambda b, t, j: (b, t, 0, j)),
                   pl.BlockSpec((1, 1, CONV_W - 1, tf), lambda b, t, j: (b, t, 0, j))],
        out_shape=[jax.ShapeDtypeStruct((B, T, d), F32),
                   jax.ShapeDtypeStruct((B, nt, CONV_W - 1, d_ff), F32),
                   jax.ShapeDtypeStruct((B, nt, CONV_W - 1, d_ff), F32)],
        scratch_shapes=[pltpu.VMEM((tm, d), F32), pltpu.VMEM((tm, d), BF16), pltpu.VMEM((tm, d), F32),
                        pltpu.VMEM((nj, 2, 8, tf), F32)],
        compiler_params=_cparams(("arbitrary", "arbitrary", "arbitrary")),
        name="ffn_prompt",
    )(x1, o, w_co, norm_ffn, w_up, w_up, conv_w, conv_w, conv_b, conv_b, w_down, hist, hist)


def _ffn_sample_kernel(x1_ref, o_ref, wco_ref, nf_ref, wua_ref, wug_ref, cwa_ref, cwg_ref,
                       cba_ref, cbg_ref, wd_ref, h0a_ref, h0g_ref, h1a_ref, h1g_ref,
                       y_ref, ua_ref, ug_ref, x2_ref, hb_ref, acc_ref):
    j = pl.program_id(0)
    nj = pl.num_programs(0)

    @pl.when(j == 0)
    def _():
        _ffn_front(x1_ref, o_ref, wco_ref, nf_ref, x2_ref, hb_ref, acc_ref)

    hb = hb_ref[...]

    def conv_part(wu_ref, cw_ref, cb_ref, h0_ref, h1_ref, u_out_ref):
        u = _mm(hb, wu_ref[...])
        u_out_ref[...] = u
        return cb_ref[...] + h0_ref[...] * cw_ref[0:1, :] + h1_ref[...] * cw_ref[1:2, :] + u * cw_ref[2:3, :]

    a = conv_part(wua_ref, cwa_ref, cba_ref, h0a_ref, h1a_ref, ua_ref)
    g = conv_part(wug_ref, cwg_ref, cbg_ref, h0g_ref, h1g_ref, ug_ref)
    acc_ref[...] += _mm((_gelu_tanh(g) * a).astype(BF16), wd_ref[...])

    @pl.when(j == nj - 1)
    def _():
        y_ref[0] = x2_ref[...] + acc_ref[...]


def _ffn_sample(x1, o, w_co, norm_ffn, w_up, conv_w, conv_b, w_down, h0, h1, *, tf):
    _, rows, d = x1.shape
    d_ff = w_down.shape[0]
    nj = d_ff // tf
    c2 = lambda shp: pl.BlockSpec(shp, lambda j: (0, 0))
    c3 = lambda shp: pl.BlockSpec(shp, lambda j: (0, 0, 0))
    col_a = lambda r: pl.BlockSpec((r, tf), lambda j: (0, j))
    col_g = lambda r: pl.BlockSpec((r, tf), lambda j: (0, nj + j))
    return pl.pallas_call(
        _ffn_sample_kernel,
        grid=(nj,),
        in_specs=[c3((1, rows, d)), c3((1, rows, d)), c2((d, d)), c2((1, d)),
                  col_a(d), col_g(d), col_a(CONV_W), col_g(CONV_W), col_a(1), col_g(1),
                  pl.BlockSpec((tf, d), lambda j: (j, 0)),
                  col_a(rows), col_g(rows), col_a(rows), col_g(rows)],
        out_specs=[c3((1, rows, d)), col_a(rows), col_a(rows)],
        out_shape=[jax.ShapeDtypeStruct((1, rows, d), F32),
                   jax.ShapeDtypeStruct((rows, d_ff), F32),
                   jax.ShapeDtypeStruct((rows, d_ff), F32)],
        scratch_shapes=[pltpu.VMEM((rows, d), F32), pltpu.VMEM((rows, d), BF16), pltpu.VMEM((rows, d), F32)],
        compiler_params=_cparams(("arbitrary",)),
        name="ffn_sample",
    )(x1, o, w_co, norm_ffn, w_up, w_up, conv_w, conv_w, conv_b, conv_b, w_down, h0, h0, h1, h1)


def _tiles(seq, mem_rows, d_ff):
    half_ff = d_ff // 2
    return dict(
        in_rows=min(256, seq),
        dsa_q=min(512, seq),
        dsa_k=min(512, seq),
        mem_rows=min(256, mem_rows),
        out_rows=min(1024, seq),
        cross_q=min(2048, seq),
        ffn_rows=min(512, seq),
        ffn_cols=half_ff if half_ff % LANES == 0 else d_ff,
    )


def _rope_tables(pos, dh_a):
    assert dh_a == LANES and 2 * D_IDX == LANES and D_IDX & (D_IDX - 1) == 0
    posf = pos.astype(F32)[:, None]
    half_a = dh_a // 2
    inv_a = ROPE_THETA ** (-jnp.arange(half_a, dtype=F32) / half_a)
    ang_a = posf * inv_a[None, :]
    half_i = D_IDX // 2
    inv_i = ROPE_THETA ** (-jnp.arange(half_i, dtype=F32) / half_i)
    ang_i = posf * inv_i[None, :]
    cos_i, sin_i = jnp.cos(ang_i), jnp.sin(ang_i)
    return jnp.concatenate([jnp.cos(ang_a), jnp.sin(ang_a), cos_i, sin_i, cos_i, sin_i], axis=1)


def kernel(x_prompt, x_sample, mem_prompt, cache_k, cache_v, cache_idx_k, cache_mem_k, cache_mem_v,
           state_mlstm_c, state_mlstm_n, state_mlstm_m, state_conv, page_table,
           norm_mix, w_in, b_if, mlstm_norm, q_norm, k_norm, w_out, norm_cross, norm_mem,
           w_cq, w_ck, w_cv, w_co, cq_norm, ck_norm, norm_ffn, w_up, conv_w, conv_b, w_down):
    B, S, D = x_prompt.shape
    Bd, T, _ = x_sample.shape
    assert T == 1 and w_in.shape[0] == 1
    n_pool, page = cache_k.shape[1], cache_k.shape[2]
    n_pages = page_table.shape[1]
    past = n_pages * page
    mix_m = mlstm_norm.shape[1]
    dh_m = mix_m // H_M
    dh_a = q_norm.shape[1]
    mix_a = H_A * dh_a
    d_ff = w_down.shape[1]
    M = mem_prompt.shape[1]
    chunk = min(128, S)
    topk_p = min(TOPK_MAX, S // 4)
    topk_s = min(TOPK_MAX, (past + T) // 4)

    w = w_in[0]
    o_gate = 4 * mix_m
    o_aq = o_gate + 2 * H_M
    o_iq = o_aq + 3 * mix_a
    o_ik = o_iq + H_IDX * D_IDX
    o_iw = o_ik + D_IDX
    tail_pad = LANES - (D_IDX + H_IDX + 2 * H_M)
    w_r = (w[:, :o_gate].astype(BF16), w[:, o_aq:o_ik].astype(BF16),
           jnp.concatenate([w[:, o_ik:o_iw + H_IDX], w[:, o_gate:o_aq],
                            jnp.zeros((D, tail_pad), w.dtype)], axis=1).astype(BF16))
    bias_tail = jnp.concatenate([jnp.zeros((D_IDX + H_IDX,), F32), b_if[0].astype(F32),
                                 jnp.zeros((tail_pad,), F32)])[None, :]
    w_out_b = w_out[0].astype(BF16)
    w_cq_b, w_ck_b, w_cv_b, w_co_b = (a[0].astype(BF16) for a in (w_cq, w_ck, w_cv, w_co))
    w_up_b = w_up[0].astype(BF16)
    w_down_b = w_down[0].astype(BF16)
    row = lambda a: a[0][None, :]

    def split_misc(misc):
        ik = misc[:, :D_IDX]
        li = misc[:, D_IDX + H_IDX:D_IDX + H_IDX + H_M]
        lf = misc[:, D_IDX + H_IDX + H_M:D_IDX + H_IDX + 2 * H_M]
        return ik, li, lf

    tiles = _tiles(S, B * M, d_ff)
    tm_in = tiles["in_rows"]
    tab_p = _rope_tables(jnp.arange(S), dh_a)
    (mq, mk, mv, mo, aqb, ak, av, akb, avb, iqb, misc) = _in_proj(
        x_prompt.reshape(B * S, D), row(norm_mix), w_r, bias_tail, row(q_norm), row(k_norm), tab_p,
        tm=tm_in, tab_tiles=S // tm_in, mix_m=mix_m, mix_a=mix_a)
    ik_p = misc[:, :D_IDX]
    r3 = lambda a: a.reshape(B, S, a.shape[-1])
    gcol = misc[:, D_IDX + H_IDX:D_IDX + H_IDX + 2 * H_M].reshape(B, S, 2 * H_M)
    grow = gcol.reshape(B, S // chunk, chunk, 2 * H_M).transpose(0, 1, 3, 2)
    y_m, c_p, n_p, m_p = _mlstm_prompt(r3(mq), r3(mk), r3(mv), r3(mo), grow, gcol, row(mlstm_norm), chunk=chunk)

    wk = tiles["dsa_k"]
    ikt = ik_p.astype(BF16).reshape(B, S // wk, wk, D_IDX).transpose(0, 1, 3, 2)
    y_a = _dsa_prompt(r3(iqb), r3(misc), ikt, r3(aqb), r3(akb), r3(avb), tq=tiles["dsa_q"], w=wk, topk=topk_p)

    mk_p, mv_p = _mem_kv(mem_prompt.reshape(B * M, D), row(norm_mem), w_ck_b, w_cv_b, row(ck_norm),
                         tm=tiles["mem_rows"])
    x1, qc = _out_cq(x_prompt.reshape(B * S, D), y_m.reshape(B * S, mix_m), y_a.reshape(B * S, mix_a),
                     w_out_b, row(norm_cross), w_cq_b, row(cq_norm), tm=tiles["out_rows"])
    dh_c = D // H_C
    o_c = _cross(qc.reshape(B, S, D), mk_p.reshape(B, M, H_C, dh_c), mv_p.reshape(B, M, H_C, dh_c),
                 tq=tiles["cross_q"])
    tf = tiles["ffn_cols"]
    xp, conv_a, conv_g = _ffn_prompt(x1.reshape(B, S, D), o_c, w_co_b, row(norm_ffn), w_up_b, conv_w[0],
                                     conv_b[0][None, :], w_down_b,
                                     jnp.zeros((B, CONV_W - 1, 2 * d_ff), F32), tm=tiles["ffn_rows"], tf=tf)
    conv_p = jnp.concatenate([conv_a[:, -1], conv_g[:, -1]], axis=-1)

    tab_s = jnp.broadcast_to(_rope_tables(jnp.full((1,), past, jnp.int32), dh_a), (Bd, 2 * LANES))
    (mq_s, mk_s, mv_s, mo_s, aqb_s, ak_s, av_s, _, _, iqb_s, misc_s) = _in_proj(
        x_sample.reshape(Bd, D), row(norm_mix), w_r, bias_tail, row(q_norm), row(k_norm), tab_s,
        tm=Bd, tab_tiles=1, mix_m=mix_m, mix_a=mix_a)
    ik_s, li_s, lf_s = split_misc(misc_s)
    gs = jnp.concatenate([li_s, lf_s, state_mlstm_m[0].astype(F32)], axis=-1)[:, None, :]
    e1 = lambda a: a[:, None, :]
    y_ms, c_s, n_s, m_s = _mlstm_sample(e1(mq_s), e1(mk_s), e1(mv_s), e1(mo_s), gs,
                                        state_mlstm_c[0], state_mlstm_n[0].reshape(Bd, 1, mix_m),
                                        row(mlstm_norm))

    iq8 = jnp.pad(iqb_s.reshape(Bd, H_IDX, D_IDX), ((0, 0), (0, 8 - H_IDX), (0, 0)))
    w8 = jnp.pad(misc_s[:, D_IDX:D_IDX + H_IDX], ((0, 0), (0, 8 - H_IDX)))[:, :, None]
    assert n_pages <= 256
    rows_t, flags = _dsa_sample_select(page_table, iq8, w8, e1(ik_s), jnp.swapaxes(cache_idx_k[0], 1, 2),
                                       topk=topk_s, cw=min(512, past))
    y_as = _dsa_sample_attend(rows_t[:, :Bd].T, flags[:, 0], e1(aqb_s), ak_s, av_s,
                              cache_k[0].reshape(n_pool * page, H_A, dh_a),
                              cache_v[0].reshape(n_pool * page, H_A, dh_a), topk=topk_s)

    x1_s, qc_s = _out_cq(x_sample.reshape(Bd, D), y_ms.reshape(Bd, mix_m), y_as.reshape(Bd, mix_a),
                         w_out_b, row(norm_cross), w_cq_b, row(cq_norm), tm=Bd)
    o_s = _cross(qc_s.reshape(Bd, 1, D), cache_mem_k[0], cache_mem_v[0], tq=1)
    xs, u_a, u_g = _ffn_sample(x1_s.reshape(1, Bd, D), o_s.reshape(1, Bd, D), w_co_b, row(norm_ffn), w_up_b,
                               conv_w[0], conv_b[0][None, :], w_down_b,
                               state_conv[0, :, 0, :], state_conv[0, :, 1, :], tf=tf)
    conv_s = jnp.stack([state_conv[0, :, 1, :], jnp.concatenate([u_a, u_g], axis=-1)], axis=1)

    lead = lambda a: a[None]
    return (xp, xs.reshape(Bd, 1, D),
            lead(ak.reshape(B, S, H_A, dh_a)), lead(av.reshape(B, S, H_A, dh_a)), lead(ik_p.reshape(B, S, D_IDX)),
            lead(c_p), lead(n_p), lead(m_p[:, :, 0]),
            lead(mk_p.reshape(B, M, H_C, D // H_C)), lead(mv_p.reshape(B, M, H_C, D // H_C)), lead(conv_p),
            lead(ak_s.reshape(Bd, 1, H_A, dh_a)), lead(av_s.reshape(Bd, 1, H_A, dh_a)),
            lead(ik_s.reshape(Bd, 1, D_IDX)),
            lead(c_s), lead(n_s.reshape(Bd, H_M, dh_m)), lead(m_s[:, 0, :H_M]), lead(conv_s))
```

```python
import functools

import jax
import jax.numpy as jnp
import numpy as np
from jax import lax
from jax.experimental import pallas as pl
from jax.experimental.pallas import tpu as pltpu

F32 = jnp.float32
BF16 = jnp.bfloat16

H_M = 4
H_A = 4
H_IDX = 4
D_IDX = 64
H_C = 4
TOPK_MAX = 256
CONV_W = 3
ROPE_THETA = 10000.0
EPS = 1e-6
LOG2E = 1.4426950408889634
NEG_INF = float("-inf")
POS_INF = float("inf")

LANES = 128
VMEM_LIMIT = 56 * 1024 * 1024
N_BISECT = 20


def _cparams(sem):
    return pltpu.CompilerParams(dimension_semantics=sem, vmem_limit_bytes=VMEM_LIMIT)


def _nt(a, b):
    return lax.dot_general(a, b, (((1,), (1,)), ((), ())), preferred_element_type=F32)


def _tn(a, b):
    return lax.dot_general(a, b, (((0,), (0,)), ((), ())), preferred_element_type=F32)


def _mm(a, b):
    return jnp.dot(a, b, preferred_element_type=F32)


def _rms(x, g):
    ms = jnp.mean(x * x, axis=-1, keepdims=True)
    return x * lax.rsqrt(ms + EPS) * g


def _sigmoid(x):
    return 1.0 / (1.0 + jnp.exp(-x))


def _in_proj_kernel(x_ref, nm_ref, wm_ref, wa_ref, wt_ref, bias_ref, qn_ref, kn_ref, tab_ref,
                    mq_ref, mk_ref, mv_ref, mo_ref, aqb_ref, ak_ref, av_ref, akb_ref, avb_ref,
                    iqb_ref, misc_ref, *, mix_m, mix_a, dh_m, dh_a):
    h = _rms(x_ref[...], nm_ref[...]).astype(BF16)

    o_mq, o_mk, o_mv, o_mo = 0, mix_m, 2 * mix_m, 3 * mix_m
    n_m = 4 * mix_m
    o_aq = n_m
    o_ak = o_aq + mix_a
    o_av = o_ak + mix_a
    o_iq = o_av + mix_a
    o_tail = o_iq + H_IDX * D_IDX

    def proj(lo, width):
        if lo < n_m:
            return _mm(h, wm_ref[:, lo:lo + width])
        if lo < o_tail:
            return _mm(h, wa_ref[:, lo - n_m:lo - n_m + width])
        return _mm(h, wt_ref[...])

    mq_ref[...] = proj(o_mq, mix_m)
    mk_ref[...] = proj(o_mk, mix_m) * (dh_m ** -0.5)
    mv_ref[...] = proj(o_mv, mix_m)
    mo_ref[...] = proj(o_mo, mix_m)

    t_a = tab_ref[:, 0:LANES]
    t_i = tab_ref[:, LANES:2 * LANES]
    lane = lax.broadcasted_iota(jnp.int32, t_a.shape, 1)
    first_a = lane < dh_a // 2
    r_a = pltpu.roll(t_a, dh_a // 2, 1)
    cos_a = jnp.where(first_a, t_a, r_a)
    sin_a = jnp.where(first_a, -r_a, t_a)
    first_i = (lane & (D_IDX - 1)) < D_IDX // 2
    c_i = jnp.where(first_i, t_i, pltpu.roll(t_i, D_IDX // 2, 1))
    s1_i = jnp.where(first_i, 0.0, t_i)
    s2_i = jnp.where(first_i, -pltpu.roll(t_i, LANES - D_IDX // 2, 1), 0.0)
    in_key = lane < D_IDX
    c_t = jnp.where(in_key, c_i, 1.0)
    s1_t = jnp.where(in_key, s1_i, 0.0)
    s2_t = jnp.where(in_key, s2_i, 0.0)

    def norm_rope(z, g_ref):
        outs = []
        for hh in range(mix_a // dh_a):
            zh = _rms(z[:, hh * dh_a:(hh + 1) * dh_a], g_ref[...])
            outs.append(zh * cos_a + pltpu.roll(zh, dh_a // 2, 1) * sin_a)
        return outs

    aq = norm_rope(proj(o_aq, mix_a), qn_ref)
    aqb_ref[...] = jnp.concatenate(aq, axis=1).astype(BF16)
    ak = norm_rope(proj(o_ak, mix_a), kn_ref)
    av = proj(o_av, mix_a)
    for hh in range(mix_a // dh_a):
        ak_ref[:, hh, :] = ak[hh]
        av_ref[:, hh, :] = av[:, hh * dh_a:(hh + 1) * dh_a]
    akb_ref[...] = jnp.concatenate(ak, axis=1).astype(BF16)
    avb_ref[...] = av.astype(BF16)

    ziq = proj(o_iq, H_IDX * D_IDX)
    cols = []
    for c in range(H_IDX * D_IDX // LANES):
        zc = ziq[:, c * LANES:(c + 1) * LANES]
        cols.append(zc * c_i + pltpu.roll(zc, D_IDX // 2, 1) * s1_i
                    + pltpu.roll(zc, LANES - D_IDX // 2, 1) * s2_i)
    iqb_ref[...] = jnp.concatenate(cols, axis=1).astype(BF16)

    zt = proj(o_tail, LANES) + bias_ref[...]
    zt = zt * c_t + pltpu.roll(zt, D_IDX // 2, 1) * s1_t + pltpu.roll(zt, LANES - D_IDX // 2, 1) * s2_t
    f_lo = D_IDX + H_IDX + H_M
    log_sig = jnp.minimum(zt, 0.0) - jnp.log(1.0 + jnp.exp(-jnp.abs(zt)))
    misc_ref[...] = jnp.where((lane >= f_lo) & (lane < f_lo + H_M), log_sig, zt)


def _in_proj(x2d, norm_mix, w_parts, bias_tail, q_norm, k_norm, tab, *, tm, tab_tiles, mix_m, mix_a):
    rows, d = x2d.shape
    dh_m = mix_m // H_M
    dh_a = mix_a // H_A
    once = lambda a: pl.BlockSpec(a.shape, lambda i: (0, 0), pipeline_mode=pl.Buffered(1))
    grid = (rows // tm,)
    row_spec = lambda wdt: pl.BlockSpec((tm, wdt), lambda i: (i, 0))
    const = lambda shp: pl.BlockSpec(shp, lambda i: (0, 0))
    out_shapes = [
        jax.ShapeDtypeStruct((rows, mix_m), F32),
        jax.ShapeDtypeStruct((rows, mix_m), F32),
        jax.ShapeDtypeStruct((rows, mix_m), F32),
        jax.ShapeDtypeStruct((rows, mix_m), F32),
        jax.ShapeDtypeStruct((rows, mix_a), BF16),
        jax.ShapeDtypeStruct((rows, H_A, dh_a), F32),
        jax.ShapeDtypeStruct((rows, H_A, dh_a), F32),
        jax.ShapeDtypeStruct((rows, mix_a), BF16),
        jax.ShapeDtypeStruct((rows, mix_a), BF16),
        jax.ShapeDtypeStruct((rows, H_IDX * D_IDX), BF16),
        jax.ShapeDtypeStruct((rows, LANES), F32),
    ]
    head_spec = pl.BlockSpec((tm, H_A, dh_a), lambda i: (i, 0, 0))
    out_specs = ([row_spec(mix_m)] * 4 + [row_spec(mix_a), head_spec, head_spec, row_spec(mix_a), row_spec(mix_a)]
                 + [row_spec(H_IDX * D_IDX), row_spec(LANES)])
    return pl.pallas_call(
        functools.partial(_in_proj_kernel, mix_m=mix_m, mix_a=mix_a, dh_m=dh_m, dh_a=dh_a),
        grid=grid,
        in_specs=[row_spec(d), const((1, d)), once(w_parts[0]), once(w_parts[1]), once(w_parts[2]),
                  const((1, LANES)),
                  const((1, dh_a)), const((1, dh_a)),
                  pl.BlockSpec((tm, 2 * LANES), lambda i: (i % tab_tiles, 0))],
        out_specs=out_specs,
        out_shape=out_shapes,
        compiler_params=_cparams(("parallel",)),
        name="in_proj",
    )(x2d, norm_mix, *w_parts, bias_tail, q_norm, k_norm, tab)


def _mlstm_prompt_kernel(q_ref, k_ref, v_ref, o_ref, grow_ref, gcol_ref, gain_ref,
                         y_ref, c_ref, n_ref, m_ref, cs_ref, ns_ref, ms_ref, *, chunk, d, nbp):
    c_idx = pl.program_id(1)
    L = chunk
    row_i = lax.broadcasted_iota(jnp.int32, (L, L), 0)
    col_i = lax.broadcasted_iota(jnp.int32, (L, L), 1)
    tril = col_i <= row_i
    triu = row_i <= col_i

    @pl.when(c_idx == 0)
    def _():
        cs_ref[...] = jnp.zeros_like(cs_ref)
        ns_ref[...] = jnp.zeros_like(ns_ref)
        ms_ref[...] = jnp.zeros_like(ms_ref)

    chains = [(bi, hd) for bi in range(nbp) for hd in range(H_M)]
    tril_b = tril.astype(BF16)
    csum = []
    for bi in range(nbp):
        g = gcol_ref[bi]
        g_hi = g.astype(BF16)
        r1 = g - g_hi.astype(F32)
        g_mid = r1.astype(BF16)
        g_lo = (r1 - g_mid.astype(F32)).astype(BF16)
        csum.append(_mm(tril_b, g_hi) + _mm(tril_b, g_mid) + _mm(tril_b, g_lo))
    st = []
    for bi, hd in chains:
        gr = grow_ref[bi, 0]
        gc = gcol_ref[bi]
        sidx = bi * H_M + hd
        hs = slice(hd * d, (hd + 1) * d)
        m = ms_ref[sidx, 0:1, 0:1]
        li_r = gr[hd:hd + 1, :]
        li_c = gc[:, hd:hd + 1]
        lf_c = gc[:, H_M + hd:H_M + hd + 1]
        b_c = csum[bi][:, H_M + hd:H_M + hd + 1]
        b_r = jnp.sum(jnp.where(triu, lf_c, 0.0), axis=0, keepdims=True)
        logd = jnp.where(tril, b_c - b_r + li_r, NEG_INF)
        inter = b_c + m
        m_t = jnp.maximum(inter, jnp.max(logd, axis=1, keepdims=True))
        st.append(dict(bi=bi, sidx=sidx, hs=hs, m=m, li_c=li_c, b_c=b_c, inter=inter, m_t=m_t,
                       dmat=jnp.exp(logd - m_t)))
    for c in st:
        q = q_ref[c["bi"], :, c["hs"]]
        c["q"] = q
        c["kb"] = k_ref[c["bi"], :, c["hs"]].astype(BF16)
        qb = q.astype(BF16)
        c["C"] = cs_ref[c["sidx"]]
        c["n"] = ns_ref[c["sidx"], 0:1, :]
        c["s"] = _nt(qb, c["kb"]) * c["dmat"]
        c_aug = jnp.concatenate([c["C"], jnp.broadcast_to(c["n"], (d, d))], axis=0).astype(BF16)
        c["qc"] = _nt(qb, c_aug)
    ones_ld = jnp.ones((L, d), BF16)
    for c in st:
        v = v_ref[c["bi"], :, c["hs"]]
        c["v"] = v
        g_inter = jnp.exp(c["inter"] - c["m_t"])
        sv = _mm(c["s"].astype(BF16), jnp.concatenate([v.astype(BF16), ones_ld], axis=1))
        num = g_inter * c["qc"][:, :d] + sv[:, :d]
        den = g_inter * c["qc"][:, d:] + sv[:, d:]
        h = num / jnp.maximum(jnp.abs(den), jnp.exp(-c["m_t"]))
        o = o_ref[c["bi"], :, c["hs"]]
        y_ref[c["bi"], :, c["hs"]] = _sigmoid(o) * _rms(h, gain_ref[:, c["hs"]])
    for c in st:
        m_new = c["m_t"][L - 1:L, :]
        b_last = c["b_c"][L - 1:L, :]
        g_prev = jnp.exp(b_last + c["m"] - m_new)
        w_c = jnp.exp(b_last - c["b_c"] + c["li_c"] - m_new)
        k = k_ref[c["bi"], :, c["hs"]]
        cs_ref[c["sidx"]] = g_prev * c["C"] + _tn((c["v"] * w_c).astype(BF16), c["kb"])
        ns_ref[c["sidx"], 0:1, :] = g_prev * c["n"] + jnp.sum(k * w_c, axis=0, keepdims=True)
        ms_ref[c["sidx"], 0:1, :] = jnp.broadcast_to(m_new, (1, LANES))

    @pl.when(c_idx == pl.num_programs(1) - 1)
    def _():
        for bi in range(nbp):
            for hd in range(H_M):
                sidx = bi * H_M + hd
                c_ref[bi, hd] = cs_ref[sidx]
                n_ref[bi, hd:hd + 1, :] = ns_ref[sidx, 0:1, :]
                m_ref[bi, hd:hd + 1, :] = ms_ref[sidx, 0:1, :]


def _mlstm_prompt(mq, mk, mv, mo, grow, gcol, gain, *, chunk):
    B, S, mix_m = mq.shape
    d = mix_m // H_M
    n_chunks = S // chunk
    nbp = 4 if B % 4 == 0 else (2 if B % 2 == 0 else 1)
    seq = pl.BlockSpec((nbp, chunk, mix_m), lambda b, c: (b, c, 0))
    return pl.pallas_call(
        functools.partial(_mlstm_prompt_kernel, chunk=chunk, d=d, nbp=nbp),
        grid=(B // nbp, n_chunks),
        in_specs=[seq, seq, seq, seq,
                  pl.BlockSpec((nbp, 1, 2 * H_M, chunk), lambda b, c: (b, c, 0, 0)),
                  pl.BlockSpec((nbp, chunk, 2 * H_M), lambda b, c: (b, c, 0)),
                  pl.BlockSpec((1, mix_m), lambda b, c: (0, 0))],
        out_specs=[seq,
                   pl.BlockSpec((nbp, H_M, d, d), lambda b, c: (b, 0, 0, 0)),
                   pl.BlockSpec((nbp, H_M, d), lambda b, c: (b, 0, 0)),
                   pl.BlockSpec((nbp, H_M, LANES), lambda b, c: (b, 0, 0))],
        out_shape=[jax.ShapeDtypeStruct((B, S, mix_m), F32),
                   jax.ShapeDtypeStruct((B, H_M, d, d), F32),
                   jax.ShapeDtypeStruct((B, H_M, d), F32),
                   jax.ShapeDtypeStruct((B, H_M, LANES), F32)],
        scratch_shapes=[pltpu.VMEM((nbp * H_M, d, d), F32), pltpu.VMEM((nbp * H_M, 8, d), F32),
                        pltpu.VMEM((nbp * H_M, 8, LANES), F32)],
        compiler_params=_cparams(("parallel", "arbitrary")),
        name="mlstm_prompt",
    )(mq, mk, mv, mo, grow, gcol, gain)


def _mlstm_sample_kernel(q_ref, k_ref, v_ref, o_ref, gs_ref, c_ref, n_ref, gain_ref,
                         y_ref, co_ref, no_ref, mo_ref, *, d):
    gs = gs_ref[0]
    eye = (lax.broadcasted_iota(jnp.int32, (d, d), 0) == lax.broadcasted_iota(jnp.int32, (d, d), 1))
    lane = lax.broadcasted_iota(jnp.int32, (1, LANES), 1)
    m_out = jnp.zeros((1, LANES), F32)
    for h in range(H_M):
        sl = slice(h * d, (h + 1) * d)
        q = q_ref[0, :, sl]
        k = k_ref[0, :, sl]
        v = v_ref[0, :, sl]
        o = o_ref[0, :, sl]
        li = gs[:, h:h + 1]
        lf = gs[:, H_M + h:H_M + h + 1]
        m = gs[:, 2 * H_M + h:2 * H_M + h + 1]
        C = c_ref[0, h]
        n = n_ref[0, :, sl]
        inter = lf + m
        m_t = jnp.maximum(inter, li)
        s = jnp.sum(q * k, axis=1, keepdims=True) * jnp.exp(li - m_t)
        g = jnp.exp(inter - m_t)
        q8 = jnp.broadcast_to(q, (8, d)).astype(BF16)
        cq = _nt(q8, C.astype(BF16))[0:1, :]
        num = g * cq + s * v
        den = g * jnp.sum(n * q, axis=1, keepdims=True) + s
        hh = num / jnp.maximum(jnp.abs(den), jnp.exp(-m_t))
        w = jnp.exp(li - m_t)
        v_col = jnp.sum(jnp.where(eye, v, 0.0), axis=1, keepdims=True)
        co_ref[0, h] = g * C + (w * v_col) * k
        no_ref[0, :, sl] = g * n + w * k
        m_out = jnp.where(lane == h, m_t, m_out)
        y_ref[0, :, sl] = _sigmoid(o) * _rms(hh, gain_ref[:, sl])
    mo_ref[0] = m_out


def _mlstm_sample(mq, mk, mv, mo, gs, c_state, n_state, gain):
    Bd, _, mix_m = mq.shape
    d = mix_m // H_M
    row = pl.BlockSpec((1, 1, mix_m), lambda b: (b, 0, 0))
    return pl.pallas_call(
        functools.partial(_mlstm_sample_kernel, d=d),
        grid=(Bd,),
        in_specs=[row, row, row, row,
                  pl.BlockSpec((1, 1, 3 * H_M), lambda b: (b, 0, 0)),
                  pl.BlockSpec((1, H_M, d, d), lambda b: (b, 0, 0, 0)),
                  row,
                  pl.BlockSpec((1, mix_m), lambda b: (0, 0))],
        out_specs=[row,
                   pl.BlockSpec((1, H_M, d, d), lambda b: (b, 0, 0, 0)),
                   row,
                   pl.BlockSpec((1, 1, LANES), lambda b: (b, 0, 0))],
        out_shape=[jax.ShapeDtypeStruct((Bd, 1, mix_m), F32),
                   jax.ShapeDtypeStruct((Bd, H_M, d, d), F32),
                   jax.ShapeDtypeStruct((Bd, 1, mix_m), F32),
                   jax.ShapeDtypeStruct((Bd, 1, LANES), F32)],
        compiler_params=_cparams(("parallel",)),
        name="mlstm_sample",
    )(mq, mk, mv, mo, gs, c_state, n_state, gain)


def _dsa_prompt_kernel(iq_ref, misc_ref, ikt_ref, aq_ref, ak_ref, av_ref, ya_ref, sc_ref, acc_ref,
                       *, tq, w, topk, dh, scale):
    i = pl.program_id(1)
    nk = ((i + 1) * tq + w - 1) // w
    kf = float(topk)
    nsub = w // LANES

    q_pos = i * tq + lax.broadcasted_iota(jnp.int32, (tq, 1), 0)
    lane_w = lax.broadcasted_iota(jnp.int32, (1, w), 1)
    iq = iq_ref[0]
    iq_h = [iq[:, h * D_IDX:(h + 1) * D_IDX] for h in range(H_IDX)]
    misc = misc_ref[0]
    w_h = [misc[:, D_IDX + h:D_IDX + h + 1] for h in range(H_IDX)]

    def score_body(c, carry, causal_edge):
        rmax, rmin = carry
        ikc = ikt_ref[0, c]
        score = jnp.zeros((tq, w), F32)
        for h in range(H_IDX):
            score = score + w_h[h] * jnp.maximum(_mm(iq_h[h], ikc), 0.0)
        if causal_edge:
            valid = (c * w + lane_w) <= q_pos
            sc_ref[c] = jnp.where(valid, score, NEG_INF)
            s_hi = jnp.where(valid, score, NEG_INF)
            s_lo = jnp.where(valid, score, POS_INF)
        else:
            sc_ref[c] = score
            s_hi = s_lo = score
        rmax = jnp.maximum(rmax, jnp.max(s_hi, axis=1, keepdims=True))
        rmin = jnp.minimum(rmin, jnp.min(s_lo, axis=1, keepdims=True))
        return rmax, rmin

    n_full = (i * tq) // w
    carry0 = (jnp.full((tq, 1), NEG_INF, F32), jnp.full((tq, 1), POS_INF, F32))
    carry0 = lax.fori_loop(0, n_full, functools.partial(score_body, causal_edge=False), carry0)
    rmax, rmin = lax.fori_loop(n_full, nk, functools.partial(score_body, causal_edge=True), carry0)

    ge = lambda x, t: x >= t
    gt = lambda x, t: x > t

    rh = min(tq, LANES)
    groups = [pl.ds(r0, rh) for r0 in range(0, tq, rh)]
    part = lambda a: [a[r0:r0 + rh] for r0 in range(0, tq, rh)]

    def pass_acc(rows, fn, init, combine):
        def body(c, acc):
            x = sc_ref[c, rows, :]
            for j in range(nsub):
                acc = combine(acc, fn(x[:, j * LANES:(j + 1) * LANES]))
            return acc
        return lax.fori_loop(0, nk, body, jnp.full((rh, LANES), init, F32))

    def count_acc(rows, pred, thr):
        thr_b = jnp.broadcast_to(thr, (rh, LANES))
        return pass_acc(rows, lambda x: jnp.where(pred(x, thr_b), 1.0, 0.0), 0.0, jnp.add)

    def count(rows, pred, thr):
        return jnp.sum(count_acc(rows, pred, thr), axis=1, keepdims=True)

    def min_where(rows, pred, thr):
        thr_b = jnp.broadcast_to(thr, (rh, LANES))
        acc = pass_acc(rows, lambda x: jnp.where(pred(x, thr_b), x, POS_INF), POS_INF, jnp.minimum)
        return jnp.min(acc, axis=1, keepdims=True)

    def bis_body(_, carry):
        los, his, clos = carry
        mids = [0.5 * (lo + hi) for lo, hi in zip(los, his)]
        accs = [count_acc(rows, ge, mid) for rows, mid in zip(groups, mids)]
        cms = [jnp.sum(a, axis=1, keepdims=True) for a in accs]
        oks = [cm >= kf for cm in cms]
        return (tuple(jnp.where(ok, mid, lo) for ok, mid, lo in zip(oks, mids, los)),
                tuple(jnp.where(ok, hi, mid) for ok, mid, hi in zip(oks, mids, his)),
                tuple(jnp.where(ok, cm, cl) for ok, cm, cl in zip(oks, cms, clos)))

    los, _, clos = lax.fori_loop(
        0, N_BISECT, bis_body,
        (tuple(part(rmin)), tuple(part(rmax + jnp.abs(rmax) + 1.0)), tuple(part((q_pos + 1).astype(F32)))))

    def finish_rows(rows, qp, rmin_h, lo, c_lo):
        active = (qp + 1) > topk
        unresolved = jnp.max(jnp.where(active & (c_lo != kf), 1.0, 0.0)) > 0.5

        @pl.when(jnp.logical_not(unresolved))
        def _():
            thr = jnp.where(active, lo, rmin_h)

            def body(c, _):
                sc_ref[c, rows, :] = jnp.where(sc_ref[c, rows, :] >= thr, 0.0, NEG_INF)
                return 0
            lax.fori_loop(0, nk, body, 0)

        @pl.when(unresolved)
        def _():
            tau = min_where(rows, ge, lo)
            g = count(rows, gt, tau)

            def undone(tau, g):
                return active & (g >= kf)

            def fix_cond(st):
                return jnp.max(jnp.where(undone(*st), 1.0, 0.0)) > 0.5

            def fix_body(st):
                tau, g = st
                nd = undone(tau, g)
                tau2 = jnp.where(nd, min_where(rows, gt, tau), tau)
                return tau2, jnp.where(nd, count(rows, gt, tau2), g)

            tau, g = lax.while_loop(fix_cond, fix_body, (tau, g))
            tau_b = jnp.broadcast_to(jnp.where(active, tau, rmin_h), (rh, LANES))
            need_b = jnp.broadcast_to(jnp.where(active, kf - g, 1e9), (rh, LANES))
            r_i = lax.broadcasted_iota(jnp.int32, (LANES, 2 * LANES), 0)
            c_i = lax.broadcasted_iota(jnp.int32, (LANES, 2 * LANES), 1)
            tri_ones = ((r_i <= c_i) | (c_i >= LANES)).astype(BF16)

            def body(c, run):
                x = sc_ref[c, rows, :]
                outs = []
                for j in range(nsub):
                    xj = x[:, j * LANES:(j + 1) * LANES]
                    is_eq = xj == tau_b
                    cnt2 = _mm(jnp.where(is_eq, 1.0, 0.0).astype(BF16), tri_ones)
                    sel = (xj > tau_b) | (is_eq & (cnt2[:, :LANES] + run <= need_b))
                    outs.append(jnp.where(sel, 0.0, NEG_INF))
                    run = run + cnt2[:, LANES:]
                sc_ref[c, rows, :] = jnp.concatenate(outs, axis=1)
                return run
            lax.fori_loop(0, nk, body, jnp.zeros((rh, LANES), F32))

    for rows, qp, rmin_h, lo, c_lo in zip(groups, part(q_pos), part(rmin), los, clos):
        finish_rows(rows, qp, rmin_h, lo, c_lo)

    aq = aq_ref[0]
    q_heads = [aq[:, h * dh:(h + 1) * dh] for h in range(H_A)]
    acc_ref[...] = jnp.zeros_like(acc_ref)
    c2 = scale * LOG2E
    ones_blk = jnp.ones((w, dh), BF16)

    def att_body(c, ms):
        k0 = pl.multiple_of(c * w, w)
        bias = sc_ref[c]
        heads = [slice(h * dh, (h + 1) * dh) for h in range(H_A)]
        ss = [_nt(q_heads[h], ak_ref[0, pl.ds(k0, w), heads[h]]) + bias for h in range(H_A)]
        ms_new = [jnp.maximum(ms[h], jnp.max(ss[h], axis=1, keepdims=True)) for h in range(H_A)]
        m_safe = [jnp.where(m == NEG_INF, 0.0, m) for m in ms_new]
        ps = [jnp.exp2((ss[h] - m_safe[h]) * c2).astype(BF16) for h in range(H_A)]
        for h in range(H_A):
            alpha = jnp.exp2((ms[h] - m_safe[h]) * c2)
            v_aug = jnp.concatenate([av_ref[0, pl.ds(k0, w), heads[h]], ones_blk], axis=1)
            acc_ref[h] = alpha * acc_ref[h] + _mm(ps[h], v_aug)
        return tuple(ms_new)

    lax.fori_loop(0, nk, att_body, tuple(jnp.full((tq, 1), NEG_INF, F32) for _ in range(H_A)))
    for h in range(H_A):
        a = acc_ref[h]
        ya_ref[0, :, h * dh:(h + 1) * dh] = a[:, :dh] / a[:, dh:]


def _dsa_prompt(iqb, misc, ikt, aqb, akb, avb, *, tq, w, topk):
    B, S, mix_a = aqb.shape
    dh = mix_a // H_A
    nq = S // tq
    nw = S // w
    return pl.pallas_call(
        functools.partial(_dsa_prompt_kernel, tq=tq, w=w, topk=topk, dh=dh, scale=dh ** -0.5),
        grid=(B, nq),
        in_specs=[pl.BlockSpec((1, tq, H_IDX * D_IDX), lambda b, i: (b, i, 0)),
                  pl.BlockSpec((1, tq, LANES), lambda b, i: (b, i, 0)),
                  pl.BlockSpec((1, nw, D_IDX, w), lambda b, i: (b, 0, 0, 0)),
                  pl.BlockSpec((1, tq, mix_a), lambda b, i: (b, i, 0)),
                  pl.BlockSpec((1, S, mix_a), lambda b, i: (b, 0, 0)),
                  pl.BlockSpec((1, S, mix_a), lambda b, i: (b, 0, 0))],
        out_specs=pl.BlockSpec((1, tq, mix_a), lambda b, i: (b, i, 0)),
        out_shape=jax.ShapeDtypeStruct((B, S, mix_a), F32),
        scratch_shapes=[pltpu.VMEM((nw, tq, w), F32), pltpu.VMEM((H_A, tq, 2 * dh), F32)],
        compiler_params=_cparams(("parallel", "arbitrary")),
        name="dsa_prompt",
    )(iqb, misc, ikt, aqb, akb, avb)


def _dsa_sample_select_kernel(pt_ref, iq_ref, w_ref, ikn_ref, ptv_ref, pool_ref, rows_ref, flag_ref,
                              ikbuf, sem, sc_ref, xn_ref, slot_ref, phys_ref,
                              *, n_pages, page, topk, cw):
    nb = iq_ref.shape[0]
    past = n_pages * page
    kf = float(topk)
    n_cw = past // cw

    def page_copy(bb, p, slot):
        return pltpu.make_async_copy(pool_ref.at[pt_ref[bb, p]],
                                     ikbuf.at[slot, :, pl.ds(p * page, page)],
                                     sem.at[slot])

    def start_all(bb, slot):
        def body(p, _):
            page_copy(bb, p, slot).start()
            return 0
        lax.fori_loop(0, n_pages, body, 0)

    start_all(0, 0)

    def score_body(b, _):
        slot = b % 2

        @pl.when(b + 1 < nb)
        def _():
            start_all(b + 1, 1 - slot)

        def wait_body(p, _):
            page_copy(b, p, slot).wait()
            return 0
        lax.fori_loop(0, n_pages, wait_body, 0)

        iq8 = iq_ref[b]
        w8 = w_ref[b]
        s8 = _mm(iq8, ikbuf[slot].astype(BF16))
        sc_ref[pl.ds(b, 1), :] = jnp.sum(w8 * jnp.maximum(s8, 0.0), axis=0, keepdims=True)
        ikn = ikn_ref[b].astype(BF16).astype(F32)
        sn8 = jnp.sum(iq8.astype(F32) * ikn, axis=1, keepdims=True)
        xn_b = jnp.sum(w8 * jnp.maximum(sn8, 0.0), axis=0, keepdims=True)
        xn_ref[pl.ds(b, 1), :] = jnp.broadcast_to(xn_b, (1, LANES))
        return 0

    lax.fori_loop(0, nb, score_body, 0)

    x = sc_ref[...]
    xn = xn_ref[:, 0:1]

    def cnt(mask_row, mask_new):
        return (jnp.sum(jnp.where(mask_row, 1.0, 0.0), axis=1, keepdims=True)
                + jnp.where(mask_new, 1.0, 0.0))

    rmax = jnp.maximum(jnp.max(x, axis=1, keepdims=True), xn)
    rmin = jnp.minimum(jnp.min(x, axis=1, keepdims=True), xn)
    hi0 = rmax + jnp.abs(rmax) + 1.0

    def bis_body(_, carry):
        lo, hi = carry
        mid = 0.5 * (lo + hi)
        ok = cnt(x >= mid, xn >= mid) >= kf
        return jnp.where(ok, mid, lo), jnp.where(ok, hi, mid)

    lo, _ = lax.fori_loop(0, N_BISECT, bis_body, (rmin, hi0))

    def min_where(mask_row, mask_new):
        return jnp.minimum(jnp.min(jnp.where(mask_row, x, POS_INF), axis=1, keepdims=True),
                           jnp.where(mask_new, xn, POS_INF))

    tau = min_where(x >= lo, xn >= lo)
    g = cnt(x > tau, xn > tau)

    def fix_cond(st):
        tau, g = st
        return jnp.max(jnp.where(g >= kf, 1.0, 0.0)) > 0.5

    def fix_body(st):
        tau, g = st
        tau2 = jnp.where(g >= kf, min_where(x > tau, xn > tau), tau)
        return tau2, cnt(x > tau2, xn > tau2)

    tau, g = lax.while_loop(fix_cond, fix_body, (tau, g))
    need = kf - g

    tri = (lax.broadcasted_iota(jnp.int32, (cw, cw), 0)
           < lax.broadcasted_iota(jnp.int32, (cw, cw), 1)).astype(BF16)

    def excl_prefix(flag):
        outs = []
        run = jnp.zeros((nb, 1), F32)
        for c in range(n_cw):
            f = flag[:, c * cw:(c + 1) * cw]
            outs.append(_mm(f.astype(BF16), tri) + run)
            run = run + jnp.sum(f, axis=1, keepdims=True)
        return jnp.concatenate(outs, axis=1), run

    is_eq = x == tau
    pre_eq, n_eq_past = excl_prefix(jnp.where(is_eq, 1.0, 0.0))
    sel = (x > tau) | (is_eq & (pre_eq < need))
    new_sel = (xn > tau) | ((xn == tau) & (n_eq_past < need))
    slot, _ = excl_prefix(jnp.where(sel, 1.0, 0.0))
    slot_ref[...] = jnp.where(sel, slot, -1.0)

    ptv = ptv_ref[...]
    jrow = lax.broadcasted_iota(jnp.int32, (1, past), 1)
    prow = lax.broadcasted_iota(jnp.int32, (n_pages, 1), 0)
    expand = ((jrow >= prow * page) & (jrow < (prow + 1) * page)).astype(BF16)
    digit_bits = 6
    pt_hi = _mm((ptv >> digit_bits).astype(F32).astype(BF16), expand)
    pt_lo = _mm((ptv & ((1 << digit_bits) - 1)).astype(F32).astype(BF16), expand)
    pidx = lax.broadcasted_iota(jnp.int32, (8, n_pages), 1).astype(F32).astype(BF16)
    pg = _mm(pidx, expand)[0:1, :]
    phys_ref[...] = (pt_hi * (1 << digit_bits) + pt_lo) * page + (jrow.astype(F32) - pg * page)

    slot_col = lax.broadcasted_iota(jnp.int32, (topk, 1), 0).astype(F32)
    lane_b = lax.broadcasted_iota(jnp.int32, (1, LANES), 1)

    def extract_body(b, out):
        srow = slot_ref[pl.ds(b, 1), :]
        frow = phys_ref[pl.ds(b, 1), :]
        acc = jnp.zeros((topk, LANES), F32)
        for c in range(past // LANES):
            cs = slice(c * LANES, (c + 1) * LANES)
            acc = acc + jnp.where(srow[:, cs] == slot_col, frow[:, cs], 0.0)
        return jnp.where(lane_b == b, jnp.sum(acc, axis=1, keepdims=True), out)

    out = lax.fori_loop(0, nb, extract_body, jnp.zeros((topk, LANES), F32))
    rows_ref[...] = out.astype(jnp.int32)
    flag_ref[...] = jnp.broadcast_to(jnp.where(new_sel, 1, 0), (nb, LANES)).astype(jnp.int32)


def _dsa_sample_select(page_table, iq8, w8, ik_new, pool_ik_t, *, topk, cw):
    Bd, n_pages = page_table.shape
    n_pool, d_idx, page = pool_ik_t.shape
    past = n_pages * page
    assert Bd <= LANES and n_pool <= 64 * 256
    full = lambda shp: pl.BlockSpec(shp, lambda i, pt: (0,) * len(shp))
    grid_spec = pltpu.PrefetchScalarGridSpec(
        num_scalar_prefetch=1,
        grid=(1,),
        in_specs=[full((Bd, 8, d_idx)), full((Bd, 8, 1)), full((Bd, 1, d_idx)), full((Bd, n_pages)),
                  pl.BlockSpec(memory_space=pl.ANY)],
        out_specs=[full((topk, LANES)), full((Bd, LANES))],
        scratch_shapes=[pltpu.VMEM((2, d_idx, past), F32),
                        pltpu.SemaphoreType.DMA((2,)),
                        pltpu.VMEM((Bd, past), F32),
                        pltpu.VMEM((Bd, LANES), F32),
                        pltpu.VMEM((Bd, past), F32),
                        pltpu.VMEM((Bd, past), F32)],
    )
    return pl.pallas_call(
        functools.partial(_dsa_sample_select_kernel, n_pages=n_pages, page=page, topk=topk, cw=cw),
        grid_spec=grid_spec,
        out_shape=[jax.ShapeDtypeStruct((topk, LANES), jnp.int32),
                   jax.ShapeDtypeStruct((Bd, LANES), jnp.int32)],
        compiler_params=_cparams(("arbitrary",)),
        name="dsa_sample_select",
    )(page_table, iq8, w8, ik_new, page_table, pool_ik_t)


def _dsa_sample_attend_kernel(rows_ref, flag_ref, aq_ref, knew_ref, vnew_ref, kpool_ref, vpool_ref,
                              ya_ref, kbuf, vbuf, sem, *, topk, dh, scale):
    b = pl.program_id(0)
    nb = pl.num_programs(0)

    def row_copies(bb, t, slot):
        r = rows_ref[bb, t]
        dst = pl.ds(t * H_A, H_A)
        return (pltpu.make_async_copy(kpool_ref.at[r], kbuf.at[slot, dst, :], sem.at[0, slot]),
                pltpu.make_async_copy(vpool_ref.at[r], vbuf.at[slot, dst, :], sem.at[1, slot]))

    def start_all(bb, slot):
        def body(t, _):
            ck, cv = row_copies(bb, t, slot)
            ck.start()
            cv.start()
            return 0
        lax.fori_loop(0, topk, body, 0, unroll=16)

    slot = b % 2

    @pl.when(b == 0)
    def _():
        start_all(0, 0)

    @pl.when(b + 1 < nb)
    def _():
        start_all(b + 1, 1 - slot)

    def wait_body(t, _):
        ck, cv = row_copies(b, t, slot)
        ck.wait()
        cv.wait()
        return 0
    lax.fori_loop(0, topk, wait_body, 0, unroll=16)

    take_new = (lax.broadcasted_iota(jnp.int32, (topk, 1), 0) == topk - 1) & (flag_ref[b] > 0)
    aq = aq_ref[0]
    for h in range(H_A):
        hs = slice(h * dh, (h + 1) * dh)
        kh = kbuf[slot, pl.ds(h, topk, stride=H_A), :]
        vh = vbuf[slot, pl.ds(h, topk, stride=H_A), :]
        kh = jnp.where(take_new, knew_ref[0, h:h + 1, :], kh).astype(BF16)
        vh = jnp.where(take_new, vnew_ref[0, h:h + 1, :], vh).astype(BF16)
        q8 = jnp.broadcast_to(aq[:, hs], (8, dh))
        s = _nt(q8, kh) * scale
        m = jnp.max(s, axis=1, keepdims=True)
        p = jnp.exp(s - m)
        p = p / jnp.sum(p, axis=1, keepdims=True)
        ya_ref[0, :, hs] = _mm(p.astype(BF16), vh)[0:1, :]


def _dsa_sample_attend(rows, flags, aqb, k_new, v_new, pool_k, pool_v, *, topk):
    Bd, _, mix_a = aqb.shape
    dh = pool_k.shape[2]
    new_spec = pl.BlockSpec((1, H_A, dh), lambda b, r, f: (b, 0, 0))
    grid_spec = pltpu.PrefetchScalarGridSpec(
        num_scalar_prefetch=2,
        grid=(Bd,),
        in_specs=[pl.BlockSpec((1, 1, mix_a), lambda b, r, f: (b, 0, 0)),
                  new_spec, new_spec,
                  pl.BlockSpec(memory_space=pl.ANY),
                  pl.BlockSpec(memory_space=pl.ANY)],
        out_specs=pl.BlockSpec((1, 1, mix_a), lambda b, r, f: (b, 0, 0)),
        scratch_shapes=[pltpu.VMEM((2, topk * H_A, dh), F32),
                        pltpu.VMEM((2, topk * H_A, dh), F32),
                        pltpu.SemaphoreType.DMA((2, 2))],
    )
    return pl.pallas_call(
        functools.partial(_dsa_sample_attend_kernel, topk=topk, dh=dh, scale=dh ** -0.5),
        grid_spec=grid_spec,
        out_shape=jax.ShapeDtypeStruct((Bd, 1, mix_a), F32),
        compiler_params=_cparams(("arbitrary",)),
        name="dsa_sample_attend",
    )(rows, flags, aqb, k_new, v_new, pool_k, pool_v)


def _mem_kv_kernel(mem_ref, nm_ref, wk_ref, wv_ref, kn_ref, k_ref, v_ref, *, dh):
    hm = _rms(mem_ref[...], nm_ref[...]).astype(BF16)
    kk = _mm(hm, wk_ref[...])
    vv = _mm(hm, wv_ref[...])
    for h in range(H_C):
        hs = slice(h * dh, (h + 1) * dh)
        k_ref[:, h, :] = _rms(kk[:, hs], kn_ref[...])
        v_ref[:, h, :] = vv[:, hs]


def _mem_kv(mem2d, norm_mem, w_ck, w_cv, ck_norm, *, tm):
    rows, d = mem2d.shape
    dh = d // H_C
    row = pl.BlockSpec((tm, d), lambda i: (i, 0))
    heads = pl.BlockSpec((tm, H_C, dh), lambda i: (i, 0, 0))
    const = lambda shp: pl.BlockSpec(shp, lambda i: (0, 0))
    return pl.pallas_call(
        functools.partial(_mem_kv_kernel, dh=dh),
        grid=(rows // tm,),
        in_specs=[row, const((1, d)), const((d, d)), const((d, d)), const((1, dh))],
        out_specs=[heads, heads],
        out_shape=[jax.ShapeDtypeStruct((rows, H_C, dh), F32)] * 2,
        compiler_params=_cparams(("parallel",)),
        name="mem_kv",
    )(mem2d, norm_mem, w_ck, w_cv, ck_norm)


def _out_cq_kernel(x_ref, ym_ref, ya_ref, wo_ref, nc_ref, wq_ref, qn_ref, x1_ref, qc_ref, *, mix_m, dh):
    upd = (_mm(ym_ref[...].astype(BF16), wo_ref[0:mix_m, :])
           + _mm(ya_ref[...].astype(BF16), wo_ref[mix_m:, :]))
    x1 = x_ref[...] + upd
    x1_ref[...] = x1
    hq = _mm(_rms(x1, nc_ref[...]).astype(BF16), wq_ref[...])
    for h in range(H_C):
        hs = slice(h * dh, (h + 1) * dh)
        qc_ref[:, hs] = _rms(hq[:, hs], qn_ref[...]).astype(BF16)


def _out_cq(x2d, ym, ya, w_out, norm_cross, w_cq, cq_norm, *, tm):
    rows, d = x2d.shape
    mix_m = ym.shape[1]
    mix_a = ya.shape[1]
    dh = d // H_C
    row = lambda wdt: pl.BlockSpec((tm, wdt), lambda i: (i, 0))
    const = lambda shp: pl.BlockSpec(shp, lambda i: (0, 0))
    return pl.pallas_call(
        functools.partial(_out_cq_kernel, mix_m=mix_m, dh=dh),
        grid=(rows // tm,),
        in_specs=[row(d), row(mix_m), row(mix_a), const((mix_m + mix_a, d)), const((1, d)),
                  const((d, d)), const((1, dh))],
        out_specs=[row(d), row(d)],
        out_shape=[jax.ShapeDtypeStruct((rows, d), F32), jax.ShapeDtypeStruct((rows, d), BF16)],
        compiler_params=_cparams(("parallel",)),
        name="out_cq",
    )(x2d, ym, ya, w_out, norm_cross, w_cq, cq_norm)


def _cross_kernel(q_ref, k_hbm, v_hbm, o_ref, kv_buf, sem, *, dh, scale):
    b = pl.program_id(0)
    t = pl.program_id(1)
    nb = pl.num_programs(0)
    slot = b % 2

    def head_copies(bb, sl):
        cps = []
        for h in range(H_C):
            cps.append(pltpu.make_async_copy(k_hbm.at[bb, :, h, :], kv_buf.at[sl, 0, h], sem.at[sl]))
            cps.append(pltpu.make_async_copy(v_hbm.at[bb, :, h, :], kv_buf.at[sl, 1, h], sem.at[sl]))
        return cps

    @pl.when(t == 0)
    def _():
        @pl.when(b == 0)
        def _():
            for cp in head_copies(0, 0):
                cp.start()

        @pl.when(b + 1 < nb)
        def _():
            for cp in head_copies(b + 1, 1 - slot):
                cp.start()

        for cp in head_copies(b, slot):
            cp.wait()

    q = q_ref[0]
    rows = q.shape[0]
    if rows < 8:
        q = jnp.broadcast_to(q, (8, q.shape[1]))
    for h in range(H_C):
        hs = slice(h * dh, (h + 1) * dh)
        kb = kv_buf[slot, 0, h].astype(BF16)
        vb = kv_buf[slot, 1, h].astype(BF16)
        s = _nt(q[:, hs], kb) * scale
        m = jnp.max(s, axis=1, keepdims=True)
        p = jnp.exp(s - m)
        p = p / jnp.sum(p, axis=1, keepdims=True)
        o = _mm(p.astype(BF16), vb)
        o_ref[0, :, hs] = o[0:rows].astype(BF16)


def _cross(qc, mem_k, mem_v, *, tq):
    B, T, d = qc.shape
    M = mem_k.shape[1]
    dh = d // H_C
    return pl.pallas_call(
        functools.partial(_cross_kernel, dh=dh, scale=dh ** -0.5),
        grid=(B, T // tq),
        in_specs=[pl.BlockSpec((1, tq, d), lambda b, t: (b, t, 0)),
                  pl.BlockSpec(memory_space=pl.ANY),
                  pl.BlockSpec(memory_space=pl.ANY)],
        out_specs=pl.BlockSpec((1, tq, d), lambda b, t: (b, t, 0)),
        out_shape=jax.ShapeDtypeStruct((B, T, d), BF16),
        scratch_shapes=[pltpu.VMEM((2, 2, H_C, M, dh), F32), pltpu.SemaphoreType.DMA((2,))],
        compiler_params=_cparams(("arbitrary", "arbitrary")),
        name="cross_attn",
    )(qc, mem_k, mem_v)


def _gelu_tanh(x):
    return 0.5 * x * (1.0 + jnp.tanh(np.sqrt(2.0 / np.pi) * (x + 0.044715 * (x * x * x))))


def _ffn_front(x1_ref, o_ref, wco_ref, nf_ref, x2_ref, hb_ref, acc_ref):
    x2 = x1_ref[0] + _mm(o_ref[0], wco_ref[...])
    x2_ref[...] = x2
    hb_ref[...] = _rms(x2, nf_ref[...]).astype(BF16)
    acc_ref[...] = jnp.zeros_like(acc_ref)


def _ffn_prompt_kernel(x1_ref, o_ref, wco_ref, nf_ref, wua_ref, wug_ref, cwa_ref, cwg_ref,
                       cba_ref, cbg_ref, wd_ref, ha_ref, hg_ref,
                       y_ref, ca_ref, cg_ref, x2_ref, hb_ref, acc_ref, carry_ref, *, tm, rs):
    t = pl.program_id(1)
    j = pl.program_id(2)
    nj = pl.num_programs(2)

    @pl.when(j == 0)
    def _():
        _ffn_front(x1_ref, o_ref, wco_ref, nf_ref, x2_ref, hb_ref, acc_ref)

    @pl.when(t == 0)
    def _():
        carry_ref[j, 0, 6:8, :] = ha_ref[0]
        carry_ref[j, 1, 6:8, :] = hg_ref[0]

    rid = lax.broadcasted_iota(jnp.int32, (rs, 1), 0)

    def conv_part(hb, part, wu_ref, cw_ref, cb_ref):
        u = _mm(hb, wu_ref[...])
        p2 = carry_ref[j, part, 6:7, :]
        p1 = carry_ref[j, part, 7:8, :]
        um1 = jnp.where(rid == 0, p1, pltpu.roll(u, 1, 0))
        um2 = jnp.where(rid == 0, p2, jnp.where(rid == 1, p1, pltpu.roll(u, 2, 0)))
        carry_ref[j, part] = u[rs - 8:rs, :]
        return cb_ref[...] + um2 * cw_ref[0:1, :] + um1 * cw_ref[1:2, :] + u * cw_ref[2:3, :]

    def sub_body(r, _):
        r0 = pl.multiple_of(r * rs, rs)
        hb = hb_ref[pl.ds(r0, rs), :]
        a = conv_part(hb, 0, wua_ref, cwa_ref, cba_ref)
        g = conv_part(hb, 1, wug_ref, cwg_ref, cbg_ref)
        acc_ref[pl.ds(r0, rs), :] += _mm((_gelu_tanh(g) * a).astype(BF16), wd_ref[...])
        return 0

    lax.fori_loop(0, tm // rs, sub_body, 0)
    ca_ref[0, 0] = carry_ref[j, 0, 6:8, :]
    cg_ref[0, 0] = carry_ref[j, 1, 6:8, :]

    @pl.when(j == nj - 1)
    def _():
        y_ref[0] = x2_ref[...] + acc_ref[...]


def _ffn_prompt(x1, o, w_co, norm_ffn, w_up, conv_w, conv_b, w_down, hist, *, tm, tf):
    B, T, d = x1.shape
    d_ff = w_down.shape[0]
    nj = d_ff // tf
    nt = T // tm
    idx3 = lambda b, t, j: (b, t, 0)
    c2 = lambda shp: pl.BlockSpec(shp, lambda b, t, j: (0, 0))
    return pl.pallas_call(
        functools.partial(_ffn_prompt_kernel, tm=tm, rs=min(512, tm)),
        grid=(B, nt, nj),
        in_specs=[pl.BlockSpec((1, tm, d), idx3), pl.BlockSpec((1, tm, d), idx3),
                  pl.BlockSpec((d, d), lambda b, t, j: (0, 0), pipeline_mode=pl.Buffered(1)), c2((1, d)),
                  pl.BlockSpec((d, tf), lambda b, t, j: (0, j)),
                  pl.BlockSpec((d, tf), lambda b, t, j: (0, nj + j)),
                  pl.BlockSpec((CONV_W, tf), lambda b, t, j: (0, j)),
                  pl.BlockSpec((CONV_W, tf), lambda b, t, j: (0, nj + j)),
                  pl.BlockSpec((1, tf), lambda b, t, j: (0, j)),
                  pl.BlockSpec((1, tf), lambda b, t, j: (0, nj + j)),
                  pl.BlockSpec((tf, d), lambda b, t, j: (j, 0)),
                  pl.BlockSpec((1, CONV_W - 1, tf), lambda b, t, j: (b, 0, j)),
                  pl.BlockSpec((1, CONV_W - 1, tf), lambda b, t, j: (b, 0, nj + j))],
        out_specs=[pl.BlockSpec((1, tm, d), idx3),
                   pl.BlockSpec((1, 1, CONV_W - 1, tf), lambda b, t, j: (b, t, 0, j)),
                   pl.BlockSpec((1, 1, CONV_W - 1, tf), lambda b, t, j: (b, t, 0, j))],
        out_shape=[jax.ShapeDtypeStruct((B, T, d), F32),
                   jax.ShapeDtypeStruct((B, nt, CONV_W - 1, d_ff), F32),
                   jax.ShapeDtypeStruct((B, nt, CONV_W - 1, d_ff), F32)],
        scratch_shapes=[pltpu.VMEM((tm, d), F32), pltpu.VMEM((tm, d), BF16), pltpu.VMEM((tm, d), F32),
                        pltpu.VMEM((nj, 2, 8, tf), F32)],
        compiler_params=_cparams(("arbitrary", "arbitrary", "arbitrary")),
        name="ffn_prompt",
    )(x1, o, w_co, norm_ffn, w_up, w_up, conv_w, conv_w, conv_b, conv_b, w_down, hist, hist)


def _ffn_sample_kernel(x1_ref, o_ref, wco_ref, nf_ref, wua_ref, wug_ref, cwa_ref, cwg_ref,
                       cba_ref, cbg_ref, wd_ref, h0a_ref, h0g_ref, h1a_ref, h1g_ref,
                       y_ref, ua_ref, ug_ref, x2_ref, hb_ref, acc_ref):
    j = pl.program_id(0)
    nj = pl.num_programs(0)

    @pl.when(j == 0)
    def _():
        _ffn_front(x1_ref, o_ref, wco_ref, nf_ref, x2_ref, hb_ref, acc_ref)

    hb = hb_ref[...]

    def conv_part(wu_ref, cw_ref, cb_ref, h0_ref, h1_ref, u_out_ref):
        u = _mm(hb, wu_ref[...])
        u_out_ref[...] = u
        return cb_ref[...] + h0_ref[...] * cw_ref[0:1, :] + h1_ref[...] * cw_ref[1:2, :] + u * cw_ref[2:3, :]

    a = conv_part(wua_ref, cwa_ref, cba_ref, h0a_ref, h1a_ref, ua_ref)
    g = conv_part(wug_ref, cwg_ref, cbg_ref, h0g_ref, h1g_ref, ug_ref)
    acc_ref[...] += _mm((_gelu_tanh(g) * a).astype(BF16), wd_ref[...])

    @pl.when(j == nj - 1)
    def _():
        y_ref[0] = x2_ref[...] + acc_ref[...]


def _ffn_sample(x1, o, w_co, norm_ffn, w_up, conv_w, conv_b, w_down, h0, h1, *, tf):
    _, rows, d = x1.shape
    d_ff = w_down.shape[0]
    nj = d_ff // tf
    c2 = lambda shp: pl.BlockSpec(shp, lambda j: (0, 0))
    c3 = lambda shp: pl.BlockSpec(shp, lambda j: (0, 0, 0))
    col_a = lambda r: pl.BlockSpec((r, tf), lambda j: (0, j))
    col_g = lambda r: pl.BlockSpec((r, tf), lambda j: (0, nj + j))
    return pl.pallas_call(
        _ffn_sample_kernel,
        grid=(nj,),
        in_specs=[c3((1, rows, d)), c3((1, rows, d)), c2((d, d)), c2((1, d)),
                  col_a(d), col_g(d), col_a(CONV_W), col_g(CONV_W), col_a(1), col_g(1),
                  pl.BlockSpec((tf, d), lambda j: (j, 0)),
                  col_a(rows), col_g(rows), col_a(rows), col_g(rows)],
        out_specs=[c3((1, rows, d)), col_a(rows), col_a(rows)],
        out_shape=[jax.ShapeDtypeStruct((1, rows, d), F32),
                   jax.ShapeDtypeStruct((rows, d_ff), F32),
                   jax.ShapeDtypeStruct((rows, d_ff), F32)],
        scratch_shapes=[pltpu.VMEM((rows, d), F32), pltpu.VMEM((rows, d), BF16), pltpu.VMEM((rows, d), F32)],
        compiler_params=_cparams(("arbitrary",)),
        name="ffn_sample",
    )(x1, o, w_co, norm_ffn, w_up, w_up, conv_w, conv_w, conv_b, conv_b, w_down, h0, h0, h1, h1)


def _tiles(seq, mem_rows, d_ff):
    half_ff = d_ff // 2
    return dict(
        in_rows=min(256, seq),
        dsa_q=min(512, seq),
        dsa_k=min(512, seq),
        mem_rows=min(256, mem_rows),
        out_rows=min(1024, seq),
        cross_q=min(2048, seq),
        ffn_rows=min(512, seq),
        ffn_cols=half_ff if half_ff % LANES == 0 else d_ff,
    )


def _rope_tables(pos, dh_a):
    assert dh_a == LANES and 2 * D_IDX == LANES and D_IDX & (D_IDX - 1) == 0
    posf = pos.astype(F32)[:, None]
    half_a = dh_a // 2
    inv_a = ROPE_THETA ** (-jnp.arange(half_a, dtype=F32) / half_a)
    ang_a = posf * inv_a[None, :]
    half_i = D_IDX // 2
    inv_i = ROPE_THETA ** (-jnp.arange(half_i, dtype=F32) / half_i)
    ang_i = posf * inv_i[None, :]
    cos_i, sin_i = jnp.cos(ang_i), jnp.sin(ang_i)
    return jnp.concatenate([jnp.cos(ang_a), jnp.sin(ang_a), cos_i, sin_i, cos_i, sin_i], axis=1)


def kernel(x_prompt, x_sample, mem_prompt, cache_k, cache_v, cache_idx_k, cache_mem_k, cache_mem_v,
           state_mlstm_c, state_mlstm_n, state_mlstm_m, state_conv, page_table,
           norm_mix, w_in, b_if, mlstm_norm, q_norm, k_norm, w_out, norm_cross, norm_mem,
           w_cq, w_ck, w_cv, w_co, cq_norm, ck_norm, norm_ffn, w_up, conv_w, conv_b, w_down):
    B, S, D = x_prompt.shape
    Bd, T, _ = x_sample.shape
    assert T == 1 and w_in.shape[0] == 1
    n_pool, page = cache_k.shape[1], cache_k.shape[2]
    n_pages = page_table.shape[1]
    past = n_pages * page
    mix_m = mlstm_norm.shape[1]
    dh_m = mix_m // H_M
    dh_a = q_norm.shape[1]
    mix_a = H_A * dh_a
    d_ff = w_down.shape[1]
    M = mem_prompt.shape[1]
    chunk = min(256, S)
    topk_p = min(TOPK_MAX, S // 4)
    topk_s = min(TOPK_MAX, (past + T) // 4)

    w = w_in[0]
    o_gate = 4 * mix_m
    o_aq = o_gate + 2 * H_M
    o_iq = o_aq + 3 * mix_a
    o_ik = o_iq + H_IDX * D_IDX
    o_iw = o_ik + D_IDX
    tail_pad = LANES - (D_IDX + H_IDX + 2 * H_M)
    w_r = (w[:, :o_gate].astype(BF16), w[:, o_aq:o_ik].astype(BF16),
           jnp.concatenate([w[:, o_ik:o_iw + H_IDX], w[:, o_gate:o_aq],
                            jnp.zeros((D, tail_pad), w.dtype)], axis=1).astype(BF16))
    bias_tail = jnp.concatenate([jnp.zeros((D_IDX + H_IDX,), F32), b_if[0].astype(F32),
                                 jnp.zeros((tail_pad,), F32)])[None, :]
    w_out_b = w_out[0].astype(BF16)
    w_cq_b, w_ck_b, w_cv_b, w_co_b = (a[0].astype(BF16) for a in (w_cq, w_ck, w_cv, w_co))
    w_up_b = w_up[0].astype(BF16)
    w_down_b = w_down[0].astype(BF16)
    row = lambda a: a[0][None, :]

    def split_misc(misc):
        ik = misc[:, :D_IDX]
        li = misc[:, D_IDX + H_IDX:D_IDX + H_IDX + H_M]
        lf = misc[:, D_IDX + H_IDX + H_M:D_IDX + H_IDX + 2 * H_M]
        return ik, li, lf

    tiles = _tiles(S, B * M, d_ff)
    tm_in = tiles["in_rows"]
    tab_p = _rope_tables(jnp.arange(S), dh_a)
    (mq, mk, mv, mo, aqb, ak, av, akb, avb, iqb, misc) = _in_proj(
        x_prompt.reshape(B * S, D), row(norm_mix), w_r, bias_tail, row(q_norm), row(k_norm), tab_p,
        tm=tm_in, tab_tiles=S // tm_in, mix_m=mix_m, mix_a=mix_a)
    ik_p = misc[:, :D_IDX]
    r3 = lambda a: a.reshape(B, S, a.shape[-1])
    gcol = misc[:, D_IDX + H_IDX:D_IDX + H_IDX + 2 * H_M].reshape(B, S, 2 * H_M)
    grow = gcol.reshape(B, S // chunk, chunk, 2 * H_M).transpose(0, 1, 3, 2)
    y_m, c_p, n_p, m_p = _mlstm_prompt(r3(mq), r3(mk), r3(mv), r3(mo), grow, gcol, row(mlstm_norm), chunk=chunk)

    wk = tiles["dsa_k"]
    ikt = ik_p.astype(BF16).reshape(B, S // wk, wk, D_IDX).transpose(0, 1, 3, 2)
    y_a = _dsa_prompt(r3(iqb), r3(misc), ikt, r3(aqb), r3(akb), r3(avb), tq=tiles["dsa_q"], w=wk, topk=topk_p)

    mk_p, mv_p = _mem_kv(mem_prompt.reshape(B * M, D), row(norm_mem), w_ck_b, w_cv_b, row(ck_norm),
                         tm=tiles["mem_rows"])
    x1, qc = _out_cq(x_prompt.reshape(B * S, D), y_m.reshape(B * S, mix_m), y_a.reshape(B * S, mix_a),
                     w_out_b, row(norm_cross), w_cq_b, row(cq_norm), tm=tiles["out_rows"])
    dh_c = D // H_C
    o_c = _cross(qc.reshape(B, S, D), mk_p.reshape(B, M, H_C, dh_c), mv_p.reshape(B, M, H_C, dh_c),
                 tq=tiles["cross_q"])
    tf = tiles["ffn_cols"]
    xp, conv_a, conv_g = _ffn_prompt(x1.reshape(B, S, D), o_c, w_co_b, row(norm_ffn), w_up_b, conv_w[0],
                                     conv_b[0][None, :], w_down_b,
                                     jnp.zeros((B, CONV_W - 1, 2 * d_ff), F32), tm=tiles["ffn_rows"], tf=tf)
    conv_p = jnp.concatenate([conv_a[:, -1], conv_g[:, -1]], axis=-1)

    tab_s = jnp.broadcast_to(_rope_tables(jnp.full((1,), past, jnp.int32), dh_a), (Bd, 2 * LANES))
    (mq_s, mk_s, mv_s, mo_s, aqb_s, ak_s, av_s, _, _, iqb_s, misc_s) = _in_proj(
        x_sample.reshape(Bd, D), row(norm_mix), w_r, bias_tail, row(q_norm), row(k_norm), tab_s,
        tm=Bd, tab_tiles=1, mix_m=mix_m, mix_a=mix_a)
    ik_s, li_s, lf_s = split_misc(misc_s)
    gs = jnp.concatenate([li_s, lf_s, state_mlstm_m[0].astype(F32)], axis=-1)[:, None, :]
    e1 = lambda a: a[:, None, :]
    y_ms, c_s, n_s, m_s = _mlstm_sample(e1(mq_s), e1(mk_s), e1(mv_s), e1(mo_s), gs,
                                        state_mlstm_c[0], state_mlstm_n[0].reshape(Bd, 1, mix_m),
                                        row(mlstm_norm))

    iq8 = jnp.pad(iqb_s.reshape(Bd, H_IDX, D_IDX), ((0, 0), (0, 8 - H_IDX), (0, 0)))
    w8 = jnp.pad(misc_s[:, D_IDX:D_IDX + H_IDX], ((0, 0), (0, 8 - H_IDX)))[:, :, None]
    assert n_pages <= 256
    rows_t, flags = _dsa_sample_select(page_table, iq8, w8, e1(ik_s), jnp.swapaxes(cache_idx_k[0], 1, 2),
                                       topk=topk_s, cw=min(512, past))
    y_as = _dsa_sample_attend(rows_t[:, :Bd].T, flags[:, 0], e1(aqb_s), ak_s, av_s,
                              cache_k[0].reshape(n_pool * page, H_A, dh_a),
                              cache_v[0].reshape(n_pool * page, H_A, dh_a), topk=topk_s)

    x1_s, qc_s = _out_cq(x_sample.reshape(Bd, D), y_ms.reshape(Bd, mix_m), y_as.reshape(Bd, mix_a),
                         w_out_b, row(norm_cross), w_cq_b, row(cq_norm), tm=Bd)
    o_s = _cross(qc_s.reshape(Bd, 1, D), cache_mem_k[0], cache_mem_v[0], tq=1)
    xs, u_a, u_g = _ffn_sample(x1_s.reshape(1, Bd, D), o_s.reshape(1, Bd, D), w_co_b, row(norm_ffn), w_up_b,
                               conv_w[0], conv_b[0][None, :], w_down_b,
                               state_conv[0, :, 0, :], state_conv[0, :, 1, :], tf=tf)
    conv_s = jnp.stack([state_conv[0, :, 1, :], jnp.concatenate([u_a, u_g], axis=-1)], axis=1)

    lead = lambda a: a[None]
    return (xp, xs.reshape(Bd, 1, D),
            lead(ak.reshape(B, S, H_A, dh_a)), lead(av.reshape(B, S, H_A, dh_a)), lead(ik_p.reshape(B, S, D_IDX)),
            lead(c_p), lead(n_p), lead(m_p[:, :, 0]),
            lead(mk_p.reshape(B, M, H_C, D // H_C)), lead(mv_p.reshape(B, M, H_C, D // H_C)), lead(conv_p),
            lead(ak_s.reshape(Bd, 1, H_A, dh_a)), lead(av_s.reshape(Bd, 1, H_A, dh_a)),
            lead(ik_s.reshape(Bd, 1, D_IDX)),
            lead(c_s), lead(n_s.reshape(Bd, H_M, dh_m)), lead(m_s[:, 0, :H_M]), lead(conv_s))
```
